```python
import jax
import jax.numpy as jnp
from jax import lax
import numpy as np

D_MODEL = 1024
BATCH = 8
SEQ = 4096
DEPTH = 4

GRID_W = 64
CTX_LEN = 256
NORM_EPS = 1e-6
ROPE_THETA = 10000.0

ATT_HEADS = 8
ATT_KV_HEADS = 2
ATT_HEAD_DIM = 64
Q_BLOCK = 128
ATT_W = ATT_HEADS * ATT_HEAD_DIM
ATT_KV_W = ATT_KV_HEADS * ATT_HEAD_DIM

RET_HEADS = 4
RET_HEAD_DIM = 128
RET_CHUNK = 128
RET_W = RET_HEADS * RET_HEAD_DIM

RWKV_HEADS = 8
RWKV_HEAD_DIM = 64
RWKV_W = RWKV_HEADS * RWKV_HEAD_DIM
RWKV_DECAY_LORA = 64
RWKV_AAA_LORA = 64
RWKV_GATE_LORA = 128
RWKV_GN_EPS = 64e-5
RWKV_SPLITS = (RWKV_W, RWKV_W, RWKV_W, RWKV_DECAY_LORA, RWKV_AAA_LORA, RWKV_GATE_LORA)
RWKV_COLS = 3 * RWKV_W + RWKV_DECAY_LORA + RWKV_AAA_LORA + RWKV_GATE_LORA

N_BRANCH = 3
BRANCH_W = 512
IN_SPLITS = (ATT_W, ATT_KV_W, ATT_KV_W, RET_W, RET_W, RET_W, RET_W, RWKV_COLS, N_BRANCH * D_MODEL)
IN_COLS = ATT_W + 2 * ATT_KV_W + 4 * RET_W + RWKV_COLS + N_BRANCH * D_MODEL

MOE_GROUPS = 4
MOE_EXPERTS_PER_GROUP = 8
MOE_EXPERTS = MOE_GROUPS * MOE_EXPERTS_PER_GROUP
MOE_TOPK = 2
MOE_HIDDEN = 512
MOE_BLOCK = 256

kernel_name = 'hybrid_prefix_dit_block'


def rms_norm(x, g):
    xf = x.astype(jnp.float32)
    y = xf * lax.rsqrt(jnp.mean(xf * xf, axis=-1, keepdims=True) + NORM_EPS)
    return (y * g.astype(jnp.float32)).astype(x.dtype)


def split_cols(u, sizes):
    bounds = []
    acc = 0
    for s in sizes[:-1]:
        acc += s
        bounds.append(acc)
    return jnp.split(u, bounds, axis=-1)


def split_heads(u, n_heads):
    b, n, w = u.shape
    return u.reshape(b, n, n_heads, w // n_heads).transpose(0, 2, 1, 3)


def merge_heads(u):
    b, h, n, d = u.shape
    return u.transpose(0, 2, 1, 3).reshape(b, n, h * d)


def axial_rope_tables(n_tokens, head_dim):
    rows = n_tokens // GRID_W
    row = jnp.broadcast_to(jnp.arange(rows, dtype=jnp.float32)[:, None], (rows, GRID_W)).reshape(-1)
    col = jnp.broadcast_to(jnp.arange(GRID_W, dtype=jnp.float32)[None, :], (rows, GRID_W)).reshape(-1)
    quarter = head_dim // 4
    inv_freq = ROPE_THETA ** (-jnp.arange(quarter, dtype=jnp.float32) / quarter)
    ang = jnp.stack([row[:, None] * inv_freq, col[:, None] * inv_freq], axis=1)
    return jnp.cos(ang), jnp.sin(ang)


def apply_axial_rope(x, cos, sin):
    shape = x.shape
    xr = x.astype(jnp.float32).reshape(*shape[:-1], 2, 2, shape[-1] // 4)
    x1, x2 = xr[..., 0, :], xr[..., 1, :]
    out = jnp.stack([x1 * cos - x2 * sin, x2 * cos + x1 * sin], axis=-2)
    return out.reshape(shape).astype(x.dtype)


def centred_shift(u, mu):
    prev = jnp.pad(u[:, :-1], ((0, 0), (1, 0), (0, 0)))
    nxt = jnp.pad(u[:, 1:], ((0, 0), (0, 1), (0, 0)))
    return u + (prev - u) * mu[0] + (nxt - u) * mu[1]


def gqa_softmax(q, k, v):
    s = jnp.einsum('bkgqd,bksd->bkgqs', q.astype(jnp.float32), k) * (q.shape[-1] ** -0.5)
    return jnp.einsum('bkgqs,bksd->bkgqd', jax.nn.softmax(s, axis=-1), v)


def attention_branch(q, k, v, qc, kc, vc, qn_g, kn_g, cos, sin, want_ctx):
    b, n, _ = q.shape
    n_ctx = qc.shape[1]
    grp = ATT_HEADS // ATT_KV_HEADS
    qh = apply_axial_rope(rms_norm(split_heads(q, ATT_HEADS), qn_g), cos, sin)
    kh = apply_axial_rope(rms_norm(split_heads(k, ATT_KV_HEADS), kn_g), cos, sin)
    kch = rms_norm(split_heads(kc, ATT_KV_HEADS), kn_g).astype(jnp.float32)
    vch = split_heads(vc, ATT_KV_HEADS).astype(jnp.float32)
    keys = jnp.concatenate([kch, kh.astype(jnp.float32)], axis=2)
    vals = jnp.concatenate([vch, split_heads(v, ATT_KV_HEADS).astype(jnp.float32)], axis=2)
    qb = qh.reshape(b, ATT_KV_HEADS, grp, n // Q_BLOCK, Q_BLOCK, ATT_HEAD_DIM).transpose(3, 0, 1, 2, 4, 5)
    ob = lax.map(lambda qi: gqa_softmax(qi, keys, vals), qb)
    out = ob.transpose(1, 2, 3, 0, 4, 5).reshape(b, ATT_HEADS, n, ATT_HEAD_DIM)
    y = merge_heads(out).astype(q.dtype)
    if not want_ctx:
        return y, None
    qch = rms_norm(split_heads(qc, ATT_HEADS), qn_g).reshape(b, ATT_KV_HEADS, grp, n_ctx, ATT_HEAD_DIM)
    oc = gqa_softmax(qch, kch, vch).reshape(b, ATT_HEADS, n_ctx, ATT_HEAD_DIM)
    return y, merge_heads(oc).astype(qc.dtype)


def chunk_retention(q, k, v, log_gamma, s0, want_out):
    bsz, nh, n, dk = k.shape
    dv = v.shape[-1]
    n_chunks = n // RET_CHUNK
    pos = jnp.arange(RET_CHUNK, dtype=jnp.float32)
    d_key = jnp.exp(log_gamma[:, None] * (RET_CHUNK - 1.0 - pos))[:, :, None]
    d_chunk = jnp.exp(log_gamma * RET_CHUNK)[:, None, None]
    if want_out:
        lag = pos[:, None] - pos[None, :]
        d_intra = jnp.where(lag >= 0, jnp.exp(log_gamma[:, None, None] * jnp.maximum(lag, 0.0)), 0.0)
        d_query = jnp.exp(log_gamma[:, None] * (pos + 1.0))[:, :, None]
    to_chunks = lambda u: u.reshape(bsz, nh, n_chunks, RET_CHUNK, u.shape[-1]).transpose(2, 0, 1, 3, 4)

    def step(s, blk):
        qc, kc, vc = blk
        s_next = d_chunk * s + jnp.einsum('bhjd,bhje->bhde', kc * d_key, vc)
        if not want_out:
            return s_next, None
        scores = jnp.einsum('bhid,bhjd->bhij', qc, kc) * d_intra
        y = jnp.einsum('bhij,bhje->bhie', scores, vc) + jnp.einsum('bhid,bhde->bhie', qc, s) * d_query
        return s_next, y

    xs = (to_chunks(q) if want_out else None, to_chunks(k), to_chunks(v))
    s_fin, ys = lax.scan(step, s0, xs)
    if not want_out:
        return s_fin, None
    return s_fin, ys.transpose(1, 2, 0, 3, 4).reshape(bsz, nh, n, dv)


def retention_branch(q, k, v, g, qc, kc, vc, gc, decay_logit, gn_g, cos, sin, want_ctx):
    f32 = jnp.float32
    scale = RET_HEAD_DIM ** -0.5
    log_gamma = jax.nn.log_sigmoid(decay_logit.astype(f32))
    ql = apply_axial_rope(split_heads(q, RET_HEADS), cos, sin).astype(f32)
    kl = apply_axial_rope(split_heads(k, RET_HEADS), cos, sin).astype(f32) * scale
    vl = split_heads(v, RET_HEADS).astype(f32)
    qcx = split_heads(qc, RET_HEADS).astype(f32) if want_ctx else None
    kcx = split_heads(kc, RET_HEADS).astype(f32) * scale
    vcx = split_heads(vc, RET_HEADS).astype(f32)
    s0 = jnp.zeros((kcx.shape[0], RET_HEADS, RET_HEAD_DIM, RET_HEAD_DIM), f32)
    flip = lambda u: None if u is None else jnp.flip(u, axis=2)
    s_cf, y_cf = chunk_retention(qcx, kcx, vcx, log_gamma[0], s0, want_ctx)
    s_cb, y_cb = chunk_retention(flip(qcx), flip(kcx), flip(vcx), log_gamma[1], s0, want_ctx)
    _, y_lf = chunk_retention(ql, kl, vl, log_gamma[0], s_cf, True)
    _, y_lb = chunk_retention(flip(ql), flip(kl), flip(vl), log_gamma[1], s_cb, True)

    def readout(y, gate):
        yn = rms_norm(y, gn_g.reshape(RET_HEADS, 1, RET_HEAD_DIM))
        return (jax.nn.silu(gate.astype(f32)) * merge_heads(yn)).astype(gate.dtype)

    y = readout(y_lf + flip(y_lb), g)
    if not want_ctx:
        return y, None
    return y, readout(y_cf + flip(y_cb), gc)


def rwkv7_scan(r, decay, k, v, a_in, b_in, s0, reverse, want_out):
    time_major = lambda u: None if u is None else jnp.swapaxes(u, 0, 1)

    def step(s, inp):
        rt, wt, kt, vt, at, bt = inp
        sa = jnp.einsum('bhij,bhj->bhi', s, at)
        s = s * wt[:, :, None, :] + sa[..., None] * bt[:, :, None, :] + vt[..., None] * kt[:, :, None, :]
        return s, (jnp.einsum('bhij,bhj->bhi', s, rt) if want_out else None)

    xs = tuple(time_major(u) for u in (r, decay, k, v, a_in, b_in))
    s_fin, ys = lax.scan(step, s0, xs, reverse=reverse)
    return s_fin, (jnp.swapaxes(ys, 0, 1) if want_out else None)


def rwkv7_branch(u, uc, mu, w0, w2, a0, a2, g2, k_k, k_a, r_k, ln_g, ln_b, want_ctx):
    f32 = jnp.float32

    def scan_inputs(t):
        bsz, n, _ = t.shape
        t = centred_shift(t.astype(f32), mu.astype(f32))
        r, k, v, xw, xa, xg = split_cols(t, RWKV_SPLITS)
        heads = lambda z: z.reshape(bsz, n, RWKV_HEADS, RWKV_HEAD_DIM)
        kk = heads(k * k_k)
        kk = kk * lax.rsqrt(jnp.maximum(jnp.sum(kk * kk, axis=-1, keepdims=True), 1e-12))
        dirs = []
        for d in range(2):
            w_log = -jax.nn.softplus(-(w0[d] + jnp.tanh(xw) @ w2[d])) - 0.5
            a = jax.nn.sigmoid(a0[d] + xa @ a2[d])
            dirs.append((heads(jnp.exp(-jnp.exp(w_log))),
                         heads(k * (1.0 + (a - 1.0) * k_a)),
                         kk * heads(a)))
        return heads(r), heads(v), kk, xg, dirs

    def readout(y, r, v, dirs, xg):
        bsz, n = y.shape[:2]
        mean = jnp.mean(y, axis=-1, keepdims=True)
        var = jnp.mean(jnp.square(y - mean), axis=-1, keepdims=True)
        yn = ((y - mean) * lax.rsqrt(var + RWKV_GN_EPS)).reshape(bsz, n, RWKV_W) * ln_g + ln_b
        bonus = jnp.sum(r * (dirs[0][1] + dirs[1][1]) * r_k, axis=-1, keepdims=True) * v
        return (yn + bonus.reshape(bsz, n, RWKV_W)) * (jax.nn.sigmoid(xg) @ g2)

    r, v, kk, xg, dirs = scan_inputs(u)
    rc, vc, kkc, xgc, dirs_c = scan_inputs(uc)
    s0 = jnp.zeros((u.shape[0], RWKV_HEADS, RWKV_HEAD_DIM, RWKV_HEAD_DIM), f32)
    y_lat, y_ctx = [], []
    for d in range(2):
        s_ctx, yc = rwkv7_scan(rc if want_ctx else None, dirs_c[d][0], dirs_c[d][1], vc, -kkc, dirs_c[d][2],
                               s0, d == 1, want_ctx)
        _, yl = rwkv7_scan(r, dirs[d][0], dirs[d][1], v, -kk, dirs[d][2], s_ctx, d == 1, True)
        y_lat.append(yl)
        y_ctx.append(yc)
    out = readout(y_lat[0] + y_lat[1], r, v, dirs, xg).astype(u.dtype)
    if not want_ctx:
        return out, None
    return out, readout(y_ctx[0] + y_ctx[1], rc, vc, dirs_c, xgc).astype(uc.dtype)


def merge_branches(y_att, y_ret, y_rwkv, gate_logits, w_branch, w_out):
    ys = jnp.stack([y_att, y_ret, y_rwkv], axis=-2)
    proj = jnp.einsum('blnw,nwd->blnd', ys, w_branch)
    gates = jax.nn.sigmoid(gate_logits.astype(jnp.float32)).reshape(proj.shape)
    return jnp.sum(gates * proj, axis=-2).astype(w_out.dtype) @ w_out


def hybrid_mixer(h, hc, w_in, att_qn_g, att_kn_g, ret_decay_logit, ret_gn_g, rwkv_mu, rwkv_w0, rwkv_w2,
                 rwkv_a0, rwkv_a2, rwkv_g2, rwkv_k_k, rwkv_k_a, rwkv_r_k, rwkv_ln_g, rwkv_ln_b,
                 w_branch, w_out, att_cos, att_sin, ret_cos, ret_sin, want_ctx):
    aq, ak, av, rq, rk, rv, rg, wu, gl = split_cols(h @ w_in, IN_SPLITS)
    aqc, akc, avc, rqc, rkc, rvc, rgc, wuc, glc = split_cols(hc @ w_in, IN_SPLITS)
    ya, yac = attention_branch(aq, ak, av, aqc, akc, avc, att_qn_g, att_kn_g, att_cos, att_sin, want_ctx)
    yr, yrc = retention_branch(rq, rk, rv, rg, rqc, rkc, rvc, rgc, ret_decay_logit, ret_gn_g,
                               ret_cos, ret_sin, want_ctx)
    yw, ywc = rwkv7_branch(wu, wuc, rwkv_mu, rwkv_w0, rwkv_w2, rwkv_a0, rwkv_a2, rwkv_g2, rwkv_k_k,
                           rwkv_k_a, rwkv_r_k, rwkv_ln_g, rwkv_ln_b, want_ctx)
    out = merge_branches(ya, yr, yw, gl, w_branch, w_out).astype(h.dtype)
    if not want_ctx:
        return out, None
    return out, merge_branches(yac, yrc, ywc, glc, w_branch, w_out).astype(hc.dtype)


def hier_moe(t, grp_w, grp_b, exp_w, exp_b, w_gate, w_up, w_down):
    n_tok, d = t.shape
    tf = t.astype(jnp.float32)
    grp_logits = tf @ grp_w.astype(jnp.float32) + grp_b.astype(jnp.float32)
    grp = jnp.argmax(grp_logits, axis=-1)
    p_grp = jnp.max(jax.nn.softmax(grp_logits, axis=-1), axis=-1, keepdims=True)
    exp_logits = (tf @ exp_w.astype(jnp.float32) + exp_b.astype(jnp.float32)).reshape(
        n_tok, MOE_GROUPS, MOE_EXPERTS_PER_GROUP)
    in_grp = exp_logits[jnp.arange(n_tok), grp]
    top_val, top_idx = lax.top_k(in_grp, MOE_TOPK)
    weights = (jax.nn.softmax(top_val, axis=-1) * p_grp).reshape(-1)
    expert = (grp[:, None] * MOE_EXPERTS_PER_GROUP + top_idx).reshape(-1)
    n_pair = n_tok * MOE_TOPK
    order = jnp.argsort(expert)
    e_sorted = expert[order]
    counts = jnp.bincount(expert, length=MOE_EXPERTS)
    padded = (counts + MOE_BLOCK - 1) // MOE_BLOCK * MOE_BLOCK
    pad_end = jnp.cumsum(padded)
    pad_start = pad_end - padded
    grp_start = jnp.cumsum(counts) - counts
    dest = pad_start[e_sorted] + jnp.arange(n_pair) - grp_start[e_sorted]
    n_blocks = -(-n_pair // MOE_BLOCK) + MOE_EXPERTS
    buf = jnp.zeros((n_blocks * MOE_BLOCK, d), t.dtype).at[dest].set(t[order // MOE_TOPK])
    block_expert = jnp.minimum(jnp.searchsorted(pad_end, jnp.arange(n_blocks) * MOE_BLOCK, side='right'),
                               MOE_EXPERTS - 1)

    def expert_block(args):
        xb, e = args
        return (jax.nn.silu(xb @ w_gate[e]) * (xb @ w_up[e])) @ w_down[e]

    yb = lax.map(expert_block, (buf.reshape(n_blocks, MOE_BLOCK, d), block_expert)).reshape(-1, d)
    y_pairs = yb[dest].astype(jnp.float32) * weights[order][:, None]
    y = jnp.zeros((n_tok, d), jnp.float32).at[order // MOE_TOPK].add(y_pairs)
    return y.astype(t.dtype)


def setup_inputs(seed: int = 0) -> dict:
    key = jax.random.key(seed)
    ks = iter(jax.random.split(key, 40))
    f32 = jnp.float32
    nrm = lambda shape, scale: jax.random.normal(next(ks), shape, f32) * scale
    L, D = DEPTH, D_MODEL
    x = nrm((BATCH, SEQ, D), 1.0)
    c = nrm((BATCH, D), 1.0)
    ctx = nrm((BATCH, CTX_LEN, D), 1.0)
    c_ctx = nrm((D,), 1.0)
    ada_w = nrm((L, D, 6 * D), 0.5 * D ** -0.5)
    ada_b = nrm((L, 6 * D), 0.02)
    norm1_g = 1.0 + nrm((L, D), 0.02)
    norm2_g = 1.0 + nrm((L, D), 0.02)
    w_in = nrm((L, D, IN_COLS), D ** -0.5)
    att_qn_g = 1.0 + nrm((L, ATT_HEAD_DIM), 0.02)
    att_kn_g = 1.0 + nrm((L, ATT_HEAD_DIM), 0.02)
    ret_base = jnp.log(2.0 ** (5.0 + jnp.arange(RET_HEADS, dtype=f32)) - 1.0)
    ret_decay_logit = ret_base[None, None, :] + nrm((L, 2, RET_HEADS), 0.1)
    ret_gn_g = 1.0 + nrm((L, RET_W), 0.02)
    rwkv_mu = jax.random.uniform(next(ks), (L, 2, RWKV_COLS), f32, 0.05, 0.45)
    rwkv_w0 = jnp.linspace(-6.5, -1.5, RWKV_W, dtype=f32)[None, None, :] + nrm((L, 2, RWKV_W), 0.1)
    rwkv_w2 = nrm((L, 2, RWKV_DECAY_LORA, RWKV_W), 0.1)
    rwkv_a0 = nrm((L, 2, RWKV_W), 0.1)
    rwkv_a2 = nrm((L, 2, RWKV_AAA_LORA, RWKV_W), 0.3 * RWKV_AAA_LORA ** -0.5)
    rwkv_g2 = nrm((L, RWKV_GATE_LORA, RWKV_W), RWKV_GATE_LORA ** -0.5)
    rwkv_k_k = 0.85 + nrm((L, RWKV_W), 0.02)
    rwkv_k_a = 1.0 + nrm((L, RWKV_W), 0.02)
    rwkv_r_k = nrm((L, RWKV_HEADS, RWKV_HEAD_DIM), 0.1)
    rwkv_ln_g = 1.0 + nrm((L, RWKV_W), 0.02)
    rwkv_ln_b = nrm((L, RWKV_W), 0.02)
    w_branch = nrm((L, N_BRANCH, BRANCH_W, D), BRANCH_W ** -0.5)
    w_out = nrm((L, D, D), D ** -0.5)
    router_grp_w = nrm((L, D, MOE_GROUPS), D ** -0.5)
    router_grp_b = nrm((L, MOE_GROUPS), 0.01)
    router_exp_w = nrm((L, D, MOE_EXPERTS), D ** -0.5)
    router_exp_b = nrm((L, MOE_EXPERTS), 0.01)
    moe_w_gate = nrm((L, MOE_EXPERTS, D, MOE_HIDDEN), D ** -0.5)
    moe_w_up = nrm((L, MOE_EXPERTS, D, MOE_HIDDEN), D ** -0.5)
    moe_w_down = nrm((L, MOE_EXPERTS, MOE_HIDDEN, D), MOE_HIDDEN ** -0.5)
    return {'x': x, 'c': c, 'ctx': ctx, 'c_ctx': c_ctx, 'ada_w': ada_w, 'ada_b': ada_b,
            'norm1_g': norm1_g, 'norm2_g': norm2_g, 'w_in': w_in, 'att_qn_g': att_qn_g, 'att_kn_g': att_kn_g,
            'ret_decay_logit': ret_decay_logit, 'ret_gn_g': ret_gn_g, 'rwkv_mu': rwkv_mu, 'rwkv_w0': rwkv_w0,
            'rwkv_w2': rwkv_w2, 'rwkv_a0': rwkv_a0, 'rwkv_a2': rwkv_a2, 'rwkv_g2': rwkv_g2,
            'rwkv_k_k': rwkv_k_k, 'rwkv_k_a': rwkv_k_a, 'rwkv_r_k': rwkv_r_k, 'rwkv_ln_g': rwkv_ln_g,
            'rwkv_ln_b': rwkv_ln_b, 'w_branch': w_branch, 'w_out': w_out, 'router_grp_w': router_grp_w,
            'router_grp_b': router_grp_b, 'router_exp_w': router_exp_w, 'router_exp_b': router_exp_b,
            'moe_w_gate': moe_w_gate, 'moe_w_up': moe_w_up, 'moe_w_down': moe_w_down}


def reference(x, c, ctx, c_ctx, ada_w, ada_b, norm1_g, norm2_g, w_in, att_qn_g, att_kn_g,
              ret_decay_logit, ret_gn_g, rwkv_mu, rwkv_w0, rwkv_w2, rwkv_a0, rwkv_a2, rwkv_g2,
              rwkv_k_k, rwkv_k_a, rwkv_r_k, rwkv_ln_g, rwkv_ln_b, w_branch, w_out,
              router_grp_w, router_grp_b, router_exp_w, router_exp_b, moe_w_gate, moe_w_up, moe_w_down):
    b, n_lat, d = x.shape
    att_cos, att_sin = axial_rope_tables(n_lat, ATT_HEAD_DIM)
    ret_cos, ret_sin = axial_rope_tables(n_lat, RET_HEAD_DIM)
    cond = jax.nn.silu(c)
    cond_ctx = jax.nn.silu(c_ctx)
    for layer in range(DEPTH):
        want_ctx = layer < DEPTH - 1
        sh1, sc1, g1, sh2, sc2, g2 = jnp.split((cond @ ada_w[layer] + ada_b[layer])[:, None, :], 6, axis=-1)
        sh1c, sc1c, g1c, sh2c, sc2c, g2c = jnp.split(cond_ctx @ ada_w[layer] + ada_b[layer], 6, axis=-1)
        h = rms_norm(x, norm1_g[layer]) * (1.0 + sc1) + sh1
        hc = rms_norm(ctx, norm1_g[layer]) * (1.0 + sc1c) + sh1c
        m, mc = hybrid_mixer(h, hc, w_in[layer], att_qn_g[layer], att_kn_g[layer], ret_decay_logit[layer],
                             ret_gn_g[layer], rwkv_mu[layer], rwkv_w0[layer], rwkv_w2[layer], rwkv_a0[layer],
                             rwkv_a2[layer], rwkv_g2[layer], rwkv_k_k[layer], rwkv_k_a[layer], rwkv_r_k[layer],
                             rwkv_ln_g[layer], rwkv_ln_b[layer], w_branch[layer], w_out[layer],
                             att_cos, att_sin, ret_cos, ret_sin, want_ctx)
        x = x + g1 * m
        h2 = rms_norm(x, norm2_g[layer]) * (1.0 + sc2) + sh2
        tokens = h2.reshape(-1, d)
        if want_ctx:
            ctx = ctx + g1c * mc
            h2c = rms_norm(ctx, norm2_g[layer]) * (1.0 + sc2c) + sh2c
            tokens = jnp.concatenate([tokens, h2c.reshape(-1, d)], axis=0)
        y = hier_moe(tokens, router_grp_w[layer], router_grp_b[layer], router_exp_w[layer], router_exp_b[layer],
                     moe_w_gate[layer], moe_w_up[layer], moe_w_down[layer])
        x = x + g2 * y[: b * n_lat].reshape(x.shape)
        if want_ctx:
            ctx = ctx + g2c * y[b * n_lat:].reshape(ctx.shape)
    return x
```

```python
import functools

import jax
import jax.numpy as jnp
from jax import lax
from jax.experimental import pallas as pl
from jax.experimental.pallas import tpu as pltpu

F32 = jnp.float32
BF16 = jnp.bfloat16

D_MODEL = 1024
GRID_W = 64
NORM_EPS = 1e-6
ROPE_THETA = 10000.0

ATT_HEADS = 8
ATT_KV_HEADS = 2
ATT_HEAD_DIM = 64
ATT_GROUP = ATT_HEADS // ATT_KV_HEADS
ATT_W = ATT_HEADS * ATT_HEAD_DIM
ATT_KV_W = ATT_KV_HEADS * ATT_HEAD_DIM

RET_HEADS = 4
RET_HEAD_DIM = 128
RET_CHUNK = 128
RET_W = RET_HEADS * RET_HEAD_DIM

RWKV_HEADS = 8
RWKV_HEAD_DIM = 64
RWKV_W = RWKV_HEADS * RWKV_HEAD_DIM
RWKV_DECAY_LORA = 64
RWKV_AAA_LORA = 64
RWKV_GATE_LORA = 128
RWKV_GN_EPS = 64e-5
RWKV_COLS = 3 * RWKV_W + RWKV_DECAY_LORA + RWKV_AAA_LORA + RWKV_GATE_LORA

N_BRANCH = 3
BRANCH_W = 512
IN_COLS = ATT_W + 2 * ATT_KV_W + 4 * RET_W + RWKV_COLS + N_BRANCH * D_MODEL

MOE_GROUPS = 4
MOE_EXPERTS_PER_GROUP = 8
MOE_EXPERTS = MOE_GROUPS * MOE_EXPERTS_PER_GROUP
MOE_HIDDEN = 512
MOE_BLOCK = 256

LANES = 128
ROW_TILE = 256
ATT_Q_TILE = 128
SCAN_STEPS = 32
VMEM_LIMIT = 56 * 1024 * 1024

U_GATE = 0
U_RWKV_RK = U_GATE + N_BRANCH * D_MODEL
U_RET = U_RWKV_RK + 2 * RWKV_W
U_ATT = U_RET + 4 * RET_W
U_RWKV_REST = U_ATT + ATT_W + 2 * ATT_KV_W
ATT_COLS = ATT_W + 2 * ATT_KV_W
RWKV_REST_COLS = RWKV_COLS - 2 * RWKV_W


def _column_permutation():
    o_att = 0
    o_ret = ATT_COLS
    o_rwkv = o_ret + 4 * RET_W
    o_gate = o_rwkv + RWKV_COLS
    parts = [jnp.arange(o_gate, o_gate + N_BRANCH * D_MODEL),
             jnp.arange(o_rwkv, o_rwkv + 2 * RWKV_W),
             jnp.arange(o_ret, o_ret + 4 * RET_W),
             jnp.arange(o_att, o_att + ATT_COLS),
             jnp.arange(o_rwkv + 2 * RWKV_W, o_rwkv + RWKV_COLS)]
    return jnp.concatenate(parts)


def _params(*sem):
    return pltpu.CompilerParams(dimension_semantics=sem, vmem_limit_bytes=VMEM_LIMIT)


def _dot(a, b):
    return jnp.dot(a, b, preferred_element_type=F32)


def _dot_nt(a, b):
    return lax.dot_general(a, b, (((1,), (1,)), ((), ())), preferred_element_type=F32)


def _split(a):
    hi = a.astype(BF16)
    lo = (a - hi.astype(F32)).astype(BF16)
    return hi, lo


def _dot3(a, b):
    ah, al = _split(a)
    bh, bl = _split(b)
    return _dot(ah, bh) + (_dot(al, bh) + _dot(ah, bl))


def _dot2_exact_rhs(a, b_bf16):
    ah, al = _split(a)
    return _dot(ah, b_bf16) + _dot(al, b_bf16)


def _group_ones(width, group):
    r = lax.broadcasted_iota(jnp.int32, (width, width), 0) // group
    c = lax.broadcasted_iota(jnp.int32, (width, width), 1) // group
    return jnp.where(r == c, 1.0, 0.0).astype(BF16)


def _silu(x):
    return x * jax.nn.sigmoid(x)


def _swap_halves(x, quarter):
    n = x.shape[-1]
    lane = lax.broadcasted_iota(jnp.int32, x.shape, x.ndim - 1)
    up = pltpu.roll(x, n - quarter, x.ndim - 1)
    down = pltpu.roll(x, quarter, x.ndim - 1)
    return jnp.where(lane % (2 * quarter) < quarter, up, down)


def _mod_kernel(c_ref, w_ref, b_ref, o_ref):
    o_ref[...] = _dot3(_silu(c_ref[...]), w_ref[...]) + b_ref[...]


def _modulation(cvec, ada_w, ada_b):
    depth, d, cols = ada_w.shape
    rows = cvec.shape[0]
    tn = 1536
    return pl.pallas_call(
        _mod_kernel,
        grid=(depth, cols // tn),
        in_specs=[pl.BlockSpec((rows, d), lambda l, j: (0, 0)),
                  pl.BlockSpec((None, d, tn), lambda l, j: (l, 0, j)),
                  pl.BlockSpec((None, 1, tn), lambda l, j: (l, 0, j))],
        out_specs=pl.BlockSpec((None, rows, tn), lambda l, j: (l, 0, j)),
        out_shape=jax.ShapeDtypeStruct((depth, rows, cols), F32),
        compiler_params=_params("parallel", "parallel"),
        name="modulation",
    )(cvec, ada_w, ada_b.reshape(depth, 1, cols))


def _in_proj_kernel(x_ref, a_ref, b_ref, w_ref, o_ref):
    x = x_ref[...]
    ms = jnp.mean(x * x, axis=-1, keepdims=True)
    h = x * lax.rsqrt(ms + NORM_EPS) * a_ref[...] + b_ref[...]
    o_ref[...] = _dot(h.astype(BF16), w_ref[...])


def _in_proj(x, mod_a, mod_b, w):
    bsz, t, d = x.shape
    cols = w.shape[1]
    tm, tn = ROW_TILE, cols // 2
    sel = lambda j, b, i: (2 * b + jnp.minimum(i, 1), 0, 0)
    return pl.pallas_call(
        _in_proj_kernel,
        grid=(cols // tn, bsz, t // tm),
        in_specs=[pl.BlockSpec((None, tm, d), lambda j, b, i: (b, i, 0)),
                  pl.BlockSpec((None, 1, d), sel),
                  pl.BlockSpec((None, 1, d), sel),
                  pl.BlockSpec((d, tn), lambda j, b, i: (0, j))],
        out_specs=pl.BlockSpec((None, tm, tn), lambda j, b, i: (b, i, j)),
        out_shape=jax.ShapeDtypeStruct((bsz, t, cols), F32),
        compiler_params=_params("parallel", "parallel", "parallel"),
        name="in_proj",
    )(x, mod_a, mod_b, w)


def _att_prep_kernel(u_ref, cos_ref, sin_ref, qg_ref, kg_ref, o_ref):
    ones = _group_ones(LANES, ATT_HEAD_DIM)
    cos = cos_ref[...]
    sin = sin_ref[...]
    n_qk = (ATT_W + ATT_KV_W) // LANES
    for j in range(n_qk):
        x = u_ref[:, j * LANES:(j + 1) * LANES]
        is_q = j < ATT_W // LANES
        gain = qg_ref[...] if is_q else kg_ref[...]
        ms = _dot2_exact_rhs(x * x, ones) * (1.0 / ATT_HEAD_DIM)
        y = x * lax.rsqrt(ms + NORM_EPS) * gain
        y = y * cos + _swap_halves(y, ATT_HEAD_DIM // 4) * sin
        if is_q:
            y = y * (ATT_HEAD_DIM ** -0.5)
        o_ref[:, j * LANES:(j + 1) * LANES] = y.astype(BF16)
    o_ref[:, ATT_W + ATT_KV_W:] = u_ref[:, ATT_W + ATT_KV_W:].astype(BF16)


def _att_prep(u, cos, sin, qn_g, kn_g):
    bsz, t, _ = u.shape
    tm = ROW_TILE
    rep = LANES // ATT_HEAD_DIM
    return pl.pallas_call(
        _att_prep_kernel,
        grid=(bsz, t // tm),
        in_specs=[pl.BlockSpec((None, tm, ATT_COLS), lambda b, i: (b, i, U_ATT // ATT_COLS)),
                  pl.BlockSpec((tm, LANES), lambda b, i: (i, 0)),
                  pl.BlockSpec((tm, LANES), lambda b, i: (i, 0)),
                  pl.BlockSpec((1, LANES), lambda b, i: (0, 0)),
                  pl.BlockSpec((1, LANES), lambda b, i: (0, 0))],
        out_specs=pl.BlockSpec((None, tm, ATT_COLS), lambda b, i: (b, i, 0)),
        out_shape=jax.ShapeDtypeStruct((bsz, t, ATT_COLS), BF16),
        compiler_params=_params("parallel", "parallel"),
        name="att_prep",
    )(u, cos, sin, jnp.tile(qn_g, rep).reshape(1, LANES), jnp.tile(kn_g, rep).reshape(1, LANES))


def _att_kernel(q_ref, k_ref, v_ref, o_ref, *, n_ctx, tq):
    i = pl.program_id(1)
    hd = ATT_HEAD_DIM

    def run(n_keys):
        for g in range(ATT_KV_HEADS):
            q = jnp.concatenate(
                [q_ref[:, (ATT_GROUP * g + h) * hd:(ATT_GROUP * g + h + 1) * hd] for h in range(ATT_GROUP)], axis=0)
            k = k_ref[0:n_keys, g * hd:(g + 1) * hd]
            v = v_ref[0:n_keys, g * hd:(g + 1) * hd]
            s = _dot_nt(q, k)
            m = jnp.max(s, axis=-1, keepdims=True)
            p = jnp.exp(s - m)
            l = jnp.sum(p, axis=-1, keepdims=True)
            o = _dot(p.astype(BF16), v) / l
            for h in range(ATT_GROUP):
                c0 = (ATT_GROUP * g + h) * hd
                o_ref[:, c0:c0 + hd] = o[h * tq:(h + 1) * tq].astype(o_ref.dtype)

    @pl.when(i < n_ctx // tq)
    def _():
        run(n_ctx)

    @pl.when(i >= n_ctx // tq)
    def _():
        run(k_ref.shape[0])


def _attention(qkv, n_ctx):
    bsz, t, _ = qkv.shape
    tq = ATT_Q_TILE
    kcol = ATT_W // ATT_KV_W
    return pl.pallas_call(
        functools.partial(_att_kernel, n_ctx=n_ctx, tq=tq),
        grid=(bsz, t // tq),
        in_specs=[pl.BlockSpec((None, tq, ATT_W), lambda b, i: (b, i, 0)),
                  pl.BlockSpec((None, t, ATT_KV_W), lambda b, i: (b, 0, kcol)),
                  pl.BlockSpec((None, t, ATT_KV_W), lambda b, i: (b, 0, kcol + 1))],
        out_specs=pl.BlockSpec((None, tq, ATT_W), lambda b, i: (b, i, 0)),
        out_shape=jax.ShapeDtypeStruct((bsz, t, ATT_W), BF16),
        compiler_params=_params("parallel", "parallel"),
        name="attention",
    )(qkv, qkv, qkv)


def _ret_kernel(q_ref, k_ref, v_ref, g_ref, cos_ref, sin_ref, lg_ref, gn_ref, o_ref, y_ref, s_ref, *, n_ctx):
    c = RET_CHUNK
    t = q_ref.shape[0]
    n_chunks = t // c
    n_cc = n_ctx // c
    quarter = RET_HEAD_DIM // 4
    scale = RET_HEAD_DIM ** -0.5
    lg_f = lg_ref[0]
    lg_b = lg_ref[1]
    row = lax.broadcasted_iota(jnp.int32, (c, c), 0)
    col = lax.broadcasted_iota(jnp.int32, (c, c), 1)
    rowf = row.astype(F32)
    lag = (row - col).astype(F32)

    def chunk(ci):
        r0 = pl.multiple_of(ci * c, c)
        cos = cos_ref[pl.ds(r0, c), :]
        sin = sin_ref[pl.ds(r0, c), :]
        q = q_ref[pl.ds(r0, c), :]
        k = k_ref[pl.ds(r0, c), :]
        q = q * cos + _swap_halves(q, quarter) * sin
        k = (k * cos + _swap_halves(k, quarter) * sin) * scale
        return r0, q, k, v_ref[pl.ds(r0, c), :]

    def step(q, k, v, d_intra, d_query, d_key, d_chunk):
        s = s_ref[...]
        qb = q.astype(BF16)
        vb = v.astype(BF16)
        scores = _dot_nt(qb, k.astype(BF16)) * d_intra
        y = _dot(scores.astype(BF16), vb) + _dot(qb, s.astype(BF16)) * d_query
        s_ref[...] = d_chunk * s + _dot((k * d_key).T.astype(BF16), vb)
        return y

    d_intra = jnp.where(lag >= 0, jnp.exp(lg_f * jnp.maximum(lag, 0.0)), 0.0)
    d_query = jnp.exp(lg_f * (rowf + 1.0))
    d_key = jnp.exp(lg_f * (c - 1.0 - rowf))
    d_chunk = jnp.exp(lg_f * float(c))
    s_ref[...] = jnp.zeros_like(s_ref)

    def fwd(n, carry):
        r0, q, k, v = chunk(n)
        y_ref[pl.ds(r0, c), :] = step(q, k, v, d_intra, d_query, d_key, d_chunk)
        return carry

    lax.fori_loop(0, n_chunks, fwd, 0)

    d_intra_b = jnp.where(lag <= 0, jnp.exp(lg_b * jnp.maximum(-lag, 0.0)), 0.0)
    d_query_b = jnp.exp(lg_b * (float(c) - rowf))
    d_key_b = jnp.exp(lg_b * rowf)
    d_chunk_b = jnp.exp(lg_b * float(c))
    s_ref[...] = jnp.zeros_like(s_ref)
    gn = gn_ref[...]

    def bwd(n, carry):
        ci = jnp.where(n < n_cc, n_cc - 1 - n, n_chunks - 1 - (n - n_cc))
        r0, q, k, v = chunk(ci)
        y = y_ref[pl.ds(r0, c), :] + step(q, k, v, d_intra_b, d_query_b, d_key_b, d_chunk_b)
        yn = y * lax.rsqrt(jnp.mean(y * y, axis=-1, keepdims=True) + NORM_EPS) * gn
        o_ref[pl.ds(r0, c), :] = (_silu(g_ref[pl.ds(r0, c), :]) * yn).astype(o_ref.dtype)
        return carry

    lax.fori_loop(0, n_chunks, bwd, 0)


def _retention(u, cos, sin, log_gamma, gn_g, n_ctx):
    bsz, t, _ = u.shape
    hd = RET_HEAD_DIM
    base = U_RET // hd
    spec = lambda off: pl.BlockSpec((None, t, hd), lambda b, h: (b, 0, base + off * RET_HEADS + h))
    lg = jnp.broadcast_to(log_gamma[:, :, None, None], (2, RET_HEADS, 1, LANES)).astype(F32)
    return pl.pallas_call(
        functools.partial(_ret_kernel, n_ctx=n_ctx),
        grid=(bsz, RET_HEADS),
        in_specs=[spec(0), spec(1), spec(2), spec(3),
                  pl.BlockSpec((t, hd), lambda b, h: (0, 0)),
                  pl.BlockSpec((t, hd), lambda b, h: (0, 0)),
                  pl.BlockSpec((2, None, 1, LANES), lambda b, h: (0, h, 0, 0)),
                  pl.BlockSpec((1, hd), lambda b, h: (0, h))],
        out_specs=pl.BlockSpec((None, t, hd), lambda b, h: (b, 0, h)),
        out_shape=jax.ShapeDtypeStruct((bsz, t, RET_W), BF16),
        scratch_shapes=[pltpu.VMEM((t, hd), F32), pltpu.VMEM((hd, hd), F32)],
        compiler_params=_params("parallel", "parallel"),
        name="retention",
    )(u, u, u, u, cos, sin, lg, gn_g.reshape(1, RET_W))


def _rwkv_prep_kernel(rk_ref, rk_prev_ref, rk_next_ref, rest_ref, rest_prev_ref, rest_next_ref,
                      mu_rk_ref, mu_rest_ref, w0_ref, w2_ref, a0_ref, a2_ref, g2_ref, kk_ref, ka_ref, rk_gain_ref,
                      r_out, v_out, nkk_out, w_out, k_out, b_out, gate_out, bonus_out, *, n_tiles):
    i = pl.program_id(1)
    tm = rk_ref.shape[0]
    has_prev = jnp.logical_and(i != 0, i != 1)
    has_next = jnp.logical_and(i != 0, i != n_tiles - 1)

    def shifted(x_ref, prev_ref, next_ref, mu_ref):
        x = x_ref[...]
        rows = lax.broadcasted_iota(jnp.int32, x.shape, 0)
        halo_prev = jnp.where(has_prev, prev_ref[7:8, :], 0.0)
        halo_next = jnp.where(has_next, next_ref[0:1, :], 0.0)
        prev = jnp.where(rows == 0, halo_prev, pltpu.roll(x, 1, 0))
        nxt = jnp.where(rows == tm - 1, halo_next, pltpu.roll(x, tm - 1, 0))
        return x + (prev - x) * mu_ref[0:1, :] + (nxt - x) * mu_ref[1:2, :]

    rk = shifted(rk_ref, rk_prev_ref, rk_next_ref, mu_rk_ref)
    rest = shifted(rest_ref, rest_prev_ref, rest_next_ref, mu_rest_ref)
    w = RWKV_W
    r = rk[:, 0:w]
    k = rk[:, w:2 * w]
    v = rest[:, 0:w]
    xw = rest[:, w:w + RWKV_DECAY_LORA]
    xa = rest[:, w + RWKV_DECAY_LORA:w + RWKV_DECAY_LORA + RWKV_AAA_LORA]
    xg = rest[:, w + RWKV_DECAY_LORA + RWKV_AAA_LORA:]

    ones = _group_ones(w, RWKV_HEAD_DIM)
    kk = k * kk_ref[...]
    kk = kk * lax.rsqrt(jnp.maximum(_dot2_exact_rhs(kk * kk, ones), 1e-12))
    r_out[...] = r
    v_out[...] = v
    nkk_out[...] = -kk
    tw = jnp.tanh(xw)
    k_sum = jnp.zeros_like(k)
    for d in range(2):
        w_log = -jax.nn.softplus(-(w0_ref[d:d + 1, :] + _dot3(tw, w2_ref[d]))) - 0.5
        a = jax.nn.sigmoid(a0_ref[d:d + 1, :] + _dot3(xa, a2_ref[d]))
        k_d = k * (1.0 + (a - 1.0) * ka_ref[...])
        w_out[d] = jnp.exp(-jnp.exp(w_log))
        k_out[d] = k_d
        b_out[d] = kk * a
        k_sum = k_sum + k_d
    gate_out[...] = _dot3(jax.nn.sigmoid(xg), g2_ref[...])
    bonus_out[...] = _dot2_exact_rhs(r * k_sum * rk_gain_ref[...], ones) * v


def _rwkv_prep(u, mu, w0, w2, a0, a2, g2, k_k, k_a, r_k):
    bsz, t, _ = u.shape
    tm = ROW_TILE
    n_tiles = t // tm
    w = RWKV_W
    rk_blk = U_RWKV_RK // (2 * w)
    rest_blk = U_RWKV_REST // RWKV_REST_COLS
    sub = tm // 8
    n_sub = t // 8
    prev_idx = lambda b, i: jnp.maximum(i * sub - 1, 0)
    next_idx = lambda b, i: jnp.minimum((i + 1) * sub, n_sub - 1)
    row = lambda a: a.reshape(1, -1)
    const = lambda shape: pl.BlockSpec(shape, lambda b, i: (0,) * len(shape))
    tok = lambda width: pl.BlockSpec((None, tm, width), lambda b, i: (b, i, 0))
    tok2 = lambda width: pl.BlockSpec((2, None, tm, width), lambda b, i: (0, b, i, 0))
    sd = lambda *lead: jax.ShapeDtypeStruct((*lead, bsz, t, w), F32)
    return pl.pallas_call(
        functools.partial(_rwkv_prep_kernel, n_tiles=n_tiles),
        grid=(bsz, n_tiles),
        in_specs=[pl.BlockSpec((None, tm, 2 * w), lambda b, i: (b, i, rk_blk)),
                  pl.BlockSpec((None, 8, 2 * w), lambda b, i: (b, prev_idx(b, i), rk_blk)),
                  pl.BlockSpec((None, 8, 2 * w), lambda b, i: (b, next_idx(b, i), rk_blk)),
                  pl.BlockSpec((None, tm, RWKV_REST_COLS), lambda b, i: (b, i, rest_blk)),
                  pl.BlockSpec((None, 8, RWKV_REST_COLS), lambda b, i: (b, prev_idx(b, i), rest_blk)),
                  pl.BlockSpec((None, 8, RWKV_REST_COLS), lambda b, i: (b, next_idx(b, i), rest_blk)),
                  const((2, 2 * w)), const((2, RWKV_REST_COLS)),
                  const((2, w)), const((2, RWKV_DECAY_LORA, w)), const((2, w)), const((2, RWKV_AAA_LORA, w)),
                  const((RWKV_GATE_LORA, w)), const((1, w)), const((1, w)), const((1, w))],
        out_specs=[tok(w), tok(w), tok(w), tok2(w), tok2(w), tok2(w), tok(w), tok(w)],
        out_shape=[sd(), sd(), sd(), sd(2), sd(2), sd(2), sd(), sd()],
        compiler_params=_params("parallel", "parallel"),
        name="rwkv_prep",
    )(u, u, u, u, u, u, mu[:, :2 * w], mu[:, 2 * w:], w0, w2, a0, a2, g2, row(k_k), row(k_a), row(r_k))


def _rwkv_scan_kernel(r_ref, w_ref, k_ref, v_ref, a_ref, b_ref, y_ref, s_ref, sa_ref):
    n = RWKV_HEAD_DIM
    steps = r_ref.shape[0]

    @pl.when(pl.program_id(0) == 0)
    def _():
        s_ref[...] = jnp.zeros_like(s_ref)

    acc = jnp.zeros((n, LANES), F32)
    for j in range(n):
        acc = acc + s_ref[j] * a_ref[0, j:j + 1, :]
    sa_ref[...] = acc

    def step(t, carry):
        tn = jnp.minimum(t + 1, steps - 1)
        sa = sa_ref[...]
        v = v_ref[t]
        y = jnp.zeros((n, LANES), F32)
        sa_next = jnp.zeros((n, LANES), F32)
        for j in range(n):
            s = (s_ref[j] * w_ref[t, j:j + 1, :] + sa * b_ref[t, j:j + 1, :]) + v * k_ref[t, j:j + 1, :]
            s_ref[j] = s
            y = y + s * r_ref[t, j:j + 1, :]
            sa_next = sa_next + s * a_ref[tn, j:j + 1, :]
        y_ref[t] = y
        sa_ref[...] = sa_next
        return carry

    lax.fori_loop(0, steps, step, 0)


def _rwkv_scan(r, w, k, v, a, b):
    steps, n, lanes = r.shape
    ts = SCAN_STEPS
    spec = pl.BlockSpec((ts, n, lanes), lambda s: (s, 0, 0))
    return pl.pallas_call(
        _rwkv_scan_kernel,
        grid=(steps // ts,),
        in_specs=[spec] * 6,
        out_specs=spec,
        out_shape=jax.ShapeDtypeStruct((steps, n, lanes), F32),
        scratch_shapes=[pltpu.VMEM((n, n, lanes), F32), pltpu.VMEM((n, lanes), F32)],
        compiler_params=_params("arbitrary"),
        name="rwkv_scan",
    )(r, w, k, v, a, b)


def _rwkv_readout_kernel(y_ref, bonus_ref, gate_ref, g_ref, b_ref, o_ref):
    ones = _group_ones(RWKV_W, RWKV_HEAD_DIM)
    inv = 1.0 / RWKV_HEAD_DIM
    y = y_ref[...]
    mean = _dot2_exact_rhs(y, ones) * inv
    yc = y - mean
    var = _dot2_exact_rhs(yc * yc, ones) * inv
    yn = yc * lax.rsqrt(var + RWKV_GN_EPS) * g_ref[...] + b_ref[...]
    o_ref[...] = ((yn + bonus_ref[...]) * gate_ref[...]).astype(o_ref.dtype)


def _rwkv_readout(y, bonus, gate, ln_g, ln_b):
    bsz, t, w = y.shape
    tm = ROW_TILE
    tok = pl.BlockSpec((None, tm, w), lambda b, i: (b, i, 0))
    vec = pl.BlockSpec((1, w), lambda b, i: (0, 0))
    return pl.pallas_call(
        _rwkv_readout_kernel,
        grid=(bsz, t // tm),
        in_specs=[tok, tok, tok, vec, vec],
        out_specs=tok,
        out_shape=jax.ShapeDtypeStruct((bsz, t, w), BF16),
        compiler_params=_params("parallel", "parallel"),
        name="rwkv_readout",
    )(y, bonus, gate, ln_g.reshape(1, w), ln_b.reshape(1, w))


def _to_scan_layout(z, z_rev, n_ctx):
    bsz, t, _ = z.shape
    flip = lambda a: jnp.concatenate([a[:, :n_ctx][:, ::-1], a[:, n_ctx:][:, ::-1]], axis=1)
    heads = lambda a: a.reshape(bsz, t, RWKV_HEADS, RWKV_HEAD_DIM).transpose(1, 3, 0, 2).reshape(
        t, RWKV_HEAD_DIM, bsz * RWKV_HEADS)
    both = jnp.concatenate([heads(z), heads(flip(z_rev))], axis=-1)
    return jnp.pad(both, ((0, 0), (0, 0), (0, LANES - both.shape[-1])))


def _from_scan_layout(y, bsz, n_ctx):
    t, n, lanes = y.shape
    half = bsz * RWKV_HEADS
    y = y[:, :, :2 * half]
    unheads = lambda a: a.reshape(t, n, bsz, RWKV_HEADS).transpose(2, 0, 3, 1).reshape(bsz, t, RWKV_HEADS * n)
    flip = lambda a: jnp.concatenate([a[:, :n_ctx][:, ::-1], a[:, n_ctx:][:, ::-1]], axis=1)
    return unheads(y[:, :, :half]) + flip(unheads(y[:, :, half:]))


def _merge_kernel(ya_ref, yr_ref, yw_ref, g0_ref, g1_ref, g2_ref, x_ref, gate1_ref, a2_ref, b2_ref,
                  wb_ref, wo_ref, wr_ref, br_ref,
                  x_out, h_out, ids_out, wts_out, cnt_out):
    first = jnp.logical_and(pl.program_id(0) == 0, pl.program_id(1) == 0)

    @pl.when(first)
    def _():
        cnt_out[...] = jnp.zeros_like(cnt_out)

    merged = (jax.nn.sigmoid(g0_ref[...]) * _dot(ya_ref[...], wb_ref[0])
              + jax.nn.sigmoid(g1_ref[...]) * _dot(yr_ref[...], wb_ref[1])
              + jax.nn.sigmoid(g2_ref[...]) * _dot(yw_ref[...], wb_ref[2]))
    x = x_ref[...] + gate1_ref[...] * _dot(merged.astype(BF16), wo_ref[...])
    x_out[...] = x
    h = x * lax.rsqrt(jnp.mean(x * x, axis=-1, keepdims=True) + NORM_EPS) * a2_ref[...] + b2_ref[...]
    h_out[...] = h.astype(BF16)

    tm = x.shape[0]
    logits = _dot3(h, wr_ref[...]) + br_ref[...]
    lane = lax.broadcasted_iota(jnp.int32, (tm, LANES), 1)
    lane_f = lane.astype(F32)
    neg = -jnp.inf
    big = float(LANES)
    first = lambda hit: jnp.min(jnp.where(hit, lane_f, big), axis=-1, keepdims=True).astype(jnp.int32)
    is_grp = jnp.logical_and(lane >= MOE_EXPERTS, lane < MOE_EXPERTS + MOE_GROUPS)
    gl = jnp.where(is_grp, logits, neg)
    gmax = jnp.max(gl, axis=-1, keepdims=True)
    gidx = first(gl == gmax) - MOE_EXPERTS
    p_grp = 1.0 / jnp.sum(jnp.where(is_grp, jnp.exp(gl - gmax), 0.0), axis=-1, keepdims=True)
    in_grp = jnp.logical_and(lane < MOE_EXPERTS, lane // MOE_EXPERTS_PER_GROUP == gidx)
    el = jnp.where(in_grp, logits, neg)
    v1 = jnp.max(el, axis=-1, keepdims=True)
    i1 = first(el == v1)
    el2 = jnp.where(lane == i1, neg, el)
    v2 = jnp.max(el2, axis=-1, keepdims=True)
    i2 = first(el2 == v2)
    e2 = jnp.exp(v2 - v1)
    w1 = p_grp / (1.0 + e2)
    w2 = p_grp * e2 / (1.0 + e2)
    wts_out[...] = jnp.where(lane == 0, w1, jnp.where(lane == 1, w2, 0.0))

    onehot = jnp.where(jnp.logical_or(lane == i1, lane == i2), 1.0, 0.0)
    rr = lax.broadcasted_iota(jnp.int32, (tm, tm), 0)
    cc = lax.broadcasted_iota(jnp.int32, (tm, tm), 1)
    below = jnp.where(cc < rr, 1.0, 0.0).astype(BF16)
    before = _dot(below, onehot.astype(BF16)) + cnt_out[0:1, :]
    rank1 = jnp.sum(jnp.where(lane == i1, before, 0.0), axis=-1, keepdims=True).astype(jnp.int32)
    rank2 = jnp.sum(jnp.where(lane == i2, before, 0.0), axis=-1, keepdims=True).astype(jnp.int32)
    ids_out[...] = jnp.where(lane == 0, i1, jnp.where(lane == 1, i2, jnp.where(lane == 2, rank1,
                                                                                  jnp.where(lane == 3, rank2, 0))))
    cnt_out[...] = cnt_out[...] + jnp.sum(onehot, axis=0, keepdims=True)


def _merge(ya, yr, yw, u, x, gate1, a2, b2, w_branch, w_out, w_router, b_router):
    bsz, t, d = x.shape
    tm = ROW_TILE
    sel = lambda b, i: (2 * b + jnp.minimum(i, 1), 0, 0)
    tok = lambda width, blk=0: pl.BlockSpec((None, tm, width), lambda b, i: (b, i, blk))
    const = lambda shape: pl.BlockSpec(shape, lambda b, i: (0,) * len(shape))
    mod = pl.BlockSpec((None, 1, d), sel)
    return pl.pallas_call(
        _merge_kernel,
        grid=(bsz, t // tm),
        in_specs=[tok(BRANCH_W), tok(BRANCH_W), tok(BRANCH_W), tok(d, 0), tok(d, 1), tok(d, 2), tok(d),
                  mod, mod, mod,
                  const((N_BRANCH, BRANCH_W, d)), const((d, d)), const((d, LANES)), const((1, LANES))],
        out_specs=[tok(d), tok(d), tok(LANES), tok(LANES), const((8, LANES))],
        out_shape=[jax.ShapeDtypeStruct((bsz, t, d), F32), jax.ShapeDtypeStruct((bsz, t, d), BF16),
                   jax.ShapeDtypeStruct((bsz, t, LANES), jnp.int32), jax.ShapeDtypeStruct((bsz, t, LANES), F32),
                   jax.ShapeDtypeStruct((8, LANES), F32)],
        compiler_params=_params("arbitrary", "arbitrary"),
        name="merge_router",
    )(ya, yr, yw, u, u, u, x, gate1, a2, b2, w_branch, w_out, w_router, b_router)


def _moe_kernel(be_ref, na_ref, x_ref, wg_ref, wu_ref, wd_ref, o_ref):
    i = pl.program_id(0)

    @pl.when(i < na_ref[0])
    def _():
        x = x_ref[...]
        act = _silu(_dot(x, wg_ref[...])) * _dot(x, wu_ref[...])
        o_ref[...] = _dot(act.astype(BF16), wd_ref[...])

    @pl.when(i >= na_ref[0])
    def _():
        o_ref[...] = jnp.zeros_like(o_ref)


def _moe_experts(buf, block_expert, n_active, w_gate, w_up, w_down):
    rows, d = buf.shape
    hid = w_gate.shape[-1]
    grid_spec = pltpu.PrefetchScalarGridSpec(
        num_scalar_prefetch=2,
        grid=(rows // MOE_BLOCK,),
        in_specs=[pl.BlockSpec((MOE_BLOCK, d), lambda i, be, na: (i, 0)),
                  pl.BlockSpec((None, d, hid), lambda i, be, na: (be[i], 0, 0)),
                  pl.BlockSpec((None, d, hid), lambda i, be, na: (be[i], 0, 0)),
                  pl.BlockSpec((None, hid, d), lambda i, be, na: (be[i], 0, 0))],
        out_specs=pl.BlockSpec((MOE_BLOCK, d), lambda i, be, na: (i, 0)))
    return pl.pallas_call(
        _moe_kernel,
        grid_spec=grid_spec,
        out_shape=jax.ShapeDtypeStruct((rows, d), F32),
        compiler_params=_params("arbitrary"),
        name="moe_experts",
    )(block_expert, n_active, buf, w_gate, w_up, w_down)


def _combine_kernel(x_ref, y0_ref, y1_ref, w_ref, g_ref, o_ref):
    w = w_ref[...]
    y = y0_ref[...] * w[:, 0:1] + y1_ref[...] * w[:, 1:2]
    o_ref[...] = x_ref[...] + g_ref[...] * y


def _combine(x, y_pairs, wts, gate2):
    bsz, t, d = x.shape
    tm = ROW_TILE
    sel = lambda b, i: (2 * b + jnp.minimum(i, 1), 0, 0)
    tok = lambda width, blk=0: pl.BlockSpec((None, tm, width), lambda b, i: (b, i, blk))
    return pl.pallas_call(
        _combine_kernel,
        grid=(bsz, t // tm),
        in_specs=[tok(d), tok(d, 0), tok(d, 1), tok(LANES), pl.BlockSpec((None, 1, d), sel)],
        out_specs=tok(d),
        out_shape=jax.ShapeDtypeStruct((bsz, t, d), F32),
        compiler_params=_params("parallel", "parallel"),
        name="moe_combine",
    )(x, y_pairs, y_pairs, wts, gate2)


def _moe(h, ids, wts, counts, w_gate, w_up, w_down):
    bsz, t, d = h.shape
    n_tok = bsz * t
    n_pair = 2 * n_tok
    n_blocks = -(-n_pair // MOE_BLOCK) + MOE_EXPERTS
    counts = counts[0, :MOE_EXPERTS].astype(jnp.int32)
    padded = (counts + MOE_BLOCK - 1) // MOE_BLOCK * MOE_BLOCK
    pad_end = jnp.cumsum(padded)
    pad_start = pad_end - padded
    expert = ids[..., 0:2].reshape(n_pair)
    rank = ids[..., 2:4].reshape(n_pair)
    dest = pad_start[expert] + rank
    token = jnp.arange(n_pair, dtype=jnp.int32) // 2
    src = jnp.zeros((n_blocks * MOE_BLOCK,), jnp.int32).at[dest].set(token)
    block_expert = jnp.minimum(jnp.searchsorted(pad_end, jnp.arange(n_blocks) * MOE_BLOCK, side='right'),
                               MOE_EXPERTS - 1).astype(jnp.int32)
    n_active = (pad_end[-1:] // MOE_BLOCK).astype(jnp.int32)
    buf = jnp.take(h.reshape(n_tok, d), src, axis=0)
    yb = _moe_experts(buf, block_expert, n_active, w_gate, w_up, w_down)
    return jnp.take(yb, dest, axis=0).reshape(bsz, t, 2 * d)


def _rope_tables(n_ctx, n_lat, head_dim):
    rows = n_lat // GRID_W
    row = jnp.broadcast_to(jnp.arange(rows, dtype=F32)[:, None], (rows, GRID_W)).reshape(-1)
    col = jnp.broadcast_to(jnp.arange(GRID_W, dtype=F32)[None, :], (rows, GRID_W)).reshape(-1)
    quarter = head_dim // 4
    inv_freq = ROPE_THETA ** (-jnp.arange(quarter, dtype=F32) / quarter)
    ang = jnp.stack([row[:, None] * inv_freq, col[:, None] * inv_freq], axis=1)
    cos, sin = jnp.cos(ang), jnp.sin(ang)
    cos_t = jnp.stack([cos, cos], axis=2).reshape(n_lat, head_dim)
    sin_t = jnp.stack([-sin, sin], axis=2).reshape(n_lat, head_dim)
    cos_t = jnp.concatenate([jnp.ones((n_ctx, head_dim), F32), cos_t], axis=0)
    sin_t = jnp.concatenate([jnp.zeros((n_ctx, head_dim), F32), sin_t], axis=0)
    rep = LANES // head_dim
    return jnp.tile(cos_t, (1, rep)), jnp.tile(sin_t, (1, rep))


def kernel(x, c, ctx, c_ctx, ada_w, ada_b, norm1_g, norm2_g, w_in, att_qn_g, att_kn_g, ret_decay_logit, ret_gn_g, rwkv_mu, rwkv_w0, rwkv_w2, rwkv_a0, rwkv_a2, rwkv_g2, rwkv_k_k, rwkv_k_a, rwkv_r_k, rwkv_ln_g, rwkv_ln_b, w_branch, w_out, router_grp_w, router_grp_b, router_exp_w, router_exp_b, moe_w_gate, moe_w_up, moe_w_down):
    bsz, n_lat, d = x.shape
    n_ctx = ctx.shape[1]
    depth = ada_w.shape[0]
    assert d == D_MODEL and n_ctx == ROW_TILE and n_lat % ROW_TILE == 0 and n_lat % GRID_W == 0
    assert 2 * bsz * RWKV_HEADS <= LANES

    att_cos, att_sin = _rope_tables(n_ctx, n_lat, ATT_HEAD_DIM)
    ret_cos, ret_sin = _rope_tables(n_ctx, n_lat, RET_HEAD_DIM)

    rows = -(-(bsz + 1) // 8) * 8
    cvec = jnp.zeros((rows, d), F32).at[:bsz].set(c).at[bsz].set(c_ctx)
    mods = _modulation(cvec, ada_w, ada_b)
    perm = _column_permutation()

    xs = jnp.concatenate([ctx, x], axis=1)
    for layer in range(depth):
        m = mods[layer].reshape(rows, 6, d)
        pick = lambda j: jnp.stack([jnp.broadcast_to(m[bsz, j], (bsz, d)), m[:bsz, j]], axis=1).reshape(2 * bsz, 1, d)
        sh1, sc1, g1, sh2, sc2, g2 = (pick(j) for j in range(6))
        w_l = w_in[layer][:, perm].astype(BF16)
        u = _in_proj(xs, norm1_g[layer] * (1.0 + sc1), sh1, w_l)

        qkv = _att_prep(u, att_cos, att_sin, att_qn_g[layer], att_kn_g[layer])
        ya = _attention(qkv, n_ctx)

        log_gamma = jax.nn.log_sigmoid(ret_decay_logit[layer].astype(F32))
        yr = _retention(u, ret_cos, ret_sin, log_gamma, ret_gn_g[layer], n_ctx)

        r, v, nkk, w_d, k_d, b_d, gate, bonus = _rwkv_prep(
            u, rwkv_mu[layer], rwkv_w0[layer], rwkv_w2[layer], rwkv_a0[layer], rwkv_a2[layer], rwkv_g2[layer],
            rwkv_k_k[layer], rwkv_k_a[layer], rwkv_r_k[layer].reshape(-1))
        y_scan = _rwkv_scan(_to_scan_layout(r, r, n_ctx), _to_scan_layout(w_d[0], w_d[1], n_ctx),
                            _to_scan_layout(k_d[0], k_d[1], n_ctx), _to_scan_layout(v, v, n_ctx),
                            _to_scan_layout(nkk, nkk, n_ctx), _to_scan_layout(b_d[0], b_d[1], n_ctx))
        yw = _rwkv_readout(_from_scan_layout(y_scan, bsz, n_ctx), bonus, gate, rwkv_ln_g[layer], rwkv_ln_b[layer])

        w_router = jnp.zeros((d, LANES), F32).at[:, :MOE_EXPERTS].set(router_exp_w[layer]).at[
            :, MOE_EXPERTS:MOE_EXPERTS + MOE_GROUPS].set(router_grp_w[layer])
        b_router = jnp.zeros((1, LANES), F32).at[0, :MOE_EXPERTS].set(router_exp_b[layer]).at[
            0, MOE_EXPERTS:MOE_EXPERTS + MOE_GROUPS].set(router_grp_b[layer])
        xs, h2, ids, wts, counts = _merge(
            ya, yr, yw, u, xs, g1, norm2_g[layer] * (1.0 + sc2), sh2,
            w_branch[layer].astype(BF16), w_out[layer].astype(BF16), w_router, b_router)

        y_pairs = _moe(h2, ids, wts, counts, moe_w_gate[layer].astype(BF16), moe_w_up[layer].astype(BF16),
                       moe_w_down[layer].astype(BF16))
        xs = _combine(xs, y_pairs, wts, g2)
    return xs[:, n_ctx:]
```

```python
import functools

import jax
import jax.numpy as jnp
from jax import lax
from jax.experimental import pallas as pl
from jax.experimental.pallas import tpu as pltpu

F32 = jnp.float32
BF16 = jnp.bfloat16

D_MODEL = 1024
GRID_W = 64
NORM_EPS = 1e-6
ROPE_THETA = 10000.0

ATT_HEADS = 8
ATT_KV_HEADS = 2
ATT_HEAD_DIM = 64
ATT_GROUP = ATT_HEADS // ATT_KV_HEADS
ATT_W = ATT_HEADS * ATT_HEAD_DIM
ATT_KV_W = ATT_KV_HEADS * ATT_HEAD_DIM

RET_HEADS = 4
RET_HEAD_DIM = 128
RET_CHUNK = 128
RET_W = RET_HEADS * RET_HEAD_DIM

RWKV_HEADS = 8
RWKV_HEAD_DIM = 64
RWKV_W = RWKV_HEADS * RWKV_HEAD_DIM
RWKV_DECAY_LORA = 64
RWKV_AAA_LORA = 64
RWKV_GATE_LORA = 128
RWKV_GN_EPS = 64e-5
RWKV_COLS = 3 * RWKV_W + RWKV_DECAY_LORA + RWKV_AAA_LORA + RWKV_GATE_LORA

N_BRANCH = 3
BRANCH_W = 512
IN_COLS = ATT_W + 2 * ATT_KV_W + 4 * RET_W + RWKV_COLS + N_BRANCH * D_MODEL

MOE_GROUPS = 4
MOE_EXPERTS_PER_GROUP = 8
MOE_EXPERTS = MOE_GROUPS * MOE_EXPERTS_PER_GROUP
MOE_HIDDEN = 512
MOE_BLOCK = 256

LANES = 128
ROW_TILE = 256
ATT_Q_TILE = 128
SCAN_STEPS = 32
VMEM_LIMIT = 56 * 1024 * 1024

ROW_R, ROW_A, ROW_W, ROW_K, ROW_B, N_SCAN_ROWS = 0, 1, 2, 4, 6, 8

U_GATE = 0
U_RWKV_RK = U_GATE + N_BRANCH * D_MODEL
U_RET = U_RWKV_RK + 2 * RWKV_W
U_ATT = U_RET + 4 * RET_W
U_RWKV_REST = U_ATT + ATT_W + 2 * ATT_KV_W
ATT_COLS = ATT_W + 2 * ATT_KV_W
RWKV_REST_COLS = RWKV_COLS - 2 * RWKV_W


def _column_permutation():
    o_att = 0
    o_ret = ATT_COLS
    o_rwkv = o_ret + 4 * RET_W
    o_gate = o_rwkv + RWKV_COLS
    parts = [jnp.arange(o_gate, o_gate + N_BRANCH * D_MODEL),
             jnp.arange(o_rwkv, o_rwkv + 2 * RWKV_W),
             jnp.arange(o_ret, o_ret + 4 * RET_W),
             jnp.arange(o_att, o_att + ATT_COLS),
             jnp.arange(o_rwkv + 2 * RWKV_W, o_rwkv + RWKV_COLS)]
    return jnp.concatenate(parts)


def _params(*sem):
    return pltpu.CompilerParams(dimension_semantics=sem, vmem_limit_bytes=VMEM_LIMIT)


def _dot(a, b):
    return jnp.dot(a, b, preferred_element_type=F32)


def _dot_nt(a, b):
    return lax.dot_general(a, b, (((1,), (1,)), ((), ())), preferred_element_type=F32)


def _split(a):
    hi = a.astype(BF16)
    lo = (a - hi.astype(F32)).astype(BF16)
    return hi, lo


def _dot3(a, b):
    ah, al = _split(a)
    bh, bl = _split(b)
    return _dot(ah, bh) + (_dot(al, bh) + _dot(ah, bl))


def _dot2_exact_rhs(a, b_bf16):
    ah, al = _split(a)
    return _dot(ah, b_bf16) + _dot(al, b_bf16)


def _group_ones(width, group):
    r = lax.broadcasted_iota(jnp.int32, (width, width), 0) // group
    c = lax.broadcasted_iota(jnp.int32, (width, width), 1) // group
    return jnp.where(r == c, 1.0, 0.0).astype(BF16)


def _silu(x):
    return x * jax.nn.sigmoid(x)


def _swap_halves(x, quarter):
    n = x.shape[-1]
    lane = lax.broadcasted_iota(jnp.int32, x.shape, x.ndim - 1)
    up = pltpu.roll(x, n - quarter, x.ndim - 1)
    down = pltpu.roll(x, quarter, x.ndim - 1)
    return jnp.where(lane % (2 * quarter) < quarter, up, down)


def _mod_kernel(c_ref, w_ref, b_ref, o_ref):
    o_ref[...] = _dot3(_silu(c_ref[...]), w_ref[...]) + b_ref[...]


def _modulation(cvec, ada_w, ada_b):
    depth, d, cols = ada_w.shape
    rows = cvec.shape[0]
    tn = 1536
    return pl.pallas_call(
        _mod_kernel,
        grid=(depth, cols // tn),
        in_specs=[pl.BlockSpec((rows, d), lambda l, j: (0, 0)),
                  pl.BlockSpec((None, d, tn), lambda l, j: (l, 0, j)),
                  pl.BlockSpec((None, 1, tn), lambda l, j: (l, 0, j))],
        out_specs=pl.BlockSpec((None, rows, tn), lambda l, j: (l, 0, j)),
        out_shape=jax.ShapeDtypeStruct((depth, rows, cols), F32),
        compiler_params=_params("parallel", "parallel"),
        name="modulation",
    )(cvec, ada_w, ada_b.reshape(depth, 1, cols))


def _in_proj_kernel(x_ref, a_ref, b_ref, w_ref, o_ref):
    x = x_ref[...]
    ms = jnp.mean(x * x, axis=-1, keepdims=True)
    h = x * lax.rsqrt(ms + NORM_EPS) * a_ref[...] + b_ref[...]
    o_ref[...] = _dot(h.astype(BF16), w_ref[...])


def _in_proj(x, mod_a, mod_b, w):
    bsz, t, d = x.shape
    cols = w.shape[1]
    tm, tn = ROW_TILE, cols // 2
    sel = lambda j, b, i: (2 * b + jnp.minimum(i, 1), 0, 0)
    return pl.pallas_call(
        _in_proj_kernel,
        grid=(cols // tn, bsz, t // tm),
        in_specs=[pl.BlockSpec((None, tm, d), lambda j, b, i: (b, i, 0)),
                  pl.BlockSpec((None, 1, d), sel),
                  pl.BlockSpec((None, 1, d), sel),
                  pl.BlockSpec((d, tn), lambda j, b, i: (0, j))],
        out_specs=pl.BlockSpec((None, tm, tn), lambda j, b, i: (b, i, j)),
        out_shape=jax.ShapeDtypeStruct((bsz, t, cols), F32),
        compiler_params=_params("parallel", "parallel", "parallel"),
        name="in_proj",
    )(x, mod_a, mod_b, w)


def _att_prep_kernel(u_ref, cos_ref, sin_ref, qg_ref, kg_ref, o_ref):
    ones = _group_ones(LANES, ATT_HEAD_DIM)
    cos = cos_ref[...]
    sin = sin_ref[...]
    n_qk = (ATT_W + ATT_KV_W) // LANES
    for j in range(n_qk):
        x = u_ref[:, j * LANES:(j + 1) * LANES]
        is_q = j < ATT_W // LANES
        gain = qg_ref[...] if is_q else kg_ref[...]
        ms = _dot2_exact_rhs(x * x, ones) * (1.0 / ATT_HEAD_DIM)
        y = x * lax.rsqrt(ms + NORM_EPS) * gain
        y = y * cos + _swap_halves(y, ATT_HEAD_DIM // 4) * sin
        if is_q:
            y = y * (ATT_HEAD_DIM ** -0.5)
        o_ref[:, j * LANES:(j + 1) * LANES] = y.astype(BF16)
    o_ref[:, ATT_W + ATT_KV_W:] = u_ref[:, ATT_W + ATT_KV_W:].astype(BF16)


def _att_prep(u, cos, sin, qn_g, kn_g):
    bsz, t, _ = u.shape
    tm = ROW_TILE
    rep = LANES // ATT_HEAD_DIM
    return pl.pallas_call(
        _att_prep_kernel,
        grid=(bsz, t // tm),
        in_specs=[pl.BlockSpec((None, tm, ATT_COLS), lambda b, i: (b, i, U_ATT // ATT_COLS)),
                  pl.BlockSpec((tm, LANES), lambda b, i: (i, 0)),
                  pl.BlockSpec((tm, LANES), lambda b, i: (i, 0)),
                  pl.BlockSpec((1, LANES), lambda b, i: (0, 0)),
                  pl.BlockSpec((1, LANES), lambda b, i: (0, 0))],
        out_specs=pl.BlockSpec((None, tm, ATT_COLS), lambda b, i: (b, i, 0)),
        out_shape=jax.ShapeDtypeStruct((bsz, t, ATT_COLS), BF16),
        compiler_params=_params("parallel", "parallel"),
        name="att_prep",
    )(u, cos, sin, jnp.tile(qn_g, rep).reshape(1, LANES), jnp.tile(kn_g, rep).reshape(1, LANES))


def _att_kernel(q_ref, k_ref, v_ref, o_ref, *, n_ctx, tq):
    i = pl.program_id(1)
    hd = ATT_HEAD_DIM

    def run(n_keys):
        for g in range(ATT_KV_HEADS):
            q = jnp.concatenate(
                [q_ref[:, (ATT_GROUP * g + h) * hd:(ATT_GROUP * g + h + 1) * hd] for h in range(ATT_GROUP)], axis=0)
            k = k_ref[0:n_keys, g * hd:(g + 1) * hd]
            v = v_ref[0:n_keys, g * hd:(g + 1) * hd]
            s = _dot_nt(q, k)
            m = jnp.max(s, axis=-1, keepdims=True)
            p = jnp.exp(s - m)
            l = jnp.sum(p, axis=-1, keepdims=True)
            o = _dot(p.astype(BF16), v) / l
            for h in range(ATT_GROUP):
                c0 = (ATT_GROUP * g + h) * hd
                o_ref[:, c0:c0 + hd] = o[h * tq:(h + 1) * tq].astype(o_ref.dtype)

    @pl.when(i < n_ctx // tq)
    def _():
        run(n_ctx)

    @pl.when(i >= n_ctx // tq)
    def _():
        run(k_ref.shape[0])


def _attention(qkv, n_ctx):
    bsz, t, _ = qkv.shape
    tq = ATT_Q_TILE
    kcol = ATT_W // ATT_KV_W
    return pl.pallas_call(
        functools.partial(_att_kernel, n_ctx=n_ctx, tq=tq),
        grid=(bsz, t // tq),
        in_specs=[pl.BlockSpec((None, tq, ATT_W), lambda b, i: (b, i, 0)),
                  pl.BlockSpec((None, t, ATT_KV_W), lambda b, i: (b, 0, kcol)),
                  pl.BlockSpec((None, t, ATT_KV_W), lambda b, i: (b, 0, kcol + 1))],
        out_specs=pl.BlockSpec((None, tq, ATT_W), lambda b, i: (b, i, 0)),
        out_shape=jax.ShapeDtypeStruct((bsz, t, ATT_W), BF16),
        compiler_params=_params("parallel", "parallel"),
        name="attention",
    )(qkv, qkv, qkv)


def _ret_kernel(q_ref, k_ref, v_ref, g_ref, cos_ref, sin_ref, lg_ref, gn_ref, o_ref, y_ref, s_ref, *, n_ctx):
    c = RET_CHUNK
    t = q_ref.shape[0]
    n_chunks = t // c
    n_cc = n_ctx // c
    quarter = RET_HEAD_DIM // 4
    scale = RET_HEAD_DIM ** -0.5
    lg_f = lg_ref[0]
    lg_b = lg_ref[1]
    row = lax.broadcasted_iota(jnp.int32, (c, c), 0)
    col = lax.broadcasted_iota(jnp.int32, (c, c), 1)
    rowf = row.astype(F32)
    lag = (row - col).astype(F32)

    def chunk(ci):
        r0 = pl.multiple_of(ci * c, c)
        cos = cos_ref[pl.ds(r0, c), :]
        sin = sin_ref[pl.ds(r0, c), :]
        q = q_ref[pl.ds(r0, c), :]
        k = k_ref[pl.ds(r0, c), :]
        q = q * cos + _swap_halves(q, quarter) * sin
        k = (k * cos + _swap_halves(k, quarter) * sin) * scale
        return r0, q, k, v_ref[pl.ds(r0, c), :]

    def step(q, k, v, d_intra, d_query, d_key, d_chunk):
        s = s_ref[...]
        qb = q.astype(BF16)
        vb = v.astype(BF16)
        scores = _dot_nt(qb, k.astype(BF16)) * d_intra
        y = _dot(scores.astype(BF16), vb) + _dot(qb, s.astype(BF16)) * d_query
        s_ref[...] = d_chunk * s + _dot((k * d_key).T.astype(BF16), vb)
        return y

    d_intra = jnp.where(lag >= 0, jnp.exp(lg_f * jnp.maximum(lag, 0.0)), 0.0)
    d_query = jnp.exp(lg_f * (rowf + 1.0))
    d_key = jnp.exp(lg_f * (c - 1.0 - rowf))
    d_chunk = jnp.exp(lg_f * float(c))
    s_ref[...] = jnp.zeros_like(s_ref)

    def fwd(n, carry):
        r0, q, k, v = chunk(n)
        y_ref[pl.ds(r0, c), :] = step(q, k, v, d_intra, d_query, d_key, d_chunk)
        return carry

    lax.fori_loop(0, n_chunks, fwd, 0)

    d_intra_b = jnp.where(lag <= 0, jnp.exp(lg_b * jnp.maximum(-lag, 0.0)), 0.0)
    d_query_b = jnp.exp(lg_b * (float(c) - rowf))
    d_key_b = jnp.exp(lg_b * rowf)
    d_chunk_b = jnp.exp(lg_b * float(c))
    s_ref[...] = jnp.zeros_like(s_ref)
    gn = gn_ref[...]

    def bwd(n, carry):
        ci = jnp.where(n < n_cc, n_cc - 1 - n, n_chunks - 1 - (n - n_cc))
        r0, q, k, v = chunk(ci)
        y = y_ref[pl.ds(r0, c), :] + step(q, k, v, d_intra_b, d_query_b, d_key_b, d_chunk_b)
        yn = y * lax.rsqrt(jnp.mean(y * y, axis=-1, keepdims=True) + NORM_EPS) * gn
        o_ref[pl.ds(r0, c), :] = (_silu(g_ref[pl.ds(r0, c), :]) * yn).astype(o_ref.dtype)
        return carry

    lax.fori_loop(0, n_chunks, bwd, 0)


def _retention(u, cos, sin, log_gamma, gn_g, n_ctx):
    bsz, t, _ = u.shape
    hd = RET_HEAD_DIM
    base = U_RET // hd
    spec = lambda off: pl.BlockSpec((None, t, hd), lambda b, h: (b, 0, base + off * RET_HEADS + h))
    lg = jnp.broadcast_to(log_gamma[:, :, None, None], (2, RET_HEADS, 1, LANES)).astype(F32)
    return pl.pallas_call(
        functools.partial(_ret_kernel, n_ctx=n_ctx),
        grid=(bsz, RET_HEADS),
        in_specs=[spec(0), spec(1), spec(2), spec(3),
                  pl.BlockSpec((t, hd), lambda b, h: (0, 0)),
                  pl.BlockSpec((t, hd), lambda b, h: (0, 0)),
                  pl.BlockSpec((2, None, 1, LANES), lambda b, h: (0, h, 0, 0)),
                  pl.BlockSpec((1, hd), lambda b, h: (0, h))],
        out_specs=pl.BlockSpec((None, t, hd), lambda b, h: (b, 0, h)),
        out_shape=jax.ShapeDtypeStruct((bsz, t, RET_W), BF16),
        scratch_shapes=[pltpu.VMEM((t, hd), F32), pltpu.VMEM((hd, hd), F32)],
        compiler_params=_params("parallel", "parallel"),
        name="retention",
    )(u, u, u, u, cos, sin, lg, gn_g.reshape(1, RET_W))


def _rwkv_prep_kernel(rk_ref, rk_prev_ref, rk_next_ref, rest_ref, rest_prev_ref, rest_next_ref,
                      mu_rk_ref, mu_rest_ref, w0_ref, w2_ref, a0_ref, a2_ref, g2_ref, kk_ref, ka_ref, rk_gain_ref,
                      rows_out, v_out, gate_out, bonus_out, *, n_tiles):
    i = pl.program_id(1)
    tm = rk_ref.shape[0]
    has_prev = jnp.logical_and(i != 0, i != 1)
    has_next = jnp.logical_and(i != 0, i != n_tiles - 1)

    def shifted(x_ref, prev_ref, next_ref, mu_ref):
        x = x_ref[...]
        rows = lax.broadcasted_iota(jnp.int32, x.shape, 0)
        halo_prev = jnp.where(has_prev, prev_ref[7:8, :], 0.0)
        halo_next = jnp.where(has_next, next_ref[0:1, :], 0.0)
        prev = jnp.where(rows == 0, halo_prev, pltpu.roll(x, 1, 0))
        nxt = jnp.where(rows == tm - 1, halo_next, pltpu.roll(x, tm - 1, 0))
        return x + (prev - x) * mu_ref[0:1, :] + (nxt - x) * mu_ref[1:2, :]

    rk = shifted(rk_ref, rk_prev_ref, rk_next_ref, mu_rk_ref)
    rest = shifted(rest_ref, rest_prev_ref, rest_next_ref, mu_rest_ref)
    w = RWKV_W
    r = rk[:, 0:w]
    k = rk[:, w:2 * w]
    v = rest[:, 0:w]
    xw = rest[:, w:w + RWKV_DECAY_LORA]
    xa = rest[:, w + RWKV_DECAY_LORA:w + RWKV_DECAY_LORA + RWKV_AAA_LORA]
    xg = rest[:, w + RWKV_DECAY_LORA + RWKV_AAA_LORA:]

    ones = _group_ones(w, RWKV_HEAD_DIM)
    kk = k * kk_ref[...]
    kk = kk * lax.rsqrt(jnp.maximum(_dot2_exact_rhs(kk * kk, ones), 1e-12))
    rows_out[ROW_R] = r
    rows_out[ROW_A] = -kk
    v_out[...] = v
    tw = jnp.tanh(xw)
    k_sum = jnp.zeros_like(k)
    for d in range(2):
        w_log = -jax.nn.softplus(-(w0_ref[d:d + 1, :] + _dot3(tw, w2_ref[d]))) - 0.5
        a = jax.nn.sigmoid(a0_ref[d:d + 1, :] + _dot3(xa, a2_ref[d]))
        k_d = k * (1.0 + (a - 1.0) * ka_ref[...])
        rows_out[ROW_W + d] = jnp.exp(-jnp.exp(w_log))
        rows_out[ROW_K + d] = k_d
        rows_out[ROW_B + d] = kk * a
        k_sum = k_sum + k_d
    gate_out[...] = _dot3(jax.nn.sigmoid(xg), g2_ref[...])
    bonus_out[...] = _dot2_exact_rhs(r * k_sum * rk_gain_ref[...], ones) * v


def _rwkv_prep(u, mu, w0, w2, a0, a2, g2, k_k, k_a, r_k):
    bsz, t, _ = u.shape
    tm = ROW_TILE
    n_tiles = t // tm
    w = RWKV_W
    rk_blk = U_RWKV_RK // (2 * w)
    rest_blk = U_RWKV_REST // RWKV_REST_COLS
    sub = tm // 8
    n_sub = t // 8
    prev_idx = lambda b, i: jnp.maximum(i * sub - 1, 0)
    next_idx = lambda b, i: jnp.minimum((i + 1) * sub, n_sub - 1)
    row = lambda a: a.reshape(1, -1)
    const = lambda shape: pl.BlockSpec(shape, lambda b, i: (0,) * len(shape))
    tok = lambda width: pl.BlockSpec((None, tm, width), lambda b, i: (b, i, 0))
    rows_spec = pl.BlockSpec((N_SCAN_ROWS, None, tm, w), lambda b, i: (0, b, i, 0))
    sd = lambda *lead: jax.ShapeDtypeStruct((*lead, bsz, t, w), F32)
    return pl.pallas_call(
        functools.partial(_rwkv_prep_kernel, n_tiles=n_tiles),
        grid=(bsz, n_tiles),
        in_specs=[pl.BlockSpec((None, tm, 2 * w), lambda b, i: (b, i, rk_blk)),
                  pl.BlockSpec((None, 8, 2 * w), lambda b, i: (b, prev_idx(b, i), rk_blk)),
                  pl.BlockSpec((None, 8, 2 * w), lambda b, i: (b, next_idx(b, i), rk_blk)),
                  pl.BlockSpec((None, tm, RWKV_REST_COLS), lambda b, i: (b, i, rest_blk)),
                  pl.BlockSpec((None, 8, RWKV_REST_COLS), lambda b, i: (b, prev_idx(b, i), rest_blk)),
                  pl.BlockSpec((None, 8, RWKV_REST_COLS), lambda b, i: (b, next_idx(b, i), rest_blk)),
                  const((2, 2 * w)), const((2, RWKV_REST_COLS)),
                  const((2, w)), const((2, RWKV_DECAY_LORA, w)), const((2, w)), const((2, RWKV_AAA_LORA, w)),
                  const((RWKV_GATE_LORA, w)), const((1, w)), const((1, w)), const((1, w))],
        out_specs=[rows_spec, tok(w), tok(w), tok(w)],
        out_shape=[sd(N_SCAN_ROWS), sd(), sd(), sd()],
        compiler_params=_params("parallel", "parallel"),
        name="rwkv_prep",
    )(u, u, u, u, u, u, mu[:, :2 * w], mu[:, 2 * w:], w0, w2, a0, a2, g2, row(k_k), row(k_a), row(r_k))


def _transpose_tokens(z_ref, scr, bsz):
    w = RWKV_W
    for b in range(bsz):
        scr[b * w:(b + 1) * w, :] = z_ref[b].T
    if bsz * w < scr.shape[0]:
        scr[bsz * w:, :] = jnp.zeros((scr.shape[0] - bsz * w, scr.shape[1]), F32)


def _layout_rows_kernel(z_ref, o_ref, scr, *, bsz):
    n = RWKV_HEAD_DIM
    _transpose_tokens(z_ref, scr, bsz)
    for j in range(n):
        x = scr[pl.ds(j, LANES // 2, stride=n), :]
        o_ref[j] = jnp.concatenate([x, x], axis=0).T


def _layout_v_kernel(z_ref, o_ref, scr, *, bsz):
    n = RWKV_HEAD_DIM
    ts = z_ref.shape[1]
    _transpose_tokens(z_ref, scr, bsz)
    for i in range(n // 2):
        x0 = scr[pl.ds(i, LANES // 2, stride=n), :]
        x1 = scr[pl.ds(n // 2 + i, LANES // 2, stride=n), :]
        o_ref[pl.ds(i, ts, stride=n // 2), :] = jnp.concatenate([x0, x1], axis=0).T


def _scan_layout(rows, v):
    g, bsz, t, w = rows.shape
    n = RWKV_HEAD_DIM
    ts = LANES
    scr = pltpu.VMEM((LANES // 2 * n, ts), F32)
    rows_l = pl.pallas_call(
        functools.partial(_layout_rows_kernel, bsz=bsz),
        grid=(g, t // ts),
        in_specs=[pl.BlockSpec((None, bsz, ts, w), lambda k, i: (k, 0, i, 0))],
        out_specs=pl.BlockSpec((None, n, ts, LANES), lambda k, i: (k, 0, i, 0)),
        out_shape=jax.ShapeDtypeStruct((g, n, t, LANES), F32),
        scratch_shapes=[scr],
        compiler_params=_params("parallel", "parallel"),
        name="rwkv_layout_rows",
    )(rows)
    v_l = pl.pallas_call(
        functools.partial(_layout_v_kernel, bsz=bsz),
        grid=(t // ts,),
        in_specs=[pl.BlockSpec((bsz, ts, w), lambda i: (0, i, 0))],
        out_specs=pl.BlockSpec((ts * n // 2, LANES), lambda i: (i, 0)),
        out_shape=jax.ShapeDtypeStruct((t * n // 2, LANES), F32),
        scratch_shapes=[scr],
        compiler_params=_params("parallel"),
        name="rwkv_layout_v",
    )(v)
    return rows_l, v_l


def _rwkv_scan_kernel(r_ref, a_ref, w_ref, k_ref, b_ref, v_ref, y_ref, s_ref, sa_ref):
    n = RWKV_HEAD_DIM
    half = n // 2
    ts = r_ref.shape[1]
    fwd = pl.program_id(0) == 0

    @pl.when(pl.program_id(1) == 0)
    def _():
        s_ref[...] = jnp.zeros_like(s_ref)

    t_first = jnp.where(fwd, 0, ts - 1)
    acc = jnp.zeros((half, LANES), F32)
    for j in range(n):
        acc = acc + s_ref[j] * a_ref[j, pl.ds(t_first, 1), :]
    sa_ref[...] = acc

    def step(m, carry):
        t = jnp.where(fwd, m, ts - 1 - m)
        tn = jnp.clip(jnp.where(fwd, t + 1, t - 1), 0, ts - 1)
        sa = sa_ref[...]
        v = v_ref[t]
        y = jnp.zeros((half, LANES), F32)
        sa_next = jnp.zeros((half, LANES), F32)
        for j in range(n):
            s = (s_ref[j] * w_ref[j, pl.ds(t, 1), :] + sa * b_ref[j, pl.ds(t, 1), :]) + v * k_ref[j, pl.ds(t, 1), :]
            s_ref[j] = s
            y = y + s * r_ref[j, pl.ds(t, 1), :]
            sa_next = sa_next + s * a_ref[j, pl.ds(tn, 1), :]
        y_ref[t] = y
        sa_ref[...] = sa_next
        return carry

    lax.fori_loop(0, ts, step, 0)


def _rwkv_scan(rows, v, n_ctx):
    _, n, t, lanes = rows.shape
    ts = SCAN_STEPS
    nb = t // ts
    ncb = n_ctx // ts

    def blk(d, s):
        back = jnp.where(s < ncb, ncb - 1 - s, nb - 1 - (s - ncb))
        return jnp.where(d == 0, s, back)

    shared = lambda kind: pl.BlockSpec((None, n, ts, lanes), lambda d, s: (kind, 0, blk(d, s), 0))
    per_dir = lambda kind: pl.BlockSpec((None, n, ts, lanes), lambda d, s: (kind + d, 0, blk(d, s), 0))
    return pl.pallas_call(
        _rwkv_scan_kernel,
        grid=(2, nb),
        in_specs=[shared(ROW_R), shared(ROW_A), per_dir(ROW_W), per_dir(ROW_K), per_dir(ROW_B),
                  pl.BlockSpec((ts, n // 2, lanes), lambda d, s: (blk(d, s), 0, 0))],
        out_specs=pl.BlockSpec((None, ts, n // 2, lanes), lambda d, s: (d, blk(d, s), 0, 0)),
        out_shape=jax.ShapeDtypeStruct((2, t, n // 2, lanes), F32),
        scratch_shapes=[pltpu.VMEM((n, n // 2, lanes), F32), pltpu.VMEM((n // 2, lanes), F32)],
        compiler_params=_params("arbitrary", "arbitrary"),
        name="rwkv_scan",
    )(rows, rows, rows, rows, rows, v)


def _rwkv_readout_kernel(yf_ref, yb_ref, bonus_ref, gate_ref, g_ref, b_ref, o_ref, scr, *, bsz):
    n = RWKV_HEAD_DIM
    w = RWKV_W
    ts = o_ref.shape[1]
    for i in range(n // 2):
        rows = pl.ds(i, ts, stride=n // 2)
        yt = (yf_ref[rows, :] + yb_ref[rows, :]).T
        scr[pl.ds(i, LANES // 2, stride=n), :] = yt[:LANES // 2]
        scr[pl.ds(n // 2 + i, LANES // 2, stride=n), :] = yt[LANES // 2:]
    ones = _group_ones(w, n)
    inv = 1.0 / n
    for b in range(bsz):
        y = scr[b * w:(b + 1) * w, :].T
        mean = _dot2_exact_rhs(y, ones) * inv
        yc = y - mean
        var = _dot2_exact_rhs(yc * yc, ones) * inv
        yn = yc * lax.rsqrt(var + RWKV_GN_EPS) * g_ref[...] + b_ref[...]
        o_ref[b] = ((yn + bonus_ref[b]) * gate_ref[b]).astype(o_ref.dtype)


def _rwkv_readout(y, bonus, gate, ln_g, ln_b):
    bsz, t, w = bonus.shape
    n = RWKV_HEAD_DIM
    ts = LANES
    tok = pl.BlockSpec((bsz, ts, w), lambda i: (0, i, 0))
    vec = pl.BlockSpec((1, w), lambda i: (0, 0))
    return pl.pallas_call(
        functools.partial(_rwkv_readout_kernel, bsz=bsz),
        grid=(t // ts,),
        in_specs=[pl.BlockSpec((None, ts * n // 2, LANES), lambda i: (0, i, 0)),
                  pl.BlockSpec((None, ts * n // 2, LANES), lambda i: (1, i, 0)), tok, tok, vec, vec],
        out_specs=tok,
        out_shape=jax.ShapeDtypeStruct((bsz, t, w), BF16),
        scratch_shapes=[pltpu.VMEM((LANES // 2 * n, ts), F32)],
        compiler_params=_params("parallel"),
        name="rwkv_readout",
    )(y, y, bonus, gate, ln_g.reshape(1, w), ln_b.reshape(1, w))


def _merge_kernel(ya_ref, yr_ref, yw_ref, g0_ref, g1_ref, g2_ref, x_ref, gate1_ref, a2_ref, b2_ref,
                  wb_ref, wo_ref, wr_ref, br_ref,
                  x_out, h_out, ids_out, wts_out, cnt_out):
    first = jnp.logical_and(pl.program_id(0) == 0, pl.program_id(1) == 0)

    @pl.when(first)
    def _():
        cnt_out[...] = jnp.zeros_like(cnt_out)

    merged = (jax.nn.sigmoid(g0_ref[...]) * _dot(ya_ref[...], wb_ref[0])
              + jax.nn.sigmoid(g1_ref[...]) * _dot(yr_ref[...], wb_ref[1])
              + jax.nn.sigmoid(g2_ref[...]) * _dot(yw_ref[...], wb_ref[2]))
    x = x_ref[...] + gate1_ref[...] * _dot(merged.astype(BF16), wo_ref[...])
    x_out[...] = x
    h = x * lax.rsqrt(jnp.mean(x * x, axis=-1, keepdims=True) + NORM_EPS) * a2_ref[...] + b2_ref[...]
    h_out[...] = h.astype(BF16)

    tm = x.shape[0]
    logits = _dot3(h, wr_ref[...]) + br_ref[...]
    lane = lax.broadcasted_iota(jnp.int32, (tm, LANES), 1)
    lane_f = lane.astype(F32)
    neg = -jnp.inf
    big = float(LANES)
    first = lambda hit: jnp.min(jnp.where(hit, lane_f, big), axis=-1, keepdims=True).astype(jnp.int32)
    is_grp = jnp.logical_and(lane >= MOE_EXPERTS, lane < MOE_EXPERTS + MOE_GROUPS)
    gl = jnp.where(is_grp, logits, neg)
    gmax = jnp.max(gl, axis=-1, keepdims=True)
    gidx = first(gl == gmax) - MOE_EXPERTS
    p_grp = 1.0 / jnp.sum(jnp.where(is_grp, jnp.exp(gl - gmax), 0.0), axis=-1, keepdims=True)
    in_grp = jnp.logical_and(lane < MOE_EXPERTS, lane // MOE_EXPERTS_PER_GROUP == gidx)
    el = jnp.where(in_grp, logits, neg)
    v1 = jnp.max(el, axis=-1, keepdims=True)
    i1 = first(el == v1)
    el2 = jnp.where(lane == i1, neg, el)
    v2 = jnp.max(el2, axis=-1, keepdims=True)
    i2 = first(el2 == v2)
    e2 = jnp.exp(v2 - v1)
    w1 = p_grp / (1.0 + e2)
    w2 = p_grp * e2 / (1.0 + e2)
    wts_out[...] = jnp.where(lane == 0, w1, jnp.where(lane == 1, w2, 0.0))

    onehot = jnp.where(jnp.logical_or(lane == i1, lane == i2), 1.0, 0.0)
    rr = lax.broadcasted_iota(jnp.int32, (tm, tm), 0)
    cc = lax.broadcasted_iota(jnp.int32, (tm, tm), 1)
    below = jnp.where(cc < rr, 1.0, 0.0).astype(BF16)
    before = _dot(below, onehot.astype(BF16)) + cnt_out[0:1, :]
    rank1 = jnp.sum(jnp.where(lane == i1, before, 0.0), axis=-1, keepdims=True).astype(jnp.int32)
    rank2 = jnp.sum(jnp.where(lane == i2, before, 0.0), axis=-1, keepdims=True).astype(jnp.int32)
    ids_out[...] = jnp.where(lane == 0, i1, jnp.where(lane == 1, i2, jnp.where(lane == 2, rank1,
                                                                                  jnp.where(lane == 3, rank2, 0))))
    cnt_out[...] = cnt_out[...] + jnp.sum(onehot, axis=0, keepdims=True)


def _merge(ya, yr, yw, u, x, gate1, a2, b2, w_branch, w_out, w_router, b_router):
    bsz, t, d = x.shape
    tm = ROW_TILE
    sel = lambda b, i: (2 * b + jnp.minimum(i, 1), 0, 0)
    tok = lambda width, blk=0: pl.BlockSpec((None, tm, width), lambda b, i: (b, i, blk))
    const = lambda shape: pl.BlockSpec(shape, lambda b, i: (0,) * len(shape))
    mod = pl.BlockSpec((None, 1, d), sel)
    return pl.pallas_call(
        _merge_kernel,
        grid=(bsz, t // tm),
        in_specs=[tok(BRANCH_W), tok(BRANCH_W), tok(BRANCH_W), tok(d, 0), tok(d, 1), tok(d, 2), tok(d),
                  mod, mod, mod,
                  const((N_BRANCH, BRANCH_W, d)), const((d, d)), const((d, LANES)), const((1, LANES))],
        out_specs=[tok(d), tok(d), tok(LANES), tok(LANES), const((8, LANES))],
        out_shape=[jax.ShapeDtypeStruct((bsz, t, d), F32), jax.ShapeDtypeStruct((bsz, t, d), BF16),
                   jax.ShapeDtypeStruct((bsz, t, LANES), jnp.int32), jax.ShapeDtypeStruct((bsz, t, LANES), F32),
                   jax.ShapeDtypeStruct((8, LANES), F32)],
        compiler_params=_params("arbitrary", "arbitrary"),
        name="merge_router",
    )(ya, yr, yw, u, u, u, x, gate1, a2, b2, w_branch, w_out, w_router, b_router)


def _moe_kernel(be_ref, na_ref, x_ref, wg_ref, wu_ref, wd_ref, o_ref):
    i = pl.program_id(0)

    @pl.when(i < na_ref[0])
    def _():
        x = x_ref[...]
        act = _silu(_dot(x, wg_ref[...])) * _dot(x, wu_ref[...])
        o_ref[...] = _dot(act.astype(BF16), wd_ref[...])

    @pl.when(i >= na_ref[0])
    def _():
        o_ref[...] = jnp.zeros_like(o_ref)


def _moe_experts(buf, block_expert, n_active, w_gate, w_up, w_down):
    rows, d = buf.shape
    hid = w_gate.shape[-1]
    grid_spec = pltpu.PrefetchScalarGridSpec(
        num_scalar_prefetch=2,
        grid=(rows // MOE_BLOCK,),
        in_specs=[pl.BlockSpec((MOE_BLOCK, d), lambda i, be, na: (i, 0)),
                  pl.BlockSpec((None, d, hid), lambda i, be, na: (be[i], 0, 0)),
                  pl.BlockSpec((None, d, hid), lambda i, be, na: (be[i], 0, 0)),
                  pl.BlockSpec((None, hid, d), lambda i, be, na: (be[i], 0, 0))],
        out_specs=pl.BlockSpec((MOE_BLOCK, d), lambda i, be, na: (i, 0)))
    return pl.pallas_call(
        _moe_kernel,
        grid_spec=grid_spec,
        out_shape=jax.ShapeDtypeStruct((rows, d), F32),
        compiler_params=_params("arbitrary"),
        name="moe_experts",
    )(block_expert, n_active, buf, w_gate, w_up, w_down)


def _combine_kernel(x_ref, y0_ref, y1_ref, w_ref, g_ref, o_ref):
    w = w_ref[...]
    y = y0_ref[...] * w[:, 0:1] + y1_ref[...] * w[:, 1:2]
    o_ref[...] = x_ref[...] + g_ref[...] * y


def _combine(x, y_pairs, wts, gate2):
    bsz, t, d = x.shape
    tm = ROW_TILE
    sel = lambda b, i: (2 * b + jnp.minimum(i, 1), 0, 0)
    tok = lambda width, blk=0: pl.BlockSpec((None, tm, width), lambda b, i: (b, i, blk))
    return pl.pallas_call(
        _combine_kernel,
        grid=(bsz, t // tm),
        in_specs=[tok(d), tok(d, 0), tok(d, 1), tok(LANES), pl.BlockSpec((None, 1, d), sel)],
        out_specs=tok(d),
        out_shape=jax.ShapeDtypeStruct((bsz, t, d), F32),
        compiler_params=_params("parallel", "parallel"),
        name="moe_combine",
    )(x, y_pairs, y_pairs, wts, gate2)


def _moe(h, ids, wts, counts, w_gate, w_up, w_down):
    bsz, t, d = h.shape
    n_tok = bsz * t
    n_pair = 2 * n_tok
    n_blocks = -(-n_pair // MOE_BLOCK) + MOE_EXPERTS
    counts = counts[0, :MOE_EXPERTS].astype(jnp.int32)
    padded = (counts + MOE_BLOCK - 1) // MOE_BLOCK * MOE_BLOCK
    pad_end = jnp.cumsum(padded)
    pad_start = pad_end - padded
    expert = ids[..., 0:2].reshape(n_pair)
    rank = ids[..., 2:4].reshape(n_pair)
    dest = pad_start[expert] + rank
    token = jnp.arange(n_pair, dtype=jnp.int32) // 2
    src = jnp.zeros((n_blocks * MOE_BLOCK,), jnp.int32).at[dest].set(token)
    block_expert = jnp.minimum(jnp.searchsorted(pad_end, jnp.arange(n_blocks) * MOE_BLOCK, side='right'),
                               MOE_EXPERTS - 1).astype(jnp.int32)
    n_active = (pad_end[-1:] // MOE_BLOCK).astype(jnp.int32)
    buf = jnp.take(h.reshape(n_tok, d), src, axis=0)
    yb = _moe_experts(buf, block_expert, n_active, w_gate, w_up, w_down)
    return jnp.take(yb, dest, axis=0).reshape(bsz, t, 2 * d)


def _rope_tables(n_ctx, n_lat, head_dim):
    rows = n_lat // GRID_W
    row = jnp.broadcast_to(jnp.arange(rows, dtype=F32)[:, None], (rows, GRID_W)).reshape(-1)
    col = jnp.broadcast_to(jnp.arange(GRID_W, dtype=F32)[None, :], (rows, GRID_W)).reshape(-1)
    quarter = head_dim // 4
    inv_freq = ROPE_THETA ** (-jnp.arange(quarter, dtype=F32) / quarter)
    ang = jnp.stack([row[:, None] * inv_freq, col[:, None] * inv_freq], axis=1)
    cos, sin = jnp.cos(ang), jnp.sin(ang)
    cos_t = jnp.stack([cos, cos], axis=2).reshape(n_lat, head_dim)
    sin_t = jnp.stack([-sin, sin], axis=2).reshape(n_lat, head_dim)
    cos_t = jnp.concatenate([jnp.ones((n_ctx, head_dim), F32), cos_t], axis=0)
    sin_t = jnp.concatenate([jnp.zeros((n_ctx, head_dim), F32), sin_t], axis=0)
    rep = LANES // head_dim
    return jnp.tile(cos_t, (1, rep)), jnp.tile(sin_t, (1, rep))


def kernel(x, c, ctx, c_ctx, ada_w, ada_b, norm1_g, norm2_g, w_in, att_qn_g, att_kn_g, ret_decay_logit, ret_gn_g, rwkv_mu, rwkv_w0, rwkv_w2, rwkv_a0, rwkv_a2, rwkv_g2, rwkv_k_k, rwkv_k_a, rwkv_r_k, rwkv_ln_g, rwkv_ln_b, w_branch, w_out, router_grp_w, router_grp_b, router_exp_w, router_exp_b, moe_w_gate, moe_w_up, moe_w_down):
    bsz, n_lat, d = x.shape
    n_ctx = ctx.shape[1]
    depth = ada_w.shape[0]
    assert d == D_MODEL and n_ctx == ROW_TILE and n_lat % ROW_TILE == 0 and n_lat % GRID_W == 0
    assert 2 * bsz * RWKV_HEADS <= LANES
    t_all = n_ctx + n_lat
    assert t_all % LANES == 0 and n_ctx % SCAN_STEPS == 0

    att_cos, att_sin = _rope_tables(n_ctx, n_lat, ATT_HEAD_DIM)
    ret_cos, ret_sin = _rope_tables(n_ctx, n_lat, RET_HEAD_DIM)

    rows = -(-(bsz + 1) // 8) * 8
    cvec = jnp.zeros((rows, d), F32).at[:bsz].set(c).at[bsz].set(c_ctx)
    mods = _modulation(cvec, ada_w, ada_b)
    perm = _column_permutation()

    xs = jnp.concatenate([ctx, x], axis=1)
    for layer in range(depth):
        m = mods[layer].reshape(rows, 6, d)
        pick = lambda j: jnp.stack([jnp.broadcast_to(m[bsz, j], (bsz, d)), m[:bsz, j]], axis=1).reshape(2 * bsz, 1, d)
        sh1, sc1, g1, sh2, sc2, g2 = (pick(j) for j in range(6))
        w_l = w_in[layer][:, perm].astype(BF16)
        u = _in_proj(xs, norm1_g[layer] * (1.0 + sc1), sh1, w_l)

        qkv = _att_prep(u, att_cos, att_sin, att_qn_g[layer], att_kn_g[layer])
        ya = _attention(qkv, n_ctx)

        log_gamma = jax.nn.log_sigmoid(ret_decay_logit[layer].astype(F32))
        yr = _retention(u, ret_cos, ret_sin, log_gamma, ret_gn_g[layer], n_ctx)

        rows_t, v_t, gate, bonus = _rwkv_prep(
            u, rwkv_mu[layer], rwkv_w0[layer], rwkv_w2[layer], rwkv_a0[layer], rwkv_a2[layer], rwkv_g2[layer],
            rwkv_k_k[layer], rwkv_k_a[layer], rwkv_r_k[layer].reshape(-1))
        rows_s, v_s = _scan_layout(rows_t, v_t)
        y_scan = _rwkv_scan(rows_s, v_s.reshape(t_all, RWKV_HEAD_DIM // 2, LANES), n_ctx)
        yw = _rwkv_readout(y_scan.reshape(2, t_all * RWKV_HEAD_DIM // 2, LANES), bonus, gate,
                           rwkv_ln_g[layer], rwkv_ln_b[layer])

        w_router = jnp.zeros((d, LANES), F32).at[:, :MOE_EXPERTS].set(router_exp_w[layer]).at[
            :, MOE_EXPERTS:MOE_EXPERTS + MOE_GROUPS].set(router_grp_w[layer])
        b_router = jnp.zeros((1, LANES), F32).at[0, :MOE_EXPERTS].set(router_exp_b[layer]).at[
            0, MOE_EXPERTS:MOE_EXPERTS + MOE_GROUPS].set(router_grp_b[layer])
        xs, h2, ids, wts, counts = _merge(
            ya, yr, yw, u, xs, g1, norm2_g[layer] * (1.0 + sc2), sh2,
            w_branch[layer].astype(BF16), w_out[layer].astype(BF16), w_router, b_router)

        y_pairs = _moe(h2, ids, wts, counts, moe_w_gate[layer].astype(BF16), moe_w_up[layer].astype(BF16),
                       moe_w_down[layer].astype(BF16))
        xs = _combine(xs, y_pairs, wts, g2)
    return xs[:, n_ctx:]
```

```python
import functools

import jax
import jax.numpy as jnp
from jax import lax
from jax.experimental import pallas as pl
from jax.experimental.pallas import tpu as pltpu

F32 = jnp.float32
BF16 = jnp.bfloat16

D_MODEL = 1024
GRID_W = 64
NORM_EPS = 1e-6
ROPE_THETA = 10000.0

ATT_HEADS = 8
ATT_KV_HEADS = 2
ATT_HEAD_DIM = 64
ATT_GROUP = ATT_HEADS // ATT_KV_HEADS
ATT_W = ATT_HEADS * ATT_HEAD_DIM
ATT_KV_W = ATT_KV_HEADS * ATT_HEAD_DIM

RET_HEADS = 4
RET_HEAD_DIM = 128
RET_CHUNK = 128
RET_W = RET_HEADS * RET_HEAD_DIM

RWKV_HEADS = 8
RWKV_HEAD_DIM = 64
RWKV_W = RWKV_HEADS * RWKV_HEAD_DIM
RWKV_DECAY_LORA = 64
RWKV_AAA_LORA = 64
RWKV_GATE_LORA = 128
RWKV_GN_EPS = 64e-5
RWKV_COLS = 3 * RWKV_W + RWKV_DECAY_LORA + RWKV_AAA_LORA + RWKV_GATE_LORA

N_BRANCH = 3
BRANCH_W = 512
IN_COLS = ATT_W + 2 * ATT_KV_W + 4 * RET_W + RWKV_COLS + N_BRANCH * D_MODEL

MOE_GROUPS = 4
MOE_EXPERTS_PER_GROUP = 8
MOE_EXPERTS = MOE_GROUPS * MOE_EXPERTS_PER_GROUP
MOE_HIDDEN = 512
MOE_BLOCK = 256

LANES = 128
ROW_TILE = 256
ATT_Q_TILE = 128
ATT_KEY_TILE = 256
ATT_Q_SCALE = ATT_HEAD_DIM ** -0.5 * 1.4426950408889634
SCAN_STEPS = 32
VMEM_LIMIT = 56 * 1024 * 1024

ROW_R, ROW_A, ROW_W, ROW_K, ROW_B, N_SCAN_ROWS = 0, 1, 2, 4, 6, 8

U_GATE = 0
U_RWKV_RK = U_GATE + N_BRANCH * D_MODEL
U_RET = U_RWKV_RK + 2 * RWKV_W
U_ATT = U_RET + 4 * RET_W
U_RWKV_REST = U_ATT + ATT_W + 2 * ATT_KV_W
ATT_COLS = ATT_W + 2 * ATT_KV_W
RWKV_REST_COLS = RWKV_COLS - 2 * RWKV_W


def _column_permutation():
    o_att = 0
    o_ret = ATT_COLS
    o_rwkv = o_ret + 4 * RET_W
    o_gate = o_rwkv + RWKV_COLS
    parts = [jnp.arange(o_gate, o_gate + N_BRANCH * D_MODEL),
             jnp.arange(o_rwkv, o_rwkv + 2 * RWKV_W),
             jnp.arange(o_ret, o_ret + 4 * RET_W),
             jnp.arange(o_att, o_att + ATT_COLS),
             jnp.arange(o_rwkv + 2 * RWKV_W, o_rwkv + RWKV_COLS)]
    return jnp.concatenate(parts)


def _params(*sem):
    return pltpu.CompilerParams(dimension_semantics=sem, vmem_limit_bytes=VMEM_LIMIT)


def _dot(a, b):
    return jnp.dot(a, b, preferred_element_type=F32)


def _dot_nt(a, b):
    return lax.dot_general(a, b, (((1,), (1,)), ((), ())), preferred_element_type=F32)


def _split(a):
    hi = a.astype(BF16)
    lo = (a - hi.astype(F32)).astype(BF16)
    return hi, lo


def _dot3(a, b):
    ah, al = _split(a)
    bh, bl = _split(b)
    return _dot(ah, bh) + (_dot(al, bh) + _dot(ah, bl))


def _dot2_exact_rhs(a, b_bf16):
    ah, al = _split(a)
    return _dot(ah, b_bf16) + _dot(al, b_bf16)


def _group_ones(width, group):
    r = lax.broadcasted_iota(jnp.int32, (width, width), 0) // group
    c = lax.broadcasted_iota(jnp.int32, (width, width), 1) // group
    return jnp.where(r == c, 1.0, 0.0).astype(BF16)


def _silu(x):
    return x * jax.nn.sigmoid(x)


def _swap_halves(x, quarter):
    n = x.shape[-1]
    lane = lax.broadcasted_iota(jnp.int32, x.shape, x.ndim - 1)
    up = pltpu.roll(x, n - quarter, x.ndim - 1)
    down = pltpu.roll(x, quarter, x.ndim - 1)
    return jnp.where(lane % (2 * quarter) < quarter, up, down)


def _mod_kernel(c_ref, w_ref, b_ref, o_ref):
    o_ref[...] = _dot3(_silu(c_ref[...]), w_ref[...]) + b_ref[...]


def _modulation(cvec, ada_w, ada_b):
    depth, d, cols = ada_w.shape
    rows = cvec.shape[0]
    tn = 1536
    return pl.pallas_call(
        _mod_kernel,
        grid=(depth, cols // tn),
        in_specs=[pl.BlockSpec((rows, d), lambda l, j: (0, 0)),
                  pl.BlockSpec((None, d, tn), lambda l, j: (l, 0, j)),
                  pl.BlockSpec((None, 1, tn), lambda l, j: (l, 0, j))],
        out_specs=pl.BlockSpec((None, rows, tn), lambda l, j: (l, 0, j)),
        out_shape=jax.ShapeDtypeStruct((depth, rows, cols), F32),
        compiler_params=_params("parallel", "parallel"),
        name="modulation",
    )(cvec, ada_w, ada_b.reshape(depth, 1, cols))


def _in_proj_kernel(x_ref, a_ref, b_ref, w_ref, o_ref):
    x = x_ref[...]
    ms = jnp.mean(x * x, axis=-1, keepdims=True)
    h = x * lax.rsqrt(ms + NORM_EPS) * a_ref[...] + b_ref[...]
    o_ref[...] = _dot(h.astype(BF16), w_ref[...])


def _in_proj(x, mod_a, mod_b, w):
    bsz, t, d = x.shape
    cols = w.shape[1]
    tm, tn = ROW_TILE, cols // 2
    sel = lambda j, b, i: (2 * b + jnp.minimum(i, 1), 0, 0)
    return pl.pallas_call(
        _in_proj_kernel,
        grid=(cols // tn, bsz, t // tm),
        in_specs=[pl.BlockSpec((None, tm, d), lambda j, b, i: (b, i, 0)),
                  pl.BlockSpec((None, 1, d), sel),
                  pl.BlockSpec((None, 1, d), sel),
                  pl.BlockSpec((d, tn), lambda j, b, i: (0, j))],
        out_specs=pl.BlockSpec((None, tm, tn), lambda j, b, i: (b, i, j)),
        out_shape=jax.ShapeDtypeStruct((bsz, t, cols), F32),
        compiler_params=_params("parallel", "parallel", "parallel"),
        name="in_proj",
    )(x, mod_a, mod_b, w)


def _att_prep_kernel(u_ref, cos_ref, sin_ref, qg_ref, kg_ref, q_out, kv_out):
    hd = ATT_HEAD_DIM
    ones = _group_ones(LANES, hd)
    cos = cos_ref[...]
    sin = sin_ref[...]
    lane = lax.broadcasted_iota(jnp.int32, cos.shape, 1)
    low = lane < hd

    def two_heads(y):
        return jnp.where(low, y, 0.0), jnp.where(low, pltpu.roll(y, hd, 1), 0.0)

    n_q = ATT_W // LANES
    for j in range(n_q + 1):
        x = u_ref[:, j * LANES:(j + 1) * LANES]
        is_q = j < n_q
        gain = qg_ref[...] if is_q else kg_ref[...]
        ms = _dot2_exact_rhs(x * x, ones) * (1.0 / hd)
        y = x * lax.rsqrt(ms + NORM_EPS) * gain
        y = y * cos + _swap_halves(y, hd // 4) * sin
        if is_q:
            y = y * ATT_Q_SCALE
        out, base = (q_out, 2 * j) if is_q else (kv_out, 0)
        for h, yh in enumerate(two_heads(y)):
            out[:, (base + h) * LANES:(base + h + 1) * LANES] = yh.astype(BF16)
    v = u_ref[:, ATT_W + ATT_KV_W:]
    for h, vh in enumerate(two_heads(v)):
        kv_out[:, (2 + h) * LANES:(3 + h) * LANES] = jnp.where(lane == hd, 1.0, vh).astype(BF16)


def _att_prep(u, cos, sin, qn_g, kn_g):
    bsz, t, _ = u.shape
    tm = ROW_TILE
    rep = LANES // ATT_HEAD_DIM
    return pl.pallas_call(
        _att_prep_kernel,
        grid=(bsz, t // tm),
        in_specs=[pl.BlockSpec((None, tm, ATT_COLS), lambda b, i: (b, i, U_ATT // ATT_COLS)),
                  pl.BlockSpec((tm, LANES), lambda b, i: (i, 0)),
                  pl.BlockSpec((tm, LANES), lambda b, i: (i, 0)),
                  pl.BlockSpec((1, LANES), lambda b, i: (0, 0)),
                  pl.BlockSpec((1, LANES), lambda b, i: (0, 0))],
        out_specs=[pl.BlockSpec((None, tm, ATT_HEADS * LANES), lambda b, i: (b, i, 0)),
                   pl.BlockSpec((None, tm, 2 * ATT_KV_HEADS * LANES), lambda b, i: (b, i, 0))],
        out_shape=[jax.ShapeDtypeStruct((bsz, t, ATT_HEADS * LANES), BF16),
                   jax.ShapeDtypeStruct((bsz, t, 2 * ATT_KV_HEADS * LANES), BF16)],
        compiler_params=_params("parallel", "parallel"),
        name="att_prep",
    )(u, cos, sin, jnp.tile(qn_g, rep).reshape(1, LANES), jnp.tile(kn_g, rep).reshape(1, LANES))


def _att_kernel(q_ref, kv_ref, o_ref, *, n_ctx, tq):
    i = pl.program_id(1)
    hd = ATT_HEAD_DIM
    k_ref = v_ref = kv_ref

    def run(n_keys):
        tk = ATT_KEY_TILE
        for g in range(ATT_KV_HEADS):
            q = jnp.concatenate(
                [q_ref[:, (ATT_GROUP * g + h) * LANES:(ATT_GROUP * g + h + 1) * LANES] for h in range(ATT_GROUP)],
                axis=0)
            scores = lambda c: _dot_nt(q, k_ref[c * tk:(c + 1) * tk, g * LANES:(g + 1) * LANES])
            m = jnp.full((ATT_GROUP * tq, LANES), -jnp.inf, F32)
            for c in range(n_keys // tk):
                s = scores(c)
                for part in range(tk // LANES):
                    m = jnp.maximum(m, s[:, part * LANES:(part + 1) * LANES])
            m = jnp.max(m, axis=-1, keepdims=True)
            acc = jnp.zeros((ATT_GROUP * tq, LANES), F32)
            for c in range(n_keys // tk):
                p = jnp.exp2(scores(c) - m).astype(BF16)
                acc = acc + _dot(p, v_ref[c * tk:(c + 1) * tk, (ATT_KV_HEADS + g) * LANES:(ATT_KV_HEADS + g + 1) * LANES])
            o = acc[:, :hd] / acc[:, hd:hd + 1]
            for h in range(ATT_GROUP):
                c0 = (ATT_GROUP * g + h) * hd
                o_ref[:, c0:c0 + hd] = o[h * tq:(h + 1) * tq].astype(o_ref.dtype)

    @pl.when(i < n_ctx // tq)
    def _():
        run(n_ctx)

    @pl.when(i >= n_ctx // tq)
    def _():
        run(k_ref.shape[0])


def _attention(q, kv, n_ctx):
    bsz, t, _ = q.shape
    tq = ATT_Q_TILE
    kv_w = kv.shape[-1]
    assert n_ctx % ATT_KEY_TILE == 0 and t % ATT_KEY_TILE == 0
    return pl.pallas_call(
        functools.partial(_att_kernel, n_ctx=n_ctx, tq=tq),
        grid=(bsz, t // tq),
        in_specs=[pl.BlockSpec((None, tq, ATT_HEADS * LANES), lambda b, i: (b, i, 0)),
                  pl.BlockSpec((None, t, kv_w), lambda b, i: (b, 0, 0))],
        out_specs=pl.BlockSpec((None, tq, ATT_W), lambda b, i: (b, i, 0)),
        out_shape=jax.ShapeDtypeStruct((bsz, t, ATT_W), BF16),
        compiler_params=_params("parallel", "parallel"),
        name="attention",
    )(q, kv)


def _ret_kernel(q_ref, k_ref, v_ref, g_ref, cos_ref, sin_ref, lg_ref, gn_ref, o_ref,
                qs_ref, ks_ref, kvf_ref, kvb_ref, sf_ref, sb_ref, *, n_ctx):
    c = RET_CHUNK
    t = q_ref.shape[0]
    n_chunks = t // c
    n_cc = n_ctx // c
    quarter = RET_HEAD_DIM // 4
    scale = RET_HEAD_DIM ** -0.5
    lg_f = lg_ref[0]
    lg_b = lg_ref[1]
    row = lax.broadcasted_iota(jnp.int32, (c, c), 0)
    col = lax.broadcasted_iota(jnp.int32, (c, c), 1)
    rowf = row.astype(F32)
    lag = (row - col).astype(F32)

    def chunk(ci):
        r0 = pl.multiple_of(ci * c, c)
        cos = cos_ref[pl.ds(r0, c), :]
        sin = sin_ref[pl.ds(r0, c), :]
        q = q_ref[pl.ds(r0, c), :]
        k = k_ref[pl.ds(r0, c), :]
        q = q * cos + _swap_halves(q, quarter) * sin
        k = (k * cos + _swap_halves(k, quarter) * sin) * scale
        return r0, q, k, v_ref[pl.ds(r0, c), :]

    d_key_f = jnp.exp(lg_f * (c - 1.0 - rowf))
    d_key_b = jnp.exp(lg_b * rowf)
    d_query_f = jnp.exp(lg_f * (rowf + 1.0))
    d_query_b = jnp.exp(lg_b * (float(c) - rowf))
    d_chunk_f = jnp.exp(lg_f * float(c))
    d_chunk_b = jnp.exp(lg_b * float(c))
    d_intra = (jnp.where(lag >= 0, jnp.exp(lg_f * jnp.maximum(lag, 0.0)), 0.0)
               + jnp.where(lag <= 0, jnp.exp(lg_b * jnp.maximum(-lag, 0.0)), 0.0))

    def summaries(n, carry):
        r0, q, k, v = chunk(n)
        qs_ref[pl.ds(r0, c), :] = q.astype(BF16)
        ks_ref[pl.ds(r0, c), :] = k.astype(BF16)
        vb = v.astype(BF16)
        kvf_ref[n] = _dot((k * d_key_f).T.astype(BF16), vb)
        kvb_ref[n] = _dot((k * d_key_b).T.astype(BF16), vb)
        return carry

    lax.fori_loop(0, n_chunks, summaries, 0, unroll=2)

    def state_f(n, s):
        sf_ref[n] = s.astype(BF16)
        return d_chunk_f * s + kvf_ref[n]

    def state_b(n, s):
        ci = jnp.where(n < n_cc, n_cc - 1 - n, n_chunks - 1 - (n - n_cc))
        sb_ref[ci] = s.astype(BF16)
        return d_chunk_b * s + kvb_ref[ci]

    zero = jnp.zeros((RET_HEAD_DIM, RET_HEAD_DIM), F32)
    lax.fori_loop(0, n_chunks, state_f, zero)
    lax.fori_loop(0, n_chunks, state_b, zero)
    gn = gn_ref[...]

    def outputs(n, carry):
        r0 = pl.multiple_of(n * c, c)
        qb = qs_ref[pl.ds(r0, c), :]
        vb = v_ref[pl.ds(r0, c), :].astype(BF16)
        scores = _dot_nt(qb, ks_ref[pl.ds(r0, c), :]) * d_intra
        y = (_dot(scores.astype(BF16), vb) + _dot(qb, sf_ref[n]) * d_query_f) + _dot(qb, sb_ref[n]) * d_query_b
        yn = y * lax.rsqrt(jnp.mean(y * y, axis=-1, keepdims=True) + NORM_EPS) * gn
        o_ref[pl.ds(r0, c), :] = (_silu(g_ref[pl.ds(r0, c), :]) * yn).astype(o_ref.dtype)
        return carry

    lax.fori_loop(0, n_chunks, outputs, 0, unroll=2)


def _retention(u, cos, sin, log_gamma, gn_g, n_ctx):
    bsz, t, _ = u.shape
    hd = RET_HEAD_DIM
    base = U_RET // hd
    spec = lambda off: pl.BlockSpec((None, t, hd), lambda b, h: (b, 0, base + off * RET_HEADS + h))
    lg = jnp.broadcast_to(log_gamma[:, :, None, None], (2, RET_HEADS, 1, LANES)).astype(F32)
    return pl.pallas_call(
        functools.partial(_ret_kernel, n_ctx=n_ctx),
        grid=(bsz, RET_HEADS),
        in_specs=[spec(0), spec(1), spec(2), spec(3),
                  pl.BlockSpec((t, hd), lambda b, h: (0, 0)),
                  pl.BlockSpec((t, hd), lambda b, h: (0, 0)),
                  pl.BlockSpec((2, None, 1, LANES), lambda b, h: (0, h, 0, 0)),
                  pl.BlockSpec((1, hd), lambda b, h: (0, h))],
        out_specs=pl.BlockSpec((None, t, hd), lambda b, h: (b, 0, h)),
        out_shape=jax.ShapeDtypeStruct((bsz, t, RET_W), BF16),
        scratch_shapes=[pltpu.VMEM((t, hd), BF16), pltpu.VMEM((t, hd), BF16),
                        pltpu.VMEM((t // RET_CHUNK, hd, hd), F32), pltpu.VMEM((t // RET_CHUNK, hd, hd), F32),
                        pltpu.VMEM((t // RET_CHUNK, hd, hd), BF16), pltpu.VMEM((t // RET_CHUNK, hd, hd), BF16)],
        compiler_params=_params("parallel", "parallel"),
        name="retention",
    )(u, u, u, u, cos, sin, lg, gn_g.reshape(1, RET_W))


def _rwkv_prep_kernel(rk_ref, rk_prev_ref, rk_next_ref, rest_ref, rest_prev_ref, rest_next_ref,
                      mu_rk_ref, mu_rest_ref, w0_ref, w2_ref, a0_ref, a2_ref, g2_ref, kk_ref, ka_ref, rk_gain_ref,
                      rows_out, v_out, gate_out, bonus_out, *, n_tiles):
    i = pl.program_id(1)
    tm = rk_ref.shape[0]
    has_prev = jnp.logical_and(i != 0, i != 1)
    has_next = jnp.logical_and(i != 0, i != n_tiles - 1)

    def shifted(x_ref, prev_ref, next_ref, mu_ref):
        x = x_ref[...]
        rows = lax.broadcasted_iota(jnp.int32, x.shape, 0)
        halo_prev = jnp.where(has_prev, prev_ref[7:8, :], 0.0)
        halo_next = jnp.where(has_next, next_ref[0:1, :], 0.0)
        prev = jnp.where(rows == 0, halo_prev, pltpu.roll(x, 1, 0))
        nxt = jnp.where(rows == tm - 1, halo_next, pltpu.roll(x, tm - 1, 0))
        return x + (prev - x) * mu_ref[0:1, :] + (nxt - x) * mu_ref[1:2, :]

    rk = shifted(rk_ref, rk_prev_ref, rk_next_ref, mu_rk_ref)
    rest = shifted(rest_ref, rest_prev_ref, rest_next_ref, mu_rest_ref)
    w = RWKV_W
    r = rk[:, 0:w]
    k = rk[:, w:2 * w]
    v = rest[:, 0:w]
    xw = rest[:, w:w + RWKV_DECAY_LORA]
    xa = rest[:, w + RWKV_DECAY_LORA:w + RWKV_DECAY_LORA + RWKV_AAA_LORA]
    xg = rest[:, w + RWKV_DECAY_LORA + RWKV_AAA_LORA:]

    ones = _group_ones(w, RWKV_HEAD_DIM)
    kk = k * kk_ref[...]
    kk = kk * lax.rsqrt(jnp.maximum(_dot2_exact_rhs(kk * kk, ones), 1e-12))
    rows_out[ROW_R] = r
    rows_out[ROW_A] = -kk
    v_out[...] = v
    tw = jnp.tanh(xw)
    k_sum = jnp.zeros_like(k)
    for d in range(2):
        w_log = -jax.nn.softplus(-(w0_ref[d:d + 1, :] + _dot3(tw, w2_ref[d]))) - 0.5
        a = jax.nn.sigmoid(a0_ref[d:d + 1, :] + _dot3(xa, a2_ref[d]))
        k_d = k * (1.0 + (a - 1.0) * ka_ref[...])
        rows_out[ROW_W + d] = jnp.exp(-jnp.exp(w_log))
        rows_out[ROW_K + d] = k_d
        rows_out[ROW_B + d] = kk * a
        k_sum = k_sum + k_d
    gate_out[...] = _dot3(jax.nn.sigmoid(xg), g2_ref[...])
    bonus_out[...] = _dot2_exact_rhs(r * k_sum * rk_gain_ref[...], ones) * v


def _rwkv_prep(u, mu, w0, w2, a0, a2, g2, k_k, k_a, r_k):
    bsz, t, _ = u.shape
    tm = ROW_TILE
    n_tiles = t // tm
    w = RWKV_W
    rk_blk = U_RWKV_RK // (2 * w)
    rest_blk = U_RWKV_REST // RWKV_REST_COLS
    sub = tm // 8
    n_sub = t // 8
    prev_idx = lambda b, i: jnp.maximum(i * sub - 1, 0)
    next_idx = lambda b, i: jnp.minimum((i + 1) * sub, n_sub - 1)
    row = lambda a: a.reshape(1, -1)
    const = lambda shape: pl.BlockSpec(shape, lambda b, i: (0,) * len(shape))
    tok = lambda width: pl.BlockSpec((None, tm, width), lambda b, i: (b, i, 0))
    rows_spec = pl.BlockSpec((N_SCAN_ROWS, None, tm, w), lambda b, i: (0, b, i, 0))
    sd = lambda *lead: jax.ShapeDtypeStruct((*lead, bsz, t, w), F32)
    return pl.pallas_call(
        functools.partial(_rwkv_prep_kernel, n_tiles=n_tiles),
        grid=(bsz, n_tiles),
        in_specs=[pl.BlockSpec((None, tm, 2 * w), lambda b, i: (b, i, rk_blk)),
                  pl.BlockSpec((None, 8, 2 * w), lambda b, i: (b, prev_idx(b, i), rk_blk)),
                  pl.BlockSpec((None, 8, 2 * w), lambda b, i: (b, next_idx(b, i), rk_blk)),
                  pl.BlockSpec((None, tm, RWKV_REST_COLS), lambda b, i: (b, i, rest_blk)),
                  pl.BlockSpec((None, 8, RWKV_REST_COLS), lambda b, i: (b, prev_idx(b, i), rest_blk)),
                  pl.BlockSpec((None, 8, RWKV_REST_COLS), lambda b, i: (b, next_idx(b, i), rest_blk)),
                  const((2, 2 * w)), const((2, RWKV_REST_COLS)),
                  const((2, w)), const((2, RWKV_DECAY_LORA, w)), const((2, w)), const((2, RWKV_AAA_LORA, w)),
                  const((RWKV_GATE_LORA, w)), const((1, w)), const((1, w)), const((1, w))],
        out_specs=[rows_spec, tok(w), tok(w), tok(w)],
        out_shape=[sd(N_SCAN_ROWS), sd(), sd(), sd()],
        compiler_params=_params("parallel", "parallel"),
        name="rwkv_prep",
    )(u, u, u, u, u, u, mu[:, :2 * w], mu[:, 2 * w:], w0, w2, a0, a2, g2, row(k_k), row(k_a), row(r_k))


def _transpose_tokens(z_ref, scr, bsz):
    w = RWKV_W
    for b in range(bsz):
        scr[b * w:(b + 1) * w, :] = z_ref[b].T
    if bsz * w < scr.shape[0]:
        scr[bsz * w:, :] = jnp.zeros((scr.shape[0] - bsz * w, scr.shape[1]), F32)


def _layout_rows_kernel(z_ref, o_ref, scr, *, bsz):
    n = RWKV_HEAD_DIM
    _transpose_tokens(z_ref, scr, bsz)
    for j in range(n):
        x = scr[pl.ds(j, LANES // 2, stride=n), :]
        o_ref[j] = jnp.concatenate([x, x], axis=0).T


def _layout_v_kernel(z_ref, o_ref, scr, *, bsz):
    n = RWKV_HEAD_DIM
    ts = z_ref.shape[1]
    _transpose_tokens(z_ref, scr, bsz)
    for i in range(n // 2):
        x0 = scr[pl.ds(i, LANES // 2, stride=n), :]
        x1 = scr[pl.ds(n // 2 + i, LANES // 2, stride=n), :]
        o_ref[pl.ds(i, ts, stride=n // 2), :] = jnp.concatenate([x0, x1], axis=0).T


def _scan_layout(rows, v):
    g, bsz, t, w = rows.shape
    n = RWKV_HEAD_DIM
    ts = LANES
    scr = pltpu.VMEM((LANES // 2 * n, ts), F32)
    rows_l = pl.pallas_call(
        functools.partial(_layout_rows_kernel, bsz=bsz),
        grid=(g, t // ts),
        in_specs=[pl.BlockSpec((None, bsz, ts, w), lambda k, i: (k, 0, i, 0))],
        out_specs=pl.BlockSpec((None, n, ts, LANES), lambda k, i: (k, 0, i, 0)),
        out_shape=jax.ShapeDtypeStruct((g, n, t, LANES), F32),
        scratch_shapes=[scr],
        compiler_params=_params("parallel", "parallel"),
        name="rwkv_layout_rows",
    )(rows)
    v_l = pl.pallas_call(
        functools.partial(_layout_v_kernel, bsz=bsz),
        grid=(t // ts,),
        in_specs=[pl.BlockSpec((bsz, ts, w), lambda i: (0, i, 0))],
        out_specs=pl.BlockSpec((ts * n // 2, LANES), lambda i: (i, 0)),
        out_shape=jax.ShapeDtypeStruct((t * n // 2, LANES), F32),
        scratch_shapes=[scr],
        compiler_params=_params("parallel"),
        name="rwkv_layout_v",
    )(v)
    return rows_l, v_l


def _rwkv_scan_kernel(r_ref, a_ref, w_ref, k_ref, b_ref, v_ref, y_ref, s_ref, sa_ref):
    n = RWKV_HEAD_DIM
    half = n // 2
    ts = r_ref.shape[1]
    fwd = pl.program_id(0) == 0

    @pl.when(pl.program_id(1) == 0)
    def _():
        s_ref[...] = jnp.zeros_like(s_ref)

    t_first = jnp.where(fwd, 0, ts - 1)
    acc = jnp.zeros((half, LANES), F32)
    for j in range(n):
        acc = acc + s_ref[j] * a_ref[j, pl.ds(t_first, 1), :]
    sa_ref[...] = acc

    def step(m, carry):
        t = jnp.where(fwd, m, ts - 1 - m)
        tn = jnp.clip(jnp.where(fwd, t + 1, t - 1), 0, ts - 1)
        sa = sa_ref[...]
        v = v_ref[t]
        y = jnp.zeros((half, LANES), F32)
        sa_next = jnp.zeros((half, LANES), F32)
        for j in range(n):
            s = (s_ref[j] * w_ref[j, pl.ds(t, 1), :] + sa * b_ref[j, pl.ds(t, 1), :]) + v * k_ref[j, pl.ds(t, 1), :]
            s_ref[j] = s
            y = y + s * r_ref[j, pl.ds(t, 1), :]
            sa_next = sa_next + s * a_ref[j, pl.ds(tn, 1), :]
        y_ref[t] = y
        sa_ref[...] = sa_next
        return carry

    lax.fori_loop(0, ts, step, 0)


def _rwkv_scan(rows, v, n_ctx):
    _, n, t, lanes = rows.shape
    ts = SCAN_STEPS
    nb = t // ts
    ncb = n_ctx // ts

    def blk(d, s):
        back = jnp.where(s < ncb, ncb - 1 - s, nb - 1 - (s - ncb))
        return jnp.where(d == 0, s, back)

    shared = lambda kind: pl.BlockSpec((None, n, ts, lanes), lambda d, s: (kind, 0, blk(d, s), 0))
    per_dir = lambda kind: pl.BlockSpec((None, n, ts, lanes), lambda d, s: (kind + d, 0, blk(d, s), 0))
    return pl.pallas_call(
        _rwkv_scan_kernel,
        grid=(2, nb),
        in_specs=[shared(ROW_R), shared(ROW_A), per_dir(ROW_W), per_dir(ROW_K), per_dir(ROW_B),
                  pl.BlockSpec((ts, n // 2, lanes), lambda d, s: (blk(d, s), 0, 0))],
        out_specs=pl.BlockSpec((None, ts, n // 2, lanes), lambda d, s: (d, blk(d, s), 0, 0)),
        out_shape=jax.ShapeDtypeStruct((2, t, n // 2, lanes), F32),
        scratch_shapes=[pltpu.VMEM((n, n // 2, lanes), F32), pltpu.VMEM((n // 2, lanes), F32)],
        compiler_params=_params("arbitrary", "arbitrary"),
        name="rwkv_scan",
    )(rows, rows, rows, rows, rows, v)


def _rwkv_readout_kernel(yf_ref, yb_ref, bonus_ref, gate_ref, g_ref, b_ref, o_ref, scr, *, bsz):
    n = RWKV_HEAD_DIM
    w = RWKV_W
    ts = o_ref.shape[1]
    for i in range(n // 2):
        rows = pl.ds(i, ts, stride=n // 2)
        yt = (yf_ref[rows, :] + yb_ref[rows, :]).T
        scr[pl.ds(i, LANES // 2, stride=n), :] = yt[:LANES // 2]
        scr[pl.ds(n // 2 + i, LANES // 2, stride=n), :] = yt[LANES // 2:]
    ones = _group_ones(w, n)
    inv = 1.0 / n
    for b in range(bsz):
        y = scr[b * w:(b + 1) * w, :].T
        mean = _dot2_exact_rhs(y, ones) * inv
        yc = y - mean
        var = _dot2_exact_rhs(yc * yc, ones) * inv
        yn = yc * lax.rsqrt(var + RWKV_GN_EPS) * g_ref[...] + b_ref[...]
        o_ref[b] = ((yn + bonus_ref[b]) * gate_ref[b]).astype(o_ref.dtype)


def _rwkv_readout(y, bonus, gate, ln_g, ln_b):
    bsz, t, w = bonus.shape
    n = RWKV_HEAD_DIM
    ts = LANES
    tok = pl.BlockSpec((bsz, ts, w), lambda i: (0, i, 0))
    vec = pl.BlockSpec((1, w), lambda i: (0, 0))
    return pl.pallas_call(
        functools.partial(_rwkv_readout_kernel, bsz=bsz),
        grid=(t // ts,),
        in_specs=[pl.BlockSpec((None, ts * n // 2, LANES), lambda i: (0, i, 0)),
                  pl.BlockSpec((None, ts * n // 2, LANES), lambda i: (1, i, 0)), tok, tok, vec, vec],
        out_specs=tok,
        out_shape=jax.ShapeDtypeStruct((bsz, t, w), BF16),
        scratch_shapes=[pltpu.VMEM((LANES // 2 * n, ts), F32)],
        compiler_params=_params("parallel"),
        name="rwkv_readout",
    )(y, y, bonus, gate, ln_g.reshape(1, w), ln_b.reshape(1, w))


def _merge_kernel(ya_ref, yr_ref, yw_ref, g0_ref, g1_ref, g2_ref, x_ref, gate1_ref, a2_ref, b2_ref,
                  wb_ref, wo_ref, wr_ref, br_ref,
                  x_out, h_out, ids_out, wts_out, cnt_out):
    first = jnp.logical_and(pl.program_id(0) == 0, pl.program_id(1) == 0)

    @pl.when(first)
    def _():
        cnt_out[...] = jnp.zeros_like(cnt_out)

    merged = (jax.nn.sigmoid(g0_ref[...]) * _dot(ya_ref[...], wb_ref[0])
              + jax.nn.sigmoid(g1_ref[...]) * _dot(yr_ref[...], wb_ref[1])
              + jax.nn.sigmoid(g2_ref[...]) * _dot(yw_ref[...], wb_ref[2]))
    x = x_ref[...] + gate1_ref[...] * _dot(merged.astype(BF16), wo_ref[...])
    x_out[...] = x
    h = x * lax.rsqrt(jnp.mean(x * x, axis=-1, keepdims=True) + NORM_EPS) * a2_ref[...] + b2_ref[...]
    h_out[...] = h.astype(BF16)

    tm = x.shape[0]
    logits = _dot3(h, wr_ref[...]) + br_ref[...]
    lane = lax.broadcasted_iota(jnp.int32, (tm, LANES), 1)
    lane_f = lane.astype(F32)
    neg = -jnp.inf
    big = float(LANES)
    first = lambda hit: jnp.min(jnp.where(hit, lane_f, big), axis=-1, keepdims=True).astype(jnp.int32)
    is_grp = jnp.logical_and(lane >= MOE_EXPERTS, lane < MOE_EXPERTS + MOE_GROUPS)
    gl = jnp.where(is_grp, logits, neg)
    gmax = jnp.max(gl, axis=-1, keepdims=True)
    gidx = first(gl == gmax) - MOE_EXPERTS
    p_grp = 1.0 / jnp.sum(jnp.where(is_grp, jnp.exp(gl - gmax), 0.0), axis=-1, keepdims=True)
    in_grp = jnp.logical_and(lane < MOE_EXPERTS, lane // MOE_EXPERTS_PER_GROUP == gidx)
    el = jnp.where(in_grp, logits, neg)
    v1 = jnp.max(el, axis=-1, keepdims=True)
    i1 = first(el == v1)
    el2 = jnp.where(lane == i1, neg, el)
    v2 = jnp.max(el2, axis=-1, keepdims=True)
    i2 = first(el2 == v2)
    e2 = jnp.exp(v2 - v1)
    w1 = p_grp / (1.0 + e2)
    w2 = p_grp * e2 / (1.0 + e2)
    wts_out[...] = jnp.where(lane == 0, w1, jnp.where(lane == 1, w2, 0.0))

    onehot = jnp.where(jnp.logical_or(lane == i1, lane == i2), 1.0, 0.0)
    rr = lax.broadcasted_iota(jnp.int32, (tm, tm), 0)
    cc = lax.broadcasted_iota(jnp.int32, (tm, tm), 1)
    below = jnp.where(cc < rr, 1.0, 0.0).astype(BF16)
    before = _dot(below, onehot.astype(BF16)) + cnt_out[0:1, :]
    rank1 = jnp.sum(jnp.where(lane == i1, before, 0.0), axis=-1, keepdims=True).astype(jnp.int32)
    rank2 = jnp.sum(jnp.where(lane == i2, before, 0.0), axis=-1, keepdims=True).astype(jnp.int32)
    ids_out[...] = jnp.where(lane == 0, i1, jnp.where(lane == 1, i2, jnp.where(lane == 2, rank1,
                                                                                  jnp.where(lane == 3, rank2, 0))))
    cnt_out[...] = cnt_out[...] + jnp.sum(onehot, axis=0, keepdims=True)


def _merge(ya, yr, yw, u, x, gate1, a2, b2, w_branch, w_out, w_router, b_router):
    bsz, t, d = x.shape
    tm = ROW_TILE
    sel = lambda b, i: (2 * b + jnp.minimum(i, 1), 0, 0)
    tok = lambda width, blk=0: pl.BlockSpec((None, tm, width), lambda b, i: (b, i, blk))
    const = lambda shape: pl.BlockSpec(shape, lambda b, i: (0,) * len(shape))
    mod = pl.BlockSpec((None, 1, d), sel)
    return pl.pallas_call(
        _merge_kernel,
        grid=(bsz, t // tm),
        in_specs=[tok(BRANCH_W), tok(BRANCH_W), tok(BRANCH_W), tok(d, 0), tok(d, 1), tok(d, 2), tok(d),
                  mod, mod, mod,
                  const((N_BRANCH, BRANCH_W, d)), const((d, d)), const((d, LANES)), const((1, LANES))],
        out_specs=[tok(d), tok(d), tok(LANES), tok(LANES), const((8, LANES))],
        out_shape=[jax.ShapeDtypeStruct((bsz, t, d), F32), jax.ShapeDtypeStruct((bsz, t, d), BF16),
                   jax.ShapeDtypeStruct((bsz, t, LANES), jnp.int32), jax.ShapeDtypeStruct((bsz, t, LANES), F32),
                   jax.ShapeDtypeStruct((8, LANES), F32)],
        compiler_params=_params("arbitrary", "arbitrary"),
        name="merge_router",
    )(ya, yr, yw, u, u, u, x, gate1, a2, b2, w_branch, w_out, w_router, b_router)


def _moe_kernel(be_ref, na_ref, x_ref, wg_ref, wu_ref, wd_ref, o_ref, wg_s, wu_s, wd_s):
    i = pl.program_id(0)
    active = i < na_ref[0]
    new_expert = jnp.logical_or(i == 0, be_ref[i] != be_ref[jnp.maximum(i - 1, 0)])

    @pl.when(jnp.logical_and(active, new_expert))
    def _():
        wg_s[...] = wg_ref[...].astype(BF16)
        wu_s[...] = wu_ref[...].astype(BF16)
        wd_s[...] = wd_ref[...].astype(BF16)

    @pl.when(active)
    def _():
        x = x_ref[...]
        act = _silu(_dot(x, wg_s[...])) * _dot(x, wu_s[...])
        o_ref[...] = _dot(act.astype(BF16), wd_s[...]).astype(o_ref.dtype)

    @pl.when(i >= na_ref[0])
    def _():
        o_ref[...] = jnp.zeros_like(o_ref)


def _moe_experts(buf, block_expert, n_active, w_gate, w_up, w_down, layer):
    rows, d = buf.shape
    hid = w_gate.shape[-1]
    grid_spec = pltpu.PrefetchScalarGridSpec(
        num_scalar_prefetch=2,
        grid=(rows // MOE_BLOCK,),
        in_specs=[pl.BlockSpec((MOE_BLOCK, d), lambda i, be, na: (i, 0)),
                  pl.BlockSpec((None, None, d, hid), lambda i, be, na: (layer, be[i], 0, 0)),
                  pl.BlockSpec((None, None, d, hid), lambda i, be, na: (layer, be[i], 0, 0)),
                  pl.BlockSpec((None, None, hid, d), lambda i, be, na: (layer, be[i], 0, 0))],
        out_specs=pl.BlockSpec((MOE_BLOCK, d), lambda i, be, na: (i, 0)),
        scratch_shapes=[pltpu.VMEM((d, hid), BF16), pltpu.VMEM((d, hid), BF16), pltpu.VMEM((hid, d), BF16)])
    return pl.pallas_call(
        _moe_kernel,
        grid_spec=grid_spec,
        out_shape=jax.ShapeDtypeStruct((rows, d), BF16),
        compiler_params=_params("arbitrary"),
        name="moe_experts",
    )(block_expert, n_active, buf, w_gate, w_up, w_down)


def _combine_kernel(x_ref, y0_ref, y1_ref, w_ref, g_ref, o_ref):
    w = w_ref[...]
    y = y0_ref[...].astype(F32) * w[:, 0:1] + y1_ref[...].astype(F32) * w[:, 1:2]
    o_ref[...] = x_ref[...] + g_ref[...] * y


def _combine(x, y_pairs, wts, gate2):
    bsz, t, d = x.shape
    tm = ROW_TILE
    sel = lambda b, i: (2 * b + jnp.minimum(i, 1), 0, 0)
    tok = lambda width: pl.BlockSpec((None, tm, width), lambda b, i: (b, i, 0))
    slot = lambda k: pl.BlockSpec((None, None, tm, d), lambda b, i: (k, b, i, 0))
    return pl.pallas_call(
        _combine_kernel,
        grid=(bsz, t // tm),
        in_specs=[tok(d), slot(0), slot(1), tok(LANES), pl.BlockSpec((None, 1, d), sel)],
        out_specs=tok(d),
        out_shape=jax.ShapeDtypeStruct((bsz, t, d), F32),
        compiler_params=_params("parallel", "parallel"),
        name="moe_combine",
    )(x, y_pairs, y_pairs, wts, gate2)


def _moe(h, ids, wts, counts, w_gate, w_up, w_down, layer):
    bsz, t, d = h.shape
    n_tok = bsz * t
    n_pair = 2 * n_tok
    n_blocks = -(-n_pair // MOE_BLOCK) + MOE_EXPERTS
    counts = counts[0, :MOE_EXPERTS].astype(jnp.int32)
    padded = (counts + MOE_BLOCK - 1) // MOE_BLOCK * MOE_BLOCK
    pad_end = jnp.cumsum(padded)
    pad_start = pad_end - padded
    flat = ids.reshape(n_tok, LANES)
    expert = jnp.stack([flat[:, 0], flat[:, 1]])
    rank = jnp.stack([flat[:, 2], flat[:, 3]])
    dest = (pad_start[expert] + rank).reshape(n_pair)
    token = jnp.tile(jnp.arange(n_tok, dtype=jnp.int32), 2)
    src = jnp.zeros((n_blocks * MOE_BLOCK,), jnp.int32).at[dest].set(token, unique_indices=True)
    block_start = jnp.arange(n_blocks, dtype=jnp.int32) * MOE_BLOCK
    block_expert = jnp.minimum(jnp.sum((pad_end[None, :] <= block_start[:, None]).astype(jnp.int32), axis=1),
                               MOE_EXPERTS - 1).astype(jnp.int32)
    n_active = (pad_end[-1:] // MOE_BLOCK).astype(jnp.int32)
    buf = jnp.take(h.reshape(n_tok, d), src, axis=0)
    yb = _moe_experts(buf, block_expert, n_active, w_gate, w_up, w_down, layer)
    return jnp.take(yb, dest, axis=0).reshape(2, bsz, t, d)


def _rope_tables(n_ctx, n_lat, head_dim):
    rows = n_lat // GRID_W
    row = jnp.broadcast_to(jnp.arange(rows, dtype=F32)[:, None], (rows, GRID_W)).reshape(-1)
    col = jnp.broadcast_to(jnp.arange(GRID_W, dtype=F32)[None, :], (rows, GRID_W)).reshape(-1)
    quarter = head_dim // 4
    inv_freq = ROPE_THETA ** (-jnp.arange(quarter, dtype=F32) / quarter)
    ang = jnp.stack([row[:, None] * inv_freq, col[:, None] * inv_freq], axis=1)
    cos, sin = jnp.cos(ang), jnp.sin(ang)
    cos_t = jnp.stack([cos, cos], axis=2).reshape(n_lat, head_dim)
    sin_t = jnp.stack([-sin, sin], axis=2).reshape(n_lat, head_dim)
    cos_t = jnp.concatenate([jnp.ones((n_ctx, head_dim), F32), cos_t], axis=0)
    sin_t = jnp.concatenate([jnp.zeros((n_ctx, head_dim), F32), sin_t], axis=0)
    rep = LANES // head_dim
    return jnp.tile(cos_t, (1, rep)), jnp.tile(sin_t, (1, rep))


def kernel(x, c, ctx, c_ctx, ada_w, ada_b, norm1_g, norm2_g, w_in, att_qn_g, att_kn_g, ret_decay_logit, ret_gn_g, rwkv_mu, rwkv_w0, rwkv_w2, rwkv_a0, rwkv_a2, rwkv_g2, rwkv_k_k, rwkv_k_a, rwkv_r_k, rwkv_ln_g, rwkv_ln_b, w_branch, w_out, router_grp_w, router_grp_b, router_exp_w, router_exp_b, moe_w_gate, moe_w_up, moe_w_down):
    bsz, n_lat, d = x.shape
    n_ctx = ctx.shape[1]
    depth = ada_w.shape[0]
    assert d == D_MODEL and n_ctx == ROW_TILE and n_lat % ROW_TILE == 0 and n_lat % GRID_W == 0
    assert 2 * bsz * RWKV_HEADS <= LANES
    t_all = n_ctx + n_lat
    assert t_all % LANES == 0 and n_ctx % SCAN_STEPS == 0

    att_cos, att_sin = _rope_tables(n_ctx, n_lat, ATT_HEAD_DIM)
    ret_cos, ret_sin = _rope_tables(n_ctx, n_lat, RET_HEAD_DIM)

    rows = -(-(bsz + 1) // 8) * 8
    cvec = jnp.zeros((rows, d), F32).at[:bsz].set(c).at[bsz].set(c_ctx)
    mods = _modulation(cvec, ada_w, ada_b)
    perm = _column_permutation()

    xs = jnp.concatenate([ctx, x], axis=1)
    for layer in range(depth):
        m = mods[layer].reshape(rows, 6, d)
        pick = lambda j: jnp.stack([jnp.broadcast_to(m[bsz, j], (bsz, d)), m[:bsz, j]], axis=1).reshape(2 * bsz, 1, d)
        sh1, sc1, g1, sh2, sc2, g2 = (pick(j) for j in range(6))
        w_l = w_in[layer][:, perm].astype(BF16)
        u = _in_proj(xs, norm1_g[layer] * (1.0 + sc1), sh1, w_l)

        q_att, kv_att = _att_prep(u, att_cos, att_sin, att_qn_g[layer], att_kn_g[layer])
        ya = _attention(q_att, kv_att, n_ctx)

        log_gamma = jax.nn.log_sigmoid(ret_decay_logit[layer].astype(F32))
        yr = _retention(u, ret_cos, ret_sin, log_gamma, ret_gn_g[layer], n_ctx)

        rows_t, v_t, gate, bonus = _rwkv_prep(
            u, rwkv_mu[layer], rwkv_w0[layer], rwkv_w2[layer], rwkv_a0[layer], rwkv_a2[layer], rwkv_g2[layer],
            rwkv_k_k[layer], rwkv_k_a[layer], rwkv_r_k[layer].reshape(-1))
        rows_s, v_s = _scan_layout(rows_t, v_t)
        y_scan = _rwkv_scan(rows_s, v_s.reshape(t_all, RWKV_HEAD_DIM // 2, LANES), n_ctx)
        yw = _rwkv_readout(y_scan.reshape(2, t_all * RWKV_HEAD_DIM // 2, LANES), bonus, gate,
                           rwkv_ln_g[layer], rwkv_ln_b[layer])

        w_router = jnp.zeros((d, LANES), F32).at[:, :MOE_EXPERTS].set(router_exp_w[layer]).at[
            :, MOE_EXPERTS:MOE_EXPERTS + MOE_GROUPS].set(router_grp_w[layer])
        b_router = jnp.zeros((1, LANES), F32).at[0, :MOE_EXPERTS].set(router_exp_b[layer]).at[
            0, MOE_EXPERTS:MOE_EXPERTS + MOE_GROUPS].set(router_grp_b[layer])
        xs, h2, ids, wts, counts = _merge(
            ya, yr, yw, u, xs, g1, norm2_g[layer] * (1.0 + sc2), sh2,
            w_branch[layer].astype(BF16), w_out[layer].astype(BF16), w_router, b_router)

        y_pairs = _moe(h2, ids, wts, counts, moe_w_gate, moe_w_up, moe_w_down, layer)
        xs = _combine(xs, y_pairs, wts, g2)
    return xs[:, n_ctx:]
```

```python
import functools

import jax
import jax.numpy as jnp
from jax import lax
from jax.experimental import pallas as pl
from jax.experimental.pallas import tpu as pltpu

F32 = jnp.float32
BF16 = jnp.bfloat16

D_MODEL = 1024
GRID_W = 64
NORM_EPS = 1e-6
ROPE_THETA = 10000.0

ATT_HEADS = 8
ATT_KV_HEADS = 2
ATT_HEAD_DIM = 64
ATT_GROUP = ATT_HEADS // ATT_KV_HEADS
ATT_W = ATT_HEADS * ATT_HEAD_DIM
ATT_KV_W = ATT_KV_HEADS * ATT_HEAD_DIM

RET_HEADS = 4
RET_HEAD_DIM = 128
RET_CHUNK = 128
RET_W = RET_HEADS * RET_HEAD_DIM

RWKV_HEADS = 8
RWKV_HEAD_DIM = 64
RWKV_W = RWKV_HEADS * RWKV_HEAD_DIM
RWKV_DECAY_LORA = 64
RWKV_AAA_LORA = 64
RWKV_GATE_LORA = 128
RWKV_GN_EPS = 64e-5
RWKV_DECAY_SCALE = 0.6065306597126334
RWKV_COLS = 3 * RWKV_W + RWKV_DECAY_LORA + RWKV_AAA_LORA + RWKV_GATE_LORA

N_BRANCH = 3
BRANCH_W = 512
IN_COLS = ATT_W + 2 * ATT_KV_W + 4 * RET_W + RWKV_COLS + N_BRANCH * D_MODEL

MOE_GROUPS = 4
MOE_EXPERTS_PER_GROUP = 8
MOE_EXPERTS = MOE_GROUPS * MOE_EXPERTS_PER_GROUP
MOE_HIDDEN = 512
MOE_BLOCK = 256

LANES = 128
ROW_TILE = 256
ATT_Q_TILE = 128
ATT_KEY_TILE = 256
ATT_Q_SCALE = ATT_HEAD_DIM ** -0.5 * 1.4426950408889634
SCAN_STEPS = 64
VMEM_LIMIT = 56 * 1024 * 1024

ROW_R, ROW_A, ROW_W, ROW_K, ROW_B, N_SCAN_ROWS = 0, 1, 2, 4, 6, 8

U_GATE = 0
U_RWKV_RK = U_GATE + N_BRANCH * D_MODEL
U_RET = U_RWKV_RK + 2 * RWKV_W
U_ATT = U_RET + 4 * RET_W
U_RWKV_REST = U_ATT + ATT_W + 2 * ATT_KV_W
ATT_COLS = ATT_W + 2 * ATT_KV_W
RWKV_REST_COLS = RWKV_COLS - 2 * RWKV_W


def _column_permutation():
    o_att = 0
    o_ret = ATT_COLS
    o_rwkv = o_ret + 4 * RET_W
    o_gate = o_rwkv + RWKV_COLS
    parts = [jnp.arange(o_gate, o_gate + N_BRANCH * D_MODEL),
             jnp.arange(o_rwkv, o_rwkv + 2 * RWKV_W),
             jnp.arange(o_ret, o_ret + 4 * RET_W),
             jnp.arange(o_att, o_att + ATT_COLS),
             jnp.arange(o_rwkv + 2 * RWKV_W, o_rwkv + RWKV_COLS)]
    return jnp.concatenate(parts)


def _params(*sem):
    return pltpu.CompilerParams(dimension_semantics=sem, vmem_limit_bytes=VMEM_LIMIT)


def _dot(a, b):
    return jnp.dot(a, b, preferred_element_type=F32)


def _dot_nt(a, b):
    return lax.dot_general(a, b, (((1,), (1,)), ((), ())), preferred_element_type=F32)


def _split(a):
    hi = a.astype(BF16)
    lo = (a - hi.astype(F32)).astype(BF16)
    return hi, lo


def _dot3(a, b):
    ah, al = _split(a)
    bh, bl = _split(b)
    return _dot(ah, bh) + (_dot(al, bh) + _dot(ah, bl))


def _dot2_exact_rhs(a, b_bf16):
    ah, al = _split(a)
    return _dot(ah, b_bf16) + _dot(al, b_bf16)


def _group_ones(width, group):
    r = lax.broadcasted_iota(jnp.int32, (width, width), 0) // group
    c = lax.broadcasted_iota(jnp.int32, (width, width), 1) // group
    return jnp.where(r == c, 1.0, 0.0).astype(BF16)


def _silu(x):
    return x * jax.nn.sigmoid(x)


def _swap_halves(x, quarter):
    n = x.shape[-1]
    lane = lax.broadcasted_iota(jnp.int32, x.shape, x.ndim - 1)
    up = pltpu.roll(x, n - quarter, x.ndim - 1)
    down = pltpu.roll(x, quarter, x.ndim - 1)
    return jnp.where(lane % (2 * quarter) < quarter, up, down)


def _mod_kernel(c_ref, w_ref, b_ref, o_ref):
    o_ref[...] = _dot3(_silu(c_ref[...]), w_ref[...]) + b_ref[...]


def _modulation(cvec, ada_w, ada_b):
    depth, d, cols = ada_w.shape
    rows = cvec.shape[0]
    tn = 1536
    return pl.pallas_call(
        _mod_kernel,
        grid=(depth, cols // tn),
        in_specs=[pl.BlockSpec((rows, d), lambda l, j: (0, 0)),
                  pl.BlockSpec((None, d, tn), lambda l, j: (l, 0, j)),
                  pl.BlockSpec((None, 1, tn), lambda l, j: (l, 0, j))],
        out_specs=pl.BlockSpec((None, rows, tn), lambda l, j: (l, 0, j)),
        out_shape=jax.ShapeDtypeStruct((depth, rows, cols), F32),
        compiler_params=_params("parallel", "parallel"),
        name="modulation",
    )(cvec, ada_w, ada_b.reshape(depth, 1, cols))


def _in_proj_kernel(x_ref, a_ref, b_ref, w_ref, o_ref):
    x = x_ref[...]
    ms = jnp.mean(x * x, axis=-1, keepdims=True)
    h = x * lax.rsqrt(ms + NORM_EPS) * a_ref[...] + b_ref[...]
    o_ref[...] = _dot(h.astype(BF16), w_ref[...])


def _in_proj(x, mod_a, mod_b, w):
    bsz, t, d = x.shape
    cols = w.shape[1]
    tm, tn = ROW_TILE, cols // 2
    sel = lambda j, b, i: (2 * b + jnp.minimum(i, 1), 0, 0)
    return pl.pallas_call(
        _in_proj_kernel,
        grid=(cols // tn, bsz, t // tm),
        in_specs=[pl.BlockSpec((None, tm, d), lambda j, b, i: (b, i, 0)),
                  pl.BlockSpec((None, 1, d), sel),
                  pl.BlockSpec((None, 1, d), sel),
                  pl.BlockSpec((d, tn), lambda j, b, i: (0, j))],
        out_specs=pl.BlockSpec((None, tm, tn), lambda j, b, i: (b, i, j)),
        out_shape=jax.ShapeDtypeStruct((bsz, t, cols), F32),
        compiler_params=_params("parallel", "parallel", "parallel"),
        name="in_proj",
    )(x, mod_a, mod_b, w)


def _att_prep_kernel(u_ref, cos_ref, sin_ref, qg_ref, kg_ref, q_out, kv_out):
    hd = ATT_HEAD_DIM
    ones = _group_ones(LANES, hd)
    cos = cos_ref[...]
    sin = sin_ref[...]
    lane = lax.broadcasted_iota(jnp.int32, cos.shape, 1)
    low = lane < hd

    def two_heads(y):
        return jnp.where(low, y, 0.0), jnp.where(low, pltpu.roll(y, hd, 1), 0.0)

    n_q = ATT_W // LANES
    for j in range(n_q + 1):
        x = u_ref[:, j * LANES:(j + 1) * LANES]
        is_q = j < n_q
        gain = qg_ref[...] if is_q else kg_ref[...]
        ms = _dot2_exact_rhs(x * x, ones) * (1.0 / hd)
        y = x * lax.rsqrt(ms + NORM_EPS) * gain
        y = y * cos + _swap_halves(y, hd // 4) * sin
        if is_q:
            y = y * ATT_Q_SCALE
        out, base = (q_out, 2 * j) if is_q else (kv_out, 0)
        for h, yh in enumerate(two_heads(y)):
            out[:, (base + h) * LANES:(base + h + 1) * LANES] = yh.astype(BF16)
    v = u_ref[:, ATT_W + ATT_KV_W:]
    for h, vh in enumerate(two_heads(v)):
        kv_out[:, (2 + h) * LANES:(3 + h) * LANES] = jnp.where(lane == hd, 1.0, vh).astype(BF16)


def _att_prep(u, cos, sin, qn_g, kn_g):
    bsz, t, _ = u.shape
    tm = ROW_TILE
    rep = LANES // ATT_HEAD_DIM
    return pl.pallas_call(
        _att_prep_kernel,
        grid=(bsz, t // tm),
        in_specs=[pl.BlockSpec((None, tm, ATT_COLS), lambda b, i: (b, i, U_ATT // ATT_COLS)),
                  pl.BlockSpec((tm, LANES), lambda b, i: (i, 0)),
                  pl.BlockSpec((tm, LANES), lambda b, i: (i, 0)),
                  pl.BlockSpec((1, LANES), lambda b, i: (0, 0)),
                  pl.BlockSpec((1, LANES), lambda b, i: (0, 0))],
        out_specs=[pl.BlockSpec((None, tm, ATT_HEADS * LANES), lambda b, i: (b, i, 0)),
                   pl.BlockSpec((None, tm, 2 * ATT_KV_HEADS * LANES), lambda b, i: (b, i, 0))],
        out_shape=[jax.ShapeDtypeStruct((bsz, t, ATT_HEADS * LANES), BF16),
                   jax.ShapeDtypeStruct((bsz, t, 2 * ATT_KV_HEADS * LANES), BF16)],
        compiler_params=_params("parallel", "parallel"),
        name="att_prep",
    )(u, cos, sin, jnp.tile(qn_g, rep).reshape(1, LANES), jnp.tile(kn_g, rep).reshape(1, LANES))


def _att_kernel(q_ref, kv_ref, o_ref, *, n_ctx, tq):
    i = pl.program_id(1)
    hd = ATT_HEAD_DIM
    k_ref = v_ref = kv_ref

    def run(n_keys):
        tk = ATT_KEY_TILE
        for g in range(ATT_KV_HEADS):
            q = jnp.concatenate(
                [q_ref[:, (ATT_GROUP * g + h) * LANES:(ATT_GROUP * g + h + 1) * LANES] for h in range(ATT_GROUP)],
                axis=0)
            scores = lambda c: _dot_nt(q, k_ref[c * tk:(c + 1) * tk, g * LANES:(g + 1) * LANES])
            m = jnp.full((ATT_GROUP * tq, LANES), -jnp.inf, F32)
            for c in range(n_keys // tk):
                s = scores(c)
                for part in range(tk // LANES):
                    m = jnp.maximum(m, s[:, part * LANES:(part + 1) * LANES])
            m = jnp.max(m, axis=-1, keepdims=True)
            acc = jnp.zeros((ATT_GROUP * tq, LANES), F32)
            for c in range(n_keys // tk):
                p = jnp.exp2(scores(c) - m).astype(BF16)
                acc = acc + _dot(p, v_ref[c * tk:(c + 1) * tk, (ATT_KV_HEADS + g) * LANES:(ATT_KV_HEADS + g + 1) * LANES])
            o = acc[:, :hd] / acc[:, hd:hd + 1]
            for h in range(ATT_GROUP):
                c0 = (ATT_GROUP * g + h) * hd
                o_ref[:, c0:c0 + hd] = o[h * tq:(h + 1) * tq].astype(o_ref.dtype)

    @pl.when(i < n_ctx // tq)
    def _():
        run(n_ctx)

    @pl.when(i >= n_ctx // tq)
    def _():
        run(k_ref.shape[0])


def _attention(q, kv, n_ctx):
    bsz, t, _ = q.shape
    tq = ATT_Q_TILE
    kv_w = kv.shape[-1]
    assert n_ctx % ATT_KEY_TILE == 0 and t % ATT_KEY_TILE == 0
    return pl.pallas_call(
        functools.partial(_att_kernel, n_ctx=n_ctx, tq=tq),
        grid=(bsz, t // tq),
        in_specs=[pl.BlockSpec((None, tq, ATT_HEADS * LANES), lambda b, i: (b, i, 0)),
                  pl.BlockSpec((None, t, kv_w), lambda b, i: (b, 0, 0))],
        out_specs=pl.BlockSpec((None, tq, ATT_W), lambda b, i: (b, i, 0)),
        out_shape=jax.ShapeDtypeStruct((bsz, t, ATT_W), BF16),
        compiler_params=_params("parallel", "parallel"),
        name="attention",
    )(q, kv)


def _ret_kernel(q_ref, k_ref, v_ref, g_ref, cos_ref, sin_ref, lg_ref, gn_ref, o_ref,
                qs_ref, ks_ref, kvf_ref, kvb_ref, sf_ref, sb_ref, *, n_ctx):
    c = RET_CHUNK
    t = q_ref.shape[0]
    n_chunks = t // c
    n_cc = n_ctx // c
    quarter = RET_HEAD_DIM // 4
    scale = RET_HEAD_DIM ** -0.5
    lg_f = lg_ref[0]
    lg_b = lg_ref[1]
    row = lax.broadcasted_iota(jnp.int32, (c, c), 0)
    col = lax.broadcasted_iota(jnp.int32, (c, c), 1)
    rowf = row.astype(F32)
    lag = (row - col).astype(F32)

    def chunk(ci):
        r0 = pl.multiple_of(ci * c, c)
        cos = cos_ref[pl.ds(r0, c), :]
        sin = sin_ref[pl.ds(r0, c), :]
        q = q_ref[pl.ds(r0, c), :]
        k = k_ref[pl.ds(r0, c), :]
        q = q * cos + _swap_halves(q, quarter) * sin
        k = (k * cos + _swap_halves(k, quarter) * sin) * scale
        return r0, q, k, v_ref[pl.ds(r0, c), :]

    d_key_f = jnp.exp(lg_f * (c - 1.0 - rowf))
    d_key_b = jnp.exp(lg_b * rowf)
    d_query_f = jnp.exp(lg_f * (rowf + 1.0))
    d_query_b = jnp.exp(lg_b * (float(c) - rowf))
    d_chunk_f = jnp.exp(lg_f * float(c))
    d_chunk_b = jnp.exp(lg_b * float(c))
    d_intra = (jnp.where(lag >= 0, jnp.exp(lg_f * jnp.maximum(lag, 0.0)), 0.0)
               + jnp.where(lag <= 0, jnp.exp(lg_b * jnp.maximum(-lag, 0.0)), 0.0))

    def summaries(n, carry):
        r0, q, k, v = chunk(n)
        qs_ref[pl.ds(r0, c), :] = q.astype(BF16)
        ks_ref[pl.ds(r0, c), :] = k.astype(BF16)
        vb = v.astype(BF16)
        kvf_ref[n] = _dot((k * d_key_f).T.astype(BF16), vb)
        kvb_ref[n] = _dot((k * d_key_b).T.astype(BF16), vb)
        return carry

    lax.fori_loop(0, n_chunks, summaries, 0, unroll=2)

    def state_f(n, s):
        sf_ref[n] = s.astype(BF16)
        return d_chunk_f * s + kvf_ref[n]

    def state_b(n, s):
        ci = jnp.where(n < n_cc, n_cc - 1 - n, n_chunks - 1 - (n - n_cc))
        sb_ref[ci] = s.astype(BF16)
        return d_chunk_b * s + kvb_ref[ci]

    zero = jnp.zeros((RET_HEAD_DIM, RET_HEAD_DIM), F32)
    lax.fori_loop(0, n_chunks, state_f, zero)
    lax.fori_loop(0, n_chunks, state_b, zero)
    gn = gn_ref[...]

    def outputs(n, carry):
        r0 = pl.multiple_of(n * c, c)
        qb = qs_ref[pl.ds(r0, c), :]
        vb = v_ref[pl.ds(r0, c), :].astype(BF16)
        scores = _dot_nt(qb, ks_ref[pl.ds(r0, c), :]) * d_intra
        y = (_dot(scores.astype(BF16), vb) + _dot(qb, sf_ref[n]) * d_query_f) + _dot(qb, sb_ref[n]) * d_query_b
        yn = y * lax.rsqrt(jnp.mean(y * y, axis=-1, keepdims=True) + NORM_EPS) * gn
        o_ref[pl.ds(r0, c), :] = (_silu(g_ref[pl.ds(r0, c), :]) * yn).astype(o_ref.dtype)
        return carry

    lax.fori_loop(0, n_chunks, outputs, 0, unroll=2)


def _retention(u, cos, sin, log_gamma, gn_g, n_ctx):
    bsz, t, _ = u.shape
    hd = RET_HEAD_DIM
    base = U_RET // hd
    spec = lambda off: pl.BlockSpec((None, t, hd), lambda b, h: (b, 0, base + off * RET_HEADS + h))
    lg = jnp.broadcast_to(log_gamma[:, :, None, None], (2, RET_HEADS, 1, LANES)).astype(F32)
    return pl.pallas_call(
        functools.partial(_ret_kernel, n_ctx=n_ctx),
        grid=(bsz, RET_HEADS),
        in_specs=[spec(0), spec(1), spec(2), spec(3),
                  pl.BlockSpec((t, hd), lambda b, h: (0, 0)),
                  pl.BlockSpec((t, hd), lambda b, h: (0, 0)),
                  pl.BlockSpec((2, None, 1, LANES), lambda b, h: (0, h, 0, 0)),
                  pl.BlockSpec((1, hd), lambda b, h: (0, h))],
        out_specs=pl.BlockSpec((None, t, hd), lambda b, h: (b, 0, h)),
        out_shape=jax.ShapeDtypeStruct((bsz, t, RET_W), BF16),
        scratch_shapes=[pltpu.VMEM((t, hd), BF16), pltpu.VMEM((t, hd), BF16),
                        pltpu.VMEM((t // RET_CHUNK, hd, hd), F32), pltpu.VMEM((t // RET_CHUNK, hd, hd), F32),
                        pltpu.VMEM((t // RET_CHUNK, hd, hd), BF16), pltpu.VMEM((t // RET_CHUNK, hd, hd), BF16)],
        compiler_params=_params("parallel", "parallel"),
        name="retention",
    )(u, u, u, u, cos, sin, lg, gn_g.reshape(1, RET_W))


def _rwkv_prep_kernel(rk_ref, rk_prev_ref, rk_next_ref, rest_ref, rest_prev_ref, rest_next_ref,
                      mu_rk_ref, mu_rest_ref, w0_ref, w2_ref, a0_ref, a2_ref, g2_ref, kk_ref, ka_ref, rk_gain_ref,
                      rows_out, v_out, gate_out, bonus_out, *, n_tiles):
    i = pl.program_id(1)
    tm = rk_ref.shape[0]
    has_prev = jnp.logical_and(i != 0, i != 1)
    has_next = jnp.logical_and(i != 0, i != n_tiles - 1)

    def shifted(x_ref, prev_ref, next_ref, mu_ref):
        x = x_ref[...]
        rows = lax.broadcasted_iota(jnp.int32, x.shape, 0)
        halo_prev = jnp.where(has_prev, prev_ref[7:8, :], 0.0)
        halo_next = jnp.where(has_next, next_ref[0:1, :], 0.0)
        prev = jnp.where(rows == 0, halo_prev, pltpu.roll(x, 1, 0))
        nxt = jnp.where(rows == tm - 1, halo_next, pltpu.roll(x, tm - 1, 0))
        return x + (prev - x) * mu_ref[0:1, :] + (nxt - x) * mu_ref[1:2, :]

    rk = shifted(rk_ref, rk_prev_ref, rk_next_ref, mu_rk_ref)
    rest = shifted(rest_ref, rest_prev_ref, rest_next_ref, mu_rest_ref)
    w = RWKV_W
    r = rk[:, 0:w]
    k = rk[:, w:2 * w]
    v = rest[:, 0:w]
    xw = rest[:, w:w + RWKV_DECAY_LORA]
    xa = rest[:, w + RWKV_DECAY_LORA:w + RWKV_DECAY_LORA + RWKV_AAA_LORA]
    xg = rest[:, w + RWKV_DECAY_LORA + RWKV_AAA_LORA:]

    ones = _group_ones(w, RWKV_HEAD_DIM)
    kk = k * kk_ref[...]
    kk = kk * lax.rsqrt(jnp.maximum(_dot2_exact_rhs(kk * kk, ones), 1e-12))
    rows_out[ROW_R] = r
    rows_out[ROW_A] = -kk
    v_out[...] = v
    tw = jnp.tanh(xw)
    k_sum = jnp.zeros_like(k)
    for d in range(2):
        decay_rate = jax.nn.sigmoid(w0_ref[d:d + 1, :] + _dot3(tw, w2_ref[d])) * RWKV_DECAY_SCALE
        a = jax.nn.sigmoid(a0_ref[d:d + 1, :] + _dot3(xa, a2_ref[d]))
        k_d = k * (1.0 + (a - 1.0) * ka_ref[...])
        rows_out[ROW_W + d] = jnp.exp(-decay_rate)
        rows_out[ROW_K + d] = k_d
        rows_out[ROW_B + d] = kk * a
        k_sum = k_sum + k_d
    gate_out[...] = _dot3(jax.nn.sigmoid(xg), g2_ref[...])
    bonus_out[...] = _dot2_exact_rhs(r * k_sum * rk_gain_ref[...], ones) * v


def _rwkv_prep(u, mu, w0, w2, a0, a2, g2, k_k, k_a, r_k):
    bsz, t, _ = u.shape
    tm = ROW_TILE
    n_tiles = t // tm
    w = RWKV_W
    rk_blk = U_RWKV_RK // (2 * w)
    rest_blk = U_RWKV_REST // RWKV_REST_COLS
    sub = tm // 8
    n_sub = t // 8
    prev_idx = lambda b, i: jnp.maximum(i * sub - 1, 0)
    next_idx = lambda b, i: jnp.minimum((i + 1) * sub, n_sub - 1)
    row = lambda a: a.reshape(1, -1)
    const = lambda shape: pl.BlockSpec(shape, lambda b, i: (0,) * len(shape))
    tok = lambda width: pl.BlockSpec((None, tm, width), lambda b, i: (b, i, 0))
    rows_spec = pl.BlockSpec((N_SCAN_ROWS, None, tm, w), lambda b, i: (0, b, i, 0))
    sd = lambda *lead: jax.ShapeDtypeStruct((*lead, bsz, t, w), F32)
    return pl.pallas_call(
        functools.partial(_rwkv_prep_kernel, n_tiles=n_tiles),
        grid=(bsz, n_tiles),
        in_specs=[pl.BlockSpec((None, tm, 2 * w), lambda b, i: (b, i, rk_blk)),
                  pl.BlockSpec((None, 8, 2 * w), lambda b, i: (b, prev_idx(b, i), rk_blk)),
                  pl.BlockSpec((None, 8, 2 * w), lambda b, i: (b, next_idx(b, i), rk_blk)),
                  pl.BlockSpec((None, tm, RWKV_REST_COLS), lambda b, i: (b, i, rest_blk)),
                  pl.BlockSpec((None, 8, RWKV_REST_COLS), lambda b, i: (b, prev_idx(b, i), rest_blk)),
                  pl.BlockSpec((None, 8, RWKV_REST_COLS), lambda b, i: (b, next_idx(b, i), rest_blk)),
                  const((2, 2 * w)), const((2, RWKV_REST_COLS)),
                  const((2, w)), const((2, RWKV_DECAY_LORA, w)), const((2, w)), const((2, RWKV_AAA_LORA, w)),
                  const((RWKV_GATE_LORA, w)), const((1, w)), const((1, w)), const((1, w))],
        out_specs=[rows_spec, tok(w), tok(w), tok(w)],
        out_shape=[sd(N_SCAN_ROWS), sd(), sd(), sd()],
        compiler_params=_params("parallel", "parallel"),
        name="rwkv_prep",
    )(u, u, u, u, u, u, mu[:, :2 * w], mu[:, 2 * w:], w0, w2, a0, a2, g2, row(k_k), row(k_a), row(r_k))


def _transpose_tokens(z_ref, scr, bsz):
    w = RWKV_W
    for b in range(bsz):
        scr[b * w:(b + 1) * w, :] = z_ref[b].T
    if bsz * w < scr.shape[0]:
        scr[bsz * w:, :] = jnp.zeros((scr.shape[0] - bsz * w, scr.shape[1]), F32)


def _layout_rows_kernel(z_ref, o_ref, scr, *, bsz):
    n = RWKV_HEAD_DIM
    _transpose_tokens(z_ref, scr, bsz)
    for j in range(n):
        x = scr[pl.ds(j, LANES // 2, stride=n), :]
        o_ref[j] = jnp.concatenate([x, x], axis=0).T


def _layout_v_kernel(z_ref, o_ref, scr, *, bsz):
    n = RWKV_HEAD_DIM
    ts = z_ref.shape[1]
    _transpose_tokens(z_ref, scr, bsz)
    for i in range(n // 2):
        x0 = scr[pl.ds(i, LANES // 2, stride=n), :]
        x1 = scr[pl.ds(n // 2 + i, LANES // 2, stride=n), :]
        o_ref[pl.ds(i, ts, stride=n // 2), :] = jnp.concatenate([x0, x1], axis=0).T


def _scan_layout(rows, v):
    g, bsz, t, w = rows.shape
    n = RWKV_HEAD_DIM
    ts = LANES
    scr = pltpu.VMEM((LANES // 2 * n, ts), F32)
    rows_l = pl.pallas_call(
        functools.partial(_layout_rows_kernel, bsz=bsz),
        grid=(g, t // ts),
        in_specs=[pl.BlockSpec((None, bsz, ts, w), lambda k, i: (k, 0, i, 0))],
        out_specs=pl.BlockSpec((None, n, ts, LANES), lambda k, i: (k, 0, i, 0)),
        out_shape=jax.ShapeDtypeStruct((g, n, t, LANES), F32),
        scratch_shapes=[scr],
        compiler_params=_params("parallel", "parallel"),
        name="rwkv_layout_rows",
    )(rows)
    v_l = pl.pallas_call(
        functools.partial(_layout_v_kernel, bsz=bsz),
        grid=(t // ts,),
        in_specs=[pl.BlockSpec((bsz, ts, w), lambda i: (0, i, 0))],
        out_specs=pl.BlockSpec((ts * n // 2, LANES), lambda i: (i, 0)),
        out_shape=jax.ShapeDtypeStruct((t * n // 2, LANES), F32),
        scratch_shapes=[scr],
        compiler_params=_params("parallel"),
        name="rwkv_layout_v",
    )(v)
    return rows_l, v_l


def _rwkv_scan_kernel(r_ref, a_ref, w_ref, k_ref, b_ref, v_ref, y_ref, s_ref, sa_ref):
    n = RWKV_HEAD_DIM
    half = n // 2
    ts = r_ref.shape[1]
    fwd = pl.program_id(0) == 0

    @pl.when(pl.program_id(1) == 0)
    def _():
        s_ref[...] = jnp.zeros_like(s_ref)

    t_first = jnp.where(fwd, 0, ts - 1)
    acc = jnp.zeros((half, LANES), F32)
    for j in range(n):
        acc = acc + s_ref[j] * a_ref[j, pl.ds(t_first, 1), :]
    sa_ref[...] = acc

    def step(m, carry):
        t = jnp.where(fwd, m, ts - 1 - m)
        tn = jnp.clip(jnp.where(fwd, t + 1, t - 1), 0, ts - 1)
        sa = sa_ref[...]
        v = v_ref[t]
        y = jnp.zeros((half, LANES), F32)
        sa_next = jnp.zeros((half, LANES), F32)
        for j in range(n):
            s = (s_ref[j] * w_ref[j, pl.ds(t, 1), :] + sa * b_ref[j, pl.ds(t, 1), :]) + v * k_ref[j, pl.ds(t, 1), :]
            s_ref[j] = s
            y = y + s * r_ref[j, pl.ds(t, 1), :]
            sa_next = sa_next + s * a_ref[j, pl.ds(tn, 1), :]
        y_ref[t] = y
        sa_ref[...] = sa_next
        return carry

    lax.fori_loop(0, ts, step, 0)


def _rwkv_scan(rows, v, n_ctx):
    _, n, t, lanes = rows.shape
    ts = SCAN_STEPS
    nb = t // ts
    ncb = n_ctx // ts

    def blk(d, s):
        back = jnp.where(s < ncb, ncb - 1 - s, nb - 1 - (s - ncb))
        return jnp.where(d == 0, s, back)

    shared = lambda kind: pl.BlockSpec((None, n, ts, lanes), lambda d, s: (kind, 0, blk(d, s), 0))
    per_dir = lambda kind: pl.BlockSpec((None, n, ts, lanes), lambda d, s: (kind + d, 0, blk(d, s), 0))
    return pl.pallas_call(
        _rwkv_scan_kernel,
        grid=(2, nb),
        in_specs=[shared(ROW_R), shared(ROW_A), per_dir(ROW_W), per_dir(ROW_K), per_dir(ROW_B),
                  pl.BlockSpec((ts, n // 2, lanes), lambda d, s: (blk(d, s), 0, 0))],
        out_specs=pl.BlockSpec((None, ts, n // 2, lanes), lambda d, s: (d, blk(d, s), 0, 0)),
        out_shape=jax.ShapeDtypeStruct((2, t, n // 2, lanes), F32),
        scratch_shapes=[pltpu.VMEM((n, n // 2, lanes), F32), pltpu.VMEM((n // 2, lanes), F32)],
        compiler_params=_params("arbitrary", "arbitrary"),
        name="rwkv_scan",
    )(rows, rows, rows, rows, rows, v)


def _rwkv_readout_kernel(yf_ref, yb_ref, bonus_ref, gate_ref, g_ref, b_ref, o_ref, scr, *, bsz):
    n = RWKV_HEAD_DIM
    w = RWKV_W
    ts = o_ref.shape[1]
    for i in range(n // 2):
        rows = pl.ds(i, ts, stride=n // 2)
        yt = (yf_ref[rows, :] + yb_ref[rows, :]).T
        scr[pl.ds(i, LANES // 2, stride=n), :] = yt[:LANES // 2]
        scr[pl.ds(n // 2 + i, LANES // 2, stride=n), :] = yt[LANES // 2:]
    ones = _group_ones(w, n)
    inv = 1.0 / n
    for b in range(bsz):
        y = scr[b * w:(b + 1) * w, :].T
        mean = _dot2_exact_rhs(y, ones) * inv
        yc = y - mean
        var = _dot2_exact_rhs(yc * yc, ones) * inv
        yn = yc * lax.rsqrt(var + RWKV_GN_EPS) * g_ref[...] + b_ref[...]
        o_ref[b] = ((yn + bonus_ref[b]) * gate_ref[b]).astype(o_ref.dtype)


def _rwkv_readout(y, bonus, gate, ln_g, ln_b):
    bsz, t, w = bonus.shape
    n = RWKV_HEAD_DIM
    ts = LANES
    tok = pl.BlockSpec((bsz, ts, w), lambda i: (0, i, 0))
    vec = pl.BlockSpec((1, w), lambda i: (0, 0))
    return pl.pallas_call(
        functools.partial(_rwkv_readout_kernel, bsz=bsz),
        grid=(t // ts,),
        in_specs=[pl.BlockSpec((None, ts * n // 2, LANES), lambda i: (0, i, 0)),
                  pl.BlockSpec((None, ts * n // 2, LANES), lambda i: (1, i, 0)), tok, tok, vec, vec],
        out_specs=tok,
        out_shape=jax.ShapeDtypeStruct((bsz, t, w), BF16),
        scratch_shapes=[pltpu.VMEM((LANES // 2 * n, ts), F32)],
        compiler_params=_params("parallel"),
        name="rwkv_readout",
    )(y, y, bonus, gate, ln_g.reshape(1, w), ln_b.reshape(1, w))


def _merge_kernel(ya_ref, yr_ref, yw_ref, g0_ref, g1_ref, g2_ref, x_ref, gate1_ref, a2_ref, b2_ref,
                  wb_ref, wo_ref, wr_ref, br_ref,
                  x_out, h_out, ids_out, wts_out, cnt_out):
    first = jnp.logical_and(pl.program_id(0) == 0, pl.program_id(1) == 0)

    @pl.when(first)
    def _():
        cnt_out[...] = jnp.zeros_like(cnt_out)

    merged = (jax.nn.sigmoid(g0_ref[...]) * _dot(ya_ref[...], wb_ref[0])
              + jax.nn.sigmoid(g1_ref[...]) * _dot(yr_ref[...], wb_ref[1])
              + jax.nn.sigmoid(g2_ref[...]) * _dot(yw_ref[...], wb_ref[2]))
    x = x_ref[...] + gate1_ref[...] * _dot(merged.astype(BF16), wo_ref[...])
    x_out[...] = x
    h = x * lax.rsqrt(jnp.mean(x * x, axis=-1, keepdims=True) + NORM_EPS) * a2_ref[...] + b2_ref[...]
    h_out[...] = h.astype(BF16)

    tm = x.shape[0]
    logits = _dot3(h, wr_ref[...]) + br_ref[...]
    lane = lax.broadcasted_iota(jnp.int32, (tm, LANES), 1)
    lane_f = lane.astype(F32)
    neg = -jnp.inf
    big = float(LANES)
    first = lambda hit: jnp.min(jnp.where(hit, lane_f, big), axis=-1, keepdims=True).astype(jnp.int32)
    is_grp = jnp.logical_and(lane >= MOE_EXPERTS, lane < MOE_EXPERTS + MOE_GROUPS)
    gl = jnp.where(is_grp, logits, neg)
    gmax = jnp.max(gl, axis=-1, keepdims=True)
    gidx = first(gl == gmax) - MOE_EXPERTS
    p_grp = 1.0 / jnp.sum(jnp.where(is_grp, jnp.exp(gl - gmax), 0.0), axis=-1, keepdims=True)
    in_grp = jnp.logical_and(lane < MOE_EXPERTS, lane // MOE_EXPERTS_PER_GROUP == gidx)
    el = jnp.where(in_grp, logits, neg)
    v1 = jnp.max(el, axis=-1, keepdims=True)
    i1 = first(el == v1)
    el2 = jnp.where(lane == i1, neg, el)
    v2 = jnp.max(el2, axis=-1, keepdims=True)
    i2 = first(el2 == v2)
    e2 = jnp.exp(v2 - v1)
    w1 = p_grp / (1.0 + e2)
    w2 = p_grp * e2 / (1.0 + e2)
    wts_out[...] = jnp.where(lane == 0, w1, jnp.where(lane == 1, w2, 0.0))

    onehot = jnp.where(jnp.logical_or(lane == i1, lane == i2), 1.0, 0.0)
    rr = lax.broadcasted_iota(jnp.int32, (tm, tm), 0)
    cc = lax.broadcasted_iota(jnp.int32, (tm, tm), 1)
    below = jnp.where(cc < rr, 1.0, 0.0).astype(BF16)
    before = _dot(below, onehot.astype(BF16)) + cnt_out[0:1, :]
    rank1 = jnp.sum(jnp.where(lane == i1, before, 0.0), axis=-1, keepdims=True).astype(jnp.int32)
    rank2 = jnp.sum(jnp.where(lane == i2, before, 0.0), axis=-1, keepdims=True).astype(jnp.int32)
    ids_out[...] = jnp.where(lane == 0, i1, jnp.where(lane == 1, i2, jnp.where(lane == 2, rank1,
                                                                                  jnp.where(lane == 3, rank2, 0))))
    cnt_out[...] = cnt_out[...] + jnp.sum(onehot, axis=0, keepdims=True)


def _merge(ya, yr, yw, u, x, gate1, a2, b2, w_branch, w_out, w_router, b_router):
    bsz, t, d = x.shape
    tm = ROW_TILE
    sel = lambda b, i: (2 * b + jnp.minimum(i, 1), 0, 0)
    tok = lambda width, blk=0: pl.BlockSpec((None, tm, width), lambda b, i: (b, i, blk))
    const = lambda shape: pl.BlockSpec(shape, lambda b, i: (0,) * len(shape))
    mod = pl.BlockSpec((None, 1, d), sel)
    return pl.pallas_call(
        _merge_kernel,
        grid=(bsz, t // tm),
        in_specs=[tok(BRANCH_W), tok(BRANCH_W), tok(BRANCH_W), tok(d, 0), tok(d, 1), tok(d, 2), tok(d),
                  mod, mod, mod,
                  const((N_BRANCH, BRANCH_W, d)), const((d, d)), const((d, LANES)), const((1, LANES))],
        out_specs=[tok(d), tok(d), tok(LANES), tok(LANES), const((8, LANES))],
        out_shape=[jax.ShapeDtypeStruct((bsz, t, d), F32), jax.ShapeDtypeStruct((bsz, t, d), BF16),
                   jax.ShapeDtypeStruct((bsz, t, LANES), jnp.int32), jax.ShapeDtypeStruct((bsz, t, LANES), F32),
                   jax.ShapeDtypeStruct((8, LANES), F32)],
        compiler_params=_params("arbitrary", "arbitrary"),
        name="merge_router",
    )(ya, yr, yw, u, u, u, x, gate1, a2, b2, w_branch, w_out, w_router, b_router)


def _moe_kernel(be_ref, na_ref, x_ref, wg_ref, wu_ref, wd_ref, o_ref, wg_s, wu_s, wd_s):
    i = pl.program_id(0)
    active = i < na_ref[0]
    new_expert = jnp.logical_or(i == 0, be_ref[i] != be_ref[jnp.maximum(i - 1, 0)])

    @pl.when(jnp.logical_and(active, new_expert))
    def _():
        wg_s[...] = wg_ref[...].astype(BF16)
        wu_s[...] = wu_ref[...].astype(BF16)
        wd_s[...] = wd_ref[...].astype(BF16)

    @pl.when(active)
    def _():
        x = x_ref[...]
        act = _silu(_dot(x, wg_s[...])) * _dot(x, wu_s[...])
        o_ref[...] = _dot(act.astype(BF16), wd_s[...]).astype(o_ref.dtype)

    @pl.when(i >= na_ref[0])
    def _():
        o_ref[...] = jnp.zeros_like(o_ref)


def _moe_experts(buf, block_expert, n_active, w_gate, w_up, w_down, layer):
    rows, d = buf.shape
    hid = w_gate.shape[-1]
    grid_spec = pltpu.PrefetchScalarGridSpec(
        num_scalar_prefetch=2,
        grid=(rows // MOE_BLOCK,),
        in_specs=[pl.BlockSpec((MOE_BLOCK, d), lambda i, be, na: (i, 0)),
                  pl.BlockSpec((None, None, d, hid), lambda i, be, na: (layer, be[i], 0, 0)),
                  pl.BlockSpec((None, None, d, hid), lambda i, be, na: (layer, be[i], 0, 0)),
                  pl.BlockSpec((None, None, hid, d), lambda i, be, na: (layer, be[i], 0, 0))],
        out_specs=pl.BlockSpec((MOE_BLOCK, d), lambda i, be, na: (i, 0)),
        scratch_shapes=[pltpu.VMEM((d, hid), BF16), pltpu.VMEM((d, hid), BF16), pltpu.VMEM((hid, d), BF16)])
    return pl.pallas_call(
        _moe_kernel,
        grid_spec=grid_spec,
        out_shape=jax.ShapeDtypeStruct((rows, d), F32),
        compiler_params=_params("arbitrary"),
        name="moe_experts",
    )(block_expert, n_active, buf, w_gate, w_up, w_down)


def _combine_kernel(x_ref, y0_ref, y1_ref, w_ref, g_ref, o_ref):
    w = w_ref[...]
    y = y0_ref[...] * w[:, 0:1] + y1_ref[...] * w[:, 1:2]
    o_ref[...] = x_ref[...] + g_ref[...] * y


def _combine(x, y_pairs, wts, gate2):
    bsz, t, d = x.shape
    tm = ROW_TILE
    sel = lambda b, i: (2 * b + jnp.minimum(i, 1), 0, 0)
    tok = lambda width: pl.BlockSpec((None, tm, width), lambda b, i: (b, i, 0))
    slot = lambda k: pl.BlockSpec((None, None, tm, d), lambda b, i: (k, b, i, 0))
    return pl.pallas_call(
        _combine_kernel,
        grid=(bsz, t // tm),
        in_specs=[tok(d), slot(0), slot(1), tok(LANES), pl.BlockSpec((None, 1, d), sel)],
        out_specs=tok(d),
        out_shape=jax.ShapeDtypeStruct((bsz, t, d), F32),
        compiler_params=_params("parallel", "parallel"),
        name="moe_combine",
    )(x, y_pairs, y_pairs, wts, gate2)


def _moe(h, ids, wts, counts, w_gate, w_up, w_down, layer):
    bsz, t, d = h.shape
    n_tok = bsz * t
    n_pair = 2 * n_tok
    n_blocks = -(-n_pair // MOE_BLOCK) + MOE_EXPERTS
    counts = counts[0, :MOE_EXPERTS].astype(jnp.int32)
    padded = (counts + MOE_BLOCK - 1) // MOE_BLOCK * MOE_BLOCK
    pad_end = jnp.cumsum(padded)
    pad_start = pad_end - padded
    small = ids[..., 0:4].reshape(n_tok, 4).T
    expert, rank = small[0:2], small[2:4]
    onehot = expert[:, :, None] == jnp.arange(MOE_EXPERTS, dtype=jnp.int32)
    dest = (jnp.sum(jnp.where(onehot, pad_start, 0), axis=-1) + rank).reshape(n_pair)
    token = jnp.tile(jnp.arange(n_tok, dtype=jnp.int32), 2)
    src = jnp.zeros((n_blocks * MOE_BLOCK,), jnp.int32).at[dest].set(token, unique_indices=True)
    block_start = jnp.arange(n_blocks, dtype=jnp.int32) * MOE_BLOCK
    block_expert = jnp.minimum(jnp.sum((pad_end[None, :] <= block_start[:, None]).astype(jnp.int32), axis=1),
                               MOE_EXPERTS - 1).astype(jnp.int32)
    n_active = (pad_end[-1:] // MOE_BLOCK).astype(jnp.int32)
    buf = jnp.take(h.reshape(n_tok, d), src, axis=0)
    yb = _moe_experts(buf, block_expert, n_active, w_gate, w_up, w_down, layer)
    return jnp.take(yb, dest, axis=0).reshape(2, bsz, t, d)


def _rope_tables(n_ctx, n_lat, head_dim):
    rows = n_lat // GRID_W
    row = jnp.broadcast_to(jnp.arange(rows, dtype=F32)[:, None], (rows, GRID_W)).reshape(-1)
    col = jnp.broadcast_to(jnp.arange(GRID_W, dtype=F32)[None, :], (rows, GRID_W)).reshape(-1)
    quarter = head_dim // 4
    inv_freq = ROPE_THETA ** (-jnp.arange(quarter, dtype=F32) / quarter)
    ang = jnp.stack([row[:, None] * inv_freq, col[:, None] * inv_freq], axis=1)
    cos, sin = jnp.cos(ang), jnp.sin(ang)
    cos_t = jnp.stack([cos, cos], axis=2).reshape(n_lat, head_dim)
    sin_t = jnp.stack([-sin, sin], axis=2).reshape(n_lat, head_dim)
    cos_t = jnp.concatenate([jnp.ones((n_ctx, head_dim), F32), cos_t], axis=0)
    sin_t = jnp.concatenate([jnp.zeros((n_ctx, head_dim), F32), sin_t], axis=0)
    rep = LANES // head_dim
    return jnp.tile(cos_t, (1, rep)), jnp.tile(sin_t, (1, rep))


def kernel(x, c, ctx, c_ctx, ada_w, ada_b, norm1_g, norm2_g, w_in, att_qn_g, att_kn_g, ret_decay_logit, ret_gn_g, rwkv_mu, rwkv_w0, rwkv_w2, rwkv_a0, rwkv_a2, rwkv_g2, rwkv_k_k, rwkv_k_a, rwkv_r_k, rwkv_ln_g, rwkv_ln_b, w_branch, w_out, router_grp_w, router_grp_b, router_exp_w, router_exp_b, moe_w_gate, moe_w_up, moe_w_down):
    bsz, n_lat, d = x.shape
    n_ctx = ctx.shape[1]
    depth = ada_w.shape[0]
    assert d == D_MODEL and n_ctx == ROW_TILE and n_lat % ROW_TILE == 0 and n_lat % GRID_W == 0
    assert 2 * bsz * RWKV_HEADS <= LANES
    t_all = n_ctx + n_lat
    assert t_all % LANES == 0 and n_ctx % SCAN_STEPS == 0

    att_cos, att_sin = _rope_tables(n_ctx, n_lat, ATT_HEAD_DIM)
    ret_cos, ret_sin = _rope_tables(n_ctx, n_lat, RET_HEAD_DIM)

    rows = -(-(bsz + 1) // 8) * 8
    cvec = jnp.zeros((rows, d), F32).at[:bsz].set(c).at[bsz].set(c_ctx)
    mods = _modulation(cvec, ada_w, ada_b)
    perm = _column_permutation()

    xs = jnp.concatenate([ctx, x], axis=1)
    for layer in range(depth):
        m = mods[layer].reshape(rows, 6, d)
        pick = lambda j: jnp.stack([jnp.broadcast_to(m[bsz, j], (bsz, d)), m[:bsz, j]], axis=1).reshape(2 * bsz, 1, d)
        sh1, sc1, g1, sh2, sc2, g2 = (pick(j) for j in range(6))
        w_l = w_in[layer][:, perm].astype(BF16)
        u = _in_proj(xs, norm1_g[layer] * (1.0 + sc1), sh1, w_l)

        q_att, kv_att = _att_prep(u, att_cos, att_sin, att_qn_g[layer], att_kn_g[layer])
        ya = _attention(q_att, kv_att, n_ctx)

        log_gamma = jax.nn.log_sigmoid(ret_decay_logit[layer].astype(F32))
        yr = _retention(u, ret_cos, ret_sin, log_gamma, ret_gn_g[layer], n_ctx)

        rows_t, v_t, gate, bonus = _rwkv_prep(
            u, rwkv_mu[layer], rwkv_w0[layer], rwkv_w2[layer], rwkv_a0[layer], rwkv_a2[layer], rwkv_g2[layer],
            rwkv_k_k[layer], rwkv_k_a[layer], rwkv_r_k[layer].reshape(-1))
        rows_s, v_s = _scan_layout(rows_t, v_t)
        y_scan = _rwkv_scan(rows_s, v_s.reshape(t_all, RWKV_HEAD_DIM // 2, LANES), n_ctx)
        yw = _rwkv_readout(y_scan.reshape(2, t_all * RWKV_HEAD_DIM // 2, LANES), bonus, gate,
                           rwkv_ln_g[layer], rwkv_ln_b[layer])

        w_router = jnp.zeros((d, LANES), F32).at[:, :MOE_EXPERTS].set(router_exp_w[layer]).at[
            :, MOE_EXPERTS:MOE_EXPERTS + MOE_GROUPS].set(router_grp_w[layer])
        b_router = jnp.zeros((1, LANES), F32).at[0, :MOE_EXPERTS].set(router_exp_b[layer]).at[
            0, MOE_EXPERTS:MOE_EXPERTS + MOE_GROUPS].set(router_grp_b[layer])
        xs, h2, ids, wts, counts = _merge(
            ya, yr, yw, u, xs, g1, norm2_g[layer] * (1.0 + sc2), sh2,
            w_branch[layer].astype(BF16), w_out[layer].astype(BF16), w_router, b_router)

        y_pairs = _moe(h2, ids, wts, counts, moe_w_gate, moe_w_up, moe_w_down, layer)
        xs = _combine(xs, y_pairs, wts, g2)
    return xs[:, n_ctx:]
```

```python
import functools

import jax
import jax.numpy as jnp
from jax import lax
from jax.experimental import pallas as pl
from jax.experimental.pallas import tpu as pltpu

F32 = jnp.float32
BF16 = jnp.bfloat16

D_MODEL = 1024
GRID_W = 64
NORM_EPS = 1e-6
ROPE_THETA = 10000.0

ATT_HEADS = 8
ATT_KV_HEADS = 2
ATT_HEAD_DIM = 64
ATT_GROUP = ATT_HEADS // ATT_KV_HEADS
ATT_W = ATT_HEADS * ATT_HEAD_DIM
ATT_KV_W = ATT_KV_HEADS * ATT_HEAD_DIM

RET_HEADS = 4
RET_HEAD_DIM = 128
RET_CHUNK = 128
RET_W = RET_HEADS * RET_HEAD_DIM

RWKV_HEADS = 8
RWKV_HEAD_DIM = 64
RWKV_W = RWKV_HEADS * RWKV_HEAD_DIM
RWKV_DECAY_LORA = 64
RWKV_AAA_LORA = 64
RWKV_GATE_LORA = 128
RWKV_GN_EPS = 64e-5
RWKV_DECAY_SCALE = 0.6065306597126334
RWKV_COLS = 3 * RWKV_W + RWKV_DECAY_LORA + RWKV_AAA_LORA + RWKV_GATE_LORA

N_BRANCH = 3
BRANCH_W = 512
IN_COLS = ATT_W + 2 * ATT_KV_W + 4 * RET_W + RWKV_COLS + N_BRANCH * D_MODEL

MOE_GROUPS = 4
MOE_EXPERTS_PER_GROUP = 8
MOE_EXPERTS = MOE_GROUPS * MOE_EXPERTS_PER_GROUP
MOE_HIDDEN = 512
MOE_BLOCK = 256
MOE_PARTS = 2

LANES = 128
ROW_TILE = 256
ATT_Q_TILE = 128
ATT_KEY_TILE = 256
ATT_Q_SCALE = ATT_HEAD_DIM ** -0.5 * 1.4426950408889634
SCAN_STEPS = 64
VMEM_LIMIT = 56 * 1024 * 1024

ROW_R, ROW_A, ROW_W, ROW_K, ROW_B, N_SCAN_ROWS = 0, 1, 2, 4, 6, 8

U_GATE = 0
U_RWKV_RK = U_GATE + N_BRANCH * D_MODEL
U_RET = U_RWKV_RK + 2 * RWKV_W
U_ATT = U_RET + 4 * RET_W
U_RWKV_REST = U_ATT + ATT_W + 2 * ATT_KV_W
ATT_COLS = ATT_W + 2 * ATT_KV_W
RWKV_REST_COLS = RWKV_COLS - 2 * RWKV_W


def _column_permutation():
    o_att = 0
    o_ret = ATT_COLS
    o_rwkv = o_ret + 4 * RET_W
    o_gate = o_rwkv + RWKV_COLS
    parts = [jnp.arange(o_gate, o_gate + N_BRANCH * D_MODEL),
             jnp.arange(o_rwkv, o_rwkv + 2 * RWKV_W),
             jnp.arange(o_ret, o_ret + 4 * RET_W),
             jnp.arange(o_att, o_att + ATT_COLS),
             jnp.arange(o_rwkv + 2 * RWKV_W, o_rwkv + RWKV_COLS)]
    return jnp.concatenate(parts)


def _params(*sem):
    return pltpu.CompilerParams(dimension_semantics=sem, vmem_limit_bytes=VMEM_LIMIT)


def _dot(a, b):
    return jnp.dot(a, b, preferred_element_type=F32)


def _dot_nt(a, b):
    return lax.dot_general(a, b, (((1,), (1,)), ((), ())), preferred_element_type=F32)


def _split(a):
    hi = a.astype(BF16)
    lo = (a - hi.astype(F32)).astype(BF16)
    return hi, lo


def _dot3(a, b):
    ah, al = _split(a)
    bh, bl = _split(b)
    return _dot(ah, bh) + (_dot(al, bh) + _dot(ah, bl))


def _dot2_exact_rhs(a, b_bf16):
    ah, al = _split(a)
    return _dot(ah, b_bf16) + _dot(al, b_bf16)


def _group_ones(width, group):
    r = lax.broadcasted_iota(jnp.int32, (width, width), 0) // group
    c = lax.broadcasted_iota(jnp.int32, (width, width), 1) // group
    return jnp.where(r == c, 1.0, 0.0).astype(BF16)


def _silu(x):
    return x * jax.nn.sigmoid(x)


def _swap_halves(x, quarter):
    n = x.shape[-1]
    lane = lax.broadcasted_iota(jnp.int32, x.shape, x.ndim - 1)
    up = pltpu.roll(x, n - quarter, x.ndim - 1)
    down = pltpu.roll(x, quarter, x.ndim - 1)
    return jnp.where(lane % (2 * quarter) < quarter, up, down)


def _mod_kernel(c_ref, w_ref, b_ref, o_ref):
    o_ref[...] = _dot3(_silu(c_ref[...]), w_ref[...]) + b_ref[...]


def _modulation(cvec, ada_w, ada_b):
    depth, d, cols = ada_w.shape
    rows = cvec.shape[0]
    tn = 1536
    return pl.pallas_call(
        _mod_kernel,
        grid=(depth, cols // tn),
        in_specs=[pl.BlockSpec((rows, d), lambda l, j: (0, 0)),
                  pl.BlockSpec((None, d, tn), lambda l, j: (l, 0, j)),
                  pl.BlockSpec((None, 1, tn), lambda l, j: (l, 0, j))],
        out_specs=pl.BlockSpec((None, rows, tn), lambda l, j: (l, 0, j)),
        out_shape=jax.ShapeDtypeStruct((depth, rows, cols), F32),
        compiler_params=_params("parallel", "parallel"),
        name="modulation",
    )(cvec, ada_w, ada_b.reshape(depth, 1, cols))


def _in_proj_kernel(x_ref, a_ref, b_ref, w_ref, o_ref):
    x = x_ref[...]
    ms = jnp.mean(x * x, axis=-1, keepdims=True)
    h = x * lax.rsqrt(ms + NORM_EPS) * a_ref[...] + b_ref[...]
    o_ref[...] = _dot(h.astype(BF16), w_ref[...])


def _in_proj(x, mod_a, mod_b, w):
    bsz, t, d = x.shape
    cols = w.shape[1]
    tm, tn = ROW_TILE, cols // 2
    sel = lambda j, b, i: (2 * b + jnp.minimum(i, 1), 0, 0)
    return pl.pallas_call(
        _in_proj_kernel,
        grid=(cols // tn, bsz, t // tm),
        in_specs=[pl.BlockSpec((None, tm, d), lambda j, b, i: (b, i, 0)),
                  pl.BlockSpec((None, 1, d), sel),
                  pl.BlockSpec((None, 1, d), sel),
                  pl.BlockSpec((d, tn), lambda j, b, i: (0, j))],
        out_specs=pl.BlockSpec((None, tm, tn), lambda j, b, i: (b, i, j)),
        out_shape=jax.ShapeDtypeStruct((bsz, t, cols), F32),
        compiler_params=_params("parallel", "parallel", "parallel"),
        name="in_proj",
    )(x, mod_a, mod_b, w)


def _att_prep_kernel(u_ref, cos_ref, sin_ref, qg_ref, kg_ref, q_out, kv_out):
    hd = ATT_HEAD_DIM
    ones = _group_ones(LANES, hd)
    cos = cos_ref[...]
    sin = sin_ref[...]
    lane = lax.broadcasted_iota(jnp.int32, cos.shape, 1)
    low = lane < hd

    def two_heads(y):
        return jnp.where(low, y, 0.0), jnp.where(low, pltpu.roll(y, hd, 1), 0.0)

    n_q = ATT_W // LANES
    for j in range(n_q + 1):
        x = u_ref[:, j * LANES:(j + 1) * LANES]
        is_q = j < n_q
        gain = qg_ref[...] if is_q else kg_ref[...]
        ms = _dot2_exact_rhs(x * x, ones) * (1.0 / hd)
        y = x * lax.rsqrt(ms + NORM_EPS) * gain
        y = y * cos + _swap_halves(y, hd // 4) * sin
        if is_q:
            y = y * ATT_Q_SCALE
        out, base = (q_out, 2 * j) if is_q else (kv_out, 0)
        for h, yh in enumerate(two_heads(y)):
            out[:, (base + h) * LANES:(base + h + 1) * LANES] = yh.astype(BF16)
    v = u_ref[:, ATT_W + ATT_KV_W:]
    for h, vh in enumerate(two_heads(v)):
        kv_out[:, (2 + h) * LANES:(3 + h) * LANES] = jnp.where(lane == hd, 1.0, vh).astype(BF16)


def _att_prep(u, cos, sin, qn_g, kn_g):
    bsz, t, _ = u.shape
    tm = ROW_TILE
    rep = LANES // ATT_HEAD_DIM
    return pl.pallas_call(
        _att_prep_kernel,
        grid=(bsz, t // tm),
        in_specs=[pl.BlockSpec((None, tm, ATT_COLS), lambda b, i: (b, i, U_ATT // ATT_COLS)),
                  pl.BlockSpec((tm, LANES), lambda b, i: (i, 0)),
                  pl.BlockSpec((tm, LANES), lambda b, i: (i, 0)),
                  pl.BlockSpec((1, LANES), lambda b, i: (0, 0)),
                  pl.BlockSpec((1, LANES), lambda b, i: (0, 0))],
        out_specs=[pl.BlockSpec((None, tm, ATT_HEADS * LANES), lambda b, i: (b, i, 0)),
                   pl.BlockSpec((None, tm, 2 * ATT_KV_HEADS * LANES), lambda b, i: (b, i, 0))],
        out_shape=[jax.ShapeDtypeStruct((bsz, t, ATT_HEADS * LANES), BF16),
                   jax.ShapeDtypeStruct((bsz, t, 2 * ATT_KV_HEADS * LANES), BF16)],
        compiler_params=_params("parallel", "parallel"),
        name="att_prep",
    )(u, cos, sin, jnp.tile(qn_g, rep).reshape(1, LANES), jnp.tile(kn_g, rep).reshape(1, LANES))


def _att_kernel(q_ref, kv_ref, o_ref, *, n_ctx, tq):
    i = pl.program_id(1)
    hd = ATT_HEAD_DIM
    k_ref = v_ref = kv_ref

    def run(n_keys):
        tk = ATT_KEY_TILE
        for g in range(ATT_KV_HEADS):
            q = jnp.concatenate(
                [q_ref[:, (ATT_GROUP * g + h) * LANES:(ATT_GROUP * g + h + 1) * LANES] for h in range(ATT_GROUP)],
                axis=0)
            scores = lambda c: _dot_nt(q, k_ref[c * tk:(c + 1) * tk, g * LANES:(g + 1) * LANES])
            m = jnp.full((ATT_GROUP * tq, LANES), -jnp.inf, F32)
            for c in range(n_keys // tk):
                s = scores(c)
                for part in range(tk // LANES):
                    m = jnp.maximum(m, s[:, part * LANES:(part + 1) * LANES])
            m = jnp.max(m, axis=-1, keepdims=True)
            acc = jnp.zeros((ATT_GROUP * tq, LANES), F32)
            for c in range(n_keys // tk):
                p = jnp.exp2(scores(c) - m).astype(BF16)
                acc = acc + _dot(p, v_ref[c * tk:(c + 1) * tk, (ATT_KV_HEADS + g) * LANES:(ATT_KV_HEADS + g + 1) * LANES])
            o = acc[:, :hd] / acc[:, hd:hd + 1]
            for h in range(ATT_GROUP):
                c0 = (ATT_GROUP * g + h) * hd
                o_ref[:, c0:c0 + hd] = o[h * tq:(h + 1) * tq].astype(o_ref.dtype)

    @pl.when(i < n_ctx // tq)
    def _():
        run(n_ctx)

    @pl.when(i >= n_ctx // tq)
    def _():
        run(k_ref.shape[0])


def _attention(q, kv, n_ctx):
    bsz, t, _ = q.shape
    tq = ATT_Q_TILE
    kv_w = kv.shape[-1]
    assert n_ctx % ATT_KEY_TILE == 0 and t % ATT_KEY_TILE == 0
    return pl.pallas_call(
        functools.partial(_att_kernel, n_ctx=n_ctx, tq=tq),
        grid=(bsz, t // tq),
        in_specs=[pl.BlockSpec((None, tq, ATT_HEADS * LANES), lambda b, i: (b, i, 0)),
                  pl.BlockSpec((None, t, kv_w), lambda b, i: (b, 0, 0))],
        out_specs=pl.BlockSpec((None, tq, ATT_W), lambda b, i: (b, i, 0)),
        out_shape=jax.ShapeDtypeStruct((bsz, t, ATT_W), BF16),
        compiler_params=_params("parallel", "parallel"),
        name="attention",
    )(q, kv)


def _ret_kernel(q_ref, k_ref, v_ref, g_ref, cos_ref, sin_ref, lg_ref, gn_ref, o_ref,
                qs_ref, ks_ref, kvf_ref, kvb_ref, sf_ref, sb_ref, *, n_ctx):
    c = RET_CHUNK
    t = q_ref.shape[0]
    n_chunks = t // c
    n_cc = n_ctx // c
    quarter = RET_HEAD_DIM // 4
    scale = RET_HEAD_DIM ** -0.5
    lg_f = lg_ref[0]
    lg_b = lg_ref[1]
    row = lax.broadcasted_iota(jnp.int32, (c, c), 0)
    col = lax.broadcasted_iota(jnp.int32, (c, c), 1)
    rowf = row.astype(F32)
    lag = (row - col).astype(F32)

    def chunk(ci):
        r0 = pl.multiple_of(ci * c, c)
        cos = cos_ref[pl.ds(r0, c), :]
        sin = sin_ref[pl.ds(r0, c), :]
        q = q_ref[pl.ds(r0, c), :]
        k = k_ref[pl.ds(r0, c), :]
        q = q * cos + _swap_halves(q, quarter) * sin
        k = (k * cos + _swap_halves(k, quarter) * sin) * scale
        return r0, q, k, v_ref[pl.ds(r0, c), :]

    d_key_f = jnp.exp(lg_f * (c - 1.0 - rowf))
    d_key_b = jnp.exp(lg_b * rowf)
    d_query_f = jnp.exp(lg_f * (rowf + 1.0))
    d_query_b = jnp.exp(lg_b * (float(c) - rowf))
    d_chunk_f = jnp.exp(lg_f * float(c))
    d_chunk_b = jnp.exp(lg_b * float(c))
    d_intra = (jnp.where(lag >= 0, jnp.exp(lg_f * jnp.maximum(lag, 0.0)), 0.0)
               + jnp.where(lag <= 0, jnp.exp(lg_b * jnp.maximum(-lag, 0.0)), 0.0))

    def summaries(n, carry):
        r0, q, k, v = chunk(n)
        qs_ref[pl.ds(r0, c), :] = q.astype(BF16)
        ks_ref[pl.ds(r0, c), :] = k.astype(BF16)
        vb = v.astype(BF16)
        kvf_ref[n] = _dot((k * d_key_f).T.astype(BF16), vb)
        kvb_ref[n] = _dot((k * d_key_b).T.astype(BF16), vb)
        return carry

    lax.fori_loop(0, n_chunks, summaries, 0, unroll=2)

    def state_f(n, s):
        sf_ref[n] = s.astype(BF16)
        return d_chunk_f * s + kvf_ref[n]

    def state_b(n, s):
        ci = jnp.where(n < n_cc, n_cc - 1 - n, n_chunks - 1 - (n - n_cc))
        sb_ref[ci] = s.astype(BF16)
        return d_chunk_b * s + kvb_ref[ci]

    zero = jnp.zeros((RET_HEAD_DIM, RET_HEAD_DIM), F32)
    lax.fori_loop(0, n_chunks, state_f, zero)
    lax.fori_loop(0, n_chunks, state_b, zero)
    gn = gn_ref[...]

    def outputs(n, carry):
        r0 = pl.multiple_of(n * c, c)
        qb = qs_ref[pl.ds(r0, c), :]
        vb = v_ref[pl.ds(r0, c), :].astype(BF16)
        scores = _dot_nt(qb, ks_ref[pl.ds(r0, c), :]) * d_intra
        y = (_dot(scores.astype(BF16), vb) + _dot(qb, sf_ref[n]) * d_query_f) + _dot(qb, sb_ref[n]) * d_query_b
        yn = y * lax.rsqrt(jnp.mean(y * y, axis=-1, keepdims=True) + NORM_EPS) * gn
        o_ref[pl.ds(r0, c), :] = (_silu(g_ref[pl.ds(r0, c), :]) * yn).astype(o_ref.dtype)
        return carry

    lax.fori_loop(0, n_chunks, outputs, 0, unroll=2)


def _retention(u, cos, sin, log_gamma, gn_g, n_ctx):
    bsz, t, _ = u.shape
    hd = RET_HEAD_DIM
    base = U_RET // hd
    spec = lambda off: pl.BlockSpec((None, t, hd), lambda b, h: (b, 0, base + off * RET_HEADS + h))
    lg = jnp.broadcast_to(log_gamma[:, :, None, None], (2, RET_HEADS, 1, LANES)).astype(F32)
    return pl.pallas_call(
        functools.partial(_ret_kernel, n_ctx=n_ctx),
        grid=(bsz, RET_HEADS),
        in_specs=[spec(0), spec(1), spec(2), spec(3),
                  pl.BlockSpec((t, hd), lambda b, h: (0, 0)),
                  pl.BlockSpec((t, hd), lambda b, h: (0, 0)),
                  pl.BlockSpec((2, None, 1, LANES), lambda b, h: (0, h, 0, 0)),
                  pl.BlockSpec((1, hd), lambda b, h: (0, h))],
        out_specs=pl.BlockSpec((None, t, hd), lambda b, h: (b, 0, h)),
        out_shape=jax.ShapeDtypeStruct((bsz, t, RET_W), BF16),
        scratch_shapes=[pltpu.VMEM((t, hd), BF16), pltpu.VMEM((t, hd), BF16),
                        pltpu.VMEM((t // RET_CHUNK, hd, hd), F32), pltpu.VMEM((t // RET_CHUNK, hd, hd), F32),
                        pltpu.VMEM((t // RET_CHUNK, hd, hd), BF16), pltpu.VMEM((t // RET_CHUNK, hd, hd), BF16)],
        compiler_params=_params("parallel", "parallel"),
        name="retention",
    )(u, u, u, u, cos, sin, lg, gn_g.reshape(1, RET_W))


def _rwkv_prep_kernel(rk_ref, rk_prev_ref, rk_next_ref, rest_ref, rest_prev_ref, rest_next_ref,
                      mu_rk_ref, mu_rest_ref, w0_ref, w2_ref, a0_ref, a2_ref, g2_ref, kk_ref, ka_ref, rk_gain_ref,
                      rows_out, v_out, gate_out, bonus_out, *, n_tiles):
    i = pl.program_id(1)
    tm = rk_ref.shape[0]
    has_prev = jnp.logical_and(i != 0, i != 1)
    has_next = jnp.logical_and(i != 0, i != n_tiles - 1)

    def shifted(x_ref, prev_ref, next_ref, mu_ref):
        x = x_ref[...]
        rows = lax.broadcasted_iota(jnp.int32, x.shape, 0)
        halo_prev = jnp.where(has_prev, prev_ref[7:8, :], 0.0)
        halo_next = jnp.where(has_next, next_ref[0:1, :], 0.0)
        prev = jnp.where(rows == 0, halo_prev, pltpu.roll(x, 1, 0))
        nxt = jnp.where(rows == tm - 1, halo_next, pltpu.roll(x, tm - 1, 0))
        return x + (prev - x) * mu_ref[0:1, :] + (nxt - x) * mu_ref[1:2, :]

    rk = shifted(rk_ref, rk_prev_ref, rk_next_ref, mu_rk_ref)
    rest = shifted(rest_ref, rest_prev_ref, rest_next_ref, mu_rest_ref)
    w = RWKV_W
    r = rk[:, 0:w]
    k = rk[:, w:2 * w]
    v = rest[:, 0:w]
    xw = rest[:, w:w + RWKV_DECAY_LORA]
    xa = rest[:, w + RWKV_DECAY_LORA:w + RWKV_DECAY_LORA + RWKV_AAA_LORA]
    xg = rest[:, w + RWKV_DECAY_LORA + RWKV_AAA_LORA:]

    ones = _group_ones(w, RWKV_HEAD_DIM)
    kk = k * kk_ref[...]
    kk = kk * lax.rsqrt(jnp.maximum(_dot2_exact_rhs(kk * kk, ones), 1e-12))
    rows_out[ROW_R] = r
    rows_out[ROW_A] = -kk
    v_out[...] = v
    tw = jnp.tanh(xw)
    k_sum = jnp.zeros_like(k)
    for d in range(2):
        decay_rate = jax.nn.sigmoid(w0_ref[d:d + 1, :] + _dot3(tw, w2_ref[d])) * RWKV_DECAY_SCALE
        a = jax.nn.sigmoid(a0_ref[d:d + 1, :] + _dot3(xa, a2_ref[d]))
        k_d = k * (1.0 + (a - 1.0) * ka_ref[...])
        rows_out[ROW_W + d] = jnp.exp(-decay_rate)
        rows_out[ROW_K + d] = k_d
        rows_out[ROW_B + d] = kk * a
        k_sum = k_sum + k_d
    gate_out[...] = _dot3(jax.nn.sigmoid(xg), g2_ref[...])
    bonus_out[...] = _dot2_exact_rhs(r * k_sum * rk_gain_ref[...], ones) * v


def _rwkv_prep(u, mu, w0, w2, a0, a2, g2, k_k, k_a, r_k):
    bsz, t, _ = u.shape
    tm = ROW_TILE
    n_tiles = t // tm
    w = RWKV_W
    rk_blk = U_RWKV_RK // (2 * w)
    rest_blk = U_RWKV_REST // RWKV_REST_COLS
    sub = tm // 8
    n_sub = t // 8
    prev_idx = lambda b, i: jnp.maximum(i * sub - 1, 0)
    next_idx = lambda b, i: jnp.minimum((i + 1) * sub, n_sub - 1)
    row = lambda a: a.reshape(1, -1)
    const = lambda shape: pl.BlockSpec(shape, lambda b, i: (0,) * len(shape))
    tok = lambda width: pl.BlockSpec((None, tm, width), lambda b, i: (b, i, 0))
    rows_spec = pl.BlockSpec((N_SCAN_ROWS, None, tm, w), lambda b, i: (0, b, i, 0))
    sd = lambda *lead: jax.ShapeDtypeStruct((*lead, bsz, t, w), F32)
    return pl.pallas_call(
        functools.partial(_rwkv_prep_kernel, n_tiles=n_tiles),
        grid=(bsz, n_tiles),
        in_specs=[pl.BlockSpec((None, tm, 2 * w), lambda b, i: (b, i, rk_blk)),
                  pl.BlockSpec((None, 8, 2 * w), lambda b, i: (b, prev_idx(b, i), rk_blk)),
                  pl.BlockSpec((None, 8, 2 * w), lambda b, i: (b, next_idx(b, i), rk_blk)),
                  pl.BlockSpec((None, tm, RWKV_REST_COLS), lambda b, i: (b, i, rest_blk)),
                  pl.BlockSpec((None, 8, RWKV_REST_COLS), lambda b, i: (b, prev_idx(b, i), rest_blk)),
                  pl.BlockSpec((None, 8, RWKV_REST_COLS), lambda b, i: (b, next_idx(b, i), rest_blk)),
                  const((2, 2 * w)), const((2, RWKV_REST_COLS)),
                  const((2, w)), const((2, RWKV_DECAY_LORA, w)), const((2, w)), const((2, RWKV_AAA_LORA, w)),
                  const((RWKV_GATE_LORA, w)), const((1, w)), const((1, w)), const((1, w))],
        out_specs=[rows_spec, tok(w), tok(w), tok(w)],
        out_shape=[sd(N_SCAN_ROWS), sd(), sd(), sd()],
        compiler_params=_params("parallel", "parallel"),
        name="rwkv_prep",
    )(u, u, u, u, u, u, mu[:, :2 * w], mu[:, 2 * w:], w0, w2, a0, a2, g2, row(k_k), row(k_a), row(r_k))


def _transpose_tokens(z_ref, scr, bsz):
    w = RWKV_W
    for b in range(bsz):
        scr[b * w:(b + 1) * w, :] = z_ref[b].T
    if bsz * w < scr.shape[0]:
        scr[bsz * w:, :] = jnp.zeros((scr.shape[0] - bsz * w, scr.shape[1]), F32)


def _layout_rows_kernel(z_ref, o_ref, scr, *, bsz):
    n = RWKV_HEAD_DIM
    _transpose_tokens(z_ref, scr, bsz)
    for j in range(n):
        x = scr[pl.ds(j, LANES // 2, stride=n), :]
        o_ref[j] = jnp.concatenate([x, x], axis=0).T


def _layout_v_kernel(z_ref, o_ref, scr, *, bsz):
    n = RWKV_HEAD_DIM
    ts = z_ref.shape[1]
    _transpose_tokens(z_ref, scr, bsz)
    for i in range(n // 2):
        x0 = scr[pl.ds(i, LANES // 2, stride=n), :]
        x1 = scr[pl.ds(n // 2 + i, LANES // 2, stride=n), :]
        o_ref[pl.ds(i, ts, stride=n // 2), :] = jnp.concatenate([x0, x1], axis=0).T


def _scan_layout(rows, v):
    g, bsz, t, w = rows.shape
    n = RWKV_HEAD_DIM
    ts = LANES
    scr = pltpu.VMEM((LANES // 2 * n, ts), F32)
    rows_l = pl.pallas_call(
        functools.partial(_layout_rows_kernel, bsz=bsz),
        grid=(g, t // ts),
        in_specs=[pl.BlockSpec((None, bsz, ts, w), lambda k, i: (k, 0, i, 0))],
        out_specs=pl.BlockSpec((None, n, ts, LANES), lambda k, i: (k, 0, i, 0)),
        out_shape=jax.ShapeDtypeStruct((g, n, t, LANES), F32),
        scratch_shapes=[scr],
        compiler_params=_params("parallel", "parallel"),
        name="rwkv_layout_rows",
    )(rows)
    v_l = pl.pallas_call(
        functools.partial(_layout_v_kernel, bsz=bsz),
        grid=(t // ts,),
        in_specs=[pl.BlockSpec((bsz, ts, w), lambda i: (0, i, 0))],
        out_specs=pl.BlockSpec((ts * n // 2, LANES), lambda i: (i, 0)),
        out_shape=jax.ShapeDtypeStruct((t * n // 2, LANES), F32),
        scratch_shapes=[scr],
        compiler_params=_params("parallel"),
        name="rwkv_layout_v",
    )(v)
    return rows_l, v_l


def _rwkv_scan_kernel(r_ref, a_ref, w_ref, k_ref, b_ref, v_ref, y_ref, s_ref, sa_ref):
    n = RWKV_HEAD_DIM
    half = n // 2
    ts = r_ref.shape[1]
    fwd = pl.program_id(0) == 0

    @pl.when(pl.program_id(1) == 0)
    def _():
        s_ref[...] = jnp.zeros_like(s_ref)

    t_first = jnp.where(fwd, 0, ts - 1)
    acc = jnp.zeros((half, LANES), F32)
    for j in range(n):
        acc = acc + s_ref[j] * a_ref[j, pl.ds(t_first, 1), :]
    sa_ref[...] = acc

    def step(m, carry):
        t = jnp.where(fwd, m, ts - 1 - m)
        tn = jnp.clip(jnp.where(fwd, t + 1, t - 1), 0, ts - 1)
        sa = sa_ref[...]
        v = v_ref[t]
        y = jnp.zeros((half, LANES), F32)
        sa_next = jnp.zeros((half, LANES), F32)
        for j in range(n):
            s = (s_ref[j] * w_ref[j, pl.ds(t, 1), :] + sa * b_ref[j, pl.ds(t, 1), :]) + v * k_ref[j, pl.ds(t, 1), :]
            s_ref[j] = s
            y = y + s * r_ref[j, pl.ds(t, 1), :]
            sa_next = sa_next + s * a_ref[j, pl.ds(tn, 1), :]
        y_ref[t] = y
        sa_ref[...] = sa_next
        return carry

    lax.fori_loop(0, ts, step, 0)


def _rwkv_scan(rows, v, n_ctx):
    _, n, t, lanes = rows.shape
    ts = SCAN_STEPS
    nb = t // ts
    ncb = n_ctx // ts

    def blk(d, s):
        back = jnp.where(s < ncb, ncb - 1 - s, nb - 1 - (s - ncb))
        return jnp.where(d == 0, s, back)

    shared = lambda kind: pl.BlockSpec((None, n, ts, lanes), lambda d, s: (kind, 0, blk(d, s), 0))
    per_dir = lambda kind: pl.BlockSpec((None, n, ts, lanes), lambda d, s: (kind + d, 0, blk(d, s), 0))
    return pl.pallas_call(
        _rwkv_scan_kernel,
        grid=(2, nb),
        in_specs=[shared(ROW_R), shared(ROW_A), per_dir(ROW_W), per_dir(ROW_K), per_dir(ROW_B),
                  pl.BlockSpec((ts, n // 2, lanes), lambda d, s: (blk(d, s), 0, 0))],
        out_specs=pl.BlockSpec((None, ts, n // 2, lanes), lambda d, s: (d, blk(d, s), 0, 0)),
        out_shape=jax.ShapeDtypeStruct((2, t, n // 2, lanes), F32),
        scratch_shapes=[pltpu.VMEM((n, n // 2, lanes), F32), pltpu.VMEM((n // 2, lanes), F32)],
        compiler_params=_params("arbitrary", "arbitrary"),
        name="rwkv_scan",
    )(rows, rows, rows, rows, rows, v)


def _rwkv_readout_kernel(yf_ref, yb_ref, bonus_ref, gate_ref, g_ref, b_ref, o_ref, scr, *, bsz):
    n = RWKV_HEAD_DIM
    w = RWKV_W
    ts = o_ref.shape[1]
    for i in range(n // 2):
        rows = pl.ds(i, ts, stride=n // 2)
        yt = (yf_ref[rows, :] + yb_ref[rows, :]).T
        scr[pl.ds(i, LANES // 2, stride=n), :] = yt[:LANES // 2]
        scr[pl.ds(n // 2 + i, LANES // 2, stride=n), :] = yt[LANES // 2:]
    ones = _group_ones(w, n)
    inv = 1.0 / n
    for b in range(bsz):
        y = scr[b * w:(b + 1) * w, :].T
        mean = _dot2_exact_rhs(y, ones) * inv
        yc = y - mean
        var = _dot2_exact_rhs(yc * yc, ones) * inv
        yn = yc * lax.rsqrt(var + RWKV_GN_EPS) * g_ref[...] + b_ref[...]
        o_ref[b] = ((yn + bonus_ref[b]) * gate_ref[b]).astype(o_ref.dtype)


def _rwkv_readout(y, bonus, gate, ln_g, ln_b):
    bsz, t, w = bonus.shape
    n = RWKV_HEAD_DIM
    ts = LANES
    tok = pl.BlockSpec((bsz, ts, w), lambda i: (0, i, 0))
    vec = pl.BlockSpec((1, w), lambda i: (0, 0))
    return pl.pallas_call(
        functools.partial(_rwkv_readout_kernel, bsz=bsz),
        grid=(t // ts,),
        in_specs=[pl.BlockSpec((None, ts * n // 2, LANES), lambda i: (0, i, 0)),
                  pl.BlockSpec((None, ts * n // 2, LANES), lambda i: (1, i, 0)), tok, tok, vec, vec],
        out_specs=tok,
        out_shape=jax.ShapeDtypeStruct((bsz, t, w), BF16),
        scratch_shapes=[pltpu.VMEM((LANES // 2 * n, ts), F32)],
        compiler_params=_params("parallel"),
        name="rwkv_readout",
    )(y, y, bonus, gate, ln_g.reshape(1, w), ln_b.reshape(1, w))


def _merge_kernel(ya_ref, yr_ref, yw_ref, g0_ref, g1_ref, g2_ref, x_ref, gate1_ref, a2_ref, b2_ref,
                  wb_ref, wo_ref, wr_ref, br_ref,
                  x_out, h_out, ids_out, wts_out, cnt_out):
    first = jnp.logical_and(pl.program_id(0) == 0, pl.program_id(1) == 0)

    @pl.when(first)
    def _():
        cnt_out[...] = jnp.zeros_like(cnt_out)

    merged = (jax.nn.sigmoid(g0_ref[...]) * _dot(ya_ref[...], wb_ref[0])
              + jax.nn.sigmoid(g1_ref[...]) * _dot(yr_ref[...], wb_ref[1])
              + jax.nn.sigmoid(g2_ref[...]) * _dot(yw_ref[...], wb_ref[2]))
    x = x_ref[...] + gate1_ref[...] * _dot(merged.astype(BF16), wo_ref[...])
    x_out[...] = x
    h = x * lax.rsqrt(jnp.mean(x * x, axis=-1, keepdims=True) + NORM_EPS) * a2_ref[...] + b2_ref[...]
    h_out[...] = h.astype(BF16)

    tm = x.shape[0]
    logits = _dot3(h, wr_ref[...]) + br_ref[...]
    lane = lax.broadcasted_iota(jnp.int32, (tm, LANES), 1)
    lane_f = lane.astype(F32)
    neg = -jnp.inf
    big = float(LANES)
    first = lambda hit: jnp.min(jnp.where(hit, lane_f, big), axis=-1, keepdims=True).astype(jnp.int32)
    is_grp = jnp.logical_and(lane >= MOE_EXPERTS, lane < MOE_EXPERTS + MOE_GROUPS)
    gl = jnp.where(is_grp, logits, neg)
    gmax = jnp.max(gl, axis=-1, keepdims=True)
    gidx = first(gl == gmax) - MOE_EXPERTS
    p_grp = 1.0 / jnp.sum(jnp.where(is_grp, jnp.exp(gl - gmax), 0.0), axis=-1, keepdims=True)
    in_grp = jnp.logical_and(lane < MOE_EXPERTS, lane // MOE_EXPERTS_PER_GROUP == gidx)
    el = jnp.where(in_grp, logits, neg)
    v1 = jnp.max(el, axis=-1, keepdims=True)
    i1 = first(el == v1)
    el2 = jnp.where(lane == i1, neg, el)
    v2 = jnp.max(el2, axis=-1, keepdims=True)
    i2 = first(el2 == v2)
    e2 = jnp.exp(v2 - v1)
    w1 = p_grp / (1.0 + e2)
    w2 = p_grp * e2 / (1.0 + e2)
    wts_out[...] = jnp.where(lane == 0, w1, jnp.where(lane == 1, w2, 0.0))

    onehot = jnp.where(jnp.logical_or(lane == i1, lane == i2), 1.0, 0.0)
    rr = lax.broadcasted_iota(jnp.int32, (tm, tm), 0)
    cc = lax.broadcasted_iota(jnp.int32, (tm, tm), 1)
    below = jnp.where(cc < rr, 1.0, 0.0).astype(BF16)
    before = _dot(below, onehot.astype(BF16)) + cnt_out[0:1, :]
    rank1 = jnp.sum(jnp.where(lane == i1, before, 0.0), axis=-1, keepdims=True).astype(jnp.int32)
    rank2 = jnp.sum(jnp.where(lane == i2, before, 0.0), axis=-1, keepdims=True).astype(jnp.int32)
    ids_out[...] = jnp.where(lane == 0, i1, jnp.where(lane == 1, i2, jnp.where(lane == 2, rank1,
                                                                                  jnp.where(lane == 3, rank2, 0))))
    cnt_out[...] = cnt_out[...] + jnp.sum(onehot, axis=0, keepdims=True)


def _merge(ya, yr, yw, u, x, gate1, a2, b2, w_branch, w_out, w_router, b_router, part):
    bsz, t, d = x.shape
    nb = bsz // MOE_PARTS
    b0 = part * nb
    tm = ROW_TILE
    sel = lambda b, i: (2 * (b + b0) + jnp.minimum(i, 1), 0, 0)
    tok = lambda width, blk=0: pl.BlockSpec((None, tm, width), lambda b, i: (b + b0, i, blk))
    own = lambda width: pl.BlockSpec((None, tm, width), lambda b, i: (b, i, 0))
    const = lambda shape: pl.BlockSpec(shape, lambda b, i: (0,) * len(shape))
    mod = pl.BlockSpec((None, 1, d), sel)
    return pl.pallas_call(
        _merge_kernel,
        grid=(nb, t // tm),
        in_specs=[tok(BRANCH_W), tok(BRANCH_W), tok(BRANCH_W), tok(d, 0), tok(d, 1), tok(d, 2), tok(d),
                  mod, mod, mod,
                  const((N_BRANCH, BRANCH_W, d)), const((d, d)), const((d, LANES)), const((1, LANES))],
        out_specs=[tok(d), own(d), own(LANES), own(LANES), const((8, LANES))],
        out_shape=[jax.ShapeDtypeStruct((bsz, t, d), F32), jax.ShapeDtypeStruct((nb, t, d), BF16),
                   jax.ShapeDtypeStruct((nb, t, LANES), jnp.int32), jax.ShapeDtypeStruct((nb, t, LANES), F32),
                   jax.ShapeDtypeStruct((8, LANES), F32)],
        input_output_aliases={6: 0},
        compiler_params=_params("arbitrary", "arbitrary"),
        name="merge_router",
    )(ya, yr, yw, u, u, u, x, gate1, a2, b2, w_branch, w_out, w_router, b_router)


def _moe_kernel(be_ref, na_ref, x_ref, wg_ref, wu_ref, wd_ref, o_ref, wg_s, wu_s, wd_s):
    i = pl.program_id(0)
    active = i < na_ref[0]
    new_expert = jnp.logical_or(i == 0, be_ref[i] != be_ref[jnp.maximum(i - 1, 0)])

    @pl.when(jnp.logical_and(active, new_expert))
    def _():
        wg_s[...] = wg_ref[...].astype(BF16)
        wu_s[...] = wu_ref[...].astype(BF16)
        wd_s[...] = wd_ref[...].astype(BF16)

    @pl.when(active)
    def _():
        x = x_ref[...]
        act = _silu(_dot(x, wg_s[...])) * _dot(x, wu_s[...])
        o_ref[...] = _dot(act.astype(BF16), wd_s[...]).astype(o_ref.dtype)

    @pl.when(i >= na_ref[0])
    def _():
        o_ref[...] = jnp.zeros_like(o_ref)


def _moe_experts(buf, block_expert, n_active, w_gate, w_up, w_down, layer):
    rows, d = buf.shape
    hid = w_gate.shape[-1]
    grid_spec = pltpu.PrefetchScalarGridSpec(
        num_scalar_prefetch=2,
        grid=(rows // MOE_BLOCK,),
        in_specs=[pl.BlockSpec((MOE_BLOCK, d), lambda i, be, na: (i, 0)),
                  pl.BlockSpec((None, None, d, hid), lambda i, be, na: (layer, be[i], 0, 0)),
                  pl.BlockSpec((None, None, d, hid), lambda i, be, na: (layer, be[i], 0, 0)),
                  pl.BlockSpec((None, None, hid, d), lambda i, be, na: (layer, be[i], 0, 0))],
        out_specs=pl.BlockSpec((MOE_BLOCK, d), lambda i, be, na: (i, 0)),
        scratch_shapes=[pltpu.VMEM((d, hid), BF16), pltpu.VMEM((d, hid), BF16), pltpu.VMEM((hid, d), BF16)])
    return pl.pallas_call(
        _moe_kernel,
        grid_spec=grid_spec,
        out_shape=jax.ShapeDtypeStruct((rows, d), F32),
        compiler_params=_params("arbitrary"),
        name="moe_experts",
    )(block_expert, n_active, buf, w_gate, w_up, w_down)


def _combine_kernel(x_ref, *refs):
    y_refs, (w_ref, g_ref, o_ref) = refs[:-3], refs[-3:]
    tm = x_ref.shape[0]
    w = w_ref[...]
    for c, y_ref in enumerate(y_refs):
        cols = slice(c * LANES, (c + 1) * LANES)
        y = y_ref[pl.ds(0, tm, stride=2), :] * w[:, 0:1] + y_ref[pl.ds(1, tm, stride=2), :] * w[:, 1:2]
        o_ref[:, cols] = x_ref[:, cols] + g_ref[:, cols] * y


def _combine(x, y_pairs, wts, gate2, part):
    bsz, t, d = x.shape
    nb = y_pairs.shape[0]
    b0 = part * nb
    tm = ROW_TILE
    sel = lambda b, i: (2 * (b + b0) + jnp.minimum(i, 1), 0, 0)
    x_spec = pl.BlockSpec((None, tm, d), lambda b, i: (b + b0, i, 0))
    return pl.pallas_call(
        _combine_kernel,
        grid=(nb, t // tm),
        in_specs=[x_spec]
                 + [pl.BlockSpec((None, 2 * tm, LANES), functools.partial(lambda c, b, i: (b, i, c), c))
                    for c in range(d // LANES)]
                 + [pl.BlockSpec((None, tm, LANES), lambda b, i: (b, i, 0)), pl.BlockSpec((None, 1, d), sel)],
        out_specs=x_spec,
        out_shape=jax.ShapeDtypeStruct((bsz, t, d), F32),
        input_output_aliases={0: 0},
        compiler_params=_params("parallel", "parallel"),
        name="moe_combine",
    )(x, *([y_pairs] * (d // LANES)), wts, gate2)


def _moe(h, ids, wts, counts, w_gate, w_up, w_down, layer):
    bsz, t, d = h.shape
    n_tok = bsz * t
    n_pair = 2 * n_tok
    n_blocks = -(-n_pair // MOE_BLOCK) + MOE_EXPERTS
    counts = counts[0, :MOE_EXPERTS].astype(jnp.int32)
    padded = (counts + MOE_BLOCK - 1) // MOE_BLOCK * MOE_BLOCK
    pad_end = jnp.cumsum(padded)
    pad_start = pad_end - padded
    expert = ids[..., 0:2].reshape(n_pair)
    rank = ids[..., 2:4].reshape(n_pair)
    dest = pad_start[expert] + rank
    token = jnp.arange(n_pair, dtype=jnp.int32) // 2
    src = jnp.zeros((n_blocks * MOE_BLOCK,), jnp.int32).at[dest].set(token, unique_indices=True)
    block_start = jnp.arange(n_blocks, dtype=jnp.int32) * MOE_BLOCK
    block_expert = jnp.minimum(jnp.sum((pad_end[None, :] <= block_start[:, None]).astype(jnp.int32), axis=1),
                               MOE_EXPERTS - 1).astype(jnp.int32)
    n_active = (pad_end[-1:] // MOE_BLOCK).astype(jnp.int32)
    buf = jnp.take(h.reshape(n_tok, d), src, axis=0)
    yb = _moe_experts(buf, block_expert, n_active, w_gate, w_up, w_down, layer)
    return jnp.take(yb, dest, axis=0).reshape(bsz, 2 * t, d)


def _rope_tables(n_ctx, n_lat, head_dim):
    rows = n_lat // GRID_W
    row = jnp.broadcast_to(jnp.arange(rows, dtype=F32)[:, None], (rows, GRID_W)).reshape(-1)
    col = jnp.broadcast_to(jnp.arange(GRID_W, dtype=F32)[None, :], (rows, GRID_W)).reshape(-1)
    quarter = head_dim // 4
    inv_freq = ROPE_THETA ** (-jnp.arange(quarter, dtype=F32) / quarter)
    ang = jnp.stack([row[:, None] * inv_freq, col[:, None] * inv_freq], axis=1)
    cos, sin = jnp.cos(ang), jnp.sin(ang)
    cos_t = jnp.stack([cos, cos], axis=2).reshape(n_lat, head_dim)
    sin_t = jnp.stack([-sin, sin], axis=2).reshape(n_lat, head_dim)
    cos_t = jnp.concatenate([jnp.ones((n_ctx, head_dim), F32), cos_t], axis=0)
    sin_t = jnp.concatenate([jnp.zeros((n_ctx, head_dim), F32), sin_t], axis=0)
    rep = LANES // head_dim
    return jnp.tile(cos_t, (1, rep)), jnp.tile(sin_t, (1, rep))


def kernel(x, c, ctx, c_ctx, ada_w, ada_b, norm1_g, norm2_g, w_in, att_qn_g, att_kn_g, ret_decay_logit, ret_gn_g, rwkv_mu, rwkv_w0, rwkv_w2, rwkv_a0, rwkv_a2, rwkv_g2, rwkv_k_k, rwkv_k_a, rwkv_r_k, rwkv_ln_g, rwkv_ln_b, w_branch, w_out, router_grp_w, router_grp_b, router_exp_w, router_exp_b, moe_w_gate, moe_w_up, moe_w_down):
    bsz, n_lat, d = x.shape
    n_ctx = ctx.shape[1]
    depth = ada_w.shape[0]
    assert d == D_MODEL and n_ctx == ROW_TILE and n_lat % ROW_TILE == 0 and n_lat % GRID_W == 0
    assert 2 * bsz * RWKV_HEADS <= LANES
    t_all = n_ctx + n_lat
    assert t_all % LANES == 0 and n_ctx % SCAN_STEPS == 0

    att_cos, att_sin = _rope_tables(n_ctx, n_lat, ATT_HEAD_DIM)
    ret_cos, ret_sin = _rope_tables(n_ctx, n_lat, RET_HEAD_DIM)

    rows = -(-(bsz + 1) // 8) * 8
    cvec = jnp.zeros((rows, d), F32).at[:bsz].set(c).at[bsz].set(c_ctx)
    mods = _modulation(cvec, ada_w, ada_b)
    perm = _column_permutation()

    xs = jnp.concatenate([ctx, x], axis=1)
    for layer in range(depth):
        m = mods[layer].reshape(rows, 6, d)
        pick = lambda j: jnp.stack([jnp.broadcast_to(m[bsz, j], (bsz, d)), m[:bsz, j]], axis=1).reshape(2 * bsz, 1, d)
        sh1, sc1, g1, sh2, sc2, g2 = (pick(j) for j in range(6))
        w_l = w_in[layer][:, perm].astype(BF16)
        u = _in_proj(xs, norm1_g[layer] * (1.0 + sc1), sh1, w_l)

        q_att, kv_att = _att_prep(u, att_cos, att_sin, att_qn_g[layer], att_kn_g[layer])
        ya = _attention(q_att, kv_att, n_ctx)

        log_gamma = jax.nn.log_sigmoid(ret_decay_logit[layer].astype(F32))
        yr = _retention(u, ret_cos, ret_sin, log_gamma, ret_gn_g[layer], n_ctx)

        rows_t, v_t, gate, bonus = _rwkv_prep(
            u, rwkv_mu[layer], rwkv_w0[layer], rwkv_w2[layer], rwkv_a0[layer], rwkv_a2[layer], rwkv_g2[layer],
            rwkv_k_k[layer], rwkv_k_a[layer], rwkv_r_k[layer].reshape(-1))
        rows_s, v_s = _scan_layout(rows_t, v_t)
        y_scan = _rwkv_scan(rows_s, v_s.reshape(t_all, RWKV_HEAD_DIM // 2, LANES), n_ctx)
        yw = _rwkv_readout(y_scan.reshape(2, t_all * RWKV_HEAD_DIM // 2, LANES), bonus, gate,
                           rwkv_ln_g[layer], rwkv_ln_b[layer])

        w_router = jnp.zeros((d, LANES), F32).at[:, :MOE_EXPERTS].set(router_exp_w[layer]).at[
            :, MOE_EXPERTS:MOE_EXPERTS + MOE_GROUPS].set(router_grp_w[layer])
        b_router = jnp.zeros((1, LANES), F32).at[0, :MOE_EXPERTS].set(router_exp_b[layer]).at[
            0, MOE_EXPERTS:MOE_EXPERTS + MOE_GROUPS].set(router_grp_b[layer])
        routed = []
        for part in range(MOE_PARTS):
            xs, h2, ids, wts, counts = _merge(
                ya, yr, yw, u, xs, g1, norm2_g[layer] * (1.0 + sc2), sh2,
                w_branch[layer].astype(BF16), w_out[layer].astype(BF16), w_router, b_router, part)
            routed.append((h2, ids, wts, counts))
        pairs = [_moe(h2, ids, wts, counts, moe_w_gate, moe_w_up, moe_w_down, layer)
                 for h2, ids, wts, counts in routed]
        for part in range(MOE_PARTS):
            xs = _combine(xs, pairs[part], routed[part][2], g2, part)
    return xs[:, n_ctx:]
```

```python
import functools

import jax
import jax.numpy as jnp
from jax import lax
from jax.experimental import pallas as pl
from jax.experimental.pallas import tpu as pltpu

F32 = jnp.float32
BF16 = jnp.bfloat16

D_MODEL = 1024
GRID_W = 64
NORM_EPS = 1e-6
ROPE_THETA = 10000.0

ATT_HEADS = 8
ATT_KV_HEADS = 2
ATT_HEAD_DIM = 64
ATT_GROUP = ATT_HEADS // ATT_KV_HEADS
ATT_W = ATT_HEADS * ATT_HEAD_DIM
ATT_KV_W = ATT_KV_HEADS * ATT_HEAD_DIM

RET_HEADS = 4
RET_HEAD_DIM = 128
RET_CHUNK = 128
RET_W = RET_HEADS * RET_HEAD_DIM

RWKV_HEADS = 8
RWKV_HEAD_DIM = 64
RWKV_W = RWKV_HEADS * RWKV_HEAD_DIM
RWKV_DECAY_LORA = 64
RWKV_AAA_LORA = 64
RWKV_GATE_LORA = 128
RWKV_GN_EPS = 64e-5
RWKV_DECAY_SCALE = 0.6065306597126334
RWKV_COLS = 3 * RWKV_W + RWKV_DECAY_LORA + RWKV_AAA_LORA + RWKV_GATE_LORA

N_BRANCH = 3
BRANCH_W = 512
IN_COLS = ATT_W + 2 * ATT_KV_W + 4 * RET_W + RWKV_COLS + N_BRANCH * D_MODEL

MOE_GROUPS = 4
MOE_EXPERTS_PER_GROUP = 8
MOE_EXPERTS = MOE_GROUPS * MOE_EXPERTS_PER_GROUP
MOE_HIDDEN = 512
MOE_BLOCK = 256
MOE_PARTS = 1

LANES = 128
ROW_TILE = 256
ATT_Q_TILE = 128
ATT_KEY_TILE = 256
ATT_Q_SCALE = ATT_HEAD_DIM ** -0.5 * 1.4426950408889634
ATT_BOUND_MARGIN = 1.01
ATT_EXP2_RANGE = 100.0
SCAN_STEPS = 64
VMEM_LIMIT = 56 * 1024 * 1024

ROW_R, ROW_A, ROW_W, ROW_K, ROW_B, N_SCAN_ROWS = 0, 1, 2, 4, 6, 8

U_GATE = 0
U_RWKV_RK = U_GATE + N_BRANCH * D_MODEL
U_RET = U_RWKV_RK + 2 * RWKV_W
U_ATT = U_RET + 4 * RET_W
U_RWKV_REST = U_ATT + ATT_W + 2 * ATT_KV_W
ATT_COLS = ATT_W + 2 * ATT_KV_W
RWKV_REST_COLS = RWKV_COLS - 2 * RWKV_W


def _column_permutation():
    o_att = 0
    o_ret = ATT_COLS
    o_rwkv = o_ret + 4 * RET_W
    o_gate = o_rwkv + RWKV_COLS
    parts = [jnp.arange(o_gate, o_gate + N_BRANCH * D_MODEL),
             jnp.arange(o_rwkv, o_rwkv + 2 * RWKV_W),
             jnp.arange(o_ret, o_ret + 4 * RET_W),
             jnp.arange(o_att, o_att + ATT_COLS),
             jnp.arange(o_rwkv + 2 * RWKV_W, o_rwkv + RWKV_COLS)]
    return jnp.concatenate(parts)


def _params(*sem):
    return pltpu.CompilerParams(dimension_semantics=sem, vmem_limit_bytes=VMEM_LIMIT)


def _dot(a, b):
    return jnp.dot(a, b, preferred_element_type=F32)


def _dot_nt(a, b):
    return lax.dot_general(a, b, (((1,), (1,)), ((), ())), preferred_element_type=F32)


def _split(a):
    hi = a.astype(BF16)
    lo = (a - hi.astype(F32)).astype(BF16)
    return hi, lo


def _dot3(a, b):
    ah, al = _split(a)
    bh, bl = _split(b)
    return _dot(ah, bh) + (_dot(al, bh) + _dot(ah, bl))


def _dot2_exact_rhs(a, b_bf16):
    ah, al = _split(a)
    return _dot(ah, b_bf16) + _dot(al, b_bf16)


def _group_ones(width, group):
    r = lax.broadcasted_iota(jnp.int32, (width, width), 0) // group
    c = lax.broadcasted_iota(jnp.int32, (width, width), 1) // group
    return jnp.where(r == c, 1.0, 0.0).astype(BF16)


def _silu(x):
    return x * jax.nn.sigmoid(x)


def _swap_halves(x, quarter):
    n = x.shape[-1]
    lane = lax.broadcasted_iota(jnp.int32, x.shape, x.ndim - 1)
    up = pltpu.roll(x, n - quarter, x.ndim - 1)
    down = pltpu.roll(x, quarter, x.ndim - 1)
    return jnp.where(lane % (2 * quarter) < quarter, up, down)


def _mod_kernel(c_ref, w_ref, b_ref, o_ref):
    o_ref[...] = _dot3(_silu(c_ref[...]), w_ref[...]) + b_ref[...]


def _modulation(cvec, ada_w, ada_b):
    depth, d, cols = ada_w.shape
    rows = cvec.shape[0]
    tn = 1536
    return pl.pallas_call(
        _mod_kernel,
        grid=(depth, cols // tn),
        in_specs=[pl.BlockSpec((rows, d), lambda l, j: (0, 0)),
                  pl.BlockSpec((None, d, tn), lambda l, j: (l, 0, j)),
                  pl.BlockSpec((None, 1, tn), lambda l, j: (l, 0, j))],
        out_specs=pl.BlockSpec((None, rows, tn), lambda l, j: (l, 0, j)),
        out_shape=jax.ShapeDtypeStruct((depth, rows, cols), F32),
        compiler_params=_params("parallel", "parallel"),
        name="modulation",
    )(cvec, ada_w, ada_b.reshape(depth, 1, cols))


def _in_proj_kernel(x_ref, a_ref, b_ref, w_ref, o_ref):
    x = x_ref[...]
    ms = jnp.mean(x * x, axis=-1, keepdims=True)
    h = x * lax.rsqrt(ms + NORM_EPS) * a_ref[...] + b_ref[...]
    o_ref[...] = _dot(h.astype(BF16), w_ref[...])


def _in_proj(x, mod_a, mod_b, w):
    bsz, t, d = x.shape
    cols = w.shape[1]
    tm, tn = ROW_TILE, cols // 2
    sel = lambda j, b, i: (2 * b + jnp.minimum(i, 1), 0, 0)
    return pl.pallas_call(
        _in_proj_kernel,
        grid=(cols // tn, bsz, t // tm),
        in_specs=[pl.BlockSpec((None, tm, d), lambda j, b, i: (b, i, 0)),
                  pl.BlockSpec((None, 1, d), sel),
                  pl.BlockSpec((None, 1, d), sel),
                  pl.BlockSpec((d, tn), lambda j, b, i: (0, j))],
        out_specs=pl.BlockSpec((None, tm, tn), lambda j, b, i: (b, i, j)),
        out_shape=jax.ShapeDtypeStruct((bsz, t, cols), F32),
        compiler_params=_params("parallel", "parallel", "parallel"),
        name="in_proj",
    )(x, mod_a, mod_b, w)


def _att_prep_kernel(u_ref, cos_ref, sin_ref, qg_ref, kg_ref, shift_ref, q_out, kv_out):
    hd = ATT_HEAD_DIM
    ones = _group_ones(LANES, hd)
    cos = cos_ref[...]
    sin = sin_ref[...]
    lane = lax.broadcasted_iota(jnp.int32, cos.shape, 1)
    low = lane < hd

    def two_heads(y, extra):
        fill = jnp.where(lane == hd, extra, 0.0)
        return jnp.where(low, y, fill), jnp.where(low, pltpu.roll(y, hd, 1), fill)

    n_q = ATT_W // LANES
    for j in range(n_q + 1):
        x = u_ref[:, j * LANES:(j + 1) * LANES]
        is_q = j < n_q
        gain = qg_ref[...] if is_q else kg_ref[...]
        ms = _dot2_exact_rhs(x * x, ones) * (1.0 / hd)
        y = x * lax.rsqrt(ms + NORM_EPS) * gain
        y = y * cos + _swap_halves(y, hd // 4) * sin
        if is_q:
            y = y * ATT_Q_SCALE
        out, base = (q_out, 2 * j) if is_q else (kv_out, 0)
        for h, yh in enumerate(two_heads(y, shift_ref[...] if is_q else 1.0)):
            out[:, (base + h) * LANES:(base + h + 1) * LANES] = yh.astype(BF16)
    v = u_ref[:, ATT_W + ATT_KV_W:]
    for h, vh in enumerate(two_heads(v, 1.0)):
        kv_out[:, (2 + h) * LANES:(3 + h) * LANES] = vh.astype(BF16)


def _att_prep(u, cos, sin, qn_g, kn_g, shift):
    bsz, t, _ = u.shape
    tm = ROW_TILE
    rep = LANES // ATT_HEAD_DIM
    shift = jnp.broadcast_to(shift.astype(F32), (1, LANES))
    return pl.pallas_call(
        _att_prep_kernel,
        grid=(bsz, t // tm),
        in_specs=[pl.BlockSpec((None, tm, ATT_COLS), lambda b, i: (b, i, U_ATT // ATT_COLS)),
                  pl.BlockSpec((tm, LANES), lambda b, i: (i, 0)),
                  pl.BlockSpec((tm, LANES), lambda b, i: (i, 0)),
                  pl.BlockSpec((1, LANES), lambda b, i: (0, 0)),
                  pl.BlockSpec((1, LANES), lambda b, i: (0, 0)),
                  pl.BlockSpec((1, LANES), lambda b, i: (0, 0))],
        out_specs=[pl.BlockSpec((None, tm, ATT_HEADS * LANES), lambda b, i: (b, i, 0)),
                   pl.BlockSpec((None, tm, 2 * ATT_KV_HEADS * LANES), lambda b, i: (b, i, 0))],
        out_shape=[jax.ShapeDtypeStruct((bsz, t, ATT_HEADS * LANES), BF16),
                   jax.ShapeDtypeStruct((bsz, t, 2 * ATT_KV_HEADS * LANES), BF16)],
        compiler_params=_params("parallel", "parallel"),
        name="att_prep",
    )(u, cos, sin, jnp.tile(qn_g, rep).reshape(1, LANES), jnp.tile(kn_g, rep).reshape(1, LANES), shift)


def _att_kernel(bounded_ref, q_ref, kv_ref, o_ref, *, n_ctx, tq):
    i = pl.program_id(1)
    hd = ATT_HEAD_DIM
    k_ref = v_ref = kv_ref

    def run(n_keys, bounded):
        tk = ATT_KEY_TILE
        for g in range(ATT_KV_HEADS):
            q = jnp.concatenate(
                [q_ref[:, (ATT_GROUP * g + h) * LANES:(ATT_GROUP * g + h + 1) * LANES] for h in range(ATT_GROUP)],
                axis=0)
            scores = lambda c: _dot_nt(q, k_ref[c * tk:(c + 1) * tk, g * LANES:(g + 1) * LANES])
            if not bounded:
                m = jnp.full((ATT_GROUP * tq, LANES), -jnp.inf, F32)
                for c in range(n_keys // tk):
                    s = scores(c)
                    for part in range(tk // LANES):
                        m = jnp.maximum(m, s[:, part * LANES:(part + 1) * LANES])
                m = jnp.max(m, axis=-1, keepdims=True)
            acc = jnp.zeros((ATT_GROUP * tq, LANES), F32)
            for c in range(n_keys // tk):
                p = jnp.exp2(scores(c) if bounded else scores(c) - m).astype(BF16)
                acc = acc + _dot(p, v_ref[c * tk:(c + 1) * tk, (ATT_KV_HEADS + g) * LANES:(ATT_KV_HEADS + g + 1) * LANES])
            o = acc[:, :hd] / acc[:, hd:hd + 1]
            for h in range(ATT_GROUP):
                c0 = (ATT_GROUP * g + h) * hd
                o_ref[:, c0:c0 + hd] = o[h * tq:(h + 1) * tq].astype(o_ref.dtype)

    is_ctx = i < n_ctx // tq
    bounded = bounded_ref[0] == 1

    @pl.when(is_ctx)
    def _():
        run(n_ctx, False)

    @pl.when(jnp.logical_and(jnp.logical_not(is_ctx), bounded))
    def _():
        run(k_ref.shape[0], True)

    @pl.when(jnp.logical_and(jnp.logical_not(is_ctx), jnp.logical_not(bounded)))
    def _():
        run(k_ref.shape[0], False)


def _attention(q, kv, bounded, n_ctx):
    bsz, t, _ = q.shape
    tq = ATT_Q_TILE
    kv_w = kv.shape[-1]
    assert n_ctx % ATT_KEY_TILE == 0 and t % ATT_KEY_TILE == 0
    grid_spec = pltpu.PrefetchScalarGridSpec(
        num_scalar_prefetch=1,
        grid=(bsz, t // tq),
        in_specs=[pl.BlockSpec((None, tq, ATT_HEADS * LANES), lambda b, i, f: (b, i, 0)),
                  pl.BlockSpec((None, t, kv_w), lambda b, i, f: (b, 0, 0))],
        out_specs=pl.BlockSpec((None, tq, ATT_W), lambda b, i, f: (b, i, 0)))
    return pl.pallas_call(
        functools.partial(_att_kernel, n_ctx=n_ctx, tq=tq),
        grid_spec=grid_spec,
        out_shape=jax.ShapeDtypeStruct((bsz, t, ATT_W), BF16),
        compiler_params=_params("parallel", "parallel"),
        name="attention",
    )(bounded, q, kv)


def _ret_kernel(q_ref, k_ref, v_ref, g_ref, cos_ref, sin_ref, lg_ref, gn_ref, o_ref,
                qs_ref, ks_ref, kvf_ref, kvb_ref, sf_ref, sb_ref, *, n_ctx):
    c = RET_CHUNK
    t = q_ref.shape[0]
    n_chunks = t // c
    n_cc = n_ctx // c
    quarter = RET_HEAD_DIM // 4
    scale = RET_HEAD_DIM ** -0.5
    lg_f = lg_ref[0]
    lg_b = lg_ref[1]
    row = lax.broadcasted_iota(jnp.int32, (c, c), 0)
    col = lax.broadcasted_iota(jnp.int32, (c, c), 1)
    rowf = row.astype(F32)
    lag = (row - col).astype(F32)

    def chunk(ci):
        r0 = pl.multiple_of(ci * c, c)
        cos = cos_ref[pl.ds(r0, c), :]
        sin = sin_ref[pl.ds(r0, c), :]
        q = q_ref[pl.ds(r0, c), :]
        k = k_ref[pl.ds(r0, c), :]
        q = q * cos + _swap_halves(q, quarter) * sin
        k = (k * cos + _swap_halves(k, quarter) * sin) * scale
        return r0, q, k, v_ref[pl.ds(r0, c), :]

    d_key_f = jnp.exp(lg_f * (c - 1.0 - rowf))
    d_key_b = jnp.exp(lg_b * rowf)
    d_query_f = jnp.exp(lg_f * (rowf + 1.0))
    d_query_b = jnp.exp(lg_b * (float(c) - rowf))
    d_chunk_f = jnp.exp(lg_f * float(c))
    d_chunk_b = jnp.exp(lg_b * float(c))
    d_intra = (jnp.where(lag >= 0, jnp.exp(lg_f * jnp.maximum(lag, 0.0)), 0.0)
               + jnp.where(lag <= 0, jnp.exp(lg_b * jnp.maximum(-lag, 0.0)), 0.0))

    def summaries(n, carry):
        r0, q, k, v = chunk(n)
        qs_ref[pl.ds(r0, c), :] = q.astype(BF16)
        ks_ref[pl.ds(r0, c), :] = k.astype(BF16)
        vb = v.astype(BF16)
        kvf_ref[n] = _dot((k * d_key_f).T.astype(BF16), vb)
        kvb_ref[n] = _dot((k * d_key_b).T.astype(BF16), vb)
        return carry

    lax.fori_loop(0, n_chunks, summaries, 0, unroll=2)

    def state_f(n, s):
        sf_ref[n] = s.astype(BF16)
        return d_chunk_f * s + kvf_ref[n]

    def state_b(n, s):
        ci = jnp.where(n < n_cc, n_cc - 1 - n, n_chunks - 1 - (n - n_cc))
        sb_ref[ci] = s.astype(BF16)
        return d_chunk_b * s + kvb_ref[ci]

    zero = jnp.zeros((RET_HEAD_DIM, RET_HEAD_DIM), F32)
    lax.fori_loop(0, n_chunks, state_f, zero)
    lax.fori_loop(0, n_chunks, state_b, zero)
    gn = gn_ref[...]

    def outputs(n, carry):
        r0 = pl.multiple_of(n * c, c)
        qb = qs_ref[pl.ds(r0, c), :]
        vb = v_ref[pl.ds(r0, c), :].astype(BF16)
        scores = _dot_nt(qb, ks_ref[pl.ds(r0, c), :]) * d_intra
        y = (_dot(scores.astype(BF16), vb) + _dot(qb, sf_ref[n]) * d_query_f) + _dot(qb, sb_ref[n]) * d_query_b
        yn = y * lax.rsqrt(jnp.mean(y * y, axis=-1, keepdims=True) + NORM_EPS) * gn
        o_ref[pl.ds(r0, c), :] = (_silu(g_ref[pl.ds(r0, c), :]) * yn).astype(o_ref.dtype)
        return carry

    lax.fori_loop(0, n_chunks, outputs, 0, unroll=2)


def _retention(u, cos, sin, log_gamma, gn_g, n_ctx):
    bsz, t, _ = u.shape
    hd = RET_HEAD_DIM
    base = U_RET // hd
    spec = lambda off: pl.BlockSpec((None, t, hd), lambda b, h: (b, 0, base + off * RET_HEADS + h))
    lg = jnp.broadcast_to(log_gamma[:, :, None, None], (2, RET_HEADS, 1, LANES)).astype(F32)
    return pl.pallas_call(
        functools.partial(_ret_kernel, n_ctx=n_ctx),
        grid=(bsz, RET_HEADS),
        in_specs=[spec(0), spec(1), spec(2), spec(3),
                  pl.BlockSpec((t, hd), lambda b, h: (0, 0)),
                  pl.BlockSpec((t, hd), lambda b, h: (0, 0)),
                  pl.BlockSpec((2, None, 1, LANES), lambda b, h: (0, h, 0, 0)),
                  pl.BlockSpec((1, hd), lambda b, h: (0, h))],
        out_specs=pl.BlockSpec((None, t, hd), lambda b, h: (b, 0, h)),
        out_shape=jax.ShapeDtypeStruct((bsz, t, RET_W), BF16),
        scratch_shapes=[pltpu.VMEM((t, hd), BF16), pltpu.VMEM((t, hd), BF16),
                        pltpu.VMEM((t // RET_CHUNK, hd, hd), F32), pltpu.VMEM((t // RET_CHUNK, hd, hd), F32),
                        pltpu.VMEM((t // RET_CHUNK, hd, hd), BF16), pltpu.VMEM((t // RET_CHUNK, hd, hd), BF16)],
        compiler_params=_params("parallel", "parallel"),
        name="retention",
    )(u, u, u, u, cos, sin, lg, gn_g.reshape(1, RET_W))


def _rwkv_prep_kernel(rk_ref, rk_prev_ref, rk_next_ref, rest_ref, rest_prev_ref, rest_next_ref,
                      mu_rk_ref, mu_rest_ref, w0_ref, w2_ref, a0_ref, a2_ref, g2_ref, kk_ref, ka_ref, rk_gain_ref,
                      rows_out, v_out, gate_out, bonus_out, *, n_tiles):
    i = pl.program_id(1)
    tm = rk_ref.shape[0]
    has_prev = jnp.logical_and(i != 0, i != 1)
    has_next = jnp.logical_and(i != 0, i != n_tiles - 1)

    def shifted(x_ref, prev_ref, next_ref, mu_ref):
        x = x_ref[...]
        rows = lax.broadcasted_iota(jnp.int32, x.shape, 0)
        halo_prev = jnp.where(has_prev, prev_ref[7:8, :], 0.0)
        halo_next = jnp.where(has_next, next_ref[0:1, :], 0.0)
        prev = jnp.where(rows == 0, halo_prev, pltpu.roll(x, 1, 0))
        nxt = jnp.where(rows == tm - 1, halo_next, pltpu.roll(x, tm - 1, 0))
        return x + (prev - x) * mu_ref[0:1, :] + (nxt - x) * mu_ref[1:2, :]

    rk = shifted(rk_ref, rk_prev_ref, rk_next_ref, mu_rk_ref)
    rest = shifted(rest_ref, rest_prev_ref, rest_next_ref, mu_rest_ref)
    w = RWKV_W
    r = rk[:, 0:w]
    k = rk[:, w:2 * w]
    v = rest[:, 0:w]
    xw = rest[:, w:w + RWKV_DECAY_LORA]
    xa = rest[:, w + RWKV_DECAY_LORA:w + RWKV_DECAY_LORA + RWKV_AAA_LORA]
    xg = rest[:, w + RWKV_DECAY_LORA + RWKV_AAA_LORA:]

    ones = _group_ones(w, RWKV_HEAD_DIM)
    kk = k * kk_ref[...]
    kk = kk * lax.rsqrt(jnp.maximum(_dot2_exact_rhs(kk * kk, ones), 1e-12))
    rows_out[ROW_R] = r
    rows_out[ROW_A] = -kk
    v_out[...] = v
    tw = jnp.tanh(xw)
    k_sum = jnp.zeros_like(k)
    for d in range(2):
        decay_rate = jax.nn.sigmoid(w0_ref[d:d + 1, :] + _dot3(tw, w2_ref[d])) * RWKV_DECAY_SCALE
        a = jax.nn.sigmoid(a0_ref[d:d + 1, :] + _dot3(xa, a2_ref[d]))
        k_d = k * (1.0 + (a - 1.0) * ka_ref[...])
        rows_out[ROW_W + d] = jnp.exp(-decay_rate)
        rows_out[ROW_K + d] = k_d
        rows_out[ROW_B + d] = kk * a
        k_sum = k_sum + k_d
    gate_out[...] = _dot3(jax.nn.sigmoid(xg), g2_ref[...])
    bonus_out[...] = _dot2_exact_rhs(r * k_sum * rk_gain_ref[...], ones) * v


def _rwkv_prep(u, mu, w0, w2, a0, a2, g2, k_k, k_a, r_k):
    bsz, t, _ = u.shape
    tm = ROW_TILE
    n_tiles = t // tm
    w = RWKV_W
    rk_blk = U_RWKV_RK // (2 * w)
    rest_blk = U_RWKV_REST // RWKV_REST_COLS
    sub = tm // 8
    n_sub = t // 8
    prev_idx = lambda b, i: jnp.maximum(i * sub - 1, 0)
    next_idx = lambda b, i: jnp.minimum((i + 1) * sub, n_sub - 1)
    row = lambda a: a.reshape(1, -1)
    const = lambda shape: pl.BlockSpec(shape, lambda b, i: (0,) * len(shape))
    tok = lambda width: pl.BlockSpec((None, tm, width), lambda b, i: (b, i, 0))
    rows_spec = pl.BlockSpec((N_SCAN_ROWS, None, tm, w), lambda b, i: (0, b, i, 0))
    sd = lambda *lead: jax.ShapeDtypeStruct((*lead, bsz, t, w), F32)
    return pl.pallas_call(
        functools.partial(_rwkv_prep_kernel, n_tiles=n_tiles),
        grid=(bsz, n_tiles),
        in_specs=[pl.BlockSpec((None, tm, 2 * w), lambda b, i: (b, i, rk_blk)),
                  pl.BlockSpec((None, 8, 2 * w), lambda b, i: (b, prev_idx(b, i), rk_blk)),
                  pl.BlockSpec((None, 8, 2 * w), lambda b, i: (b, next_idx(b, i), rk_blk)),
                  pl.BlockSpec((None, tm, RWKV_REST_COLS), lambda b, i: (b, i, rest_blk)),
                  pl.BlockSpec((None, 8, RWKV_REST_COLS), lambda b, i: (b, prev_idx(b, i), rest_blk)),
                  pl.BlockSpec((None, 8, RWKV_REST_COLS), lambda b, i: (b, next_idx(b, i), rest_blk)),
                  const((2, 2 * w)), const((2, RWKV_REST_COLS)),
                  const((2, w)), const((2, RWKV_DECAY_LORA, w)), const((2, w)), const((2, RWKV_AAA_LORA, w)),
                  const((RWKV_GATE_LORA, w)), const((1, w)), const((1, w)), const((1, w))],
        out_specs=[rows_spec, tok(w), tok(w), tok(w)],
        out_shape=[sd(N_SCAN_ROWS), sd(), sd(), sd()],
        compiler_params=_params("parallel", "parallel"),
        name="rwkv_prep",
    )(u, u, u, u, u, u, mu[:, :2 * w], mu[:, 2 * w:], w0, w2, a0, a2, g2, row(k_k), row(k_a), row(r_k))


def _transpose_tokens(z_ref, scr, bsz):
    w = RWKV_W
    for b in range(bsz):
        scr[b * w:(b + 1) * w, :] = z_ref[b].T
    if bsz * w < scr.shape[0]:
        scr[bsz * w:, :] = jnp.zeros((scr.shape[0] - bsz * w, scr.shape[1]), F32)


def _layout_rows_kernel(z_ref, o_ref, scr, *, bsz):
    n = RWKV_HEAD_DIM
    _transpose_tokens(z_ref, scr, bsz)
    for j in range(n):
        x = scr[pl.ds(j, LANES // 2, stride=n), :]
        o_ref[j] = jnp.concatenate([x, x], axis=0).T


def _layout_v_kernel(z_ref, o_ref, scr, *, bsz):
    n = RWKV_HEAD_DIM
    ts = z_ref.shape[1]
    _transpose_tokens(z_ref, scr, bsz)
    for i in range(n // 2):
        x0 = scr[pl.ds(i, LANES // 2, stride=n), :]
        x1 = scr[pl.ds(n // 2 + i, LANES // 2, stride=n), :]
        o_ref[pl.ds(i, ts, stride=n // 2), :] = jnp.concatenate([x0, x1], axis=0).T


def _scan_layout(rows, v):
    g, bsz, t, w = rows.shape
    n = RWKV_HEAD_DIM
    ts = LANES
    scr = pltpu.VMEM((LANES // 2 * n, ts), F32)
    rows_l = pl.pallas_call(
        functools.partial(_layout_rows_kernel, bsz=bsz),
        grid=(g, t // ts),
        in_specs=[pl.BlockSpec((None, bsz, ts, w), lambda k, i: (k, 0, i, 0))],
        out_specs=pl.BlockSpec((None, n, ts, LANES), lambda k, i: (k, 0, i, 0)),
        out_shape=jax.ShapeDtypeStruct((g, n, t, LANES), F32),
        scratch_shapes=[scr],
        compiler_params=_params("parallel", "parallel"),
        name="rwkv_layout_rows",
    )(rows)
    v_l = pl.pallas_call(
        functools.partial(_layout_v_kernel, bsz=bsz),
        grid=(t // ts,),
        in_specs=[pl.BlockSpec((bsz, ts, w), lambda i: (0, i, 0))],
        out_specs=pl.BlockSpec((ts * n // 2, LANES), lambda i: (i, 0)),
        out_shape=jax.ShapeDtypeStruct((t * n // 2, LANES), F32),
        scratch_shapes=[scr],
        compiler_params=_params("parallel"),
        name="rwkv_layout_v",
    )(v)
    return rows_l, v_l


def _rwkv_scan_kernel(r_ref, a_ref, w_ref, k_ref, b_ref, v_ref, y_ref, s_ref, sa_ref):
    n = RWKV_HEAD_DIM
    half = n // 2
    ts = r_ref.shape[1]
    fwd = pl.program_id(0) == 0

    @pl.when(pl.program_id(1) == 0)
    def _():
        s_ref[...] = jnp.zeros_like(s_ref)

    t_first = jnp.where(fwd, 0, ts - 1)
    acc = jnp.zeros((half, LANES), F32)
    for j in range(n):
        acc = acc + s_ref[j] * a_ref[j, pl.ds(t_first, 1), :]
    sa_ref[...] = acc

    def step(m, carry):
        t = jnp.where(fwd, m, ts - 1 - m)
        tn = jnp.clip(jnp.where(fwd, t + 1, t - 1), 0, ts - 1)
        sa = sa_ref[...]
        v = v_ref[t]
        y = jnp.zeros((half, LANES), F32)
        sa_next = jnp.zeros((half, LANES), F32)
        for j in range(n):
            s = (s_ref[j] * w_ref[j, pl.ds(t, 1), :] + sa * b_ref[j, pl.ds(t, 1), :]) + v * k_ref[j, pl.ds(t, 1), :]
            s_ref[j] = s
            y = y + s * r_ref[j, pl.ds(t, 1), :]
            sa_next = sa_next + s * a_ref[j, pl.ds(tn, 1), :]
        y_ref[t] = y
        sa_ref[...] = sa_next
        return carry

    lax.fori_loop(0, ts, step, 0)


def _rwkv_scan(rows, v, n_ctx):
    _, n, t, lanes = rows.shape
    ts = SCAN_STEPS
    nb = t // ts
    ncb = n_ctx // ts

    def blk(d, s):
        back = jnp.where(s < ncb, ncb - 1 - s, nb - 1 - (s - ncb))
        return jnp.where(d == 0, s, back)

    shared = lambda kind: pl.BlockSpec((None, n, ts, lanes), lambda d, s: (kind, 0, blk(d, s), 0))
    per_dir = lambda kind: pl.BlockSpec((None, n, ts, lanes), lambda d, s: (kind + d, 0, blk(d, s), 0))
    return pl.pallas_call(
        _rwkv_scan_kernel,
        grid=(2, nb),
        in_specs=[shared(ROW_R), shared(ROW_A), per_dir(ROW_W), per_dir(ROW_K), per_dir(ROW_B),
                  pl.BlockSpec((ts, n // 2, lanes), lambda d, s: (blk(d, s), 0, 0))],
        out_specs=pl.BlockSpec((None, ts, n // 2, lanes), lambda d, s: (d, blk(d, s), 0, 0)),
        out_shape=jax.ShapeDtypeStruct((2, t, n // 2, lanes), F32),
        scratch_shapes=[pltpu.VMEM((n, n // 2, lanes), F32), pltpu.VMEM((n // 2, lanes), F32)],
        compiler_params=_params("arbitrary", "arbitrary"),
        name="rwkv_scan",
    )(rows, rows, rows, rows, rows, v)


def _rwkv_readout_kernel(yf_ref, yb_ref, bonus_ref, gate_ref, g_ref, b_ref, o_ref, scr, *, bsz):
    n = RWKV_HEAD_DIM
    w = RWKV_W
    ts = o_ref.shape[1]
    for i in range(n // 2):
        rows = pl.ds(i, ts, stride=n // 2)
        yt = (yf_ref[rows, :] + yb_ref[rows, :]).T
        scr[pl.ds(i, LANES // 2, stride=n), :] = yt[:LANES // 2]
        scr[pl.ds(n // 2 + i, LANES // 2, stride=n), :] = yt[LANES // 2:]
    ones = _group_ones(w, n)
    inv = 1.0 / n
    for b in range(bsz):
        y = scr[b * w:(b + 1) * w, :].T
        mean = _dot2_exact_rhs(y, ones) * inv
        yc = y - mean
        var = _dot2_exact_rhs(yc * yc, ones) * inv
        yn = yc * lax.rsqrt(var + RWKV_GN_EPS) * g_ref[...] + b_ref[...]
        o_ref[b] = ((yn + bonus_ref[b]) * gate_ref[b]).astype(o_ref.dtype)


def _rwkv_readout(y, bonus, gate, ln_g, ln_b):
    bsz, t, w = bonus.shape
    n = RWKV_HEAD_DIM
    ts = LANES
    tok = pl.BlockSpec((bsz, ts, w), lambda i: (0, i, 0))
    vec = pl.BlockSpec((1, w), lambda i: (0, 0))
    return pl.pallas_call(
        functools.partial(_rwkv_readout_kernel, bsz=bsz),
        grid=(t // ts,),
        in_specs=[pl.BlockSpec((None, ts * n // 2, LANES), lambda i: (0, i, 0)),
                  pl.BlockSpec((None, ts * n // 2, LANES), lambda i: (1, i, 0)), tok, tok, vec, vec],
        out_specs=tok,
        out_shape=jax.ShapeDtypeStruct((bsz, t, w), BF16),
        scratch_shapes=[pltpu.VMEM((LANES // 2 * n, ts), F32)],
        compiler_params=_params("parallel"),
        name="rwkv_readout",
    )(y, y, bonus, gate, ln_g.reshape(1, w), ln_b.reshape(1, w))


def _merge_kernel(ya_ref, yr_ref, yw_ref, g0_ref, g1_ref, g2_ref, x_ref, gate1_ref, a2_ref, b2_ref,
                  wb_ref, wo_ref, wr_ref, br_ref,
                  x_out, h_out, ids_out, wts_out, cnt_out):
    first = jnp.logical_and(pl.program_id(0) == 0, pl.program_id(1) == 0)

    @pl.when(first)
    def _():
        cnt_out[...] = jnp.zeros_like(cnt_out)

    merged = (jax.nn.sigmoid(g0_ref[...]) * _dot(ya_ref[...], wb_ref[0])
              + jax.nn.sigmoid(g1_ref[...]) * _dot(yr_ref[...], wb_ref[1])
              + jax.nn.sigmoid(g2_ref[...]) * _dot(yw_ref[...], wb_ref[2]))
    x = x_ref[...] + gate1_ref[...] * _dot(merged.astype(BF16), wo_ref[...])
    x_out[...] = x
    h = x * lax.rsqrt(jnp.mean(x * x, axis=-1, keepdims=True) + NORM_EPS) * a2_ref[...] + b2_ref[...]
    h_out[...] = h.astype(BF16)

    tm = x.shape[0]
    logits = _dot3(h, wr_ref[...]) + br_ref[...]
    lane = lax.broadcasted_iota(jnp.int32, (tm, LANES), 1)
    lane_f = lane.astype(F32)
    neg = -jnp.inf
    big = float(LANES)
    first = lambda hit: jnp.min(jnp.where(hit, lane_f, big), axis=-1, keepdims=True).astype(jnp.int32)
    is_grp = jnp.logical_and(lane >= MOE_EXPERTS, lane < MOE_EXPERTS + MOE_GROUPS)
    gl = jnp.where(is_grp, logits, neg)
    gmax = jnp.max(gl, axis=-1, keepdims=True)
    gidx = first(gl == gmax) - MOE_EXPERTS
    p_grp = 1.0 / jnp.sum(jnp.where(is_grp, jnp.exp(gl - gmax), 0.0), axis=-1, keepdims=True)
    in_grp = jnp.logical_and(lane < MOE_EXPERTS, lane // MOE_EXPERTS_PER_GROUP == gidx)
    el = jnp.where(in_grp, logits, neg)
    v1 = jnp.max(el, axis=-1, keepdims=True)
    i1 = first(el == v1)
    el2 = jnp.where(lane == i1, neg, el)
    v2 = jnp.max(el2, axis=-1, keepdims=True)
    i2 = first(el2 == v2)
    e2 = jnp.exp(v2 - v1)
    w1 = p_grp / (1.0 + e2)
    w2 = p_grp * e2 / (1.0 + e2)
    wts_out[...] = jnp.where(lane == 0, w1, jnp.where(lane == 1, w2, 0.0))

    onehot = jnp.where(jnp.logical_or(lane == i1, lane == i2), 1.0, 0.0)
    rr = lax.broadcasted_iota(jnp.int32, (tm, tm), 0)
    cc = lax.broadcasted_iota(jnp.int32, (tm, tm), 1)
    below = jnp.where(cc < rr, 1.0, 0.0).astype(BF16)
    before = _dot(below, onehot.astype(BF16)) + cnt_out[0:1, :]
    rank1 = jnp.sum(jnp.where(lane == i1, before, 0.0), axis=-1, keepdims=True).astype(jnp.int32)
    rank2 = jnp.sum(jnp.where(lane == i2, before, 0.0), axis=-1, keepdims=True).astype(jnp.int32)
    ids_out[...] = jnp.where(lane == 0, i1, jnp.where(lane == 1, i2, jnp.where(lane == 2, rank1,
                                                                                  jnp.where(lane == 3, rank2, 0))))
    cnt_out[...] = cnt_out[...] + jnp.sum(onehot, axis=0, keepdims=True)


def _merge(ya, yr, yw, u, x, gate1, a2, b2, w_branch, w_out, w_router, b_router, part):
    bsz, t, d = x.shape
    nb = bsz // MOE_PARTS
    b0 = part * nb
    tm = ROW_TILE
    sel = lambda b, i: (2 * (b + b0) + jnp.minimum(i, 1), 0, 0)
    tok = lambda width, blk=0: pl.BlockSpec((None, tm, width), lambda b, i: (b + b0, i, blk))
    own = lambda width: pl.BlockSpec((None, tm, width), lambda b, i: (b, i, 0))
    const = lambda shape: pl.BlockSpec(shape, lambda b, i: (0,) * len(shape))
    mod = pl.BlockSpec((None, 1, d), sel)
    return pl.pallas_call(
        _merge_kernel,
        grid=(nb, t // tm),
        in_specs=[tok(BRANCH_W), tok(BRANCH_W), tok(BRANCH_W), tok(d, 0), tok(d, 1), tok(d, 2), tok(d),
                  mod, mod, mod,
                  const((N_BRANCH, BRANCH_W, d)), const((d, d)), const((d, LANES)), const((1, LANES))],
        out_specs=[tok(d), own(d), own(LANES), own(LANES), const((8, LANES))],
        out_shape=[jax.ShapeDtypeStruct((bsz, t, d), F32), jax.ShapeDtypeStruct((nb, t, d), BF16),
                   jax.ShapeDtypeStruct((nb, t, LANES), jnp.int32), jax.ShapeDtypeStruct((nb, t, LANES), F32),
                   jax.ShapeDtypeStruct((8, LANES), F32)],
        input_output_aliases={6: 0},
        compiler_params=_params("arbitrary", "arbitrary"),
        name="merge_router",
    )(ya, yr, yw, u, u, u, x, gate1, a2, b2, w_branch, w_out, w_router, b_router)


def _moe_kernel(be_ref, na_ref, x_ref, wg_ref, wu_ref, wd_ref, o_ref, wg_s, wu_s, wd_s):
    i = pl.program_id(0)
    active = i < na_ref[0]
    new_expert = jnp.logical_or(i == 0, be_ref[i] != be_ref[jnp.maximum(i - 1, 0)])

    @pl.when(jnp.logical_and(active, new_expert))
    def _():
        wg_s[...] = wg_ref[...].astype(BF16)
        wu_s[...] = wu_ref[...].astype(BF16)
        wd_s[...] = wd_ref[...].astype(BF16)

    @pl.when(active)
    def _():
        x = x_ref[...]
        act = _silu(_dot(x, wg_s[...])) * _dot(x, wu_s[...])
        o_ref[...] = _dot(act.astype(BF16), wd_s[...]).astype(o_ref.dtype)

    @pl.when(i >= na_ref[0])
    def _():
        o_ref[...] = jnp.zeros_like(o_ref)


def _moe_experts(buf, block_expert, n_active, w_gate, w_up, w_down, layer):
    rows, d = buf.shape
    hid = w_gate.shape[-1]
    grid_spec = pltpu.PrefetchScalarGridSpec(
        num_scalar_prefetch=2,
        grid=(rows // MOE_BLOCK,),
        in_specs=[pl.BlockSpec((MOE_BLOCK, d), lambda i, be, na: (i, 0)),
                  pl.BlockSpec((None, None, d, hid), lambda i, be, na: (layer, be[i], 0, 0)),
                  pl.BlockSpec((None, None, d, hid), lambda i, be, na: (layer, be[i], 0, 0)),
                  pl.BlockSpec((None, None, hid, d), lambda i, be, na: (layer, be[i], 0, 0))],
        out_specs=pl.BlockSpec((MOE_BLOCK, d), lambda i, be, na: (i, 0)),
        scratch_shapes=[pltpu.VMEM((d, hid), BF16), pltpu.VMEM((d, hid), BF16), pltpu.VMEM((hid, d), BF16)])
    return pl.pallas_call(
        _moe_kernel,
        grid_spec=grid_spec,
        out_shape=jax.ShapeDtypeStruct((rows, d), F32),
        compiler_params=_params("arbitrary"),
        name="moe_experts",
    )(block_expert, n_active, buf, w_gate, w_up, w_down)


def _combine_kernel(x_ref, *refs):
    y_refs, (w_ref, g_ref, o_ref) = refs[:-3], refs[-3:]
    tm = x_ref.shape[0]
    w = w_ref[...]
    for c, y_ref in enumerate(y_refs):
        cols = slice(c * LANES, (c + 1) * LANES)
        y = y_ref[pl.ds(0, tm, stride=2), :] * w[:, 0:1] + y_ref[pl.ds(1, tm, stride=2), :] * w[:, 1:2]
        o_ref[:, cols] = x_ref[:, cols] + g_ref[:, cols] * y


def _combine(x, y_pairs, wts, gate2, part):
    bsz, t, d = x.shape
    nb = y_pairs.shape[0]
    b0 = part * nb
    tm = ROW_TILE
    sel = lambda b, i: (2 * (b + b0) + jnp.minimum(i, 1), 0, 0)
    x_spec = pl.BlockSpec((None, tm, d), lambda b, i: (b + b0, i, 0))
    return pl.pallas_call(
        _combine_kernel,
        grid=(nb, t // tm),
        in_specs=[x_spec]
                 + [pl.BlockSpec((None, 2 * tm, LANES), functools.partial(lambda c, b, i: (b, i, c), c))
                    for c in range(d // LANES)]
                 + [pl.BlockSpec((None, tm, LANES), lambda b, i: (b, i, 0)), pl.BlockSpec((None, 1, d), sel)],
        out_specs=x_spec,
        out_shape=jax.ShapeDtypeStruct((bsz, t, d), F32),
        input_output_aliases={0: 0},
        compiler_params=_params("parallel", "parallel"),
        name="moe_combine",
    )(x, *([y_pairs] * (d // LANES)), wts, gate2)


def _moe(h, ids, wts, counts, w_gate, w_up, w_down, layer):
    bsz, t, d = h.shape
    n_tok = bsz * t
    n_pair = 2 * n_tok
    n_blocks = -(-n_pair // MOE_BLOCK) + MOE_EXPERTS
    counts = counts[0, :MOE_EXPERTS].astype(jnp.int32)
    padded = (counts + MOE_BLOCK - 1) // MOE_BLOCK * MOE_BLOCK
    pad_end = jnp.cumsum(padded)
    pad_start = pad_end - padded
    expert = ids[..., 0:2].reshape(n_pair)
    rank = ids[..., 2:4].reshape(n_pair)
    dest = pad_start[expert] + rank
    token = jnp.arange(n_pair, dtype=jnp.int32) // 2
    src = jnp.zeros((n_blocks * MOE_BLOCK,), jnp.int32).at[dest].set(token, unique_indices=True)
    block_start = jnp.arange(n_blocks, dtype=jnp.int32) * MOE_BLOCK
    block_expert = jnp.minimum(jnp.sum((pad_end[None, :] <= block_start[:, None]).astype(jnp.int32), axis=1),
                               MOE_EXPERTS - 1).astype(jnp.int32)
    n_active = (pad_end[-1:] // MOE_BLOCK).astype(jnp.int32)
    buf = jnp.take(h.reshape(n_tok, d), src, axis=0)
    yb = _moe_experts(buf, block_expert, n_active, w_gate, w_up, w_down, layer)
    return jnp.take(yb, dest, axis=0).reshape(bsz, 2 * t, d)


def _rope_tables(n_ctx, n_lat, head_dim):
    rows = n_lat // GRID_W
    row = jnp.broadcast_to(jnp.arange(rows, dtype=F32)[:, None], (rows, GRID_W)).reshape(-1)
    col = jnp.broadcast_to(jnp.arange(GRID_W, dtype=F32)[None, :], (rows, GRID_W)).reshape(-1)
    quarter = head_dim // 4
    inv_freq = ROPE_THETA ** (-jnp.arange(quarter, dtype=F32) / quarter)
    ang = jnp.stack([row[:, None] * inv_freq, col[:, None] * inv_freq], axis=1)
    cos, sin = jnp.cos(ang), jnp.sin(ang)
    cos_t = jnp.stack([cos, cos], axis=2).reshape(n_lat, head_dim)
    sin_t = jnp.stack([-sin, sin], axis=2).reshape(n_lat, head_dim)
    cos_t = jnp.concatenate([jnp.ones((n_ctx, head_dim), F32), cos_t], axis=0)
    sin_t = jnp.concatenate([jnp.zeros((n_ctx, head_dim), F32), sin_t], axis=0)
    rep = LANES // head_dim
    return jnp.tile(cos_t, (1, rep)), jnp.tile(sin_t, (1, rep))


def kernel(x, c, ctx, c_ctx, ada_w, ada_b, norm1_g, norm2_g, w_in, att_qn_g, att_kn_g, ret_decay_logit, ret_gn_g, rwkv_mu, rwkv_w0, rwkv_w2, rwkv_a0, rwkv_a2, rwkv_g2, rwkv_k_k, rwkv_k_a, rwkv_r_k, rwkv_ln_g, rwkv_ln_b, w_branch, w_out, router_grp_w, router_grp_b, router_exp_w, router_exp_b, moe_w_gate, moe_w_up, moe_w_down):
    bsz, n_lat, d = x.shape
    n_ctx = ctx.shape[1]
    depth = ada_w.shape[0]
    assert d == D_MODEL and n_ctx == ROW_TILE and n_lat % ROW_TILE == 0 and n_lat % GRID_W == 0
    assert 2 * bsz * RWKV_HEADS <= LANES
    t_all = n_ctx + n_lat
    assert t_all % LANES == 0 and n_ctx % SCAN_STEPS == 0

    att_cos, att_sin = _rope_tables(n_ctx, n_lat, ATT_HEAD_DIM)
    ret_cos, ret_sin = _rope_tables(n_ctx, n_lat, RET_HEAD_DIM)

    rows = -(-(bsz + 1) // 8) * 8
    cvec = jnp.zeros((rows, d), F32).at[:bsz].set(c).at[bsz].set(c_ctx)
    mods = _modulation(cvec, ada_w, ada_b)
    perm = _column_permutation()

    xs = jnp.concatenate([ctx, x], axis=1)
    for layer in range(depth):
        m = mods[layer].reshape(rows, 6, d)
        pick = lambda j: jnp.stack([jnp.broadcast_to(m[bsz, j], (bsz, d)), m[:bsz, j]], axis=1).reshape(2 * bsz, 1, d)
        sh1, sc1, g1, sh2, sc2, g2 = (pick(j) for j in range(6))
        w_l = w_in[layer][:, perm].astype(BF16)
        u = _in_proj(xs, norm1_g[layer] * (1.0 + sc1), sh1, w_l)

        score_bound = (ATT_HEAD_DIM * ATT_Q_SCALE * ATT_BOUND_MARGIN) * (
            jnp.max(jnp.abs(att_qn_g[layer])) * jnp.max(jnp.abs(att_kn_g[layer])))
        bounded = 2.0 * score_bound <= ATT_EXP2_RANGE
        q_att, kv_att = _att_prep(u, att_cos, att_sin, att_qn_g[layer], att_kn_g[layer],
                                  jnp.where(bounded, -score_bound, 0.0))
        ya = _attention(q_att, kv_att, bounded.astype(jnp.int32).reshape(1), n_ctx)

        log_gamma = jax.nn.log_sigmoid(ret_decay_logit[layer].astype(F32))
        yr = _retention(u, ret_cos, ret_sin, log_gamma, ret_gn_g[layer], n_ctx)

        rows_t, v_t, gate, bonus = _rwkv_prep(
            u, rwkv_mu[layer], rwkv_w0[layer], rwkv_w2[layer], rwkv_a0[layer], rwkv_a2[layer], rwkv_g2[layer],
            rwkv_k_k[layer], rwkv_k_a[layer], rwkv_r_k[layer].reshape(-1))
        rows_s, v_s = _scan_layout(rows_t, v_t)
        y_scan = _rwkv_scan(rows_s, v_s.reshape(t_all, RWKV_HEAD_DIM // 2, LANES), n_ctx)
        yw = _rwkv_readout(y_scan.reshape(2, t_all * RWKV_HEAD_DIM // 2, LANES), bonus, gate,
                           rwkv_ln_g[layer], rwkv_ln_b[layer])

        w_router = jnp.zeros((d, LANES), F32).at[:, :MOE_EXPERTS].set(router_exp_w[layer]).at[
            :, MOE_EXPERTS:MOE_EXPERTS + MOE_GROUPS].set(router_grp_w[layer])
        b_router = jnp.zeros((1, LANES), F32).at[0, :MOE_EXPERTS].set(router_exp_b[layer]).at[
            0, MOE_EXPERTS:MOE_EXPERTS + MOE_GROUPS].set(router_grp_b[layer])
        routed = []
        for part in range(MOE_PARTS):
            xs, h2, ids, wts, counts = _merge(
                ya, yr, yw, u, xs, g1, norm2_g[layer] * (1.0 + sc2), sh2,
                w_branch[layer].astype(BF16), w_out[layer].astype(BF16), w_router, b_router, part)
            routed.append((h2, ids, wts, counts))
        pairs = [_moe(h2, ids, wts, counts, moe_w_gate, moe_w_up, moe_w_down, layer)
                 for h2, ids, wts, counts in routed]
        for part in range(MOE_PARTS):
            xs = _combine(xs, pairs[part], routed[part][2], g2, part)
    return xs[:, n_ctx:]
```

```python
import functools

import jax
import jax.numpy as jnp
from jax import lax
from jax.experimental import pallas as pl
from jax.experimental.pallas import tpu as pltpu

F32 = jnp.float32
BF16 = jnp.bfloat16

D_MODEL = 1024
GRID_W = 64
NORM_EPS = 1e-6
ROPE_THETA = 10000.0

ATT_HEADS = 8
ATT_KV_HEADS = 2
ATT_HEAD_DIM = 64
ATT_GROUP = ATT_HEADS // ATT_KV_HEADS
ATT_W = ATT_HEADS * ATT_HEAD_DIM
ATT_KV_W = ATT_KV_HEADS * ATT_HEAD_DIM

RET_HEADS = 4
RET_HEAD_DIM = 128
RET_CHUNK = 128
RET_W = RET_HEADS * RET_HEAD_DIM

RWKV_HEADS = 8
RWKV_HEAD_DIM = 64
RWKV_W = RWKV_HEADS * RWKV_HEAD_DIM
RWKV_DECAY_LORA = 64
RWKV_AAA_LORA = 64
RWKV_GATE_LORA = 128
RWKV_GN_EPS = 64e-5
RWKV_DECAY_SCALE = 0.6065306597126334
RWKV_COLS = 3 * RWKV_W + RWKV_DECAY_LORA + RWKV_AAA_LORA + RWKV_GATE_LORA

N_BRANCH = 3
BRANCH_W = 512
IN_COLS = ATT_W + 2 * ATT_KV_W + 4 * RET_W + RWKV_COLS + N_BRANCH * D_MODEL

MOE_GROUPS = 4
MOE_EXPERTS_PER_GROUP = 8
MOE_EXPERTS = MOE_GROUPS * MOE_EXPERTS_PER_GROUP
MOE_HIDDEN = 512
MOE_BLOCK = 256
MOE_PARTS = 1

LANES = 128
ROW_TILE = 256
ATT_Q_TILE = 128
ATT_KEY_TILE = 256
ATT_Q_SCALE = ATT_HEAD_DIM ** -0.5 * 1.4426950408889634
ATT_BOUND_MARGIN = 1.01
ATT_EXP2_RANGE = 100.0
SCAN_STEPS = 64
VMEM_LIMIT = 56 * 1024 * 1024

ROW_R, ROW_A, ROW_W, ROW_K, ROW_B, N_SCAN_ROWS = 0, 1, 2, 4, 6, 8

ATT_COLS = ATT_W + 2 * ATT_KV_W
RWKV_REST_COLS = RWKV_COLS - 2 * RWKV_W
U_RWKV_RK = 0
U_RET = U_RWKV_RK + 2 * RWKV_W
U_ATT = U_RET + 4 * RET_W
U_RWKV_REST = U_ATT + ATT_COLS
U_GATE = U_RWKV_REST + RWKV_REST_COLS
GATE_BLOCK = 512
IN_PROJ_MOVES = (((0, U_ATT, ATT_COLS), (ATT_COLS, U_RET, 4 * RET_W),
                  (ATT_COLS + 4 * RET_W, U_RWKV_RK, 2 * RWKV_W)),
                 ((0, 0, IN_COLS // 2),))
assert U_ATT + ATT_COLS == IN_COLS // 2 and ATT_COLS + 4 * RET_W + 2 * RWKV_W == IN_COLS // 2


def _params(*sem):
    return pltpu.CompilerParams(dimension_semantics=sem, vmem_limit_bytes=VMEM_LIMIT)


def _dot(a, b):
    return jnp.dot(a, b, preferred_element_type=F32)


def _dot_nt(a, b):
    return lax.dot_general(a, b, (((1,), (1,)), ((), ())), preferred_element_type=F32)


def _split(a):
    hi = a.astype(BF16)
    lo = (a - hi.astype(F32)).astype(BF16)
    return hi, lo


def _dot3(a, b):
    ah, al = _split(a)
    bh, bl = _split(b)
    return _dot(ah, bh) + (_dot(al, bh) + _dot(ah, bl))


def _dot2_exact_rhs(a, b_bf16):
    ah, al = _split(a)
    return _dot(ah, b_bf16) + _dot(al, b_bf16)


def _group_ones(width, group):
    r = lax.broadcasted_iota(jnp.int32, (width, width), 0) // group
    c = lax.broadcasted_iota(jnp.int32, (width, width), 1) // group
    return jnp.where(r == c, 1.0, 0.0).astype(BF16)


def _silu(x):
    return x * jax.nn.sigmoid(x)


def _swap_halves(x, quarter):
    n = x.shape[-1]
    lane = lax.broadcasted_iota(jnp.int32, x.shape, x.ndim - 1)
    up = pltpu.roll(x, n - quarter, x.ndim - 1)
    down = pltpu.roll(x, quarter, x.ndim - 1)
    return jnp.where(lane % (2 * quarter) < quarter, up, down)


def _mod_kernel(c_ref, w_ref, b_ref, o_ref):
    o_ref[...] = _dot3(_silu(c_ref[...]), w_ref[...]) + b_ref[...]


def _modulation(cvec, ada_w, ada_b):
    depth, d, cols = ada_w.shape
    rows = cvec.shape[0]
    tn = 1536
    return pl.pallas_call(
        _mod_kernel,
        grid=(depth, cols // tn),
        in_specs=[pl.BlockSpec((rows, d), lambda l, j: (0, 0)),
                  pl.BlockSpec((None, d, tn), lambda l, j: (l, 0, j)),
                  pl.BlockSpec((None, 1, tn), lambda l, j: (l, 0, j))],
        out_specs=pl.BlockSpec((None, rows, tn), lambda l, j: (l, 0, j)),
        out_shape=jax.ShapeDtypeStruct((depth, rows, cols), F32),
        compiler_params=_params("parallel", "parallel"),
        name="modulation",
    )(cvec, ada_w, ada_b.reshape(depth, 1, cols))


def _in_proj_kernel(x_ref, a_ref, b_ref, w_ref, o_ref):
    x = x_ref[...]
    ms = jnp.mean(x * x, axis=-1, keepdims=True)
    h = (x * lax.rsqrt(ms + NORM_EPS) * a_ref[...] + b_ref[...]).astype(BF16)
    for half, moves in enumerate(IN_PROJ_MOVES):
        @pl.when(pl.program_id(0) == half)
        def _():
            for src, dst, width in moves:
                o_ref[:, dst:dst + width] = _dot(h, w_ref[:, src:src + width])


def _in_proj(x, mod_a, mod_b, w):
    bsz, t, d = x.shape
    cols = w.shape[1]
    tm, tn = ROW_TILE, cols // 2
    sel = lambda j, b, i: (2 * b + jnp.minimum(i, 1), 0, 0)
    return pl.pallas_call(
        _in_proj_kernel,
        grid=(cols // tn, bsz, t // tm),
        in_specs=[pl.BlockSpec((None, tm, d), lambda j, b, i: (b, i, 0)),
                  pl.BlockSpec((None, 1, d), sel),
                  pl.BlockSpec((None, 1, d), sel),
                  pl.BlockSpec((d, tn), lambda j, b, i: (0, j))],
        out_specs=pl.BlockSpec((None, tm, tn), lambda j, b, i: (b, i, j)),
        out_shape=jax.ShapeDtypeStruct((bsz, t, cols), F32),
        compiler_params=_params("parallel", "parallel", "parallel"),
        name="in_proj",
    )(x, mod_a, mod_b, w)


def _att_prep_kernel(u_ref, cos_ref, sin_ref, qg_ref, kg_ref, shift_ref, q_out, kv_out):
    hd = ATT_HEAD_DIM
    ones = _group_ones(LANES, hd)
    cos = cos_ref[...]
    sin = sin_ref[...]
    lane = lax.broadcasted_iota(jnp.int32, cos.shape, 1)
    low = lane < hd

    def two_heads(y, extra):
        fill = jnp.where(lane == hd, extra, 0.0)
        return jnp.where(low, y, fill), jnp.where(low, pltpu.roll(y, hd, 1), fill)

    n_q = ATT_W // LANES
    for j in range(n_q + 1):
        x = u_ref[:, j * LANES:(j + 1) * LANES]
        is_q = j < n_q
        gain = qg_ref[...] if is_q else kg_ref[...]
        ms = _dot2_exact_rhs(x * x, ones) * (1.0 / hd)
        y = x * lax.rsqrt(ms + NORM_EPS) * gain
        y = y * cos + _swap_halves(y, hd // 4) * sin
        if is_q:
            y = y * ATT_Q_SCALE
        out, base = (q_out, 2 * j) if is_q else (kv_out, 0)
        for h, yh in enumerate(two_heads(y, shift_ref[...] if is_q else 1.0)):
            out[:, (base + h) * LANES:(base + h + 1) * LANES] = yh.astype(BF16)
    v = u_ref[:, ATT_W + ATT_KV_W:]
    for h, vh in enumerate(two_heads(v, 1.0)):
        kv_out[:, (2 + h) * LANES:(3 + h) * LANES] = vh.astype(BF16)


def _att_prep(u, cos, sin, qn_g, kn_g, shift):
    bsz, t, _ = u.shape
    tm = ROW_TILE
    rep = LANES // ATT_HEAD_DIM
    shift = jnp.broadcast_to(shift.astype(F32), (1, LANES))
    return pl.pallas_call(
        _att_prep_kernel,
        grid=(bsz, t // tm),
        in_specs=[pl.BlockSpec((None, tm, ATT_COLS), lambda b, i: (b, i, U_ATT // ATT_COLS)),
                  pl.BlockSpec((tm, LANES), lambda b, i: (i, 0)),
                  pl.BlockSpec((tm, LANES), lambda b, i: (i, 0)),
                  pl.BlockSpec((1, LANES), lambda b, i: (0, 0)),
                  pl.BlockSpec((1, LANES), lambda b, i: (0, 0)),
                  pl.BlockSpec((1, LANES), lambda b, i: (0, 0))],
        out_specs=[pl.BlockSpec((None, tm, ATT_HEADS * LANES), lambda b, i: (b, i, 0)),
                   pl.BlockSpec((None, tm, 2 * ATT_KV_HEADS * LANES), lambda b, i: (b, i, 0))],
        out_shape=[jax.ShapeDtypeStruct((bsz, t, ATT_HEADS * LANES), BF16),
                   jax.ShapeDtypeStruct((bsz, t, 2 * ATT_KV_HEADS * LANES), BF16)],
        compiler_params=_params("parallel", "parallel"),
        name="att_prep",
    )(u, cos, sin, jnp.tile(qn_g, rep).reshape(1, LANES), jnp.tile(kn_g, rep).reshape(1, LANES), shift)


def _att_kernel(bounded_ref, q_ref, kv_ref, o_ref, *, n_ctx, tq):
    i = pl.program_id(1)
    hd = ATT_HEAD_DIM
    k_ref = v_ref = kv_ref

    def run(n_keys, bounded):
        tk = ATT_KEY_TILE
        for g in range(ATT_KV_HEADS):
            q = jnp.concatenate(
                [q_ref[:, (ATT_GROUP * g + h) * LANES:(ATT_GROUP * g + h + 1) * LANES] for h in range(ATT_GROUP)],
                axis=0)
            scores = lambda c: _dot_nt(q, k_ref[c * tk:(c + 1) * tk, g * LANES:(g + 1) * LANES])
            if not bounded:
                m = jnp.full((ATT_GROUP * tq, LANES), -jnp.inf, F32)
                for c in range(n_keys // tk):
                    s = scores(c)
                    for part in range(tk // LANES):
                        m = jnp.maximum(m, s[:, part * LANES:(part + 1) * LANES])
                m = jnp.max(m, axis=-1, keepdims=True)
            acc = jnp.zeros((ATT_GROUP * tq, LANES), F32)
            for c in range(n_keys // tk):
                p = jnp.exp2(scores(c) if bounded else scores(c) - m).astype(BF16)
                acc = acc + _dot(p, v_ref[c * tk:(c + 1) * tk, (ATT_KV_HEADS + g) * LANES:(ATT_KV_HEADS + g + 1) * LANES])
            o = acc[:, :hd] / acc[:, hd:hd + 1]
            for h in range(ATT_GROUP):
                c0 = (ATT_GROUP * g + h) * hd
                o_ref[:, c0:c0 + hd] = o[h * tq:(h + 1) * tq].astype(o_ref.dtype)

    is_ctx = i < n_ctx // tq
    bounded = bounded_ref[0] == 1

    @pl.when(is_ctx)
    def _():
        run(n_ctx, False)

    @pl.when(jnp.logical_and(jnp.logical_not(is_ctx), bounded))
    def _():
        run(k_ref.shape[0], True)

    @pl.when(jnp.logical_and(jnp.logical_not(is_ctx), jnp.logical_not(bounded)))
    def _():
        run(k_ref.shape[0], False)


def _attention(q, kv, bounded, n_ctx):
    bsz, t, _ = q.shape
    tq = ATT_Q_TILE
    kv_w = kv.shape[-1]
    assert n_ctx % ATT_KEY_TILE == 0 and t % ATT_KEY_TILE == 0
    grid_spec = pltpu.PrefetchScalarGridSpec(
        num_scalar_prefetch=1,
        grid=(bsz, t // tq),
        in_specs=[pl.BlockSpec((None, tq, ATT_HEADS * LANES), lambda b, i, f: (b, i, 0)),
                  pl.BlockSpec((None, t, kv_w), lambda b, i, f: (b, 0, 0))],
        out_specs=pl.BlockSpec((None, tq, ATT_W), lambda b, i, f: (b, i, 0)))
    return pl.pallas_call(
        functools.partial(_att_kernel, n_ctx=n_ctx, tq=tq),
        grid_spec=grid_spec,
        out_shape=jax.ShapeDtypeStruct((bsz, t, ATT_W), BF16),
        compiler_params=_params("parallel", "parallel"),
        name="attention",
    )(bounded, q, kv)


def _ret_kernel(q_ref, k_ref, v_ref, g_ref, cos_ref, sin_ref, lg_ref, gn_ref, o_ref,
                qs_ref, ks_ref, kvf_ref, kvb_ref, sf_ref, sb_ref, *, n_ctx):
    c = RET_CHUNK
    t = q_ref.shape[0]
    n_chunks = t // c
    n_cc = n_ctx // c
    quarter = RET_HEAD_DIM // 4
    scale = RET_HEAD_DIM ** -0.5
    lg_f = lg_ref[0]
    lg_b = lg_ref[1]
    row = lax.broadcasted_iota(jnp.int32, (c, c), 0)
    col = lax.broadcasted_iota(jnp.int32, (c, c), 1)
    rowf = row.astype(F32)
    lag = (row - col).astype(F32)

    def chunk(ci):
        r0 = pl.multiple_of(ci * c, c)
        cos = cos_ref[pl.ds(r0, c), :]
        sin = sin_ref[pl.ds(r0, c), :]
        q = q_ref[pl.ds(r0, c), :]
        k = k_ref[pl.ds(r0, c), :]
        q = q * cos + _swap_halves(q, quarter) * sin
        k = (k * cos + _swap_halves(k, quarter) * sin) * scale
        return r0, q, k, v_ref[pl.ds(r0, c), :]

    d_key_f = jnp.exp(lg_f * (c - 1.0 - rowf))
    d_key_b = jnp.exp(lg_b * rowf)
    d_query_f = jnp.exp(lg_f * (rowf + 1.0))
    d_query_b = jnp.exp(lg_b * (float(c) - rowf))
    d_chunk_f = jnp.exp(lg_f * float(c))
    d_chunk_b = jnp.exp(lg_b * float(c))
    d_intra = (jnp.where(lag >= 0, jnp.exp(lg_f * jnp.maximum(lag, 0.0)), 0.0)
               + jnp.where(lag <= 0, jnp.exp(lg_b * jnp.maximum(-lag, 0.0)), 0.0))

    def summaries(n, carry):
        r0, q, k, v = chunk(n)
        qs_ref[pl.ds(r0, c), :] = q.astype(BF16)
        ks_ref[pl.ds(r0, c), :] = k.astype(BF16)
        vb = v.astype(BF16)
        kvf_ref[n] = _dot((k * d_key_f).T.astype(BF16), vb)
        kvb_ref[n] = _dot((k * d_key_b).T.astype(BF16), vb)
        return carry

    lax.fori_loop(0, n_chunks, summaries, 0, unroll=2)

    def state_f(n, s):
        sf_ref[n] = s.astype(BF16)
        return d_chunk_f * s + kvf_ref[n]

    def state_b(n, s):
        ci = jnp.where(n < n_cc, n_cc - 1 - n, n_chunks - 1 - (n - n_cc))
        sb_ref[ci] = s.astype(BF16)
        return d_chunk_b * s + kvb_ref[ci]

    zero = jnp.zeros((RET_HEAD_DIM, RET_HEAD_DIM), F32)
    lax.fori_loop(0, n_chunks, state_f, zero)
    lax.fori_loop(0, n_chunks, state_b, zero)
    gn = gn_ref[...]

    def outputs(n, carry):
        r0 = pl.multiple_of(n * c, c)
        qb = qs_ref[pl.ds(r0, c), :]
        vb = v_ref[pl.ds(r0, c), :].astype(BF16)
        scores = _dot_nt(qb, ks_ref[pl.ds(r0, c), :]) * d_intra
        y = (_dot(scores.astype(BF16), vb) + _dot(qb, sf_ref[n]) * d_query_f) + _dot(qb, sb_ref[n]) * d_query_b
        yn = y * lax.rsqrt(jnp.mean(y * y, axis=-1, keepdims=True) + NORM_EPS) * gn
        o_ref[pl.ds(r0, c), :] = (_silu(g_ref[pl.ds(r0, c), :]) * yn).astype(o_ref.dtype)
        return carry

    lax.fori_loop(0, n_chunks, outputs, 0, unroll=2)


def _retention(u, cos, sin, log_gamma, gn_g, n_ctx):
    bsz, t, _ = u.shape
    hd = RET_HEAD_DIM
    base = U_RET // hd
    spec = lambda off: pl.BlockSpec((None, t, hd), lambda b, h: (b, 0, base + off * RET_HEADS + h))
    lg = jnp.broadcast_to(log_gamma[:, :, None, None], (2, RET_HEADS, 1, LANES)).astype(F32)
    return pl.pallas_call(
        functools.partial(_ret_kernel, n_ctx=n_ctx),
        grid=(bsz, RET_HEADS),
        in_specs=[spec(0), spec(1), spec(2), spec(3),
                  pl.BlockSpec((t, hd), lambda b, h: (0, 0)),
                  pl.BlockSpec((t, hd), lambda b, h: (0, 0)),
                  pl.BlockSpec((2, None, 1, LANES), lambda b, h: (0, h, 0, 0)),
                  pl.BlockSpec((1, hd), lambda b, h: (0, h))],
        out_specs=pl.BlockSpec((None, t, hd), lambda b, h: (b, 0, h)),
        out_shape=jax.ShapeDtypeStruct((bsz, t, RET_W), BF16),
        scratch_shapes=[pltpu.VMEM((t, hd), BF16), pltpu.VMEM((t, hd), BF16),
                        pltpu.VMEM((t // RET_CHUNK, hd, hd), F32), pltpu.VMEM((t // RET_CHUNK, hd, hd), F32),
                        pltpu.VMEM((t // RET_CHUNK, hd, hd), BF16), pltpu.VMEM((t // RET_CHUNK, hd, hd), BF16)],
        compiler_params=_params("parallel", "parallel"),
        name="retention",
    )(u, u, u, u, cos, sin, lg, gn_g.reshape(1, RET_W))


def _rwkv_prep_kernel(rk_ref, rk_prev_ref, rk_next_ref, rest_ref, rest_prev_ref, rest_next_ref,
                      mu_rk_ref, mu_rest_ref, w0_ref, w2_ref, a0_ref, a2_ref, g2_ref, kk_ref, ka_ref, rk_gain_ref,
                      rows_out, v_out, gate_out, bonus_out, *, n_tiles):
    i = pl.program_id(1)
    tm = rk_ref.shape[0]
    has_prev = jnp.logical_and(i != 0, i != 1)
    has_next = jnp.logical_and(i != 0, i != n_tiles - 1)

    def shifted(x_ref, prev_ref, next_ref, mu_ref):
        x = x_ref[...]
        rows = lax.broadcasted_iota(jnp.int32, x.shape, 0)
        halo_prev = jnp.where(has_prev, prev_ref[7:8, :], 0.0)
        halo_next = jnp.where(has_next, next_ref[0:1, :], 0.0)
        prev = jnp.where(rows == 0, halo_prev, pltpu.roll(x, 1, 0))
        nxt = jnp.where(rows == tm - 1, halo_next, pltpu.roll(x, tm - 1, 0))
        return x + (prev - x) * mu_ref[0:1, :] + (nxt - x) * mu_ref[1:2, :]

    rk = shifted(rk_ref, rk_prev_ref, rk_next_ref, mu_rk_ref)
    rest = shifted(rest_ref, rest_prev_ref, rest_next_ref, mu_rest_ref)
    w = RWKV_W
    r = rk[:, 0:w]
    k = rk[:, w:2 * w]
    v = rest[:, 0:w]
    xw = rest[:, w:w + RWKV_DECAY_LORA]
    xa = rest[:, w + RWKV_DECAY_LORA:w + RWKV_DECAY_LORA + RWKV_AAA_LORA]
    xg = rest[:, w + RWKV_DECAY_LORA + RWKV_AAA_LORA:]

    ones = _group_ones(w, RWKV_HEAD_DIM)
    kk = k * kk_ref[...]
    kk = kk * lax.rsqrt(jnp.maximum(_dot2_exact_rhs(kk * kk, ones), 1e-12))
    rows_out[ROW_R] = r
    rows_out[ROW_A] = -kk
    v_out[...] = v
    tw = jnp.tanh(xw)
    k_sum = jnp.zeros_like(k)
    for d in range(2):
        decay_rate = jax.nn.sigmoid(w0_ref[d:d + 1, :] + _dot3(tw, w2_ref[d])) * RWKV_DECAY_SCALE
        a = jax.nn.sigmoid(a0_ref[d:d + 1, :] + _dot3(xa, a2_ref[d]))
        k_d = k * (1.0 + (a - 1.0) * ka_ref[...])
        rows_out[ROW_W + d] = jnp.exp(-decay_rate)
        rows_out[ROW_K + d] = k_d
        rows_out[ROW_B + d] = kk * a
        k_sum = k_sum + k_d
    gate_out[...] = _dot3(jax.nn.sigmoid(xg), g2_ref[...])
    bonus_out[...] = _dot2_exact_rhs(r * k_sum * rk_gain_ref[...], ones) * v


def _rwkv_prep(u, mu, w0, w2, a0, a2, g2, k_k, k_a, r_k):
    bsz, t, _ = u.shape
    tm = ROW_TILE
    n_tiles = t // tm
    w = RWKV_W
    rk_blk = U_RWKV_RK // (2 * w)
    rest_blk = U_RWKV_REST // RWKV_REST_COLS
    sub = tm // 8
    n_sub = t // 8
    prev_idx = lambda b, i: jnp.maximum(i * sub - 1, 0)
    next_idx = lambda b, i: jnp.minimum((i + 1) * sub, n_sub - 1)
    row = lambda a: a.reshape(1, -1)
    const = lambda shape: pl.BlockSpec(shape, lambda b, i: (0,) * len(shape))
    tok = lambda width: pl.BlockSpec((None, tm, width), lambda b, i: (b, i, 0))
    rows_spec = pl.BlockSpec((N_SCAN_ROWS, None, tm, w), lambda b, i: (0, b, i, 0))
    sd = lambda *lead: jax.ShapeDtypeStruct((*lead, bsz, t, w), F32)
    return pl.pallas_call(
        functools.partial(_rwkv_prep_kernel, n_tiles=n_tiles),
        grid=(bsz, n_tiles),
        in_specs=[pl.BlockSpec((None, tm, 2 * w), lambda b, i: (b, i, rk_blk)),
                  pl.BlockSpec((None, 8, 2 * w), lambda b, i: (b, prev_idx(b, i), rk_blk)),
                  pl.BlockSpec((None, 8, 2 * w), lambda b, i: (b, next_idx(b, i), rk_blk)),
                  pl.BlockSpec((None, tm, RWKV_REST_COLS), lambda b, i: (b, i, rest_blk)),
                  pl.BlockSpec((None, 8, RWKV_REST_COLS), lambda b, i: (b, prev_idx(b, i), rest_blk)),
                  pl.BlockSpec((None, 8, RWKV_REST_COLS), lambda b, i: (b, next_idx(b, i), rest_blk)),
                  const((2, 2 * w)), const((2, RWKV_REST_COLS)),
                  const((2, w)), const((2, RWKV_DECAY_LORA, w)), const((2, w)), const((2, RWKV_AAA_LORA, w)),
                  const((RWKV_GATE_LORA, w)), const((1, w)), const((1, w)), const((1, w))],
        out_specs=[rows_spec, tok(w), tok(w), tok(w)],
        out_shape=[sd(N_SCAN_ROWS), sd(), sd(), sd()],
        compiler_params=_params("parallel", "parallel"),
        name="rwkv_prep",
    )(u, u, u, u, u, u, mu[:, :2 * w], mu[:, 2 * w:], w0, w2, a0, a2, g2, row(k_k), row(k_a), row(r_k))


def _transpose_tokens(z_ref, scr, bsz):
    w = RWKV_W
    for b in range(bsz):
        scr[b * w:(b + 1) * w, :] = z_ref[b].T
    if bsz * w < scr.shape[0]:
        scr[bsz * w:, :] = jnp.zeros((scr.shape[0] - bsz * w, scr.shape[1]), F32)


def _layout_rows_kernel(z_ref, o_ref, scr, *, bsz):
    n = RWKV_HEAD_DIM
    _transpose_tokens(z_ref, scr, bsz)
    for j in range(n):
        x = scr[pl.ds(j, LANES // 2, stride=n), :]
        o_ref[j] = jnp.concatenate([x, x], axis=0).T


def _layout_v_kernel(z_ref, o_ref, scr, *, bsz):
    n = RWKV_HEAD_DIM
    ts = z_ref.shape[1]
    _transpose_tokens(z_ref, scr, bsz)
    for i in range(n // 2):
        x0 = scr[pl.ds(i, LANES // 2, stride=n), :]
        x1 = scr[pl.ds(n // 2 + i, LANES // 2, stride=n), :]
        o_ref[pl.ds(i, ts, stride=n // 2), :] = jnp.concatenate([x0, x1], axis=0).T


def _scan_layout(rows, v):
    g, bsz, t, w = rows.shape
    n = RWKV_HEAD_DIM
    ts = LANES
    scr = pltpu.VMEM((LANES // 2 * n, ts), F32)
    rows_l = pl.pallas_call(
        functools.partial(_layout_rows_kernel, bsz=bsz),
        grid=(g, t // ts),
        in_specs=[pl.BlockSpec((None, bsz, ts, w), lambda k, i: (k, 0, i, 0))],
        out_specs=pl.BlockSpec((None, n, ts, LANES), lambda k, i: (k, 0, i, 0)),
        out_shape=jax.ShapeDtypeStruct((g, n, t, LANES), F32),
        scratch_shapes=[scr],
        compiler_params=_params("parallel", "parallel"),
        name="rwkv_layout_rows",
    )(rows)
    v_l = pl.pallas_call(
        functools.partial(_layout_v_kernel, bsz=bsz),
        grid=(t // ts,),
        in_specs=[pl.BlockSpec((bsz, ts, w), lambda i: (0, i, 0))],
        out_specs=pl.BlockSpec((ts * n // 2, LANES), lambda i: (i, 0)),
        out_shape=jax.ShapeDtypeStruct((t * n // 2, LANES), F32),
        scratch_shapes=[scr],
        compiler_params=_params("parallel"),
        name="rwkv_layout_v",
    )(v)
    return rows_l, v_l


def _rwkv_scan_kernel(r_ref, a_ref, w_ref, k_ref, b_ref, v_ref, y_ref, s_ref, sa_ref):
    n = RWKV_HEAD_DIM
    half = n // 2
    ts = r_ref.shape[1]
    fwd = pl.program_id(0) == 0

    @pl.when(pl.program_id(1) == 0)
    def _():
        s_ref[...] = jnp.zeros_like(s_ref)

    t_first = jnp.where(fwd, 0, ts - 1)
    acc = jnp.zeros((half, LANES), F32)
    for j in range(n):
        acc = acc + s_ref[j] * a_ref[j, pl.ds(t_first, 1), :]
    sa_ref[...] = acc

    def step(m, carry):
        t = jnp.where(fwd, m, ts - 1 - m)
        tn = jnp.clip(jnp.where(fwd, t + 1, t - 1), 0, ts - 1)
        sa = sa_ref[...]
        v = v_ref[t]
        y = jnp.zeros((half, LANES), F32)
        sa_next = jnp.zeros((half, LANES), F32)
        for j in range(n):
            s = (s_ref[j] * w_ref[j, pl.ds(t, 1), :] + sa * b_ref[j, pl.ds(t, 1), :]) + v * k_ref[j, pl.ds(t, 1), :]
            s_ref[j] = s
            y = y + s * r_ref[j, pl.ds(t, 1), :]
            sa_next = sa_next + s * a_ref[j, pl.ds(tn, 1), :]
        y_ref[t] = y
        sa_ref[...] = sa_next
        return carry

    lax.fori_loop(0, ts, step, 0)


def _rwkv_scan(rows, v, n_ctx):
    _, n, t, lanes = rows.shape
    ts = SCAN_STEPS
    nb = t // ts
    ncb = n_ctx // ts

    def blk(d, s):
        back = jnp.where(s < ncb, ncb - 1 - s, nb - 1 - (s - ncb))
        return jnp.where(d == 0, s, back)

    shared = lambda kind: pl.BlockSpec((None, n, ts, lanes), lambda d, s: (kind, 0, blk(d, s), 0))
    per_dir = lambda kind: pl.BlockSpec((None, n, ts, lanes), lambda d, s: (kind + d, 0, blk(d, s), 0))
    return pl.pallas_call(
        _rwkv_scan_kernel,
        grid=(2, nb),
        in_specs=[shared(ROW_R), shared(ROW_A), per_dir(ROW_W), per_dir(ROW_K), per_dir(ROW_B),
                  pl.BlockSpec((ts, n // 2, lanes), lambda d, s: (blk(d, s), 0, 0))],
        out_specs=pl.BlockSpec((None, ts, n // 2, lanes), lambda d, s: (d, blk(d, s), 0, 0)),
        out_shape=jax.ShapeDtypeStruct((2, t, n // 2, lanes), F32),
        scratch_shapes=[pltpu.VMEM((n, n // 2, lanes), F32), pltpu.VMEM((n // 2, lanes), F32)],
        compiler_params=_params("arbitrary", "arbitrary"),
        name="rwkv_scan",
    )(rows, rows, rows, rows, rows, v)


def _rwkv_readout_kernel(yf_ref, yb_ref, bonus_ref, gate_ref, g_ref, b_ref, o_ref, scr, *, bsz):
    n = RWKV_HEAD_DIM
    w = RWKV_W
    ts = o_ref.shape[1]
    for i in range(n // 2):
        rows = pl.ds(i, ts, stride=n // 2)
        yt = (yf_ref[rows, :] + yb_ref[rows, :]).T
        scr[pl.ds(i, LANES // 2, stride=n), :] = yt[:LANES // 2]
        scr[pl.ds(n // 2 + i, LANES // 2, stride=n), :] = yt[LANES // 2:]
    ones = _group_ones(w, n)
    inv = 1.0 / n
    for b in range(bsz):
        y = scr[b * w:(b + 1) * w, :].T
        mean = _dot2_exact_rhs(y, ones) * inv
        yc = y - mean
        var = _dot2_exact_rhs(yc * yc, ones) * inv
        yn = yc * lax.rsqrt(var + RWKV_GN_EPS) * g_ref[...] + b_ref[...]
        o_ref[b] = ((yn + bonus_ref[b]) * gate_ref[b]).astype(o_ref.dtype)


def _rwkv_readout(y, bonus, gate, ln_g, ln_b):
    bsz, t, w = bonus.shape
    n = RWKV_HEAD_DIM
    ts = LANES
    tok = pl.BlockSpec((bsz, ts, w), lambda i: (0, i, 0))
    vec = pl.BlockSpec((1, w), lambda i: (0, 0))
    return pl.pallas_call(
        functools.partial(_rwkv_readout_kernel, bsz=bsz),
        grid=(t // ts,),
        in_specs=[pl.BlockSpec((None, ts * n // 2, LANES), lambda i: (0, i, 0)),
                  pl.BlockSpec((None, ts * n // 2, LANES), lambda i: (1, i, 0)), tok, tok, vec, vec],
        out_specs=tok,
        out_shape=jax.ShapeDtypeStruct((bsz, t, w), BF16),
        scratch_shapes=[pltpu.VMEM((LANES // 2 * n, ts), F32)],
        compiler_params=_params("parallel"),
        name="rwkv_readout",
    )(y, y, bonus, gate, ln_g.reshape(1, w), ln_b.reshape(1, w))


def _merge_kernel(ya_ref, yr_ref, yw_ref, g0a_ref, g0b_ref, g1a_ref, g1b_ref, g2a_ref, g2b_ref,
                  x_ref, gate1_ref, a2_ref, b2_ref, wb_ref, wo_ref, wr_ref, br_ref,
                  x_out, h_out, ids_out, wts_out, cnt_out):
    first = jnp.logical_and(pl.program_id(0) == 0, pl.program_id(1) == 0)

    @pl.when(first)
    def _():
        cnt_out[...] = jnp.zeros_like(cnt_out)

    gate = lambda lo, hi: jax.nn.sigmoid(jnp.concatenate([lo[...], hi[...]], axis=1))
    merged = (gate(g0a_ref, g0b_ref) * _dot(ya_ref[...], wb_ref[0])
              + gate(g1a_ref, g1b_ref) * _dot(yr_ref[...], wb_ref[1])
              + gate(g2a_ref, g2b_ref) * _dot(yw_ref[...], wb_ref[2]))
    x = x_ref[...] + gate1_ref[...] * _dot(merged.astype(BF16), wo_ref[...])
    x_out[...] = x
    h = x * lax.rsqrt(jnp.mean(x * x, axis=-1, keepdims=True) + NORM_EPS) * a2_ref[...] + b2_ref[...]
    h_out[...] = h.astype(BF16)

    tm = x.shape[0]
    logits = _dot3(h, wr_ref[...]) + br_ref[...]
    lane = lax.broadcasted_iota(jnp.int32, (tm, LANES), 1)
    lane_f = lane.astype(F32)
    neg = -jnp.inf
    big = float(LANES)
    first = lambda hit: jnp.min(jnp.where(hit, lane_f, big), axis=-1, keepdims=True).astype(jnp.int32)
    is_grp = jnp.logical_and(lane >= MOE_EXPERTS, lane < MOE_EXPERTS + MOE_GROUPS)
    gl = jnp.where(is_grp, logits, neg)
    gmax = jnp.max(gl, axis=-1, keepdims=True)
    gidx = first(gl == gmax) - MOE_EXPERTS
    p_grp = 1.0 / jnp.sum(jnp.where(is_grp, jnp.exp(gl - gmax), 0.0), axis=-1, keepdims=True)
    in_grp = jnp.logical_and(lane < MOE_EXPERTS, lane // MOE_EXPERTS_PER_GROUP == gidx)
    el = jnp.where(in_grp, logits, neg)
    v1 = jnp.max(el, axis=-1, keepdims=True)
    i1 = first(el == v1)
    el2 = jnp.where(lane == i1, neg, el)
    v2 = jnp.max(el2, axis=-1, keepdims=True)
    i2 = first(el2 == v2)
    e2 = jnp.exp(v2 - v1)
    w1 = p_grp / (1.0 + e2)
    w2 = p_grp * e2 / (1.0 + e2)
    wts_out[...] = jnp.where(lane == 0, w1, jnp.where(lane == 1, w2, 0.0))

    onehot = jnp.where(jnp.logical_or(lane == i1, lane == i2), 1.0, 0.0)
    rr = lax.broadcasted_iota(jnp.int32, (tm, tm), 0)
    cc = lax.broadcasted_iota(jnp.int32, (tm, tm), 1)
    below = jnp.where(cc < rr, 1.0, 0.0).astype(BF16)
    before = _dot(below, onehot.astype(BF16)) + cnt_out[0:1, :]
    rank1 = jnp.sum(jnp.where(lane == i1, before, 0.0), axis=-1, keepdims=True).astype(jnp.int32)
    rank2 = jnp.sum(jnp.where(lane == i2, before, 0.0), axis=-1, keepdims=True).astype(jnp.int32)
    ids_out[...] = jnp.where(lane == 0, i1, jnp.where(lane == 1, i2, jnp.where(lane == 2, rank1,
                                                                                  jnp.where(lane == 3, rank2, 0))))
    cnt_out[...] = cnt_out[...] + jnp.sum(onehot, axis=0, keepdims=True)


def _merge(ya, yr, yw, u, x, gate1, a2, b2, w_branch, w_out, w_router, b_router, part):
    bsz, t, d = x.shape
    nb = bsz // MOE_PARTS
    b0 = part * nb
    tm = ROW_TILE
    sel = lambda b, i: (2 * (b + b0) + jnp.minimum(i, 1), 0, 0)
    tok = lambda width, blk=0: pl.BlockSpec((None, tm, width), lambda b, i: (b + b0, i, blk))
    own = lambda width: pl.BlockSpec((None, tm, width), lambda b, i: (b, i, 0))
    const = lambda shape: pl.BlockSpec(shape, lambda b, i: (0,) * len(shape))
    mod = pl.BlockSpec((None, 1, d), sel)
    n_gate_blocks = N_BRANCH * d // GATE_BLOCK
    return pl.pallas_call(
        _merge_kernel,
        grid=(nb, t // tm),
        in_specs=[tok(BRANCH_W), tok(BRANCH_W), tok(BRANCH_W)]
                 + [tok(GATE_BLOCK, U_GATE // GATE_BLOCK + k) for k in range(n_gate_blocks)]
                 + [tok(d),
                  mod, mod, mod,
                  const((N_BRANCH, BRANCH_W, d)), const((d, d)), const((d, LANES)), const((1, LANES))],
        out_specs=[tok(d), own(d), own(LANES), own(LANES), const((8, LANES))],
        out_shape=[jax.ShapeDtypeStruct((bsz, t, d), F32), jax.ShapeDtypeStruct((nb, t, d), BF16),
                   jax.ShapeDtypeStruct((nb, t, LANES), jnp.int32), jax.ShapeDtypeStruct((nb, t, LANES), F32),
                   jax.ShapeDtypeStruct((8, LANES), F32)],
        input_output_aliases={3 + n_gate_blocks: 0},
        compiler_params=_params("arbitrary", "arbitrary"),
        name="merge_router",
    )(ya, yr, yw, *([u] * n_gate_blocks), x, gate1, a2, b2, w_branch, w_out, w_router, b_router)


def _moe_kernel(be_ref, na_ref, x_ref, wg_ref, wu_ref, wd_ref, o_ref, wg_s, wu_s, wd_s):
    i = pl.program_id(0)
    active = i < na_ref[0]
    new_expert = jnp.logical_or(i == 0, be_ref[i] != be_ref[jnp.maximum(i - 1, 0)])

    @pl.when(jnp.logical_and(active, new_expert))
    def _():
        wg_s[...] = wg_ref[...].astype(BF16)
        wu_s[...] = wu_ref[...].astype(BF16)
        wd_s[...] = wd_ref[...].astype(BF16)

    @pl.when(active)
    def _():
        x = x_ref[...]
        act = _silu(_dot(x, wg_s[...])) * _dot(x, wu_s[...])
        o_ref[...] = _dot(act.astype(BF16), wd_s[...]).astype(o_ref.dtype)

    @pl.when(i >= na_ref[0])
    def _():
        o_ref[...] = jnp.zeros_like(o_ref)


def _moe_experts(buf, block_expert, n_active, w_gate, w_up, w_down, layer):
    rows, d = buf.shape
    hid = w_gate.shape[-1]
    grid_spec = pltpu.PrefetchScalarGridSpec(
        num_scalar_prefetch=2,
        grid=(rows // MOE_BLOCK,),
        in_specs=[pl.BlockSpec((MOE_BLOCK, d), lambda i, be, na: (i, 0)),
                  pl.BlockSpec((None, None, d, hid), lambda i, be, na: (layer, be[i], 0, 0)),
                  pl.BlockSpec((None, None, d, hid), lambda i, be, na: (layer, be[i], 0, 0)),
                  pl.BlockSpec((None, None, hid, d), lambda i, be, na: (layer, be[i], 0, 0))],
        out_specs=pl.BlockSpec((MOE_BLOCK, d), lambda i, be, na: (i, 0)),
        scratch_shapes=[pltpu.VMEM((d, hid), BF16), pltpu.VMEM((d, hid), BF16), pltpu.VMEM((hid, d), BF16)])
    return pl.pallas_call(
        _moe_kernel,
        grid_spec=grid_spec,
        out_shape=jax.ShapeDtypeStruct((rows, d), F32),
        compiler_params=_params("arbitrary"),
        name="moe_experts",
    )(block_expert, n_active, buf, w_gate, w_up, w_down)


def _combine_kernel(x_ref, *refs):
    y_refs, (w_ref, g_ref, o_ref) = refs[:-3], refs[-3:]
    tm = x_ref.shape[0]
    w = w_ref[...]
    for c, y_ref in enumerate(y_refs):
        cols = slice(c * LANES, (c + 1) * LANES)
        y = y_ref[pl.ds(0, tm, stride=2), :] * w[:, 0:1] + y_ref[pl.ds(1, tm, stride=2), :] * w[:, 1:2]
        o_ref[:, cols] = x_ref[:, cols] + g_ref[:, cols] * y


def _combine(x, y_pairs, wts, gate2, part):
    bsz, t, d = x.shape
    nb = y_pairs.shape[0]
    b0 = part * nb
    tm = ROW_TILE
    sel = lambda b, i: (2 * (b + b0) + jnp.minimum(i, 1), 0, 0)
    x_spec = pl.BlockSpec((None, tm, d), lambda b, i: (b + b0, i, 0))
    return pl.pallas_call(
        _combine_kernel,
        grid=(nb, t // tm),
        in_specs=[x_spec]
                 + [pl.BlockSpec((None, 2 * tm, LANES), functools.partial(lambda c, b, i: (b, i, c), c))
                    for c in range(d // LANES)]
                 + [pl.BlockSpec((None, tm, LANES), lambda b, i: (b, i, 0)), pl.BlockSpec((None, 1, d), sel)],
        out_specs=x_spec,
        out_shape=jax.ShapeDtypeStruct((bsz, t, d), F32),
        input_output_aliases={0: 0},
        compiler_params=_params("parallel", "parallel"),
        name="moe_combine",
    )(x, *([y_pairs] * (d // LANES)), wts, gate2)


def _moe(h, ids, wts, counts, w_gate, w_up, w_down, layer):
    bsz, t, d = h.shape
    n_tok = bsz * t
    n_pair = 2 * n_tok
    n_blocks = -(-n_pair // MOE_BLOCK) + MOE_EXPERTS
    counts = counts[0, :MOE_EXPERTS].astype(jnp.int32)
    padded = (counts + MOE_BLOCK - 1) // MOE_BLOCK * MOE_BLOCK
    pad_end = jnp.cumsum(padded)
    pad_start = pad_end - padded
    expert = ids[..., 0:2].reshape(n_pair)
    rank = ids[..., 2:4].reshape(n_pair)
    dest = pad_start[expert] + rank
    token = jnp.arange(n_pair, dtype=jnp.int32) // 2
    src = jnp.zeros((n_blocks * MOE_BLOCK,), jnp.int32).at[dest].set(token, unique_indices=True)
    block_start = jnp.arange(n_blocks, dtype=jnp.int32) * MOE_BLOCK
    block_expert = jnp.minimum(jnp.sum((pad_end[None, :] <= block_start[:, None]).astype(jnp.int32), axis=1),
                               MOE_EXPERTS - 1).astype(jnp.int32)
    n_active = (pad_end[-1:] // MOE_BLOCK).astype(jnp.int32)
    buf = jnp.take(h.reshape(n_tok, d), src, axis=0)
    yb = _moe_experts(buf, block_expert, n_active, w_gate, w_up, w_down, layer)
    return jnp.take(yb, dest, axis=0).reshape(bsz, 2 * t, d)


def _rope_tables(n_ctx, n_lat, head_dim):
    rows = n_lat // GRID_W
    row = jnp.broadcast_to(jnp.arange(rows, dtype=F32)[:, None], (rows, GRID_W)).reshape(-1)
    col = jnp.broadcast_to(jnp.arange(GRID_W, dtype=F32)[None, :], (rows, GRID_W)).reshape(-1)
    quarter = head_dim // 4
    inv_freq = ROPE_THETA ** (-jnp.arange(quarter, dtype=F32) / quarter)
    ang = jnp.stack([row[:, None] * inv_freq, col[:, None] * inv_freq], axis=1)
    cos, sin = jnp.cos(ang), jnp.sin(ang)
    cos_t = jnp.stack([cos, cos], axis=2).reshape(n_lat, head_dim)
    sin_t = jnp.stack([-sin, sin], axis=2).reshape(n_lat, head_dim)
    cos_t = jnp.concatenate([jnp.ones((n_ctx, head_dim), F32), cos_t], axis=0)
    sin_t = jnp.concatenate([jnp.zeros((n_ctx, head_dim), F32), sin_t], axis=0)
    rep = LANES // head_dim
    return jnp.tile(cos_t, (1, rep)), jnp.tile(sin_t, (1, rep))


def kernel(x, c, ctx, c_ctx, ada_w, ada_b, norm1_g, norm2_g, w_in, att_qn_g, att_kn_g, ret_decay_logit, ret_gn_g, rwkv_mu, rwkv_w0, rwkv_w2, rwkv_a0, rwkv_a2, rwkv_g2, rwkv_k_k, rwkv_k_a, rwkv_r_k, rwkv_ln_g, rwkv_ln_b, w_branch, w_out, router_grp_w, router_grp_b, router_exp_w, router_exp_b, moe_w_gate, moe_w_up, moe_w_down):
    bsz, n_lat, d = x.shape
    n_ctx = ctx.shape[1]
    depth = ada_w.shape[0]
    assert d == D_MODEL and n_ctx == ROW_TILE and n_lat % ROW_TILE == 0 and n_lat % GRID_W == 0
    assert 2 * bsz * RWKV_HEADS <= LANES
    t_all = n_ctx + n_lat
    assert t_all % LANES == 0 and n_ctx % SCAN_STEPS == 0

    att_cos, att_sin = _rope_tables(n_ctx, n_lat, ATT_HEAD_DIM)
    ret_cos, ret_sin = _rope_tables(n_ctx, n_lat, RET_HEAD_DIM)

    rows = -(-(bsz + 1) // 8) * 8
    cvec = jnp.zeros((rows, d), F32).at[:bsz].set(c).at[bsz].set(c_ctx)
    mods = _modulation(cvec, ada_w, ada_b)

    xs = jnp.concatenate([ctx, x], axis=1)
    for layer in range(depth):
        m = mods[layer].reshape(rows, 6, d)
        pick = lambda j: jnp.stack([jnp.broadcast_to(m[bsz, j], (bsz, d)), m[:bsz, j]], axis=1).reshape(2 * bsz, 1, d)
        sh1, sc1, g1, sh2, sc2, g2 = (pick(j) for j in range(6))
        w_l = w_in[layer].astype(BF16)
        u = _in_proj(xs, norm1_g[layer] * (1.0 + sc1), sh1, w_l)

        score_bound = (ATT_HEAD_DIM * ATT_Q_SCALE * ATT_BOUND_MARGIN) * (
            jnp.max(jnp.abs(att_qn_g[layer])) * jnp.max(jnp.abs(att_kn_g[layer])))
        bounded = 2.0 * score_bound <= ATT_EXP2_RANGE
        q_att, kv_att = _att_prep(u, att_cos, att_sin, att_qn_g[layer], att_kn_g[layer],
                                  jnp.where(bounded, -score_bound, 0.0))
        ya = _attention(q_att, kv_att, bounded.astype(jnp.int32).reshape(1), n_ctx)

        log_gamma = jax.nn.log_sigmoid(ret_decay_logit[layer].astype(F32))
        yr = _retention(u, ret_cos, ret_sin, log_gamma, ret_gn_g[layer], n_ctx)

        rows_t, v_t, gate, bonus = _rwkv_prep(
            u, rwkv_mu[layer], rwkv_w0[layer], rwkv_w2[layer], rwkv_a0[layer], rwkv_a2[layer], rwkv_g2[layer],
            rwkv_k_k[layer], rwkv_k_a[layer], rwkv_r_k[layer].reshape(-1))
        rows_s, v_s = _scan_layout(rows_t, v_t)
        y_scan = _rwkv_scan(rows_s, v_s.reshape(t_all, RWKV_HEAD_DIM // 2, LANES), n_ctx)
        yw = _rwkv_readout(y_scan.reshape(2, t_all * RWKV_HEAD_DIM // 2, LANES), bonus, gate,
                           rwkv_ln_g[layer], rwkv_ln_b[layer])

        w_router = jnp.zeros((d, LANES), F32).at[:, :MOE_EXPERTS].set(router_exp_w[layer]).at[
            :, MOE_EXPERTS:MOE_EXPERTS + MOE_GROUPS].set(router_grp_w[layer])
        b_router = jnp.zeros((1, LANES), F32).at[0, :MOE_EXPERTS].set(router_exp_b[layer]).at[
            0, MOE_EXPERTS:MOE_EXPERTS + MOE_GROUPS].set(router_grp_b[layer])
        routed = []
        for part in range(MOE_PARTS):
            xs, h2, ids, wts, counts = _merge(
                ya, yr, yw, u, xs, g1, norm2_g[layer] * (1.0 + sc2), sh2,
                w_branch[layer].astype(BF16), w_out[layer].astype(BF16), w_router, b_router, part)
            routed.append((h2, ids, wts, counts))
        pairs = [_moe(h2, ids, wts, counts, moe_w_gate, moe_w_up, moe_w_down, layer)
                 for h2, ids, wts, counts in routed]
        for part in range(MOE_PARTS):
            xs = _combine(xs, pairs[part], routed[part][2], g2, part)
    return xs[:, n_ctx:]
```

```python
import functools

import jax
import jax.numpy as jnp
from jax import lax
from jax.experimental import pallas as pl
from jax.experimental.pallas import tpu as pltpu

F32 = jnp.float32
BF16 = jnp.bfloat16

D_MODEL = 1024
GRID_W = 64
NORM_EPS = 1e-6
ROPE_THETA = 10000.0

ATT_HEADS = 8
ATT_KV_HEADS = 2
ATT_HEAD_DIM = 64
ATT_GROUP = ATT_HEADS // ATT_KV_HEADS
ATT_W = ATT_HEADS * ATT_HEAD_DIM
ATT_KV_W = ATT_KV_HEADS * ATT_HEAD_DIM

RET_HEADS = 4
RET_HEAD_DIM = 128
RET_CHUNK = 128
RET_W = RET_HEADS * RET_HEAD_DIM

RWKV_HEADS = 8
RWKV_HEAD_DIM = 64
RWKV_W = RWKV_HEADS * RWKV_HEAD_DIM
RWKV_DECAY_LORA = 64
RWKV_AAA_LORA = 64
RWKV_GATE_LORA = 128
RWKV_GN_EPS = 64e-5
RWKV_DECAY_SCALE = 0.6065306597126334
RWKV_COLS = 3 * RWKV_W + RWKV_DECAY_LORA + RWKV_AAA_LORA + RWKV_GATE_LORA

N_BRANCH = 3
BRANCH_W = 512
IN_COLS = ATT_W + 2 * ATT_KV_W + 4 * RET_W + RWKV_COLS + N_BRANCH * D_MODEL

MOE_GROUPS = 4
MOE_EXPERTS_PER_GROUP = 8
MOE_EXPERTS = MOE_GROUPS * MOE_EXPERTS_PER_GROUP
MOE_HIDDEN = 512
MOE_BLOCK = 256
MOE_PARTS = 1

LANES = 128
ROW_TILE = 256
ATT_Q_TILE = 128
ATT_KEY_TILE = 256
ATT_Q_SCALE = ATT_HEAD_DIM ** -0.5 * 1.4426950408889634
ATT_BOUND_MARGIN = 1.01
ATT_EXP2_RANGE = 100.0
SCAN_STEPS = 64
VMEM_LIMIT = 56 * 1024 * 1024

ROW_R, ROW_A, ROW_W, ROW_K, ROW_B, N_SCAN_ROWS = 0, 1, 2, 4, 6, 8

ATT_COLS = ATT_W + 2 * ATT_KV_W
RWKV_REST_COLS = RWKV_COLS - 2 * RWKV_W
U_RWKV_RK = 0
U_RET = U_RWKV_RK + 2 * RWKV_W
U_ATT = U_RET + 4 * RET_W
U_RWKV_REST = U_ATT + ATT_COLS
U_GATE = U_RWKV_REST + RWKV_REST_COLS
GATE_BLOCK = 512
IN_PROJ_MOVES = (((0, U_ATT, ATT_COLS), (ATT_COLS, U_RET, 4 * RET_W),
                  (ATT_COLS + 4 * RET_W, U_RWKV_RK, 2 * RWKV_W)),
                 ((0, 0, IN_COLS // 2),))
assert U_ATT + ATT_COLS == IN_COLS // 2 and ATT_COLS + 4 * RET_W + 2 * RWKV_W == IN_COLS // 2


def _params(*sem):
    return pltpu.CompilerParams(dimension_semantics=sem, vmem_limit_bytes=VMEM_LIMIT)


def _dot(a, b):
    return jnp.dot(a, b, preferred_element_type=F32)


def _dot_nt(a, b):
    return lax.dot_general(a, b, (((1,), (1,)), ((), ())), preferred_element_type=F32)


def _split(a):
    hi = a.astype(BF16)
    lo = (a - hi.astype(F32)).astype(BF16)
    return hi, lo


def _dot3(a, b):
    ah, al = _split(a)
    bh, bl = _split(b)
    return _dot(ah, bh) + (_dot(al, bh) + _dot(ah, bl))


def _dot2_exact_rhs(a, b_bf16):
    ah, al = _split(a)
    return _dot(ah, b_bf16) + _dot(al, b_bf16)


def _group_ones(width, group):
    r = lax.broadcasted_iota(jnp.int32, (width, width), 0) // group
    c = lax.broadcasted_iota(jnp.int32, (width, width), 1) // group
    return jnp.where(r == c, 1.0, 0.0).astype(BF16)


def _silu(x):
    return x * jax.nn.sigmoid(x)


def _swap_halves(x, quarter):
    n = x.shape[-1]
    lane = lax.broadcasted_iota(jnp.int32, x.shape, x.ndim - 1)
    up = pltpu.roll(x, n - quarter, x.ndim - 1)
    down = pltpu.roll(x, quarter, x.ndim - 1)
    return jnp.where(lane % (2 * quarter) < quarter, up, down)


def _mod_kernel(c_ref, w_ref, b_ref, o_ref):
    o_ref[...] = _dot3(_silu(c_ref[...]), w_ref[...]) + b_ref[...]


def _modulation(cvec, ada_w, ada_b):
    depth, d, cols = ada_w.shape
    rows = cvec.shape[0]
    tn = 1536
    return pl.pallas_call(
        _mod_kernel,
        grid=(depth, cols // tn),
        in_specs=[pl.BlockSpec((rows, d), lambda l, j: (0, 0)),
                  pl.BlockSpec((None, d, tn), lambda l, j: (l, 0, j)),
                  pl.BlockSpec((None, 1, tn), lambda l, j: (l, 0, j))],
        out_specs=pl.BlockSpec((None, rows, tn), lambda l, j: (l, 0, j)),
        out_shape=jax.ShapeDtypeStruct((depth, rows, cols), F32),
        compiler_params=_params("parallel", "parallel"),
        name="modulation",
    )(cvec, ada_w, ada_b.reshape(depth, 1, cols))


def _in_proj_kernel(x_ref, a_ref, b_ref, w_ref, o_ref):
    x = x_ref[...]
    ms = jnp.mean(x * x, axis=-1, keepdims=True)
    h = (x * lax.rsqrt(ms + NORM_EPS) * a_ref[...] + b_ref[...]).astype(BF16)
    for half, moves in enumerate(IN_PROJ_MOVES):
        @pl.when(pl.program_id(0) == half)
        def _():
            for src, dst, width in moves:
                o_ref[:, dst:dst + width] = _dot(h, w_ref[:, src:src + width])


def _in_proj(x, mod_a, mod_b, w):
    bsz, t, d = x.shape
    cols = w.shape[1]
    tm, tn = ROW_TILE, cols // 2
    sel = lambda j, b, i: (2 * b + jnp.minimum(i, 1), 0, 0)
    return pl.pallas_call(
        _in_proj_kernel,
        grid=(cols // tn, bsz, t // tm),
        in_specs=[pl.BlockSpec((None, tm, d), lambda j, b, i: (b, i, 0)),
                  pl.BlockSpec((None, 1, d), sel),
                  pl.BlockSpec((None, 1, d), sel),
                  pl.BlockSpec((d, tn), lambda j, b, i: (0, j))],
        out_specs=pl.BlockSpec((None, tm, tn), lambda j, b, i: (b, i, j)),
        out_shape=jax.ShapeDtypeStruct((bsz, t, cols), F32),
        compiler_params=_params("parallel", "parallel", "parallel"),
        name="in_proj",
    )(x, mod_a, mod_b, w)


def _att_prep_kernel(u_ref, cos_ref, sin_ref, qg_ref, kg_ref, shift_ref, q_out, kv_out):
    hd = ATT_HEAD_DIM
    ones = _group_ones(LANES, hd)
    cos = cos_ref[...]
    sin = sin_ref[...]
    lane = lax.broadcasted_iota(jnp.int32, cos.shape, 1)
    low = lane < hd

    def two_heads(y, extra):
        fill = jnp.where(lane == hd, extra, 0.0)
        return jnp.where(low, y, fill), jnp.where(low, pltpu.roll(y, hd, 1), fill)

    n_q = ATT_W // LANES
    for j in range(n_q + 1):
        x = u_ref[:, j * LANES:(j + 1) * LANES]
        is_q = j < n_q
        gain = qg_ref[...] if is_q else kg_ref[...]
        ms = _dot2_exact_rhs(x * x, ones) * (1.0 / hd)
        y = x * lax.rsqrt(ms + NORM_EPS) * gain
        y = y * cos + _swap_halves(y, hd // 4) * sin
        if is_q:
            y = y * ATT_Q_SCALE
        out, base = (q_out, 2 * j) if is_q else (kv_out, 0)
        for h, yh in enumerate(two_heads(y, shift_ref[...] if is_q else 1.0)):
            out[:, (base + h) * LANES:(base + h + 1) * LANES] = yh.astype(BF16)
    v = u_ref[:, ATT_W + ATT_KV_W:]
    for h, vh in enumerate(two_heads(v, 1.0)):
        kv_out[:, (2 + h) * LANES:(3 + h) * LANES] = vh.astype(BF16)


def _att_prep(u, cos, sin, qn_g, kn_g, shift):
    bsz, t, _ = u.shape
    tm = ROW_TILE
    rep = LANES // ATT_HEAD_DIM
    shift = jnp.broadcast_to(shift.astype(F32), (1, LANES))
    return pl.pallas_call(
        _att_prep_kernel,
        grid=(bsz, t // tm),
        in_specs=[pl.BlockSpec((None, tm, ATT_COLS), lambda b, i: (b, i, U_ATT // ATT_COLS)),
                  pl.BlockSpec((tm, LANES), lambda b, i: (i, 0)),
                  pl.BlockSpec((tm, LANES), lambda b, i: (i, 0)),
                  pl.BlockSpec((1, LANES), lambda b, i: (0, 0)),
                  pl.BlockSpec((1, LANES), lambda b, i: (0, 0)),
                  pl.BlockSpec((1, LANES), lambda b, i: (0, 0))],
        out_specs=[pl.BlockSpec((None, tm, ATT_HEADS * LANES), lambda b, i: (b, i, 0)),
                   pl.BlockSpec((None, tm, 2 * ATT_KV_HEADS * LANES), lambda b, i: (b, i, 0))],
        out_shape=[jax.ShapeDtypeStruct((bsz, t, ATT_HEADS * LANES), BF16),
                   jax.ShapeDtypeStruct((bsz, t, 2 * ATT_KV_HEADS * LANES), BF16)],
        compiler_params=_params("parallel", "parallel"),
        name="att_prep",
    )(u, cos, sin, jnp.tile(qn_g, rep).reshape(1, LANES), jnp.tile(kn_g, rep).reshape(1, LANES), shift)


def _att_kernel(bounded_ref, q_ref, kv_ref, o_ref, *, n_ctx, tq):
    i = pl.program_id(1)
    hd = ATT_HEAD_DIM
    k_ref = v_ref = kv_ref

    def run(n_keys, bounded):
        tk = ATT_KEY_TILE
        for g in range(ATT_KV_HEADS):
            q = jnp.concatenate(
                [q_ref[:, (ATT_GROUP * g + h) * LANES:(ATT_GROUP * g + h + 1) * LANES] for h in range(ATT_GROUP)],
                axis=0)
            scores = lambda c: _dot_nt(q, k_ref[c * tk:(c + 1) * tk, g * LANES:(g + 1) * LANES])
            if not bounded:
                m = jnp.full((ATT_GROUP * tq, LANES), -jnp.inf, F32)
                for c in range(n_keys // tk):
                    s = scores(c)
                    for part in range(tk // LANES):
                        m = jnp.maximum(m, s[:, part * LANES:(part + 1) * LANES])
                m = jnp.max(m, axis=-1, keepdims=True)
            acc = jnp.zeros((ATT_GROUP * tq, LANES), F32)
            for c in range(n_keys // tk):
                p = jnp.exp2(scores(c) if bounded else scores(c) - m).astype(BF16)
                acc = acc + _dot(p, v_ref[c * tk:(c + 1) * tk, (ATT_KV_HEADS + g) * LANES:(ATT_KV_HEADS + g + 1) * LANES])
            o = acc[:, :hd] / acc[:, hd:hd + 1]
            for h in range(ATT_GROUP):
                c0 = (ATT_GROUP * g + h) * hd
                o_ref[:, c0:c0 + hd] = o[h * tq:(h + 1) * tq].astype(o_ref.dtype)

    is_ctx = i < n_ctx // tq
    bounded = bounded_ref[0] == 1

    @pl.when(is_ctx)
    def _():
        run(n_ctx, False)

    @pl.when(jnp.logical_and(jnp.logical_not(is_ctx), bounded))
    def _():
        run(k_ref.shape[0], True)

    @pl.when(jnp.logical_and(jnp.logical_not(is_ctx), jnp.logical_not(bounded)))
    def _():
        run(k_ref.shape[0], False)


def _attention(q, kv, bounded, n_ctx):
    bsz, t, _ = q.shape
    tq = ATT_Q_TILE
    kv_w = kv.shape[-1]
    assert n_ctx % ATT_KEY_TILE == 0 and t % ATT_KEY_TILE == 0
    grid_spec = pltpu.PrefetchScalarGridSpec(
        num_scalar_prefetch=1,
        grid=(bsz, t // tq),
        in_specs=[pl.BlockSpec((None, tq, ATT_HEADS * LANES), lambda b, i, f: (b, i, 0)),
                  pl.BlockSpec((None, t, kv_w), lambda b, i, f: (b, 0, 0))],
        out_specs=pl.BlockSpec((None, tq, ATT_W), lambda b, i, f: (b, i, 0)))
    return pl.pallas_call(
        functools.partial(_att_kernel, n_ctx=n_ctx, tq=tq),
        grid_spec=grid_spec,
        out_shape=jax.ShapeDtypeStruct((bsz, t, ATT_W), BF16),
        compiler_params=_params("parallel", "parallel"),
        name="attention",
    )(bounded, q, kv)


def _ret_kernel(q_ref, k_ref, v_ref, g_ref, cos_ref, sin_ref, lg_ref, gn_ref, o_ref,
                qs_ref, ks_ref, kvf_ref, kvb_ref, sf_ref, sb_ref, *, n_ctx):
    c = RET_CHUNK
    t = q_ref.shape[0]
    n_chunks = t // c
    n_cc = n_ctx // c
    quarter = RET_HEAD_DIM // 4
    scale = RET_HEAD_DIM ** -0.5
    lg_f = lg_ref[0]
    lg_b = lg_ref[1]
    row = lax.broadcasted_iota(jnp.int32, (c, c), 0)
    col = lax.broadcasted_iota(jnp.int32, (c, c), 1)
    rowf = row.astype(F32)
    lag = (row - col).astype(F32)

    def chunk(ci):
        r0 = pl.multiple_of(ci * c, c)
        cos = cos_ref[pl.ds(r0, c), :]
        sin = sin_ref[pl.ds(r0, c), :]
        q = q_ref[pl.ds(r0, c), :]
        k = k_ref[pl.ds(r0, c), :]
        q = q * cos + _swap_halves(q, quarter) * sin
        k = (k * cos + _swap_halves(k, quarter) * sin) * scale
        return r0, q, k, v_ref[pl.ds(r0, c), :]

    d_key_f = jnp.exp(lg_f * (c - 1.0 - rowf))
    d_key_b = jnp.exp(lg_b * rowf)
    d_query_f = jnp.exp(lg_f * (rowf + 1.0))
    d_query_b = jnp.exp(lg_b * (float(c) - rowf))
    d_chunk_f = jnp.exp(lg_f * float(c))
    d_chunk_b = jnp.exp(lg_b * float(c))
    d_intra = (jnp.where(lag >= 0, jnp.exp(lg_f * jnp.maximum(lag, 0.0)), 0.0)
               + jnp.where(lag <= 0, jnp.exp(lg_b * jnp.maximum(-lag, 0.0)), 0.0))

    def summaries(n, carry):
        r0, q, k, v = chunk(n)
        qs_ref[pl.ds(r0, c), :] = q.astype(BF16)
        ks_ref[pl.ds(r0, c), :] = k.astype(BF16)
        vb = v.astype(BF16)
        kvf_ref[n] = _dot((k * d_key_f).T.astype(BF16), vb)
        kvb_ref[n] = _dot((k * d_key_b).T.astype(BF16), vb)
        return carry

    lax.fori_loop(0, n_chunks, summaries, 0, unroll=2)

    def state_f(n, s):
        sf_ref[n] = s.astype(BF16)
        return d_chunk_f * s + kvf_ref[n]

    def state_b(n, s):
        ci = jnp.where(n < n_cc, n_cc - 1 - n, n_chunks - 1 - (n - n_cc))
        sb_ref[ci] = s.astype(BF16)
        return d_chunk_b * s + kvb_ref[ci]

    zero = jnp.zeros((RET_HEAD_DIM, RET_HEAD_DIM), F32)
    lax.fori_loop(0, n_chunks, state_f, zero)
    lax.fori_loop(0, n_chunks, state_b, zero)
    gn = gn_ref[...]

    def outputs(n, carry):
        r0 = pl.multiple_of(n * c, c)
        qb = qs_ref[pl.ds(r0, c), :]
        vb = v_ref[pl.ds(r0, c), :].astype(BF16)
        scores = _dot_nt(qb, ks_ref[pl.ds(r0, c), :]) * d_intra
        y = (_dot(scores.astype(BF16), vb) + _dot(qb, sf_ref[n]) * d_query_f) + _dot(qb, sb_ref[n]) * d_query_b
        yn = y * lax.rsqrt(jnp.mean(y * y, axis=-1, keepdims=True) + NORM_EPS) * gn
        o_ref[pl.ds(r0, c), :] = (_silu(g_ref[pl.ds(r0, c), :]) * yn).astype(o_ref.dtype)
        return carry

    lax.fori_loop(0, n_chunks, outputs, 0, unroll=2)


def _retention(u, cos, sin, log_gamma, gn_g, n_ctx):
    bsz, t, _ = u.shape
    hd = RET_HEAD_DIM
    base = U_RET // hd
    spec = lambda off: pl.BlockSpec((None, t, hd), lambda b, h: (b, 0, base + off * RET_HEADS + h))
    lg = jnp.broadcast_to(log_gamma[:, :, None, None], (2, RET_HEADS, 1, LANES)).astype(F32)
    return pl.pallas_call(
        functools.partial(_ret_kernel, n_ctx=n_ctx),
        grid=(bsz, RET_HEADS),
        in_specs=[spec(0), spec(1), spec(2), spec(3),
                  pl.BlockSpec((t, hd), lambda b, h: (0, 0)),
                  pl.BlockSpec((t, hd), lambda b, h: (0, 0)),
                  pl.BlockSpec((2, None, 1, LANES), lambda b, h: (0, h, 0, 0)),
                  pl.BlockSpec((1, hd), lambda b, h: (0, h))],
        out_specs=pl.BlockSpec((None, t, hd), lambda b, h: (b, 0, h)),
        out_shape=jax.ShapeDtypeStruct((bsz, t, RET_W), BF16),
        scratch_shapes=[pltpu.VMEM((t, hd), BF16), pltpu.VMEM((t, hd), BF16),
                        pltpu.VMEM((t // RET_CHUNK, hd, hd), F32), pltpu.VMEM((t // RET_CHUNK, hd, hd), F32),
                        pltpu.VMEM((t // RET_CHUNK, hd, hd), BF16), pltpu.VMEM((t // RET_CHUNK, hd, hd), BF16)],
        compiler_params=_params("parallel", "parallel"),
        name="retention",
    )(u, u, u, u, cos, sin, lg, gn_g.reshape(1, RET_W))


def _rwkv_prep_kernel(rk_ref, rk_prev_ref, rk_next_ref, rest_ref, rest_prev_ref, rest_next_ref,
                      mu_rk_ref, mu_rest_ref, w0_ref, w2_ref, a0_ref, a2_ref, g2_ref, kk_ref, ka_ref, rk_gain_ref,
                      rows_out, v_out, gate_out, bonus_out, *, n_tiles):
    i = pl.program_id(1)
    tm = rk_ref.shape[0]
    has_prev = jnp.logical_and(i != 0, i != 1)
    has_next = jnp.logical_and(i != 0, i != n_tiles - 1)

    def shifted(x_ref, prev_ref, next_ref, mu_ref):
        x = x_ref[...]
        rows = lax.broadcasted_iota(jnp.int32, x.shape, 0)
        halo_prev = jnp.where(has_prev, prev_ref[7:8, :], 0.0)
        halo_next = jnp.where(has_next, next_ref[0:1, :], 0.0)
        prev = jnp.where(rows == 0, halo_prev, pltpu.roll(x, 1, 0))
        nxt = jnp.where(rows == tm - 1, halo_next, pltpu.roll(x, tm - 1, 0))
        return x + (prev - x) * mu_ref[0:1, :] + (nxt - x) * mu_ref[1:2, :]

    rk = shifted(rk_ref, rk_prev_ref, rk_next_ref, mu_rk_ref)
    rest = shifted(rest_ref, rest_prev_ref, rest_next_ref, mu_rest_ref)
    w = RWKV_W
    r = rk[:, 0:w]
    k = rk[:, w:2 * w]
    v = rest[:, 0:w]
    xw = rest[:, w:w + RWKV_DECAY_LORA]
    xa = rest[:, w + RWKV_DECAY_LORA:w + RWKV_DECAY_LORA + RWKV_AAA_LORA]
    xg = rest[:, w + RWKV_DECAY_LORA + RWKV_AAA_LORA:]

    ones = _group_ones(w, RWKV_HEAD_DIM)
    kk = k * kk_ref[...]
    kk = kk * lax.rsqrt(jnp.maximum(_dot2_exact_rhs(kk * kk, ones), 1e-12))
    rows_out[ROW_R] = r
    rows_out[ROW_A] = -kk
    v_out[...] = v
    tw = jnp.tanh(xw)
    k_sum = jnp.zeros_like(k)
    for d in range(2):
        decay_rate = jax.nn.sigmoid(w0_ref[d:d + 1, :] + _dot3(tw, w2_ref[d])) * RWKV_DECAY_SCALE
        a = jax.nn.sigmoid(a0_ref[d:d + 1, :] + _dot3(xa, a2_ref[d]))
        k_d = k * (1.0 + (a - 1.0) * ka_ref[...])
        rows_out[ROW_W + d] = jnp.exp(-decay_rate)
        rows_out[ROW_K + d] = k_d
        rows_out[ROW_B + d] = kk * a
        k_sum = k_sum + k_d
    gate_out[...] = _dot3(jax.nn.sigmoid(xg), g2_ref[...])
    bonus_out[...] = _dot2_exact_rhs(r * k_sum * rk_gain_ref[...], ones) * v


def _rwkv_prep(u, mu, w0, w2, a0, a2, g2, k_k, k_a, r_k):
    bsz, t, _ = u.shape
    tm = ROW_TILE
    n_tiles = t // tm
    w = RWKV_W
    rk_blk = U_RWKV_RK // (2 * w)
    rest_blk = U_RWKV_REST // RWKV_REST_COLS
    sub = tm // 8
    n_sub = t // 8
    prev_idx = lambda b, i: jnp.maximum(i * sub - 1, 0)
    next_idx = lambda b, i: jnp.minimum((i + 1) * sub, n_sub - 1)
    row = lambda a: a.reshape(1, -1)
    const = lambda shape: pl.BlockSpec(shape, lambda b, i: (0,) * len(shape))
    tok = lambda width: pl.BlockSpec((None, tm, width), lambda b, i: (b, i, 0))
    rows_spec = pl.BlockSpec((N_SCAN_ROWS, None, tm, w), lambda b, i: (0, b, i, 0))
    sd = lambda *lead: jax.ShapeDtypeStruct((*lead, bsz, t, w), F32)
    return pl.pallas_call(
        functools.partial(_rwkv_prep_kernel, n_tiles=n_tiles),
        grid=(bsz, n_tiles),
        in_specs=[pl.BlockSpec((None, tm, 2 * w), lambda b, i: (b, i, rk_blk)),
                  pl.BlockSpec((None, 8, 2 * w), lambda b, i: (b, prev_idx(b, i), rk_blk)),
                  pl.BlockSpec((None, 8, 2 * w), lambda b, i: (b, next_idx(b, i), rk_blk)),
                  pl.BlockSpec((None, tm, RWKV_REST_COLS), lambda b, i: (b, i, rest_blk)),
                  pl.BlockSpec((None, 8, RWKV_REST_COLS), lambda b, i: (b, prev_idx(b, i), rest_blk)),
                  pl.BlockSpec((None, 8, RWKV_REST_COLS), lambda b, i: (b, next_idx(b, i), rest_blk)),
                  const((2, 2 * w)), const((2, RWKV_REST_COLS)),
                  const((2, w)), const((2, RWKV_DECAY_LORA, w)), const((2, w)), const((2, RWKV_AAA_LORA, w)),
                  const((RWKV_GATE_LORA, w)), const((1, w)), const((1, w)), const((1, w))],
        out_specs=[rows_spec, tok(w), tok(w), tok(w)],
        out_shape=[sd(N_SCAN_ROWS), sd(), sd(), sd()],
        compiler_params=_params("parallel", "parallel"),
        name="rwkv_prep",
    )(u, u, u, u, u, u, mu[:, :2 * w], mu[:, 2 * w:], w0, w2, a0, a2, g2, row(k_k), row(k_a), row(r_k))


def _transpose_tokens(z_ref, scr, bsz):
    w = RWKV_W
    for b in range(bsz):
        scr[b * w:(b + 1) * w, :] = z_ref[b].T
    if bsz * w < scr.shape[0]:
        scr[bsz * w:, :] = jnp.zeros((scr.shape[0] - bsz * w, scr.shape[1]), F32)


def _layout_rows_kernel(z_ref, o_ref, scr, *, bsz):
    n = RWKV_HEAD_DIM
    _transpose_tokens(z_ref, scr, bsz)
    for j in range(n):
        x = scr[pl.ds(j, LANES // 2, stride=n), :]
        o_ref[j] = jnp.concatenate([x, x], axis=0).T


def _layout_v_kernel(z_ref, o_ref, scr, *, bsz):
    n = RWKV_HEAD_DIM
    ts = z_ref.shape[1]
    _transpose_tokens(z_ref, scr, bsz)
    for i in range(n // 2):
        x0 = scr[pl.ds(i, LANES // 2, stride=n), :]
        x1 = scr[pl.ds(n // 2 + i, LANES // 2, stride=n), :]
        o_ref[pl.ds(i, ts, stride=n // 2), :] = jnp.concatenate([x0, x1], axis=0).T


def _scan_layout(rows, v):
    g, bsz, t, w = rows.shape
    n = RWKV_HEAD_DIM
    ts = LANES
    scr = pltpu.VMEM((LANES // 2 * n, ts), F32)
    rows_l = pl.pallas_call(
        functools.partial(_layout_rows_kernel, bsz=bsz),
        grid=(g, t // ts),
        in_specs=[pl.BlockSpec((None, bsz, ts, w), lambda k, i: (k, 0, i, 0))],
        out_specs=pl.BlockSpec((None, n, ts, LANES), lambda k, i: (k, 0, i, 0)),
        out_shape=jax.ShapeDtypeStruct((g, n, t, LANES), F32),
        scratch_shapes=[scr],
        compiler_params=_params("parallel", "parallel"),
        name="rwkv_layout_rows",
    )(rows)
    v_l = pl.pallas_call(
        functools.partial(_layout_v_kernel, bsz=bsz),
        grid=(t // ts,),
        in_specs=[pl.BlockSpec((bsz, ts, w), lambda i: (0, i, 0))],
        out_specs=pl.BlockSpec((ts * n // 2, LANES), lambda i: (i, 0)),
        out_shape=jax.ShapeDtypeStruct((t * n // 2, LANES), F32),
        scratch_shapes=[scr],
        compiler_params=_params("parallel"),
        name="rwkv_layout_v",
    )(v)
    return rows_l, v_l


def _rwkv_scan_kernel(r_ref, a_ref, w_ref, k_ref, b_ref, v_ref, y_ref, s_ref, sa_ref):
    n = RWKV_HEAD_DIM
    half = n // 2
    ts = r_ref.shape[1]
    fwd = pl.program_id(0) == 0

    @pl.when(pl.program_id(1) == 0)
    def _():
        s_ref[...] = jnp.zeros_like(s_ref)

    t_first = jnp.where(fwd, 0, ts - 1)
    acc = jnp.zeros((half, LANES), F32)
    for j in range(n):
        acc = acc + s_ref[j] * a_ref[j, pl.ds(t_first, 1), :]
    sa_ref[...] = acc

    def step(m, carry):
        t = jnp.where(fwd, m, ts - 1 - m)
        tn = jnp.clip(jnp.where(fwd, t + 1, t - 1), 0, ts - 1)
        sa = sa_ref[...]
        v = v_ref[t]
        y = jnp.zeros((half, LANES), F32)
        sa_next = jnp.zeros((half, LANES), F32)
        for j in range(n):
            s = (s_ref[j] * w_ref[j, pl.ds(t, 1), :] + sa * b_ref[j, pl.ds(t, 1), :]) + v * k_ref[j, pl.ds(t, 1), :]
            s_ref[j] = s
            y = y + s * r_ref[j, pl.ds(t, 1), :]
            sa_next = sa_next + s * a_ref[j, pl.ds(tn, 1), :]
        y_ref[t] = y
        sa_ref[...] = sa_next
        return carry

    lax.fori_loop(0, ts, step, 0)


def _rwkv_scan(rows, v, n_ctx):
    _, n, t, lanes = rows.shape
    ts = SCAN_STEPS
    nb = t // ts
    ncb = n_ctx // ts

    def blk(d, s):
        back = jnp.where(s < ncb, ncb - 1 - s, nb - 1 - (s - ncb))
        return jnp.where(d == 0, s, back)

    shared = lambda kind: pl.BlockSpec((None, n, ts, lanes), lambda d, s: (kind, 0, blk(d, s), 0))
    per_dir = lambda kind: pl.BlockSpec((None, n, ts, lanes), lambda d, s: (kind + d, 0, blk(d, s), 0))
    return pl.pallas_call(
        _rwkv_scan_kernel,
        grid=(2, nb),
        in_specs=[shared(ROW_R), shared(ROW_A), per_dir(ROW_W), per_dir(ROW_K), per_dir(ROW_B),
                  pl.BlockSpec((ts, n // 2, lanes), lambda d, s: (blk(d, s), 0, 0))],
        out_specs=pl.BlockSpec((None, ts, n // 2, lanes), lambda d, s: (d, blk(d, s), 0, 0)),
        out_shape=jax.ShapeDtypeStruct((2, t, n // 2, lanes), F32),
        scratch_shapes=[pltpu.VMEM((n, n // 2, lanes), F32), pltpu.VMEM((n // 2, lanes), F32)],
        compiler_params=_params("arbitrary", "arbitrary"),
        name="rwkv_scan",
    )(rows, rows, rows, rows, rows, v)


def _rwkv_readout_kernel(yf_ref, yb_ref, bonus_ref, gate_ref, g_ref, b_ref, o_ref, scr, *, bsz):
    n = RWKV_HEAD_DIM
    w = RWKV_W
    ts = o_ref.shape[1]
    for i in range(n // 2):
        rows = pl.ds(i, ts, stride=n // 2)
        yt = (yf_ref[rows, :] + yb_ref[rows, :]).T
        scr[pl.ds(i, LANES // 2, stride=n), :] = yt[:LANES // 2]
        scr[pl.ds(n // 2 + i, LANES // 2, stride=n), :] = yt[LANES // 2:]
    ones = _group_ones(w, n)
    inv = 1.0 / n
    for b in range(bsz):
        y = scr[b * w:(b + 1) * w, :].T
        mean = _dot2_exact_rhs(y, ones) * inv
        yc = y - mean
        var = _dot2_exact_rhs(yc * yc, ones) * inv
        yn = yc * lax.rsqrt(var + RWKV_GN_EPS) * g_ref[...] + b_ref[...]
        o_ref[b] = ((yn + bonus_ref[b]) * gate_ref[b]).astype(o_ref.dtype)


def _rwkv_readout(y, bonus, gate, ln_g, ln_b):
    bsz, t, w = bonus.shape
    n = RWKV_HEAD_DIM
    ts = LANES
    tok = pl.BlockSpec((bsz, ts, w), lambda i: (0, i, 0))
    vec = pl.BlockSpec((1, w), lambda i: (0, 0))
    return pl.pallas_call(
        functools.partial(_rwkv_readout_kernel, bsz=bsz),
        grid=(t // ts,),
        in_specs=[pl.BlockSpec((None, ts * n // 2, LANES), lambda i: (0, i, 0)),
                  pl.BlockSpec((None, ts * n // 2, LANES), lambda i: (1, i, 0)), tok, tok, vec, vec],
        out_specs=tok,
        out_shape=jax.ShapeDtypeStruct((bsz, t, w), BF16),
        scratch_shapes=[pltpu.VMEM((LANES // 2 * n, ts), F32)],
        compiler_params=_params("parallel"),
        name="rwkv_readout",
    )(y, y, bonus, gate, ln_g.reshape(1, w), ln_b.reshape(1, w))


def _merge_kernel(ya_ref, yr_ref, yw_ref, g0a_ref, g0b_ref, g1a_ref, g1b_ref, g2a_ref, g2b_ref,
                  x_ref, gate1_ref, a2_ref, b2_ref, wb_ref, wo_ref, wr_ref, br_ref,
                  x_out, h_out, ids_out, wts_out, cnt_out):
    first = jnp.logical_and(pl.program_id(0) == 0, pl.program_id(1) == 0)

    @pl.when(first)
    def _():
        cnt_out[...] = jnp.zeros_like(cnt_out)

    gate = lambda lo, hi: jax.nn.sigmoid(jnp.concatenate([lo[...], hi[...]], axis=1))
    merged = (gate(g0a_ref, g0b_ref) * _dot(ya_ref[...], wb_ref[0])
              + gate(g1a_ref, g1b_ref) * _dot(yr_ref[...], wb_ref[1])
              + gate(g2a_ref, g2b_ref) * _dot(yw_ref[...], wb_ref[2]))
    x = x_ref[...] + gate1_ref[...] * _dot(merged.astype(BF16), wo_ref[...])
    x_out[...] = x
    h = x * lax.rsqrt(jnp.mean(x * x, axis=-1, keepdims=True) + NORM_EPS) * a2_ref[...] + b2_ref[...]
    h_out[...] = h.astype(BF16)

    tm = x.shape[0]
    logits = _dot3(h, wr_ref[...]) + br_ref[...]
    lane = lax.broadcasted_iota(jnp.int32, (tm, LANES), 1)
    lane_f = lane.astype(F32)
    neg = -jnp.inf
    big = float(LANES)
    first = lambda hit: jnp.min(jnp.where(hit, lane_f, big), axis=-1, keepdims=True).astype(jnp.int32)
    is_grp = jnp.logical_and(lane >= MOE_EXPERTS, lane < MOE_EXPERTS + MOE_GROUPS)
    gl = jnp.where(is_grp, logits, neg)
    gmax = jnp.max(gl, axis=-1, keepdims=True)
    gidx = first(gl == gmax) - MOE_EXPERTS
    p_grp = 1.0 / jnp.sum(jnp.where(is_grp, jnp.exp(gl - gmax), 0.0), axis=-1, keepdims=True)
    in_grp = jnp.logical_and(lane < MOE_EXPERTS, lane // MOE_EXPERTS_PER_GROUP == gidx)
    el = jnp.where(in_grp, logits, neg)
    v1 = jnp.max(el, axis=-1, keepdims=True)
    i1 = first(el == v1)
    el2 = jnp.where(lane == i1, neg, el)
    v2 = jnp.max(el2, axis=-1, keepdims=True)
    i2 = first(el2 == v2)
    e2 = jnp.exp(v2 - v1)
    w1 = p_grp / (1.0 + e2)
    w2 = p_grp * e2 / (1.0 + e2)
    wts_out[...] = jnp.where(lane == 0, w1, jnp.where(lane == 1, w2, 0.0))

    onehot = jnp.where(jnp.logical_or(lane == i1, lane == i2), 1.0, 0.0)
    rr = lax.broadcasted_iota(jnp.int32, (tm, tm), 0)
    cc = lax.broadcasted_iota(jnp.int32, (tm, tm), 1)
    below = jnp.where(cc < rr, 1.0, 0.0).astype(BF16)
    before = _dot(below, onehot.astype(BF16)) + cnt_out[0:1, :]
    rank1 = jnp.sum(jnp.where(lane == i1, before, 0.0), axis=-1, keepdims=True).astype(jnp.int32)
    rank2 = jnp.sum(jnp.where(lane == i2, before, 0.0), axis=-1, keepdims=True).astype(jnp.int32)
    ids_out[...] = jnp.where(lane == 0, i1, jnp.where(lane == 1, i2, jnp.where(lane == 2, rank1,
                                                                                  jnp.where(lane == 3, rank2, 0))))
    cnt_out[...] = cnt_out[...] + jnp.sum(onehot, axis=0, keepdims=True)


def _merge(ya, yr, yw, u, x, gate1, a2, b2, w_branch, w_out, w_router, b_router, part):
    bsz, t, d = x.shape
    nb = bsz // MOE_PARTS
    b0 = part * nb
    tm = ROW_TILE
    sel = lambda b, i: (2 * (b + b0) + jnp.minimum(i, 1), 0, 0)
    tok = lambda width, blk=0: pl.BlockSpec((None, tm, width), lambda b, i: (b + b0, i, blk))
    own = lambda width: pl.BlockSpec((None, tm, width), lambda b, i: (b, i, 0))
    const = lambda shape: pl.BlockSpec(shape, lambda b, i: (0,) * len(shape))
    mod = pl.BlockSpec((None, 1, d), sel)
    n_gate_blocks = N_BRANCH * d // GATE_BLOCK
    return pl.pallas_call(
        _merge_kernel,
        grid=(nb, t // tm),
        in_specs=[tok(BRANCH_W), tok(BRANCH_W), tok(BRANCH_W)]
                 + [tok(GATE_BLOCK, U_GATE // GATE_BLOCK + k) for k in range(n_gate_blocks)]
                 + [tok(d),
                  mod, mod, mod,
                  const((N_BRANCH, BRANCH_W, d)), const((d, d)), const((d, LANES)), const((1, LANES))],
        out_specs=[tok(d), own(d), own(LANES), own(LANES), const((8, LANES))],
        out_shape=[jax.ShapeDtypeStruct((bsz, t, d), F32), jax.ShapeDtypeStruct((nb, t, d), BF16),
                   jax.ShapeDtypeStruct((nb, t, LANES), jnp.int32), jax.ShapeDtypeStruct((nb, t, LANES), F32),
                   jax.ShapeDtypeStruct((8, LANES), F32)],
        input_output_aliases={3 + n_gate_blocks: 0},
        compiler_params=_params("arbitrary", "arbitrary"),
        name="merge_router",
    )(ya, yr, yw, *([u] * n_gate_blocks), x, gate1, a2, b2, w_branch, w_out, w_router, b_router)


def _moe_kernel(be_ref, na_ref, x_ref, wg_ref, wu_ref, wd_ref, o_ref, wg_s, wu_s, wd_s):
    i = pl.program_id(0)
    active = i < na_ref[0]
    new_expert = jnp.logical_or(i == 0, be_ref[i] != be_ref[jnp.maximum(i - 1, 0)])

    @pl.when(jnp.logical_and(active, new_expert))
    def _():
        wg_s[...] = wg_ref[...].astype(BF16)
        wu_s[...] = wu_ref[...].astype(BF16)
        wd_s[...] = wd_ref[...].astype(BF16)

    @pl.when(active)
    def _():
        x = x_ref[...]
        act = _silu(_dot(x, wg_s[...])) * _dot(x, wu_s[...])
        o_ref[...] = _dot(act.astype(BF16), wd_s[...]).astype(o_ref.dtype)

    @pl.when(i >= na_ref[0])
    def _():
        o_ref[...] = jnp.zeros_like(o_ref)


def _moe_experts(buf, block_expert, n_active, w_gate, w_up, w_down, layer):
    rows, d = buf.shape
    hid = w_gate.shape[-1]
    grid_spec = pltpu.PrefetchScalarGridSpec(
        num_scalar_prefetch=2,
        grid=(rows // MOE_BLOCK,),
        in_specs=[pl.BlockSpec((MOE_BLOCK, d), lambda i, be, na: (i, 0)),
                  pl.BlockSpec((None, None, d, hid), lambda i, be, na: (layer, be[i], 0, 0)),
                  pl.BlockSpec((None, None, d, hid), lambda i, be, na: (layer, be[i], 0, 0)),
                  pl.BlockSpec((None, None, hid, d), lambda i, be, na: (layer, be[i], 0, 0))],
        out_specs=pl.BlockSpec((MOE_BLOCK, d), lambda i, be, na: (i, 0)),
        scratch_shapes=[pltpu.VMEM((d, hid), BF16), pltpu.VMEM((d, hid), BF16), pltpu.VMEM((hid, d), BF16)])
    return pl.pallas_call(
        _moe_kernel,
        grid_spec=grid_spec,
        out_shape=jax.ShapeDtypeStruct((rows, d), F32),
        compiler_params=_params("arbitrary"),
        name="moe_experts",
    )(block_expert, n_active, buf, w_gate, w_up, w_down)


def _combine_kernel(x_ref, *refs):
    y_refs, (w_ref, g_ref, o_ref) = refs[:-3], refs[-3:]
    tm = x_ref.shape[0]
    w = w_ref[...]
    for c, y_ref in enumerate(y_refs):
        cols = slice(c * LANES, (c + 1) * LANES)
        y = y_ref[pl.ds(0, tm, stride=2), :] * w[:, 0:1] + y_ref[pl.ds(1, tm, stride=2), :] * w[:, 1:2]
        o_ref[:, cols] = x_ref[:, cols] + g_ref[:, cols] * y


def _combine(x, y_pairs, wts, gate2, part):
    bsz, t, d = x.shape
    nb = y_pairs.shape[0]
    b0 = part * nb
    tm = ROW_TILE
    sel = lambda b, i: (2 * (b + b0) + jnp.minimum(i, 1), 0, 0)
    x_spec = pl.BlockSpec((None, tm, d), lambda b, i: (b + b0, i, 0))
    return pl.pallas_call(
        _combine_kernel,
        grid=(nb, t // tm),
        in_specs=[x_spec]
                 + [pl.BlockSpec((None, 2 * tm, LANES), functools.partial(lambda c, b, i: (b, i, c), c))
                    for c in range(d // LANES)]
                 + [pl.BlockSpec((None, tm, LANES), lambda b, i: (b, i, 0)), pl.BlockSpec((None, 1, d), sel)],
        out_specs=x_spec,
        out_shape=jax.ShapeDtypeStruct((bsz, t, d), F32),
        input_output_aliases={0: 0},
        compiler_params=_params("parallel", "parallel"),
        name="moe_combine",
    )(x, *([y_pairs] * (d // LANES)), wts, gate2)


def _moe(h, ids, wts, counts, w_gate, w_up, w_down, layer):
    bsz, t, d = h.shape
    n_tok = bsz * t
    n_pair = 2 * n_tok
    n_blocks = -(-n_pair // MOE_BLOCK) + MOE_EXPERTS
    counts = counts[0, :MOE_EXPERTS].astype(jnp.int32)
    padded = (counts + MOE_BLOCK - 1) // MOE_BLOCK * MOE_BLOCK
    pad_end = jnp.cumsum(padded)
    pad_start = pad_end - padded
    expert = ids[..., 0:2].reshape(n_pair)
    rank = ids[..., 2:4].reshape(n_pair)
    dest = pad_start.at[expert].get(mode="promise_in_bounds") + rank
    token = jnp.arange(n_pair, dtype=jnp.int32) // 2
    src = jnp.zeros((n_blocks * MOE_BLOCK,), jnp.int32).at[dest].set(
        token, unique_indices=True, mode="promise_in_bounds")
    block_start = jnp.arange(n_blocks, dtype=jnp.int32) * MOE_BLOCK
    block_expert = jnp.minimum(jnp.sum((pad_end[None, :] <= block_start[:, None]).astype(jnp.int32), axis=1),
                               MOE_EXPERTS - 1).astype(jnp.int32)
    n_active = (pad_end[-1:] // MOE_BLOCK).astype(jnp.int32)
    buf = h.reshape(n_tok, d).at[src].get(mode="promise_in_bounds")
    yb = _moe_experts(buf, block_expert, n_active, w_gate, w_up, w_down, layer)
    pairs = yb.at[dest].get(mode="promise_in_bounds", unique_indices=True)
    return pairs.reshape(bsz, 2 * t, d)


def _rope_tables(n_ctx, n_lat, head_dim):
    rows = n_lat // GRID_W
    row = jnp.broadcast_to(jnp.arange(rows, dtype=F32)[:, None], (rows, GRID_W)).reshape(-1)
    col = jnp.broadcast_to(jnp.arange(GRID_W, dtype=F32)[None, :], (rows, GRID_W)).reshape(-1)
    quarter = head_dim // 4
    inv_freq = ROPE_THETA ** (-jnp.arange(quarter, dtype=F32) / quarter)
    ang = jnp.stack([row[:, None] * inv_freq, col[:, None] * inv_freq], axis=1)
    cos, sin = jnp.cos(ang), jnp.sin(ang)
    cos_t = jnp.stack([cos, cos], axis=2).reshape(n_lat, head_dim)
    sin_t = jnp.stack([-sin, sin], axis=2).reshape(n_lat, head_dim)
    cos_t = jnp.concatenate([jnp.ones((n_ctx, head_dim), F32), cos_t], axis=0)
    sin_t = jnp.concatenate([jnp.zeros((n_ctx, head_dim), F32), sin_t], axis=0)
    rep = LANES // head_dim
    return jnp.tile(cos_t, (1, rep)), jnp.tile(sin_t, (1, rep))


def kernel(x, c, ctx, c_ctx, ada_w, ada_b, norm1_g, norm2_g, w_in, att_qn_g, att_kn_g, ret_decay_logit, ret_gn_g, rwkv_mu, rwkv_w0, rwkv_w2, rwkv_a0, rwkv_a2, rwkv_g2, rwkv_k_k, rwkv_k_a, rwkv_r_k, rwkv_ln_g, rwkv_ln_b, w_branch, w_out, router_grp_w, router_grp_b, router_exp_w, router_exp_b, moe_w_gate, moe_w_up, moe_w_down):
    bsz, n_lat, d = x.shape
    n_ctx = ctx.shape[1]
    depth = ada_w.shape[0]
    assert d == D_MODEL and n_ctx == ROW_TILE and n_lat % ROW_TILE == 0 and n_lat % GRID_W == 0
    assert 2 * bsz * RWKV_HEADS <= LANES
    t_all = n_ctx + n_lat
    assert t_all % LANES == 0 and n_ctx % SCAN_STEPS == 0

    att_cos, att_sin = _rope_tables(n_ctx, n_lat, ATT_HEAD_DIM)
    ret_cos, ret_sin = _rope_tables(n_ctx, n_lat, RET_HEAD_DIM)

    rows = -(-(bsz + 1) // 8) * 8
    cvec = jnp.zeros((rows, d), F32).at[:bsz].set(c).at[bsz].set(c_ctx)
    mods = _modulation(cvec, ada_w, ada_b)

    xs = jnp.concatenate([ctx, x], axis=1)
    for layer in range(depth):
        m = mods[layer].reshape(rows, 6, d)
        pick = lambda j: jnp.stack([jnp.broadcast_to(m[bsz, j], (bsz, d)), m[:bsz, j]], axis=1).reshape(2 * bsz, 1, d)
        sh1, sc1, g1, sh2, sc2, g2 = (pick(j) for j in range(6))
        w_l = w_in[layer].astype(BF16)
        u = _in_proj(xs, norm1_g[layer] * (1.0 + sc1), sh1, w_l)

        score_bound = (ATT_HEAD_DIM * ATT_Q_SCALE * ATT_BOUND_MARGIN) * (
            jnp.max(jnp.abs(att_qn_g[layer])) * jnp.max(jnp.abs(att_kn_g[layer])))
        bounded = 2.0 * score_bound <= ATT_EXP2_RANGE
        q_att, kv_att = _att_prep(u, att_cos, att_sin, att_qn_g[layer], att_kn_g[layer],
                                  jnp.where(bounded, -score_bound, 0.0))
        ya = _attention(q_att, kv_att, bounded.astype(jnp.int32).reshape(1), n_ctx)

        log_gamma = jax.nn.log_sigmoid(ret_decay_logit[layer].astype(F32))
        yr = _retention(u, ret_cos, ret_sin, log_gamma, ret_gn_g[layer], n_ctx)

        rows_t, v_t, gate, bonus = _rwkv_prep(
            u, rwkv_mu[layer], rwkv_w0[layer], rwkv_w2[layer], rwkv_a0[layer], rwkv_a2[layer], rwkv_g2[layer],
            rwkv_k_k[layer], rwkv_k_a[layer], rwkv_r_k[layer].reshape(-1))
        rows_s, v_s = _scan_layout(rows_t, v_t)
        y_scan = _rwkv_scan(rows_s, v_s.reshape(t_all, RWKV_HEAD_DIM // 2, LANES), n_ctx)
        yw = _rwkv_readout(y_scan.reshape(2, t_all * RWKV_HEAD_DIM // 2, LANES), bonus, gate,
                           rwkv_ln_g[layer], rwkv_ln_b[layer])

        w_router = jnp.zeros((d, LANES), F32).at[:, :MOE_EXPERTS].set(router_exp_w[layer]).at[
            :, MOE_EXPERTS:MOE_EXPERTS + MOE_GROUPS].set(router_grp_w[layer])
        b_router = jnp.zeros((1, LANES), F32).at[0, :MOE_EXPERTS].set(router_exp_b[layer]).at[
            0, MOE_EXPERTS:MOE_EXPERTS + MOE_GROUPS].set(router_grp_b[layer])
        routed = []
        for part in range(MOE_PARTS):
            xs, h2, ids, wts, counts = _merge(
                ya, yr, yw, u, xs, g1, norm2_g[layer] * (1.0 + sc2), sh2,
                w_branch[layer].astype(BF16), w_out[layer].astype(BF16), w_router, b_router, part)
            routed.append((h2, ids, wts, counts))
        pairs = [_moe(h2, ids, wts, counts, moe_w_gate, moe_w_up, moe_w_down, layer)
                 for h2, ids, wts, counts in routed]
        for part in range(MOE_PARTS):
            xs = _combine(xs, pairs[part], routed[part][2], g2, part)
    return xs[:, n_ctx:]
```

```python
import functools

import jax
import jax.numpy as jnp
from jax import lax
from jax.experimental import pallas as pl
from jax.experimental.pallas import tpu as pltpu

F32 = jnp.float32
BF16 = jnp.bfloat16

D_MODEL = 1024
GRID_W = 64
NORM_EPS = 1e-6
ROPE_THETA = 10000.0

ATT_HEADS = 8
ATT_KV_HEADS = 2
ATT_HEAD_DIM = 64
ATT_GROUP = ATT_HEADS // ATT_KV_HEADS
ATT_W = ATT_HEADS * ATT_HEAD_DIM
ATT_KV_W = ATT_KV_HEADS * ATT_HEAD_DIM

RET_HEADS = 4
RET_HEAD_DIM = 128
RET_CHUNK = 128
RET_UNROLL = 17
RET_W = RET_HEADS * RET_HEAD_DIM

RWKV_HEADS = 8
RWKV_HEAD_DIM = 64
RWKV_W = RWKV_HEADS * RWKV_HEAD_DIM
RWKV_DECAY_LORA = 64
RWKV_AAA_LORA = 64
RWKV_GATE_LORA = 128
RWKV_GN_EPS = 64e-5
RWKV_DECAY_SCALE = 0.6065306597126334
RWKV_COLS = 3 * RWKV_W + RWKV_DECAY_LORA + RWKV_AAA_LORA + RWKV_GATE_LORA

N_BRANCH = 3
BRANCH_W = 512
IN_COLS = ATT_W + 2 * ATT_KV_W + 4 * RET_W + RWKV_COLS + N_BRANCH * D_MODEL

MOE_GROUPS = 4
MOE_EXPERTS_PER_GROUP = 8
MOE_EXPERTS = MOE_GROUPS * MOE_EXPERTS_PER_GROUP
MOE_HIDDEN = 512
MOE_BLOCK = 256
MOE_PARTS = 1

LANES = 128
ROW_TILE = 256
ATT_Q_TILE = 128
ATT_KEY_TILE = 256
ATT_Q_SCALE = ATT_HEAD_DIM ** -0.5 * 1.4426950408889634
ATT_BOUND_MARGIN = 1.01
ATT_EXP2_RANGE = 100.0
SCAN_STEPS = 64
VMEM_LIMIT = 56 * 1024 * 1024

ROW_R, ROW_A, ROW_W, ROW_K, ROW_B, N_SCAN_ROWS = 0, 1, 2, 4, 6, 8

ATT_COLS = ATT_W + 2 * ATT_KV_W
RWKV_REST_COLS = RWKV_COLS - 2 * RWKV_W
U_RWKV_RK = 0
U_RET = U_RWKV_RK + 2 * RWKV_W
U_ATT = U_RET + 4 * RET_W
U_RWKV_REST = U_ATT + ATT_COLS
U_GATE = U_RWKV_REST + RWKV_REST_COLS
GATE_BLOCK = 512
IN_PROJ_MOVES = (((0, U_ATT, ATT_COLS), (ATT_COLS, U_RET, 4 * RET_W),
                  (ATT_COLS + 4 * RET_W, U_RWKV_RK, 2 * RWKV_W)),
                 ((0, 0, IN_COLS // 2),))
assert U_ATT + ATT_COLS == IN_COLS // 2 and ATT_COLS + 4 * RET_W + 2 * RWKV_W == IN_COLS // 2


def _params(*sem):
    return pltpu.CompilerParams(dimension_semantics=sem, vmem_limit_bytes=VMEM_LIMIT)


def _dot(a, b):
    return jnp.dot(a, b, preferred_element_type=F32)


def _dot_nt(a, b):
    return lax.dot_general(a, b, (((1,), (1,)), ((), ())), preferred_element_type=F32)


def _split(a):
    hi = a.astype(BF16)
    lo = (a - hi.astype(F32)).astype(BF16)
    return hi, lo


def _dot3(a, b):
    ah, al = _split(a)
    bh, bl = _split(b)
    return _dot(ah, bh) + (_dot(al, bh) + _dot(ah, bl))


def _dot2_exact_rhs(a, b_bf16):
    ah, al = _split(a)
    return _dot(ah, b_bf16) + _dot(al, b_bf16)


def _group_ones(width, group):
    r = lax.broadcasted_iota(jnp.int32, (width, width), 0) // group
    c = lax.broadcasted_iota(jnp.int32, (width, width), 1) // group
    return jnp.where(r == c, 1.0, 0.0).astype(BF16)


def _silu(x):
    return x * jax.nn.sigmoid(x)


def _swap_halves(x, quarter):
    n = x.shape[-1]
    lane = lax.broadcasted_iota(jnp.int32, x.shape, x.ndim - 1)
    up = pltpu.roll(x, n - quarter, x.ndim - 1)
    down = pltpu.roll(x, quarter, x.ndim - 1)
    return jnp.where(lane % (2 * quarter) < quarter, up, down)


def _mod_kernel(c_ref, w_ref, b_ref, o_ref):
    o_ref[...] = _dot3(_silu(c_ref[...]), w_ref[...]) + b_ref[...]


def _modulation(cvec, ada_w, ada_b):
    depth, d, cols = ada_w.shape
    rows = cvec.shape[0]
    tn = 1536
    return pl.pallas_call(
        _mod_kernel,
        grid=(depth, cols // tn),
        in_specs=[pl.BlockSpec((rows, d), lambda l, j: (0, 0)),
                  pl.BlockSpec((None, d, tn), lambda l, j: (l, 0, j)),
                  pl.BlockSpec((None, 1, tn), lambda l, j: (l, 0, j))],
        out_specs=pl.BlockSpec((None, rows, tn), lambda l, j: (l, 0, j)),
        out_shape=jax.ShapeDtypeStruct((depth, rows, cols), F32),
        compiler_params=_params("parallel", "parallel"),
        name="modulation",
    )(cvec, ada_w, ada_b.reshape(depth, 1, cols))


def _in_proj_kernel(x_ref, a_ref, b_ref, w_ref, o_ref):
    x = x_ref[...]
    ms = jnp.mean(x * x, axis=-1, keepdims=True)
    h = (x * lax.rsqrt(ms + NORM_EPS) * a_ref[...] + b_ref[...]).astype(BF16)
    for half, moves in enumerate(IN_PROJ_MOVES):
        @pl.when(pl.program_id(0) == half)
        def _():
            for src, dst, width in moves:
                o_ref[:, dst:dst + width] = _dot(h, w_ref[:, src:src + width])


def _in_proj(x, mod_a, mod_b, w):
    bsz, t, d = x.shape
    cols = w.shape[1]
    tm, tn = ROW_TILE, cols // 2
    sel = lambda j, b, i: (2 * b + jnp.minimum(i, 1), 0, 0)
    return pl.pallas_call(
        _in_proj_kernel,
        grid=(cols // tn, bsz, t // tm),
        in_specs=[pl.BlockSpec((None, tm, d), lambda j, b, i: (b, i, 0)),
                  pl.BlockSpec((None, 1, d), sel),
                  pl.BlockSpec((None, 1, d), sel),
                  pl.BlockSpec((d, tn), lambda j, b, i: (0, j))],
        out_specs=pl.BlockSpec((None, tm, tn), lambda j, b, i: (b, i, j)),
        out_shape=jax.ShapeDtypeStruct((bsz, t, cols), F32),
        compiler_params=_params("parallel", "parallel", "parallel"),
        name="in_proj",
    )(x, mod_a, mod_b, w)


def _att_prep_kernel(u_ref, cos_ref, sin_ref, qg_ref, kg_ref, shift_ref, q_out, kv_out):
    hd = ATT_HEAD_DIM
    ones = _group_ones(LANES, hd)
    cos = cos_ref[...]
    sin = sin_ref[...]
    lane = lax.broadcasted_iota(jnp.int32, cos.shape, 1)
    low = lane < hd

    def two_heads(y, extra):
        fill = jnp.where(lane == hd, extra, 0.0)
        return jnp.where(low, y, fill), jnp.where(low, pltpu.roll(y, hd, 1), fill)

    n_q = ATT_W // LANES
    for j in range(n_q + 1):
        x = u_ref[:, j * LANES:(j + 1) * LANES]
        is_q = j < n_q
        gain = qg_ref[...] if is_q else kg_ref[...]
        ms = _dot2_exact_rhs(x * x, ones) * (1.0 / hd)
        y = x * lax.rsqrt(ms + NORM_EPS) * gain
        y = y * cos + _swap_halves(y, hd // 4) * sin
        if is_q:
            y = y * ATT_Q_SCALE
        out, base = (q_out, 2 * j) if is_q else (kv_out, 0)
        for h, yh in enumerate(two_heads(y, shift_ref[...] if is_q else 1.0)):
            out[:, (base + h) * LANES:(base + h + 1) * LANES] = yh.astype(BF16)
    v = u_ref[:, ATT_W + ATT_KV_W:]
    for h, vh in enumerate(two_heads(v, 1.0)):
        kv_out[:, (2 + h) * LANES:(3 + h) * LANES] = vh.astype(BF16)


def _att_prep(u, cos, sin, qn_g, kn_g, shift):
    bsz, t, _ = u.shape
    tm = ROW_TILE
    rep = LANES // ATT_HEAD_DIM
    shift = jnp.broadcast_to(shift.astype(F32), (1, LANES))
    return pl.pallas_call(
        _att_prep_kernel,
        grid=(bsz, t // tm),
        in_specs=[pl.BlockSpec((None, tm, ATT_COLS), lambda b, i: (b, i, U_ATT // ATT_COLS)),
                  pl.BlockSpec((tm, LANES), lambda b, i: (i, 0)),
                  pl.BlockSpec((tm, LANES), lambda b, i: (i, 0)),
                  pl.BlockSpec((1, LANES), lambda b, i: (0, 0)),
                  pl.BlockSpec((1, LANES), lambda b, i: (0, 0)),
                  pl.BlockSpec((1, LANES), lambda b, i: (0, 0))],
        out_specs=[pl.BlockSpec((None, tm, ATT_HEADS * LANES), lambda b, i: (b, i, 0)),
                   pl.BlockSpec((None, tm, 2 * ATT_KV_HEADS * LANES), lambda b, i: (b, i, 0))],
        out_shape=[jax.ShapeDtypeStruct((bsz, t, ATT_HEADS * LANES), BF16),
                   jax.ShapeDtypeStruct((bsz, t, 2 * ATT_KV_HEADS * LANES), BF16)],
        compiler_params=_params("parallel", "parallel"),
        name="att_prep",
    )(u, cos, sin, jnp.tile(qn_g, rep).reshape(1, LANES), jnp.tile(kn_g, rep).reshape(1, LANES), shift)


def _att_kernel(bounded_ref, q_ref, kv_ref, o_ref, *, n_ctx, tq):
    i = pl.program_id(1)
    hd = ATT_HEAD_DIM
    k_ref = v_ref = kv_ref

    def run(n_keys, bounded):
        tk = ATT_KEY_TILE
        for g in range(ATT_KV_HEADS):
            q = jnp.concatenate(
                [q_ref[:, (ATT_GROUP * g + h) * LANES:(ATT_GROUP * g + h + 1) * LANES] for h in range(ATT_GROUP)],
                axis=0)
            scores = lambda c: _dot_nt(q, k_ref[c * tk:(c + 1) * tk, g * LANES:(g + 1) * LANES])
            if not bounded:
                m = jnp.full((ATT_GROUP * tq, LANES), -jnp.inf, F32)
                for c in range(n_keys // tk):
                    s = scores(c)
                    for part in range(tk // LANES):
                        m = jnp.maximum(m, s[:, part * LANES:(part + 1) * LANES])
                m = jnp.max(m, axis=-1, keepdims=True)
            acc = jnp.zeros((ATT_GROUP * tq, LANES), F32)
            for c in range(n_keys // tk):
                p = jnp.exp2(scores(c) if bounded else scores(c) - m).astype(BF16)
                acc = acc + _dot(p, v_ref[c * tk:(c + 1) * tk, (ATT_KV_HEADS + g) * LANES:(ATT_KV_HEADS + g + 1) * LANES])
            o = acc[:, :hd] / acc[:, hd:hd + 1]
            for h in range(ATT_GROUP):
                c0 = (ATT_GROUP * g + h) * hd
                o_ref[:, c0:c0 + hd] = o[h * tq:(h + 1) * tq].astype(o_ref.dtype)

    is_ctx = i < n_ctx // tq
    bounded = bounded_ref[0] == 1

    @pl.when(is_ctx)
    def _():
        run(n_ctx, False)

    @pl.when(jnp.logical_and(jnp.logical_not(is_ctx), bounded))
    def _():
        run(k_ref.shape[0], True)

    @pl.when(jnp.logical_and(jnp.logical_not(is_ctx), jnp.logical_not(bounded)))
    def _():
        run(k_ref.shape[0], False)


def _attention(q, kv, bounded, n_ctx):
    bsz, t, _ = q.shape
    tq = ATT_Q_TILE
    kv_w = kv.shape[-1]
    assert n_ctx % ATT_KEY_TILE == 0 and t % ATT_KEY_TILE == 0
    grid_spec = pltpu.PrefetchScalarGridSpec(
        num_scalar_prefetch=1,
        grid=(bsz, t // tq),
        in_specs=[pl.BlockSpec((None, tq, ATT_HEADS * LANES), lambda b, i, f: (b, i, 0)),
                  pl.BlockSpec((None, t, kv_w), lambda b, i, f: (b, 0, 0))],
        out_specs=pl.BlockSpec((None, tq, ATT_W), lambda b, i, f: (b, i, 0)))
    return pl.pallas_call(
        functools.partial(_att_kernel, n_ctx=n_ctx, tq=tq),
        grid_spec=grid_spec,
        out_shape=jax.ShapeDtypeStruct((bsz, t, ATT_W), BF16),
        compiler_params=_params("parallel", "parallel"),
        name="attention",
    )(bounded, q, kv)


def _ret_kernel(q_ref, k_ref, v_ref, g_ref, cos_ref, sin_ref, lg_ref, gn_ref, o_ref,
                qs_ref, ks_ref, kvf_ref, kvb_ref, sf_ref, sb_ref, *, n_ctx):
    c = RET_CHUNK
    t = q_ref.shape[0]
    n_chunks = t // c
    n_cc = n_ctx // c
    quarter = RET_HEAD_DIM // 4
    scale = RET_HEAD_DIM ** -0.5
    lg_f = lg_ref[0]
    lg_b = lg_ref[1]
    row = lax.broadcasted_iota(jnp.int32, (c, c), 0)
    col = lax.broadcasted_iota(jnp.int32, (c, c), 1)
    rowf = row.astype(F32)
    lag = (row - col).astype(F32)

    def chunk(ci):
        r0 = pl.multiple_of(ci * c, c)
        cos = cos_ref[pl.ds(r0, c), :]
        sin = sin_ref[pl.ds(r0, c), :]
        q = q_ref[pl.ds(r0, c), :]
        k = k_ref[pl.ds(r0, c), :]
        q = q * cos + _swap_halves(q, quarter) * sin
        k = (k * cos + _swap_halves(k, quarter) * sin) * scale
        return r0, q, k, v_ref[pl.ds(r0, c), :]

    colf = col.astype(F32)
    d_key_f = jnp.exp(lg_f * (c - 1.0 - colf))
    d_key_b = jnp.exp(lg_b * colf)
    d_query_f = jnp.exp(lg_f * (rowf + 1.0))
    d_query_b = jnp.exp(lg_b * (float(c) - rowf))
    d_chunk_f = jnp.exp(lg_f * float(c))
    d_chunk_b = jnp.exp(lg_b * float(c))
    d_intra = (jnp.where(lag >= 0, jnp.exp(lg_f * jnp.maximum(lag, 0.0)), 0.0)
               + jnp.where(lag <= 0, jnp.exp(lg_b * jnp.maximum(-lag, 0.0)), 0.0))

    def summaries(n, carry):
        r0, q, k, v = chunk(n)
        qs_ref[pl.ds(r0, c), :] = q.astype(BF16)
        ks_ref[pl.ds(r0, c), :] = k.astype(BF16)
        vb = v.astype(BF16)
        kt = k.T
        kvf_ref[n] = _dot((kt * d_key_f).astype(BF16), vb)
        kvb_ref[n] = _dot((kt * d_key_b).astype(BF16), vb)
        return carry

    lax.fori_loop(0, n_chunks, summaries, 0, unroll=RET_UNROLL)

    def state_f(n, s):
        sf_ref[n] = s.astype(BF16)
        return d_chunk_f * s + kvf_ref[n]

    def state_b(n, s):
        ci = jnp.where(n < n_cc, n_cc - 1 - n, n_chunks - 1 - (n - n_cc))
        sb_ref[ci] = s.astype(BF16)
        return d_chunk_b * s + kvb_ref[ci]

    zero = jnp.zeros((RET_HEAD_DIM, RET_HEAD_DIM), F32)
    lax.fori_loop(0, n_chunks, state_f, zero)
    lax.fori_loop(0, n_chunks, state_b, zero)
    gn = gn_ref[...]

    def outputs(n, carry):
        r0 = pl.multiple_of(n * c, c)
        qb = qs_ref[pl.ds(r0, c), :]
        vb = v_ref[pl.ds(r0, c), :].astype(BF16)
        scores = _dot_nt(qb, ks_ref[pl.ds(r0, c), :]) * d_intra
        y = (_dot(scores.astype(BF16), vb) + _dot(qb, sf_ref[n]) * d_query_f) + _dot(qb, sb_ref[n]) * d_query_b
        yn = y * lax.rsqrt(jnp.mean(y * y, axis=-1, keepdims=True) + NORM_EPS) * gn
        o_ref[pl.ds(r0, c), :] = (_silu(g_ref[pl.ds(r0, c), :]) * yn).astype(o_ref.dtype)
        return carry

    lax.fori_loop(0, n_chunks, outputs, 0, unroll=RET_UNROLL)


def _retention(u, cos, sin, log_gamma, gn_g, n_ctx):
    bsz, t, _ = u.shape
    hd = RET_HEAD_DIM
    base = U_RET // hd
    spec = lambda off: pl.BlockSpec((None, t, hd), lambda b, h: (b, 0, base + off * RET_HEADS + h))
    lg = jnp.broadcast_to(log_gamma[:, :, None, None], (2, RET_HEADS, 1, LANES)).astype(F32)
    return pl.pallas_call(
        functools.partial(_ret_kernel, n_ctx=n_ctx),
        grid=(bsz, RET_HEADS),
        in_specs=[spec(0), spec(1), spec(2), spec(3),
                  pl.BlockSpec((t, hd), lambda b, h: (0, 0)),
                  pl.BlockSpec((t, hd), lambda b, h: (0, 0)),
                  pl.BlockSpec((2, None, 1, LANES), lambda b, h: (0, h, 0, 0)),
                  pl.BlockSpec((1, hd), lambda b, h: (0, h))],
        out_specs=pl.BlockSpec((None, t, hd), lambda b, h: (b, 0, h)),
        out_shape=jax.ShapeDtypeStruct((bsz, t, RET_W), BF16),
        scratch_shapes=[pltpu.VMEM((t, hd), BF16), pltpu.VMEM((t, hd), BF16),
                        pltpu.VMEM((t // RET_CHUNK, hd, hd), F32), pltpu.VMEM((t // RET_CHUNK, hd, hd), F32),
                        pltpu.VMEM((t // RET_CHUNK, hd, hd), BF16), pltpu.VMEM((t // RET_CHUNK, hd, hd), BF16)],
        compiler_params=_params("parallel", "parallel"),
        name="retention",
    )(u, u, u, u, cos, sin, lg, gn_g.reshape(1, RET_W))


def _rwkv_prep_kernel(rk_ref, rk_prev_ref, rk_next_ref, rest_ref, rest_prev_ref, rest_next_ref,
                      mu_rk_ref, mu_rest_ref, w0_ref, w2_ref, a0_ref, a2_ref, g2_ref, kk_ref, ka_ref, rk_gain_ref,
                      rows_out, v_out, gate_out, bonus_out, *, n_tiles):
    i = pl.program_id(1)
    tm = rk_ref.shape[0]
    has_prev = jnp.logical_and(i != 0, i != 1)
    has_next = jnp.logical_and(i != 0, i != n_tiles - 1)

    def shifted(x_ref, prev_ref, next_ref, mu_ref):
        x = x_ref[...]
        rows = lax.broadcasted_iota(jnp.int32, x.shape, 0)
        halo_prev = jnp.where(has_prev, prev_ref[7:8, :], 0.0)
        halo_next = jnp.where(has_next, next_ref[0:1, :], 0.0)
        prev = jnp.where(rows == 0, halo_prev, pltpu.roll(x, 1, 0))
        nxt = jnp.where(rows == tm - 1, halo_next, pltpu.roll(x, tm - 1, 0))
        return x + (prev - x) * mu_ref[0:1, :] + (nxt - x) * mu_ref[1:2, :]

    rk = shifted(rk_ref, rk_prev_ref, rk_next_ref, mu_rk_ref)
    rest = shifted(rest_ref, rest_prev_ref, rest_next_ref, mu_rest_ref)
    w = RWKV_W
    r = rk[:, 0:w]
    k = rk[:, w:2 * w]
    v = rest[:, 0:w]
    xw = rest[:, w:w + RWKV_DECAY_LORA]
    xa = rest[:, w + RWKV_DECAY_LORA:w + RWKV_DECAY_LORA + RWKV_AAA_LORA]
    xg = rest[:, w + RWKV_DECAY_LORA + RWKV_AAA_LORA:]

    ones = _group_ones(w, RWKV_HEAD_DIM)
    kk = k * kk_ref[...]
    kk = kk * lax.rsqrt(jnp.maximum(_dot2_exact_rhs(kk * kk, ones), 1e-12))
    rows_out[ROW_R] = r
    rows_out[ROW_A] = -kk
    v_out[...] = v
    tw = jnp.tanh(xw)
    k_sum = jnp.zeros_like(k)
    for d in range(2):
        decay_rate = jax.nn.sigmoid(w0_ref[d:d + 1, :] + _dot3(tw, w2_ref[d])) * RWKV_DECAY_SCALE
        a = jax.nn.sigmoid(a0_ref[d:d + 1, :] + _dot3(xa, a2_ref[d]))
        k_d = k * (1.0 + (a - 1.0) * ka_ref[...])
        rows_out[ROW_W + d] = jnp.exp(-decay_rate)
        rows_out[ROW_K + d] = k_d
        rows_out[ROW_B + d] = kk * a
        k_sum = k_sum + k_d
    gate_out[...] = _dot3(jax.nn.sigmoid(xg), g2_ref[...])
    bonus_out[...] = _dot2_exact_rhs(r * k_sum * rk_gain_ref[...], ones) * v


def _rwkv_prep(u, mu, w0, w2, a0, a2, g2, k_k, k_a, r_k):
    bsz, t, _ = u.shape
    tm = ROW_TILE
    n_tiles = t // tm
    w = RWKV_W
    rk_blk = U_RWKV_RK // (2 * w)
    rest_blk = U_RWKV_REST // RWKV_REST_COLS
    sub = tm // 8
    n_sub = t // 8
    prev_idx = lambda b, i: jnp.maximum(i * sub - 1, 0)
    next_idx = lambda b, i: jnp.minimum((i + 1) * sub, n_sub - 1)
    row = lambda a: a.reshape(1, -1)
    const = lambda shape: pl.BlockSpec(shape, lambda b, i: (0,) * len(shape))
    tok = lambda width: pl.BlockSpec((None, tm, width), lambda b, i: (b, i, 0))
    rows_spec = pl.BlockSpec((N_SCAN_ROWS, None, tm, w), lambda b, i: (0, b, i, 0))
    sd = lambda *lead: jax.ShapeDtypeStruct((*lead, bsz, t, w), F32)
    return pl.pallas_call(
        functools.partial(_rwkv_prep_kernel, n_tiles=n_tiles),
        grid=(bsz, n_tiles),
        in_specs=[pl.BlockSpec((None, tm, 2 * w), lambda b, i: (b, i, rk_blk)),
                  pl.BlockSpec((None, 8, 2 * w), lambda b, i: (b, prev_idx(b, i), rk_blk)),
                  pl.BlockSpec((None, 8, 2 * w), lambda b, i: (b, next_idx(b, i), rk_blk)),
                  pl.BlockSpec((None, tm, RWKV_REST_COLS), lambda b, i: (b, i, rest_blk)),
                  pl.BlockSpec((None, 8, RWKV_REST_COLS), lambda b, i: (b, prev_idx(b, i), rest_blk)),
                  pl.BlockSpec((None, 8, RWKV_REST_COLS), lambda b, i: (b, next_idx(b, i), rest_blk)),
                  const((2, 2 * w)), const((2, RWKV_REST_COLS)),
                  const((2, w)), const((2, RWKV_DECAY_LORA, w)), const((2, w)), const((2, RWKV_AAA_LORA, w)),
                  const((RWKV_GATE_LORA, w)), const((1, w)), const((1, w)), const((1, w))],
        out_specs=[rows_spec, tok(w), tok(w), tok(w)],
        out_shape=[sd(N_SCAN_ROWS), sd(), sd(), sd()],
        compiler_params=_params("parallel", "parallel"),
        name="rwkv_prep",
    )(u, u, u, u, u, u, mu[:, :2 * w], mu[:, 2 * w:], w0, w2, a0, a2, g2, row(k_k), row(k_a), row(r_k))


def _transpose_tokens(z_ref, scr, bsz):
    w = RWKV_W
    for b in range(bsz):
        scr[b * w:(b + 1) * w, :] = z_ref[b].T
    if bsz * w < scr.shape[0]:
        scr[bsz * w:, :] = jnp.zeros((scr.shape[0] - bsz * w, scr.shape[1]), F32)


def _layout_rows_kernel(z_ref, o_ref, scr, *, bsz):
    n = RWKV_HEAD_DIM
    _transpose_tokens(z_ref, scr, bsz)
    for j in range(n):
        x = scr[pl.ds(j, LANES // 2, stride=n), :]
        o_ref[j] = jnp.concatenate([x, x], axis=0).T


def _layout_v_kernel(z_ref, o_ref, scr, *, bsz):
    n = RWKV_HEAD_DIM
    ts = z_ref.shape[1]
    _transpose_tokens(z_ref, scr, bsz)
    for i in range(n // 2):
        x0 = scr[pl.ds(i, LANES // 2, stride=n), :]
        x1 = scr[pl.ds(n // 2 + i, LANES // 2, stride=n), :]
        o_ref[pl.ds(i, ts, stride=n // 2), :] = jnp.concatenate([x0, x1], axis=0).T


def _scan_layout(rows, v):
    g, bsz, t, w = rows.shape
    n = RWKV_HEAD_DIM
    ts = LANES
    scr = pltpu.VMEM((LANES // 2 * n, ts), F32)
    rows_l = pl.pallas_call(
        functools.partial(_layout_rows_kernel, bsz=bsz),
        grid=(g, t // ts),
        in_specs=[pl.BlockSpec((None, bsz, ts, w), lambda k, i: (k, 0, i, 0))],
        out_specs=pl.BlockSpec((None, n, ts, LANES), lambda k, i: (k, 0, i, 0)),
        out_shape=jax.ShapeDtypeStruct((g, n, t, LANES), F32),
        scratch_shapes=[scr],
        compiler_params=_params("parallel", "parallel"),
        name="rwkv_layout_rows",
    )(rows)
    v_l = pl.pallas_call(
        functools.partial(_layout_v_kernel, bsz=bsz),
        grid=(t // ts,),
        in_specs=[pl.BlockSpec((bsz, ts, w), lambda i: (0, i, 0))],
        out_specs=pl.BlockSpec((ts * n // 2, LANES), lambda i: (i, 0)),
        out_shape=jax.ShapeDtypeStruct((t * n // 2, LANES), F32),
        scratch_shapes=[scr],
        compiler_params=_params("parallel"),
        name="rwkv_layout_v",
    )(v)
    return rows_l, v_l


def _rwkv_scan_kernel(r_ref, a_ref, w_ref, k_ref, b_ref, v_ref, y_ref, s_ref, sa_ref):
    n = RWKV_HEAD_DIM
    half = n // 2
    ts = r_ref.shape[1]
    fwd = pl.program_id(0) == 0

    @pl.when(pl.program_id(1) == 0)
    def _():
        s_ref[...] = jnp.zeros_like(s_ref)

    t_first = jnp.where(fwd, 0, ts - 1)
    acc = jnp.zeros((half, LANES), F32)
    for j in range(n):
        acc = acc + s_ref[j] * a_ref[j, pl.ds(t_first, 1), :]
    sa_ref[...] = acc

    def step(m, carry):
        t = jnp.where(fwd, m, ts - 1 - m)
        tn = jnp.clip(jnp.where(fwd, t + 1, t - 1), 0, ts - 1)
        sa = sa_ref[...]
        v = v_ref[t]
        y = jnp.zeros((half, LANES), F32)
        sa_next = jnp.zeros((half, LANES), F32)
        for j in range(n):
            s = (s_ref[j] * w_ref[j, pl.ds(t, 1), :] + sa * b_ref[j, pl.ds(t, 1), :]) + v * k_ref[j, pl.ds(t, 1), :]
            s_ref[j] = s
            y = y + s * r_ref[j, pl.ds(t, 1), :]
            sa_next = sa_next + s * a_ref[j, pl.ds(tn, 1), :]
        y_ref[t] = y
        sa_ref[...] = sa_next
        return carry

    lax.fori_loop(0, ts, step, 0)


def _rwkv_scan(rows, v, n_ctx):
    _, n, t, lanes = rows.shape
    ts = SCAN_STEPS
    nb = t // ts
    ncb = n_ctx // ts

    def blk(d, s):
        back = jnp.where(s < ncb, ncb - 1 - s, nb - 1 - (s - ncb))
        return jnp.where(d == 0, s, back)

    shared = lambda kind: pl.BlockSpec((None, n, ts, lanes), lambda d, s: (kind, 0, blk(d, s), 0))
    per_dir = lambda kind: pl.BlockSpec((None, n, ts, lanes), lambda d, s: (kind + d, 0, blk(d, s), 0))
    return pl.pallas_call(
        _rwkv_scan_kernel,
        grid=(2, nb),
        in_specs=[shared(ROW_R), shared(ROW_A), per_dir(ROW_W), per_dir(ROW_K), per_dir(ROW_B),
                  pl.BlockSpec((ts, n // 2, lanes), lambda d, s: (blk(d, s), 0, 0))],
        out_specs=pl.BlockSpec((None, ts, n // 2, lanes), lambda d, s: (d, blk(d, s), 0, 0)),
        out_shape=jax.ShapeDtypeStruct((2, t, n // 2, lanes), F32),
        scratch_shapes=[pltpu.VMEM((n, n // 2, lanes), F32), pltpu.VMEM((n // 2, lanes), F32)],
        compiler_params=_params("arbitrary", "arbitrary"),
        name="rwkv_scan",
    )(rows, rows, rows, rows, rows, v)


def _rwkv_readout_kernel(yf_ref, yb_ref, bonus_ref, gate_ref, g_ref, b_ref, o_ref, scr, *, bsz):
    n = RWKV_HEAD_DIM
    w = RWKV_W
    ts = o_ref.shape[1]
    for i in range(n // 2):
        rows = pl.ds(i, ts, stride=n // 2)
        yt = (yf_ref[rows, :] + yb_ref[rows, :]).T
        scr[pl.ds(i, LANES // 2, stride=n), :] = yt[:LANES // 2]
        scr[pl.ds(n // 2 + i, LANES // 2, stride=n), :] = yt[LANES // 2:]
    ones = _group_ones(w, n)
    inv = 1.0 / n
    for b in range(bsz):
        y = scr[b * w:(b + 1) * w, :].T
        mean = _dot2_exact_rhs(y, ones) * inv
        yc = y - mean
        var = _dot2_exact_rhs(yc * yc, ones) * inv
        yn = yc * lax.rsqrt(var + RWKV_GN_EPS) * g_ref[...] + b_ref[...]
        o_ref[b] = ((yn + bonus_ref[b]) * gate_ref[b]).astype(o_ref.dtype)


def _rwkv_readout(y, bonus, gate, ln_g, ln_b):
    bsz, t, w = bonus.shape
    n = RWKV_HEAD_DIM
    ts = LANES
    tok = pl.BlockSpec((bsz, ts, w), lambda i: (0, i, 0))
    vec = pl.BlockSpec((1, w), lambda i: (0, 0))
    return pl.pallas_call(
        functools.partial(_rwkv_readout_kernel, bsz=bsz),
        grid=(t // ts,),
        in_specs=[pl.BlockSpec((None, ts * n // 2, LANES), lambda i: (0, i, 0)),
                  pl.BlockSpec((None, ts * n // 2, LANES), lambda i: (1, i, 0)), tok, tok, vec, vec],
        out_specs=tok,
        out_shape=jax.ShapeDtypeStruct((bsz, t, w), BF16),
        scratch_shapes=[pltpu.VMEM((LANES // 2 * n, ts), F32)],
        compiler_params=_params("parallel"),
        name="rwkv_readout",
    )(y, y, bonus, gate, ln_g.reshape(1, w), ln_b.reshape(1, w))


def _merge_kernel(ya_ref, yr_ref, yw_ref, g0a_ref, g0b_ref, g1a_ref, g1b_ref, g2a_ref, g2b_ref,
                  x_ref, gate1_ref, a2_ref, b2_ref, wb_ref, wo_ref, wr_ref, br_ref,
                  x_out, h_out, ids_out, wts_out, cnt_out):
    first = jnp.logical_and(pl.program_id(0) == 0, pl.program_id(1) == 0)

    @pl.when(first)
    def _():
        cnt_out[...] = jnp.zeros_like(cnt_out)

    gate = lambda lo, hi: jax.nn.sigmoid(jnp.concatenate([lo[...], hi[...]], axis=1))
    merged = (gate(g0a_ref, g0b_ref) * _dot(ya_ref[...], wb_ref[0])
              + gate(g1a_ref, g1b_ref) * _dot(yr_ref[...], wb_ref[1])
              + gate(g2a_ref, g2b_ref) * _dot(yw_ref[...], wb_ref[2]))
    x = x_ref[...] + gate1_ref[...] * _dot(merged.astype(BF16), wo_ref[...])
    x_out[...] = x
    h = x * lax.rsqrt(jnp.mean(x * x, axis=-1, keepdims=True) + NORM_EPS) * a2_ref[...] + b2_ref[...]
    h_out[...] = h.astype(BF16)

    tm = x.shape[0]
    logits = _dot3(h, wr_ref[...]) + br_ref[...]
    lane = lax.broadcasted_iota(jnp.int32, (tm, LANES), 1)
    lane_f = lane.astype(F32)
    neg = -jnp.inf
    big = float(LANES)
    first = lambda hit: jnp.min(jnp.where(hit, lane_f, big), axis=-1, keepdims=True).astype(jnp.int32)
    is_grp = jnp.logical_and(lane >= MOE_EXPERTS, lane < MOE_EXPERTS + MOE_GROUPS)
    gl = jnp.where(is_grp, logits, neg)
    gmax = jnp.max(gl, axis=-1, keepdims=True)
    gidx = first(gl == gmax) - MOE_EXPERTS
    p_grp = 1.0 / jnp.sum(jnp.where(is_grp, jnp.exp(gl - gmax), 0.0), axis=-1, keepdims=True)
    in_grp = jnp.logical_and(lane < MOE_EXPERTS, lane // MOE_EXPERTS_PER_GROUP == gidx)
    el = jnp.where(in_grp, logits, neg)
    v1 = jnp.max(el, axis=-1, keepdims=True)
    i1 = first(el == v1)
    el2 = jnp.where(lane == i1, neg, el)
    v2 = jnp.max(el2, axis=-1, keepdims=True)
    i2 = first(el2 == v2)
    e2 = jnp.exp(v2 - v1)
    w1 = p_grp / (1.0 + e2)
    w2 = p_grp * e2 / (1.0 + e2)
    wts_out[...] = jnp.where(lane == 0, w1, jnp.where(lane == 1, w2, 0.0))

    onehot = jnp.where(jnp.logical_or(lane == i1, lane == i2), 1.0, 0.0)
    rr = lax.broadcasted_iota(jnp.int32, (tm, tm), 0)
    cc = lax.broadcasted_iota(jnp.int32, (tm, tm), 1)
    below = jnp.where(cc < rr, 1.0, 0.0).astype(BF16)
    before = _dot(below, onehot.astype(BF16)) + cnt_out[0:1, :]
    rank1 = jnp.sum(jnp.where(lane == i1, before, 0.0), axis=-1, keepdims=True).astype(jnp.int32)
    rank2 = jnp.sum(jnp.where(lane == i2, before, 0.0), axis=-1, keepdims=True).astype(jnp.int32)
    ids_out[...] = jnp.where(lane == 0, i1, jnp.where(lane == 1, i2, jnp.where(lane == 2, rank1,
                                                                                  jnp.where(lane == 3, rank2, 0))))
    cnt_out[...] = cnt_out[...] + jnp.sum(onehot, axis=0, keepdims=True)


def _merge(ya, yr, yw, u, x, gate1, a2, b2, w_branch, w_out, w_router, b_router, part):
    bsz, t, d = x.shape
    nb = bsz // MOE_PARTS
    b0 = part * nb
    tm = ROW_TILE
    sel = lambda b, i: (2 * (b + b0) + jnp.minimum(i, 1), 0, 0)
    tok = lambda width, blk=0: pl.BlockSpec((None, tm, width), lambda b, i: (b + b0, i, blk))
    own = lambda width: pl.BlockSpec((None, tm, width), lambda b, i: (b, i, 0))
    const = lambda shape: pl.BlockSpec(shape, lambda b, i: (0,) * len(shape))
    mod = pl.BlockSpec((None, 1, d), sel)
    n_gate_blocks = N_BRANCH * d // GATE_BLOCK
    return pl.pallas_call(
        _merge_kernel,
        grid=(nb, t // tm),
        in_specs=[tok(BRANCH_W), tok(BRANCH_W), tok(BRANCH_W)]
                 + [tok(GATE_BLOCK, U_GATE // GATE_BLOCK + k) for k in range(n_gate_blocks)]
                 + [tok(d),
                  mod, mod, mod,
                  const((N_BRANCH, BRANCH_W, d)), const((d, d)), const((d, LANES)), const((1, LANES))],
        out_specs=[tok(d), own(d), own(LANES), own(LANES), const((8, LANES))],
        out_shape=[jax.ShapeDtypeStruct((bsz, t, d), F32), jax.ShapeDtypeStruct((nb, t, d), BF16),
                   jax.ShapeDtypeStruct((nb, t, LANES), jnp.int32), jax.ShapeDtypeStruct((nb, t, LANES), F32),
                   jax.ShapeDtypeStruct((8, LANES), F32)],
        input_output_aliases={3 + n_gate_blocks: 0},
        compiler_params=_params("arbitrary", "arbitrary"),
        name="merge_router",
    )(ya, yr, yw, *([u] * n_gate_blocks), x, gate1, a2, b2, w_branch, w_out, w_router, b_router)


def _moe_kernel(be_ref, na_ref, x_ref, wg_ref, wu_ref, wd_ref, o_ref, wg_s, wu_s, wd_s):
    i = pl.program_id(0)
    active = i < na_ref[0]
    new_expert = jnp.logical_or(i == 0, be_ref[i] != be_ref[jnp.maximum(i - 1, 0)])

    @pl.when(jnp.logical_and(active, new_expert))
    def _():
        wg_s[...] = wg_ref[...].astype(BF16)
        wu_s[...] = wu_ref[...].astype(BF16)
        wd_s[...] = wd_ref[...].astype(BF16)

    @pl.when(active)
    def _():
        x = x_ref[...]
        act = _silu(_dot(x, wg_s[...])) * _dot(x, wu_s[...])
        o_ref[...] = _dot(act.astype(BF16), wd_s[...]).astype(o_ref.dtype)

    @pl.when(i >= na_ref[0])
    def _():
        o_ref[...] = jnp.zeros_like(o_ref)


def _moe_experts(buf, block_expert, n_active, w_gate, w_up, w_down, layer):
    rows, d = buf.shape
    hid = w_gate.shape[-1]
    grid_spec = pltpu.PrefetchScalarGridSpec(
        num_scalar_prefetch=2,
        grid=(rows // MOE_BLOCK,),
        in_specs=[pl.BlockSpec((MOE_BLOCK, d), lambda i, be, na: (i, 0)),
                  pl.BlockSpec((None, None, d, hid), lambda i, be, na: (layer, be[i], 0, 0)),
                  pl.BlockSpec((None, None, d, hid), lambda i, be, na: (layer, be[i], 0, 0)),
                  pl.BlockSpec((None, None, hid, d), lambda i, be, na: (layer, be[i], 0, 0))],
        out_specs=pl.BlockSpec((MOE_BLOCK, d), lambda i, be, na: (i, 0)),
        scratch_shapes=[pltpu.VMEM((d, hid), BF16), pltpu.VMEM((d, hid), BF16), pltpu.VMEM((hid, d), BF16)])
    return pl.pallas_call(
        _moe_kernel,
        grid_spec=grid_spec,
        out_shape=jax.ShapeDtypeStruct((rows, d), F32),
        compiler_params=_params("arbitrary"),
        name="moe_experts",
    )(block_expert, n_active, buf, w_gate, w_up, w_down)


def _combine_kernel(x_ref, *refs):
    y_refs, (w_ref, g_ref, o_ref) = refs[:-3], refs[-3:]
    tm = x_ref.shape[0]
    w = w_ref[...]
    for c, y_ref in enumerate(y_refs):
        cols = slice(c * LANES, (c + 1) * LANES)
        y = y_ref[pl.ds(0, tm, stride=2), :] * w[:, 0:1] + y_ref[pl.ds(1, tm, stride=2), :] * w[:, 1:2]
        o_ref[:, cols] = x_ref[:, cols] + g_ref[:, cols] * y


def _combine(x, y_pairs, wts, gate2, part):
    bsz, t, d = x.shape
    nb = y_pairs.shape[0]
    b0 = part * nb
    tm = ROW_TILE
    sel = lambda b, i: (2 * (b + b0) + jnp.minimum(i, 1), 0, 0)
    x_spec = pl.BlockSpec((None, tm, d), lambda b, i: (b + b0, i, 0))
    return pl.pallas_call(
        _combine_kernel,
        grid=(nb, t // tm),
        in_specs=[x_spec]
                 + [pl.BlockSpec((None, 2 * tm, LANES), functools.partial(lambda c, b, i: (b, i, c), c))
                    for c in range(d // LANES)]
                 + [pl.BlockSpec((None, tm, LANES), lambda b, i: (b, i, 0)), pl.BlockSpec((None, 1, d), sel)],
        out_specs=x_spec,
        out_shape=jax.ShapeDtypeStruct((bsz, t, d), F32),
        input_output_aliases={0: 0},
        compiler_params=_params("parallel", "parallel"),
        name="moe_combine",
    )(x, *([y_pairs] * (d // LANES)), wts, gate2)


def _moe(h, ids, wts, counts, w_gate, w_up, w_down, layer):
    bsz, t, d = h.shape
    n_tok = bsz * t
    n_pair = 2 * n_tok
    n_blocks = -(-n_pair // MOE_BLOCK) + MOE_EXPERTS
    counts = counts[0, :MOE_EXPERTS].astype(jnp.int32)
    padded = (counts + MOE_BLOCK - 1) // MOE_BLOCK * MOE_BLOCK
    pad_end = jnp.cumsum(padded)
    pad_start = pad_end - padded
    expert = ids[..., 0:2].reshape(n_pair)
    rank = ids[..., 2:4].reshape(n_pair)
    dest = pad_start.at[expert].get(mode="promise_in_bounds") + rank
    token = jnp.arange(n_pair, dtype=jnp.int32) // 2
    src = jnp.zeros((n_blocks * MOE_BLOCK,), jnp.int32).at[dest].set(
        token, unique_indices=True, mode="promise_in_bounds")
    block_start = jnp.arange(n_blocks, dtype=jnp.int32) * MOE_BLOCK
    block_expert = jnp.minimum(jnp.sum((pad_end[None, :] <= block_start[:, None]).astype(jnp.int32), axis=1),
                               MOE_EXPERTS - 1).astype(jnp.int32)
    n_active = (pad_end[-1:] // MOE_BLOCK).astype(jnp.int32)
    buf = h.reshape(n_tok, d).at[src].get(mode="promise_in_bounds")
    yb = _moe_experts(buf, block_expert, n_active, w_gate, w_up, w_down, layer)
    pairs = yb.at[dest].get(mode="promise_in_bounds", unique_indices=True)
    return pairs.reshape(bsz, 2 * t, d)


def _rope_tables(n_ctx, n_lat, head_dim):
    rows = n_lat // GRID_W
    row = jnp.broadcast_to(jnp.arange(rows, dtype=F32)[:, None], (rows, GRID_W)).reshape(-1)
    col = jnp.broadcast_to(jnp.arange(GRID_W, dtype=F32)[None, :], (rows, GRID_W)).reshape(-1)
    quarter = head_dim // 4
    inv_freq = ROPE_THETA ** (-jnp.arange(quarter, dtype=F32) / quarter)
    ang = jnp.stack([row[:, None] * inv_freq, col[:, None] * inv_freq], axis=1)
    cos, sin = jnp.cos(ang), jnp.sin(ang)
    cos_t = jnp.stack([cos, cos], axis=2).reshape(n_lat, head_dim)
    sin_t = jnp.stack([-sin, sin], axis=2).reshape(n_lat, head_dim)
    cos_t = jnp.concatenate([jnp.ones((n_ctx, head_dim), F32), cos_t], axis=0)
    sin_t = jnp.concatenate([jnp.zeros((n_ctx, head_dim), F32), sin_t], axis=0)
    rep = LANES // head_dim
    return jnp.tile(cos_t, (1, rep)), jnp.tile(sin_t, (1, rep))


def kernel(x, c, ctx, c_ctx, ada_w, ada_b, norm1_g, norm2_g, w_in, att_qn_g, att_kn_g, ret_decay_logit, ret_gn_g, rwkv_mu, rwkv_w0, rwkv_w2, rwkv_a0, rwkv_a2, rwkv_g2, rwkv_k_k, rwkv_k_a, rwkv_r_k, rwkv_ln_g, rwkv_ln_b, w_branch, w_out, router_grp_w, router_grp_b, router_exp_w, router_exp_b, moe_w_gate, moe_w_up, moe_w_down):
    bsz, n_lat, d = x.shape
    n_ctx = ctx.shape[1]
    depth = ada_w.shape[0]
    assert d == D_MODEL and n_ctx == ROW_TILE and n_lat % ROW_TILE == 0 and n_lat % GRID_W == 0
    assert 2 * bsz * RWKV_HEADS <= LANES
    t_all = n_ctx + n_lat
    assert t_all % LANES == 0 and n_ctx % SCAN_STEPS == 0

    att_cos, att_sin = _rope_tables(n_ctx, n_lat, ATT_HEAD_DIM)
    ret_cos, ret_sin = _rope_tables(n_ctx, n_lat, RET_HEAD_DIM)

    rows = -(-(bsz + 1) // 8) * 8
    cvec = jnp.zeros((rows, d), F32).at[:bsz].set(c).at[bsz].set(c_ctx)
    mods = _modulation(cvec, ada_w, ada_b)

    xs = jnp.concatenate([ctx, x], axis=1)
    for layer in range(depth):
        m = mods[layer].reshape(rows, 6, d)
        pick = lambda j: jnp.stack([jnp.broadcast_to(m[bsz, j], (bsz, d)), m[:bsz, j]], axis=1).reshape(2 * bsz, 1, d)
        sh1, sc1, g1, sh2, sc2, g2 = (pick(j) for j in range(6))
        w_l = w_in[layer].astype(BF16)
        u = _in_proj(xs, norm1_g[layer] * (1.0 + sc1), sh1, w_l)

        score_bound = (ATT_HEAD_DIM * ATT_Q_SCALE * ATT_BOUND_MARGIN) * (
            jnp.max(jnp.abs(att_qn_g[layer])) * jnp.max(jnp.abs(att_kn_g[layer])))
        bounded = 2.0 * score_bound <= ATT_EXP2_RANGE
        q_att, kv_att = _att_prep(u, att_cos, att_sin, att_qn_g[layer], att_kn_g[layer],
                                  jnp.where(bounded, -score_bound, 0.0))
        ya = _attention(q_att, kv_att, bounded.astype(jnp.int32).reshape(1), n_ctx)

        log_gamma = jax.nn.log_sigmoid(ret_decay_logit[layer].astype(F32))
        yr = _retention(u, ret_cos, ret_sin, log_gamma, ret_gn_g[layer], n_ctx)

        rows_t, v_t, gate, bonus = _rwkv_prep(
            u, rwkv_mu[layer], rwkv_w0[layer], rwkv_w2[layer], rwkv_a0[layer], rwkv_a2[layer], rwkv_g2[layer],
            rwkv_k_k[layer], rwkv_k_a[layer], rwkv_r_k[layer].reshape(-1))
        rows_s, v_s = _scan_layout(rows_t, v_t)
        y_scan = _rwkv_scan(rows_s, v_s.reshape(t_all, RWKV_HEAD_DIM // 2, LANES), n_ctx)
        yw = _rwkv_readout(y_scan.reshape(2, t_all * RWKV_HEAD_DIM // 2, LANES), bonus, gate,
                           rwkv_ln_g[layer], rwkv_ln_b[layer])

        w_router = jnp.zeros((d, LANES), F32).at[:, :MOE_EXPERTS].set(router_exp_w[layer]).at[
            :, MOE_EXPERTS:MOE_EXPERTS + MOE_GROUPS].set(router_grp_w[layer])
        b_router = jnp.zeros((1, LANES), F32).at[0, :MOE_EXPERTS].set(router_exp_b[layer]).at[
            0, MOE_EXPERTS:MOE_EXPERTS + MOE_GROUPS].set(router_grp_b[layer])
        routed = []
        for part in range(MOE_PARTS):
            xs, h2, ids, wts, counts = _merge(
                ya, yr, yw, u, xs, g1, norm2_g[layer] * (1.0 + sc2), sh2,
                w_branch[layer].astype(BF16), w_out[layer].astype(BF16), w_router, b_router, part)
            routed.append((h2, ids, wts, counts))
        pairs = [_moe(h2, ids, wts, counts, moe_w_gate, moe_w_up, moe_w_down, layer)
                 for h2, ids, wts, counts in routed]
        for part in range(MOE_PARTS):
            xs = _combine(xs, pairs[part], routed[part][2], g2, part)
    return xs[:, n_ctx:]
```

```python
import functools

import jax
import jax.numpy as jnp
from jax import lax
from jax.experimental import pallas as pl
from jax.experimental.pallas import tpu as pltpu

F32 = jnp.float32
BF16 = jnp.bfloat16

D_MODEL = 1024
GRID_W = 64
NORM_EPS = 1e-6
ROPE_THETA = 10000.0

ATT_HEADS = 8
ATT_KV_HEADS = 2
ATT_HEAD_DIM = 64
ATT_GROUP = ATT_HEADS // ATT_KV_HEADS
ATT_W = ATT_HEADS * ATT_HEAD_DIM
ATT_KV_W = ATT_KV_HEADS * ATT_HEAD_DIM

RET_HEADS = 4
RET_HEAD_DIM = 128
RET_CHUNK = 128
RET_UNROLL = 17
RET_W = RET_HEADS * RET_HEAD_DIM

RWKV_HEADS = 8
RWKV_HEAD_DIM = 64
RWKV_W = RWKV_HEADS * RWKV_HEAD_DIM
RWKV_DECAY_LORA = 64
RWKV_AAA_LORA = 64
RWKV_GATE_LORA = 128
RWKV_GN_EPS = 64e-5
RWKV_DECAY_SCALE = 0.6065306597126334
RWKV_COLS = 3 * RWKV_W + RWKV_DECAY_LORA + RWKV_AAA_LORA + RWKV_GATE_LORA

N_BRANCH = 3
BRANCH_W = 512
IN_COLS = ATT_W + 2 * ATT_KV_W + 4 * RET_W + RWKV_COLS + N_BRANCH * D_MODEL

MOE_GROUPS = 4
MOE_EXPERTS_PER_GROUP = 8
MOE_EXPERTS = MOE_GROUPS * MOE_EXPERTS_PER_GROUP
MOE_HIDDEN = 512
MOE_BLOCK = 512
MOE_PARTS = 1

LANES = 128
ROW_TILE = 256
ATT_Q_TILE = 128
ATT_KEY_TILE = 256
ATT_Q_SCALE = ATT_HEAD_DIM ** -0.5 * 1.4426950408889634
ATT_BOUND_MARGIN = 1.01
ATT_EXP2_RANGE = 100.0
SCAN_STEPS = 64
VMEM_LIMIT = 56 * 1024 * 1024

ROW_R, ROW_A, ROW_W, ROW_K, ROW_B, N_SCAN_ROWS = 0, 1, 2, 4, 6, 8

ATT_COLS = ATT_W + 2 * ATT_KV_W
RWKV_REST_COLS = RWKV_COLS - 2 * RWKV_W
U_RWKV_RK = 0
U_RET = U_RWKV_RK + 2 * RWKV_W
U_ATT = U_RET + 4 * RET_W
U_RWKV_REST = U_ATT + ATT_COLS
U_GATE = U_RWKV_REST + RWKV_REST_COLS
GATE_BLOCK = 512
IN_PROJ_MOVES = (((0, U_ATT, ATT_COLS), (ATT_COLS, U_RET, 4 * RET_W),
                  (ATT_COLS + 4 * RET_W, U_RWKV_RK, 2 * RWKV_W)),
                 ((0, 0, IN_COLS // 2),))
assert U_ATT + ATT_COLS == IN_COLS // 2 and ATT_COLS + 4 * RET_W + 2 * RWKV_W == IN_COLS // 2


def _params(*sem):
    return pltpu.CompilerParams(dimension_semantics=sem, vmem_limit_bytes=VMEM_LIMIT)


def _dot(a, b):
    return jnp.dot(a, b, preferred_element_type=F32)


def _dot_nt(a, b):
    return lax.dot_general(a, b, (((1,), (1,)), ((), ())), preferred_element_type=F32)


def _split(a):
    hi = a.astype(BF16)
    lo = (a - hi.astype(F32)).astype(BF16)
    return hi, lo


def _dot3(a, b):
    ah, al = _split(a)
    bh, bl = _split(b)
    return _dot(ah, bh) + (_dot(al, bh) + _dot(ah, bl))


def _dot2_exact_rhs(a, b_bf16):
    ah, al = _split(a)
    return _dot(ah, b_bf16) + _dot(al, b_bf16)


def _group_ones(width, group):
    r = lax.broadcasted_iota(jnp.int32, (width, width), 0) // group
    c = lax.broadcasted_iota(jnp.int32, (width, width), 1) // group
    return jnp.where(r == c, 1.0, 0.0).astype(BF16)


def _silu(x):
    return x * jax.nn.sigmoid(x)


def _swap_halves(x, quarter):
    n = x.shape[-1]
    lane = lax.broadcasted_iota(jnp.int32, x.shape, x.ndim - 1)
    up = pltpu.roll(x, n - quarter, x.ndim - 1)
    down = pltpu.roll(x, quarter, x.ndim - 1)
    return jnp.where(lane % (2 * quarter) < quarter, up, down)


def _mod_kernel(c_ref, w_ref, b_ref, o_ref):
    o_ref[...] = _dot3(_silu(c_ref[...]), w_ref[...]) + b_ref[...]


def _modulation(cvec, ada_w, ada_b):
    depth, d, cols = ada_w.shape
    rows = cvec.shape[0]
    tn = 1536
    return pl.pallas_call(
        _mod_kernel,
        grid=(depth, cols // tn),
        in_specs=[pl.BlockSpec((rows, d), lambda l, j: (0, 0)),
                  pl.BlockSpec((None, d, tn), lambda l, j: (l, 0, j)),
                  pl.BlockSpec((None, 1, tn), lambda l, j: (l, 0, j))],
        out_specs=pl.BlockSpec((None, rows, tn), lambda l, j: (l, 0, j)),
        out_shape=jax.ShapeDtypeStruct((depth, rows, cols), F32),
        compiler_params=_params("parallel", "parallel"),
        name="modulation",
    )(cvec, ada_w, ada_b.reshape(depth, 1, cols))


def _in_proj_kernel(x_ref, a_ref, b_ref, w_ref, o_ref):
    x = x_ref[...]
    ms = jnp.mean(x * x, axis=-1, keepdims=True)
    h = (x * lax.rsqrt(ms + NORM_EPS) * a_ref[...] + b_ref[...]).astype(BF16)
    for half, moves in enumerate(IN_PROJ_MOVES):
        @pl.when(pl.program_id(0) == half)
        def _():
            for src, dst, width in moves:
                o_ref[:, dst:dst + width] = _dot(h, w_ref[:, src:src + width])


def _in_proj(x, mod_a, mod_b, w):
    bsz, t, d = x.shape
    cols = w.shape[1]
    tm, tn = ROW_TILE, cols // 2
    sel = lambda j, b, i: (2 * b + jnp.minimum(i, 1), 0, 0)
    return pl.pallas_call(
        _in_proj_kernel,
        grid=(cols // tn, bsz, t // tm),
        in_specs=[pl.BlockSpec((None, tm, d), lambda j, b, i: (b, i, 0)),
                  pl.BlockSpec((None, 1, d), sel),
                  pl.BlockSpec((None, 1, d), sel),
                  pl.BlockSpec((d, tn), lambda j, b, i: (0, j))],
        out_specs=pl.BlockSpec((None, tm, tn), lambda j, b, i: (b, i, j)),
        out_shape=jax.ShapeDtypeStruct((bsz, t, cols), F32),
        compiler_params=_params("parallel", "parallel", "parallel"),
        name="in_proj",
    )(x, mod_a, mod_b, w)


def _att_prep_kernel(u_ref, cos_ref, sin_ref, qg_ref, kg_ref, shift_ref, q_out, kv_out):
    hd = ATT_HEAD_DIM
    ones = _group_ones(LANES, hd)
    cos = cos_ref[...]
    sin = sin_ref[...]
    lane = lax.broadcasted_iota(jnp.int32, cos.shape, 1)
    low = lane < hd

    def two_heads(y, extra):
        fill = jnp.where(lane == hd, extra, 0.0)
        return jnp.where(low, y, fill), jnp.where(low, pltpu.roll(y, hd, 1), fill)

    n_q = ATT_W // LANES
    for j in range(n_q + 1):
        x = u_ref[:, j * LANES:(j + 1) * LANES]
        is_q = j < n_q
        gain = qg_ref[...] if is_q else kg_ref[...]
        ms = _dot2_exact_rhs(x * x, ones) * (1.0 / hd)
        y = x * lax.rsqrt(ms + NORM_EPS) * gain
        y = y * cos + _swap_halves(y, hd // 4) * sin
        if is_q:
            y = y * ATT_Q_SCALE
        out, base = (q_out, 2 * j) if is_q else (kv_out, 0)
        for h, yh in enumerate(two_heads(y, shift_ref[...] if is_q else 1.0)):
            out[:, (base + h) * LANES:(base + h + 1) * LANES] = yh.astype(BF16)
    v = u_ref[:, ATT_W + ATT_KV_W:]
    for h, vh in enumerate(two_heads(v, 1.0)):
        kv_out[:, (2 + h) * LANES:(3 + h) * LANES] = vh.astype(BF16)


def _att_prep(u, cos, sin, qn_g, kn_g, shift):
    bsz, t, _ = u.shape
    tm = ROW_TILE
    rep = LANES // ATT_HEAD_DIM
    shift = jnp.broadcast_to(shift.astype(F32), (1, LANES))
    return pl.pallas_call(
        _att_prep_kernel,
        grid=(bsz, t // tm),
        in_specs=[pl.BlockSpec((None, tm, ATT_COLS), lambda b, i: (b, i, U_ATT // ATT_COLS)),
                  pl.BlockSpec((tm, LANES), lambda b, i: (i, 0)),
                  pl.BlockSpec((tm, LANES), lambda b, i: (i, 0)),
                  pl.BlockSpec((1, LANES), lambda b, i: (0, 0)),
                  pl.BlockSpec((1, LANES), lambda b, i: (0, 0)),
                  pl.BlockSpec((1, LANES), lambda b, i: (0, 0))],
        out_specs=[pl.BlockSpec((None, tm, ATT_HEADS * LANES), lambda b, i: (b, i, 0)),
                   pl.BlockSpec((None, tm, 2 * ATT_KV_HEADS * LANES), lambda b, i: (b, i, 0))],
        out_shape=[jax.ShapeDtypeStruct((bsz, t, ATT_HEADS * LANES), BF16),
                   jax.ShapeDtypeStruct((bsz, t, 2 * ATT_KV_HEADS * LANES), BF16)],
        compiler_params=_params("parallel", "parallel"),
        name="att_prep",
    )(u, cos, sin, jnp.tile(qn_g, rep).reshape(1, LANES), jnp.tile(kn_g, rep).reshape(1, LANES), shift)


def _att_kernel(bounded_ref, q_ref, kv_ref, o_ref, *, n_ctx, tq):
    i = pl.program_id(1)
    hd = ATT_HEAD_DIM
    k_ref = v_ref = kv_ref

    def run(n_keys, bounded):
        tk = ATT_KEY_TILE
        for g in range(ATT_KV_HEADS):
            q = jnp.concatenate(
                [q_ref[:, (ATT_GROUP * g + h) * LANES:(ATT_GROUP * g + h + 1) * LANES] for h in range(ATT_GROUP)],
                axis=0)
            scores = lambda c: _dot_nt(q, k_ref[c * tk:(c + 1) * tk, g * LANES:(g + 1) * LANES])
            if not bounded:
                m = jnp.full((ATT_GROUP * tq, LANES), -jnp.inf, F32)
                for c in range(n_keys // tk):
                    s = scores(c)
                    for part in range(tk // LANES):
                        m = jnp.maximum(m, s[:, part * LANES:(part + 1) * LANES])
                m = jnp.max(m, axis=-1, keepdims=True)
            acc = jnp.zeros((ATT_GROUP * tq, LANES), F32)
            for c in range(n_keys // tk):
                p = jnp.exp2(scores(c) if bounded else scores(c) - m).astype(BF16)
                acc = acc + _dot(p, v_ref[c * tk:(c + 1) * tk, (ATT_KV_HEADS + g) * LANES:(ATT_KV_HEADS + g + 1) * LANES])
            o = acc[:, :hd] / acc[:, hd:hd + 1]
            for h in range(ATT_GROUP):
                c0 = (ATT_GROUP * g + h) * hd
                o_ref[:, c0:c0 + hd] = o[h * tq:(h + 1) * tq].astype(o_ref.dtype)

    is_ctx = i < n_ctx // tq
    bounded = bounded_ref[0] == 1

    @pl.when(is_ctx)
    def _():
        run(n_ctx, False)

    @pl.when(jnp.logical_and(jnp.logical_not(is_ctx), bounded))
    def _():
        run(k_ref.shape[0], True)

    @pl.when(jnp.logical_and(jnp.logical_not(is_ctx), jnp.logical_not(bounded)))
    def _():
        run(k_ref.shape[0], False)


def _attention(q, kv, bounded, n_ctx):
    bsz, t, _ = q.shape
    tq = ATT_Q_TILE
    kv_w = kv.shape[-1]
    assert n_ctx % ATT_KEY_TILE == 0 and t % ATT_KEY_TILE == 0
    grid_spec = pltpu.PrefetchScalarGridSpec(
        num_scalar_prefetch=1,
        grid=(bsz, t // tq),
        in_specs=[pl.BlockSpec((None, tq, ATT_HEADS * LANES), lambda b, i, f: (b, i, 0)),
                  pl.BlockSpec((None, t, kv_w), lambda b, i, f: (b, 0, 0))],
        out_specs=pl.BlockSpec((None, tq, ATT_W), lambda b, i, f: (b, i, 0)))
    return pl.pallas_call(
        functools.partial(_att_kernel, n_ctx=n_ctx, tq=tq),
        grid_spec=grid_spec,
        out_shape=jax.ShapeDtypeStruct((bsz, t, ATT_W), BF16),
        compiler_params=_params("parallel", "parallel"),
        name="attention",
    )(bounded, q, kv)


def _ret_kernel(q_ref, k_ref, v_ref, g_ref, cos_ref, sin_ref, lg_ref, gn_ref, o_ref,
                qs_ref, ks_ref, kvf_ref, kvb_ref, sf_ref, sb_ref, *, n_ctx):
    c = RET_CHUNK
    t = q_ref.shape[0]
    n_chunks = t // c
    n_cc = n_ctx // c
    quarter = RET_HEAD_DIM // 4
    scale = RET_HEAD_DIM ** -0.5
    lg_f = lg_ref[0]
    lg_b = lg_ref[1]
    row = lax.broadcasted_iota(jnp.int32, (c, c), 0)
    col = lax.broadcasted_iota(jnp.int32, (c, c), 1)
    rowf = row.astype(F32)
    lag = (row - col).astype(F32)

    def chunk(ci):
        r0 = pl.multiple_of(ci * c, c)
        cos = cos_ref[pl.ds(r0, c), :]
        sin = sin_ref[pl.ds(r0, c), :]
        q = q_ref[pl.ds(r0, c), :]
        k = k_ref[pl.ds(r0, c), :]
        q = q * cos + _swap_halves(q, quarter) * sin
        k = (k * cos + _swap_halves(k, quarter) * sin) * scale
        return r0, q, k, v_ref[pl.ds(r0, c), :]

    colf = col.astype(F32)
    d_key_f = jnp.exp(lg_f * (c - 1.0 - colf))
    d_key_b = jnp.exp(lg_b * colf)
    d_query_f = jnp.exp(lg_f * (rowf + 1.0))
    d_query_b = jnp.exp(lg_b * (float(c) - rowf))
    d_chunk_f = jnp.exp(lg_f * float(c))
    d_chunk_b = jnp.exp(lg_b * float(c))
    d_intra = (jnp.where(lag >= 0, jnp.exp(lg_f * jnp.maximum(lag, 0.0)), 0.0)
               + jnp.where(lag <= 0, jnp.exp(lg_b * jnp.maximum(-lag, 0.0)), 0.0))

    def summaries(n, carry):
        r0, q, k, v = chunk(n)
        qs_ref[pl.ds(r0, c), :] = q.astype(BF16)
        ks_ref[pl.ds(r0, c), :] = k.astype(BF16)
        vb = v.astype(BF16)
        kt = k.T
        kvf_ref[n] = _dot((kt * d_key_f).astype(BF16), vb)
        kvb_ref[n] = _dot((kt * d_key_b).astype(BF16), vb)
        return carry

    lax.fori_loop(0, n_chunks, summaries, 0, unroll=RET_UNROLL)

    def state_f(n, s):
        sf_ref[n] = s.astype(BF16)
        return d_chunk_f * s + kvf_ref[n]

    def state_b(n, s):
        ci = jnp.where(n < n_cc, n_cc - 1 - n, n_chunks - 1 - (n - n_cc))
        sb_ref[ci] = s.astype(BF16)
        return d_chunk_b * s + kvb_ref[ci]

    zero = jnp.zeros((RET_HEAD_DIM, RET_HEAD_DIM), F32)
    lax.fori_loop(0, n_chunks, state_f, zero)
    lax.fori_loop(0, n_chunks, state_b, zero)
    gn = gn_ref[...]

    def outputs(n, carry):
        r0 = pl.multiple_of(n * c, c)
        qb = qs_ref[pl.ds(r0, c), :]
        vb = v_ref[pl.ds(r0, c), :].astype(BF16)
        scores = _dot_nt(qb, ks_ref[pl.ds(r0, c), :]) * d_intra
        y = (_dot(scores.astype(BF16), vb) + _dot(qb, sf_ref[n]) * d_query_f) + _dot(qb, sb_ref[n]) * d_query_b
        yn = y * lax.rsqrt(jnp.mean(y * y, axis=-1, keepdims=True) + NORM_EPS) * gn
        o_ref[pl.ds(r0, c), :] = (_silu(g_ref[pl.ds(r0, c), :]) * yn).astype(o_ref.dtype)
        return carry

    lax.fori_loop(0, n_chunks, outputs, 0, unroll=RET_UNROLL)


def _retention(u, cos, sin, log_gamma, gn_g, n_ctx):
    bsz, t, _ = u.shape
    hd = RET_HEAD_DIM
    base = U_RET // hd
    spec = lambda off: pl.BlockSpec((None, t, hd), lambda b, h: (b, 0, base + off * RET_HEADS + h))
    lg = jnp.broadcast_to(log_gamma[:, :, None, None], (2, RET_HEADS, 1, LANES)).astype(F32)
    return pl.pallas_call(
        functools.partial(_ret_kernel, n_ctx=n_ctx),
        grid=(bsz, RET_HEADS),
        in_specs=[spec(0), spec(1), spec(2), spec(3),
                  pl.BlockSpec((t, hd), lambda b, h: (0, 0)),
                  pl.BlockSpec((t, hd), lambda b, h: (0, 0)),
                  pl.BlockSpec((2, None, 1, LANES), lambda b, h: (0, h, 0, 0)),
                  pl.BlockSpec((1, hd), lambda b, h: (0, h))],
        out_specs=pl.BlockSpec((None, t, hd), lambda b, h: (b, 0, h)),
        out_shape=jax.ShapeDtypeStruct((bsz, t, RET_W), BF16),
        scratch_shapes=[pltpu.VMEM((t, hd), BF16), pltpu.VMEM((t, hd), BF16),
                        pltpu.VMEM((t // RET_CHUNK, hd, hd), F32), pltpu.VMEM((t // RET_CHUNK, hd, hd), F32),
                        pltpu.VMEM((t // RET_CHUNK, hd, hd), BF16), pltpu.VMEM((t // RET_CHUNK, hd, hd), BF16)],
        compiler_params=_params("parallel", "parallel"),
        name="retention",
    )(u, u, u, u, cos, sin, lg, gn_g.reshape(1, RET_W))


def _rwkv_prep_kernel(rk_ref, rk_prev_ref, rk_next_ref, rest_ref, rest_prev_ref, rest_next_ref,
                      mu_rk_ref, mu_rest_ref, w0_ref, w2_ref, a0_ref, a2_ref, g2_ref, kk_ref, ka_ref, rk_gain_ref,
                      rows_out, v_out, gate_out, bonus_out, *, n_tiles):
    i = pl.program_id(1)
    tm = rk_ref.shape[0]
    has_prev = jnp.logical_and(i != 0, i != 1)
    has_next = jnp.logical_and(i != 0, i != n_tiles - 1)

    def shifted(x_ref, prev_ref, next_ref, mu_ref):
        x = x_ref[...]
        rows = lax.broadcasted_iota(jnp.int32, x.shape, 0)
        halo_prev = jnp.where(has_prev, prev_ref[7:8, :], 0.0)
        halo_next = jnp.where(has_next, next_ref[0:1, :], 0.0)
        prev = jnp.where(rows == 0, halo_prev, pltpu.roll(x, 1, 0))
        nxt = jnp.where(rows == tm - 1, halo_next, pltpu.roll(x, tm - 1, 0))
        return x + (prev - x) * mu_ref[0:1, :] + (nxt - x) * mu_ref[1:2, :]

    rk = shifted(rk_ref, rk_prev_ref, rk_next_ref, mu_rk_ref)
    rest = shifted(rest_ref, rest_prev_ref, rest_next_ref, mu_rest_ref)
    w = RWKV_W
    r = rk[:, 0:w]
    k = rk[:, w:2 * w]
    v = rest[:, 0:w]
    xw = rest[:, w:w + RWKV_DECAY_LORA]
    xa = rest[:, w + RWKV_DECAY_LORA:w + RWKV_DECAY_LORA + RWKV_AAA_LORA]
    xg = rest[:, w + RWKV_DECAY_LORA + RWKV_AAA_LORA:]

    ones = _group_ones(w, RWKV_HEAD_DIM)
    kk = k * kk_ref[...]
    kk = kk * lax.rsqrt(jnp.maximum(_dot2_exact_rhs(kk * kk, ones), 1e-12))
    rows_out[ROW_R] = r
    rows_out[ROW_A] = -kk
    v_out[...] = v
    tw = jnp.tanh(xw)
    k_sum = jnp.zeros_like(k)
    for d in range(2):
        decay_rate = jax.nn.sigmoid(w0_ref[d:d + 1, :] + _dot3(tw, w2_ref[d])) * RWKV_DECAY_SCALE
        a = jax.nn.sigmoid(a0_ref[d:d + 1, :] + _dot3(xa, a2_ref[d]))
        k_d = k * (1.0 + (a - 1.0) * ka_ref[...])
        rows_out[ROW_W + d] = jnp.exp(-decay_rate)
        rows_out[ROW_K + d] = k_d
        rows_out[ROW_B + d] = kk * a
        k_sum = k_sum + k_d
    gate_out[...] = _dot3(jax.nn.sigmoid(xg), g2_ref[...])
    bonus_out[...] = _dot2_exact_rhs(r * k_sum * rk_gain_ref[...], ones) * v


def _rwkv_prep(u, mu, w0, w2, a0, a2, g2, k_k, k_a, r_k):
    bsz, t, _ = u.shape
    tm = ROW_TILE
    n_tiles = t // tm
    w = RWKV_W
    rk_blk = U_RWKV_RK // (2 * w)
    rest_blk = U_RWKV_REST // RWKV_REST_COLS
    sub = tm // 8
    n_sub = t // 8
    prev_idx = lambda b, i: jnp.maximum(i * sub - 1, 0)
    next_idx = lambda b, i: jnp.minimum((i + 1) * sub, n_sub - 1)
    row = lambda a: a.reshape(1, -1)
    const = lambda shape: pl.BlockSpec(shape, lambda b, i: (0,) * len(shape))
    tok = lambda width: pl.BlockSpec((None, tm, width), lambda b, i: (b, i, 0))
    rows_spec = pl.BlockSpec((N_SCAN_ROWS, None, tm, w), lambda b, i: (0, b, i, 0))
    sd = lambda *lead: jax.ShapeDtypeStruct((*lead, bsz, t, w), F32)
    return pl.pallas_call(
        functools.partial(_rwkv_prep_kernel, n_tiles=n_tiles),
        grid=(bsz, n_tiles),
        in_specs=[pl.BlockSpec((None, tm, 2 * w), lambda b, i: (b, i, rk_blk)),
                  pl.BlockSpec((None, 8, 2 * w), lambda b, i: (b, prev_idx(b, i), rk_blk)),
                  pl.BlockSpec((None, 8, 2 * w), lambda b, i: (b, next_idx(b, i), rk_blk)),
                  pl.BlockSpec((None, tm, RWKV_REST_COLS), lambda b, i: (b, i, rest_blk)),
                  pl.BlockSpec((None, 8, RWKV_REST_COLS), lambda b, i: (b, prev_idx(b, i), rest_blk)),
                  pl.BlockSpec((None, 8, RWKV_REST_COLS), lambda b, i: (b, next_idx(b, i), rest_blk)),
                  const((2, 2 * w)), const((2, RWKV_REST_COLS)),
                  const((2, w)), const((2, RWKV_DECAY_LORA, w)), const((2, w)), const((2, RWKV_AAA_LORA, w)),
                  const((RWKV_GATE_LORA, w)), const((1, w)), const((1, w)), const((1, w))],
        out_specs=[rows_spec, tok(w), tok(w), tok(w)],
        out_shape=[sd(N_SCAN_ROWS), sd(), sd(), sd()],
        compiler_params=_params("parallel", "parallel"),
        name="rwkv_prep",
    )(u, u, u, u, u, u, mu[:, :2 * w], mu[:, 2 * w:], w0, w2, a0, a2, g2, row(k_k), row(k_a), row(r_k))


def _transpose_tokens(z_ref, scr, bsz):
    w = RWKV_W
    for b in range(bsz):
        scr[b * w:(b + 1) * w, :] = z_ref[b].T
    if bsz * w < scr.shape[0]:
        scr[bsz * w:, :] = jnp.zeros((scr.shape[0] - bsz * w, scr.shape[1]), F32)


def _layout_rows_kernel(z_ref, o_ref, scr, *, bsz):
    n = RWKV_HEAD_DIM
    _transpose_tokens(z_ref, scr, bsz)
    for j in range(n):
        x = scr[pl.ds(j, LANES // 2, stride=n), :]
        o_ref[j] = jnp.concatenate([x, x], axis=0).T


def _layout_v_kernel(z_ref, o_ref, scr, *, bsz):
    n = RWKV_HEAD_DIM
    ts = z_ref.shape[1]
    _transpose_tokens(z_ref, scr, bsz)
    for i in range(n // 2):
        x0 = scr[pl.ds(i, LANES // 2, stride=n), :]
        x1 = scr[pl.ds(n // 2 + i, LANES // 2, stride=n), :]
        o_ref[pl.ds(i, ts, stride=n // 2), :] = jnp.concatenate([x0, x1], axis=0).T


def _scan_layout(rows, v):
    g, bsz, t, w = rows.shape
    n = RWKV_HEAD_DIM
    ts = LANES
    scr = pltpu.VMEM((LANES // 2 * n, ts), F32)
    rows_l = pl.pallas_call(
        functools.partial(_layout_rows_kernel, bsz=bsz),
        grid=(g, t // ts),
        in_specs=[pl.BlockSpec((None, bsz, ts, w), lambda k, i: (k, 0, i, 0))],
        out_specs=pl.BlockSpec((None, n, ts, LANES), lambda k, i: (k, 0, i, 0)),
        out_shape=jax.ShapeDtypeStruct((g, n, t, LANES), F32),
        scratch_shapes=[scr],
        compiler_params=_params("parallel", "parallel"),
        name="rwkv_layout_rows",
    )(rows)
    v_l = pl.pallas_call(
        functools.partial(_layout_v_kernel, bsz=bsz),
        grid=(t // ts,),
        in_specs=[pl.BlockSpec((bsz, ts, w), lambda i: (0, i, 0))],
        out_specs=pl.BlockSpec((ts * n // 2, LANES), lambda i: (i, 0)),
        out_shape=jax.ShapeDtypeStruct((t * n // 2, LANES), F32),
        scratch_shapes=[scr],
        compiler_params=_params("parallel"),
        name="rwkv_layout_v",
    )(v)
    return rows_l, v_l


def _rwkv_scan_kernel(r_ref, a_ref, w_ref, k_ref, b_ref, v_ref, y_ref, s_ref, sa_ref):
    n = RWKV_HEAD_DIM
    half = n // 2
    ts = r_ref.shape[1]
    fwd = pl.program_id(0) == 0

    @pl.when(pl.program_id(1) == 0)
    def _():
        s_ref[...] = jnp.zeros_like(s_ref)

    t_first = jnp.where(fwd, 0, ts - 1)
    acc = jnp.zeros((half, LANES), F32)
    for j in range(n):
        acc = acc + s_ref[j] * a_ref[j, pl.ds(t_first, 1), :]
    sa_ref[...] = acc

    def step(m, carry):
        t = jnp.where(fwd, m, ts - 1 - m)
        tn = jnp.clip(jnp.where(fwd, t + 1, t - 1), 0, ts - 1)
        sa = sa_ref[...]
        v = v_ref[t]
        y = jnp.zeros((half, LANES), F32)
        sa_next = jnp.zeros((half, LANES), F32)
        for j in range(n):
            s = (s_ref[j] * w_ref[j, pl.ds(t, 1), :] + sa * b_ref[j, pl.ds(t, 1), :]) + v * k_ref[j, pl.ds(t, 1), :]
            s_ref[j] = s
            y = y + s * r_ref[j, pl.ds(t, 1), :]
            sa_next = sa_next + s * a_ref[j, pl.ds(tn, 1), :]
        y_ref[t] = y
        sa_ref[...] = sa_next
        return carry

    lax.fori_loop(0, ts, step, 0)


def _rwkv_scan(rows, v, n_ctx):
    _, n, t, lanes = rows.shape
    ts = SCAN_STEPS
    nb = t // ts
    ncb = n_ctx // ts

    def blk(d, s):
        back = jnp.where(s < ncb, ncb - 1 - s, nb - 1 - (s - ncb))
        return jnp.where(d == 0, s, back)

    shared = lambda kind: pl.BlockSpec((None, n, ts, lanes), lambda d, s: (kind, 0, blk(d, s), 0))
    per_dir = lambda kind: pl.BlockSpec((None, n, ts, lanes), lambda d, s: (kind + d, 0, blk(d, s), 0))
    return pl.pallas_call(
        _rwkv_scan_kernel,
        grid=(2, nb),
        in_specs=[shared(ROW_R), shared(ROW_A), per_dir(ROW_W), per_dir(ROW_K), per_dir(ROW_B),
                  pl.BlockSpec((ts, n // 2, lanes), lambda d, s: (blk(d, s), 0, 0))],
        out_specs=pl.BlockSpec((None, ts, n // 2, lanes), lambda d, s: (d, blk(d, s), 0, 0)),
        out_shape=jax.ShapeDtypeStruct((2, t, n // 2, lanes), F32),
        scratch_shapes=[pltpu.VMEM((n, n // 2, lanes), F32), pltpu.VMEM((n // 2, lanes), F32)],
        compiler_params=_params("arbitrary", "arbitrary"),
        name="rwkv_scan",
    )(rows, rows, rows, rows, rows, v)


def _rwkv_readout_kernel(yf_ref, yb_ref, bonus_ref, gate_ref, g_ref, b_ref, o_ref, scr, *, bsz):
    n = RWKV_HEAD_DIM
    w = RWKV_W
    ts = o_ref.shape[1]
    for i in range(n // 2):
        rows = pl.ds(i, ts, stride=n // 2)
        yt = (yf_ref[rows, :] + yb_ref[rows, :]).T
        scr[pl.ds(i, LANES // 2, stride=n), :] = yt[:LANES // 2]
        scr[pl.ds(n // 2 + i, LANES // 2, stride=n), :] = yt[LANES // 2:]
    ones = _group_ones(w, n)
    inv = 1.0 / n
    for b in range(bsz):
        y = scr[b * w:(b + 1) * w, :].T
        mean = _dot2_exact_rhs(y, ones) * inv
        yc = y - mean
        var = _dot2_exact_rhs(yc * yc, ones) * inv
        yn = yc * lax.rsqrt(var + RWKV_GN_EPS) * g_ref[...] + b_ref[...]
        o_ref[b] = ((yn + bonus_ref[b]) * gate_ref[b]).astype(o_ref.dtype)


def _rwkv_readout(y, bonus, gate, ln_g, ln_b):
    bsz, t, w = bonus.shape
    n = RWKV_HEAD_DIM
    ts = LANES
    tok = pl.BlockSpec((bsz, ts, w), lambda i: (0, i, 0))
    vec = pl.BlockSpec((1, w), lambda i: (0, 0))
    return pl.pallas_call(
        functools.partial(_rwkv_readout_kernel, bsz=bsz),
        grid=(t // ts,),
        in_specs=[pl.BlockSpec((None, ts * n // 2, LANES), lambda i: (0, i, 0)),
                  pl.BlockSpec((None, ts * n // 2, LANES), lambda i: (1, i, 0)), tok, tok, vec, vec],
        out_specs=tok,
        out_shape=jax.ShapeDtypeStruct((bsz, t, w), BF16),
        scratch_shapes=[pltpu.VMEM((LANES // 2 * n, ts), F32)],
        compiler_params=_params("parallel"),
        name="rwkv_readout",
    )(y, y, bonus, gate, ln_g.reshape(1, w), ln_b.reshape(1, w))


def _merge_kernel(ya_ref, yr_ref, yw_ref, g0a_ref, g0b_ref, g1a_ref, g1b_ref, g2a_ref, g2b_ref,
                  x_ref, gate1_ref, a2_ref, b2_ref, wb_ref, wo_ref, wr_ref, br_ref,
                  x_out, h_out, ids_out, wts_out, cnt_out):
    first = jnp.logical_and(pl.program_id(0) == 0, pl.program_id(1) == 0)

    @pl.when(first)
    def _():
        cnt_out[...] = jnp.zeros_like(cnt_out)

    gate = lambda lo, hi: jax.nn.sigmoid(jnp.concatenate([lo[...], hi[...]], axis=1))
    merged = (gate(g0a_ref, g0b_ref) * _dot(ya_ref[...], wb_ref[0])
              + gate(g1a_ref, g1b_ref) * _dot(yr_ref[...], wb_ref[1])
              + gate(g2a_ref, g2b_ref) * _dot(yw_ref[...], wb_ref[2]))
    x = x_ref[...] + gate1_ref[...] * _dot(merged.astype(BF16), wo_ref[...])
    x_out[...] = x
    h = x * lax.rsqrt(jnp.mean(x * x, axis=-1, keepdims=True) + NORM_EPS) * a2_ref[...] + b2_ref[...]
    h_out[...] = h.astype(BF16)

    tm = x.shape[0]
    logits = _dot3(h, wr_ref[...]) + br_ref[...]
    lane = lax.broadcasted_iota(jnp.int32, (tm, LANES), 1)
    lane_f = lane.astype(F32)
    neg = -jnp.inf
    big = float(LANES)
    first = lambda hit: jnp.min(jnp.where(hit, lane_f, big), axis=-1, keepdims=True).astype(jnp.int32)
    is_grp = jnp.logical_and(lane >= MOE_EXPERTS, lane < MOE_EXPERTS + MOE_GROUPS)
    gl = jnp.where(is_grp, logits, neg)
    gmax = jnp.max(gl, axis=-1, keepdims=True)
    gidx = first(gl == gmax) - MOE_EXPERTS
    p_grp = 1.0 / jnp.sum(jnp.where(is_grp, jnp.exp(gl - gmax), 0.0), axis=-1, keepdims=True)
    in_grp = jnp.logical_and(lane < MOE_EXPERTS, lane // MOE_EXPERTS_PER_GROUP == gidx)
    el = jnp.where(in_grp, logits, neg)
    v1 = jnp.max(el, axis=-1, keepdims=True)
    i1 = first(el == v1)
    el2 = jnp.where(lane == i1, neg, el)
    v2 = jnp.max(el2, axis=-1, keepdims=True)
    i2 = first(el2 == v2)
    e2 = jnp.exp(v2 - v1)
    w1 = p_grp / (1.0 + e2)
    w2 = p_grp * e2 / (1.0 + e2)
    wts_out[...] = jnp.where(lane == 0, w1, jnp.where(lane == 1, w2, 0.0))

    onehot = jnp.where(jnp.logical_or(lane == i1, lane == i2), 1.0, 0.0)
    rr = lax.broadcasted_iota(jnp.int32, (tm, tm), 0)
    cc = lax.broadcasted_iota(jnp.int32, (tm, tm), 1)
    below = jnp.where(cc < rr, 1.0, 0.0).astype(BF16)
    before = _dot(below, onehot.astype(BF16)) + cnt_out[0:1, :]
    rank1 = jnp.sum(jnp.where(lane == i1, before, 0.0), axis=-1, keepdims=True).astype(jnp.int32)
    rank2 = jnp.sum(jnp.where(lane == i2, before, 0.0), axis=-1, keepdims=True).astype(jnp.int32)
    ids_out[...] = jnp.where(lane == 0, i1, jnp.where(lane == 1, i2, jnp.where(lane == 2, rank1,
                                                                                  jnp.where(lane == 3, rank2, 0))))
    cnt_out[...] = cnt_out[...] + jnp.sum(onehot, axis=0, keepdims=True)


def _merge(ya, yr, yw, u, x, gate1, a2, b2, w_branch, w_out, w_router, b_router, part):
    bsz, t, d = x.shape
    nb = bsz // MOE_PARTS
    b0 = part * nb
    tm = ROW_TILE
    sel = lambda b, i: (2 * (b + b0) + jnp.minimum(i, 1), 0, 0)
    tok = lambda width, blk=0: pl.BlockSpec((None, tm, width), lambda b, i: (b + b0, i, blk))
    own = lambda width: pl.BlockSpec((None, tm, width), lambda b, i: (b, i, 0))
    const = lambda shape: pl.BlockSpec(shape, lambda b, i: (0,) * len(shape))
    mod = pl.BlockSpec((None, 1, d), sel)
    n_gate_blocks = N_BRANCH * d // GATE_BLOCK
    return pl.pallas_call(
        _merge_kernel,
        grid=(nb, t // tm),
        in_specs=[tok(BRANCH_W), tok(BRANCH_W), tok(BRANCH_W)]
                 + [tok(GATE_BLOCK, U_GATE // GATE_BLOCK + k) for k in range(n_gate_blocks)]
                 + [tok(d),
                  mod, mod, mod,
                  const((N_BRANCH, BRANCH_W, d)), const((d, d)), const((d, LANES)), const((1, LANES))],
        out_specs=[tok(d), own(d), own(LANES), own(LANES), const((8, LANES))],
        out_shape=[jax.ShapeDtypeStruct((bsz, t, d), F32), jax.ShapeDtypeStruct((nb, t, d), BF16),
                   jax.ShapeDtypeStruct((nb, t, LANES), jnp.int32), jax.ShapeDtypeStruct((nb, t, LANES), F32),
                   jax.ShapeDtypeStruct((8, LANES), F32)],
        input_output_aliases={3 + n_gate_blocks: 0},
        compiler_params=_params("arbitrary", "arbitrary"),
        name="merge_router",
    )(ya, yr, yw, *([u] * n_gate_blocks), x, gate1, a2, b2, w_branch, w_out, w_router, b_router)


def _moe_kernel(be_ref, na_ref, x_ref, wg_ref, wu_ref, wd_ref, o_ref, wg_s, wu_s, wd_s):
    i = pl.program_id(0)
    active = i < na_ref[0]
    new_expert = jnp.logical_or(i == 0, be_ref[i] != be_ref[jnp.maximum(i - 1, 0)])

    @pl.when(jnp.logical_and(active, new_expert))
    def _():
        wg_s[...] = wg_ref[...].astype(BF16)
        wu_s[...] = wu_ref[...].astype(BF16)
        wd_s[...] = wd_ref[...].astype(BF16)

    @pl.when(active)
    def _():
        x = x_ref[...]
        act = _silu(_dot(x, wg_s[...])) * _dot(x, wu_s[...])
        o_ref[...] = _dot(act.astype(BF16), wd_s[...]).astype(o_ref.dtype)

    @pl.when(i >= na_ref[0])
    def _():
        o_ref[...] = jnp.zeros_like(o_ref)


def _moe_experts(buf, block_expert, n_active, w_gate, w_up, w_down, layer):
    rows, d = buf.shape
    hid = w_gate.shape[-1]
    grid_spec = pltpu.PrefetchScalarGridSpec(
        num_scalar_prefetch=2,
        grid=(rows // MOE_BLOCK,),
        in_specs=[pl.BlockSpec((MOE_BLOCK, d), lambda i, be, na: (i, 0)),
                  pl.BlockSpec((None, None, d, hid), lambda i, be, na: (layer, be[i], 0, 0)),
                  pl.BlockSpec((None, None, d, hid), lambda i, be, na: (layer, be[i], 0, 0)),
                  pl.BlockSpec((None, None, hid, d), lambda i, be, na: (layer, be[i], 0, 0))],
        out_specs=pl.BlockSpec((MOE_BLOCK, d), lambda i, be, na: (i, 0)),
        scratch_shapes=[pltpu.VMEM((d, hid), BF16), pltpu.VMEM((d, hid), BF16), pltpu.VMEM((hid, d), BF16)])
    return pl.pallas_call(
        _moe_kernel,
        grid_spec=grid_spec,
        out_shape=jax.ShapeDtypeStruct((rows, d), F32),
        compiler_params=_params("arbitrary"),
        name="moe_experts",
    )(block_expert, n_active, buf, w_gate, w_up, w_down)


def _combine_kernel(x_ref, *refs):
    y_refs, (w_ref, g_ref, o_ref) = refs[:-3], refs[-3:]
    tm = x_ref.shape[0]
    w = w_ref[...]
    for c, y_ref in enumerate(y_refs):
        cols = slice(c * LANES, (c + 1) * LANES)
        y = y_ref[pl.ds(0, tm, stride=2), :] * w[:, 0:1] + y_ref[pl.ds(1, tm, stride=2), :] * w[:, 1:2]
        o_ref[:, cols] = x_ref[:, cols] + g_ref[:, cols] * y


def _combine(x, y_pairs, wts, gate2, part):
    bsz, t, d = x.shape
    nb = y_pairs.shape[0]
    b0 = part * nb
    tm = ROW_TILE
    sel = lambda b, i: (2 * (b + b0) + jnp.minimum(i, 1), 0, 0)
    x_spec = pl.BlockSpec((None, tm, d), lambda b, i: (b + b0, i, 0))
    return pl.pallas_call(
        _combine_kernel,
        grid=(nb, t // tm),
        in_specs=[x_spec]
                 + [pl.BlockSpec((None, 2 * tm, LANES), functools.partial(lambda c, b, i: (b, i, c), c))
                    for c in range(d // LANES)]
                 + [pl.BlockSpec((None, tm, LANES), lambda b, i: (b, i, 0)), pl.BlockSpec((None, 1, d), sel)],
        out_specs=x_spec,
        out_shape=jax.ShapeDtypeStruct((bsz, t, d), F32),
        input_output_aliases={0: 0},
        compiler_params=_params("parallel", "parallel"),
        name="moe_combine",
    )(x, *([y_pairs] * (d // LANES)), wts, gate2)


def _moe(h, ids, wts, counts, w_gate, w_up, w_down, layer):
    bsz, t, d = h.shape
    n_tok = bsz * t
    n_pair = 2 * n_tok
    n_blocks = -(-n_pair // MOE_BLOCK) + MOE_EXPERTS
    counts = counts[0, :MOE_EXPERTS].astype(jnp.int32)
    padded = (counts + MOE_BLOCK - 1) // MOE_BLOCK * MOE_BLOCK
    pad_end = jnp.cumsum(padded)
    pad_start = pad_end - padded
    expert = ids[..., 0:2].reshape(n_pair)
    rank = ids[..., 2:4].reshape(n_pair)
    dest = pad_start.at[expert].get(mode="promise_in_bounds") + rank
    token = jnp.arange(n_pair, dtype=jnp.int32) // 2
    src = jnp.zeros((n_blocks * MOE_BLOCK,), jnp.int32).at[dest].set(
        token, unique_indices=True, mode="promise_in_bounds")
    block_start = jnp.arange(n_blocks, dtype=jnp.int32) * MOE_BLOCK
    block_expert = jnp.minimum(jnp.sum((pad_end[None, :] <= block_start[:, None]).astype(jnp.int32), axis=1),
                               MOE_EXPERTS - 1).astype(jnp.int32)
    n_active = (pad_end[-1:] // MOE_BLOCK).astype(jnp.int32)
    buf = h.reshape(n_tok, d).at[src].get(mode="promise_in_bounds")
    yb = _moe_experts(buf, block_expert, n_active, w_gate, w_up, w_down, layer)
    pairs = yb.at[dest].get(mode="promise_in_bounds", unique_indices=True)
    return pairs.reshape(bsz, 2 * t, d)


def _rope_tables(n_ctx, n_lat, head_dim):
    rows = n_lat // GRID_W
    row = jnp.broadcast_to(jnp.arange(rows, dtype=F32)[:, None], (rows, GRID_W)).reshape(-1)
    col = jnp.broadcast_to(jnp.arange(GRID_W, dtype=F32)[None, :], (rows, GRID_W)).reshape(-1)
    quarter = head_dim // 4
    inv_freq = ROPE_THETA ** (-jnp.arange(quarter, dtype=F32) / quarter)
    ang = jnp.stack([row[:, None] * inv_freq, col[:, None] * inv_freq], axis=1)
    cos, sin = jnp.cos(ang), jnp.sin(ang)
    cos_t = jnp.stack([cos, cos], axis=2).reshape(n_lat, head_dim)
    sin_t = jnp.stack([-sin, sin], axis=2).reshape(n_lat, head_dim)
    cos_t = jnp.concatenate([jnp.ones((n_ctx, head_dim), F32), cos_t], axis=0)
    sin_t = jnp.concatenate([jnp.zeros((n_ctx, head_dim), F32), sin_t], axis=0)
    rep = LANES // head_dim
    return jnp.tile(cos_t, (1, rep)), jnp.tile(sin_t, (1, rep))


def kernel(x, c, ctx, c_ctx, ada_w, ada_b, norm1_g, norm2_g, w_in, att_qn_g, att_kn_g, ret_decay_logit, ret_gn_g, rwkv_mu, rwkv_w0, rwkv_w2, rwkv_a0, rwkv_a2, rwkv_g2, rwkv_k_k, rwkv_k_a, rwkv_r_k, rwkv_ln_g, rwkv_ln_b, w_branch, w_out, router_grp_w, router_grp_b, router_exp_w, router_exp_b, moe_w_gate, moe_w_up, moe_w_down):
    bsz, n_lat, d = x.shape
    n_ctx = ctx.shape[1]
    depth = ada_w.shape[0]
    assert d == D_MODEL and n_ctx == ROW_TILE and n_lat % ROW_TILE == 0 and n_lat % GRID_W == 0
    assert 2 * bsz * RWKV_HEADS <= LANES
    t_all = n_ctx + n_lat
    assert t_all % LANES == 0 and n_ctx % SCAN_STEPS == 0

    att_cos, att_sin = _rope_tables(n_ctx, n_lat, ATT_HEAD_DIM)
    ret_cos, ret_sin = _rope_tables(n_ctx, n_lat, RET_HEAD_DIM)

    rows = -(-(bsz + 1) // 8) * 8
    cvec = jnp.zeros((rows, d), F32).at[:bsz].set(c).at[bsz].set(c_ctx)
    mods = _modulation(cvec, ada_w, ada_b)

    xs = jnp.concatenate([ctx, x], axis=1)
    for layer in range(depth):
        m = mods[layer].reshape(rows, 6, d)
        pick = lambda j: jnp.stack([jnp.broadcast_to(m[bsz, j], (bsz, d)), m[:bsz, j]], axis=1).reshape(2 * bsz, 1, d)
        sh1, sc1, g1, sh2, sc2, g2 = (pick(j) for j in range(6))
        w_l = w_in[layer].astype(BF16)
        u = _in_proj(xs, norm1_g[layer] * (1.0 + sc1), sh1, w_l)

        score_bound = (ATT_HEAD_DIM * ATT_Q_SCALE * ATT_BOUND_MARGIN) * (
            jnp.max(jnp.abs(att_qn_g[layer])) * jnp.max(jnp.abs(att_kn_g[layer])))
        bounded = 2.0 * score_bound <= ATT_EXP2_RANGE
        q_att, kv_att = _att_prep(u, att_cos, att_sin, att_qn_g[layer], att_kn_g[layer],
                                  jnp.where(bounded, -score_bound, 0.0))
        ya = _attention(q_att, kv_att, bounded.astype(jnp.int32).reshape(1), n_ctx)

        log_gamma = jax.nn.log_sigmoid(ret_decay_logit[layer].astype(F32))
        yr = _retention(u, ret_cos, ret_sin, log_gamma, ret_gn_g[layer], n_ctx)

        rows_t, v_t, gate, bonus = _rwkv_prep(
            u, rwkv_mu[layer], rwkv_w0[layer], rwkv_w2[layer], rwkv_a0[layer], rwkv_a2[layer], rwkv_g2[layer],
            rwkv_k_k[layer], rwkv_k_a[layer], rwkv_r_k[layer].reshape(-1))
        rows_s, v_s = _scan_layout(rows_t, v_t)
        y_scan = _rwkv_scan(rows_s, v_s.reshape(t_all, RWKV_HEAD_DIM // 2, LANES), n_ctx)
        yw = _rwkv_readout(y_scan.reshape(2, t_all * RWKV_HEAD_DIM // 2, LANES), bonus, gate,
                           rwkv_ln_g[layer], rwkv_ln_b[layer])

        w_router = jnp.zeros((d, LANES), F32).at[:, :MOE_EXPERTS].set(router_exp_w[layer]).at[
            :, MOE_EXPERTS:MOE_EXPERTS + MOE_GROUPS].set(router_grp_w[layer])
        b_router = jnp.zeros((1, LANES), F32).at[0, :MOE_EXPERTS].set(router_exp_b[layer]).at[
            0, MOE_EXPERTS:MOE_EXPERTS + MOE_GROUPS].set(router_grp_b[layer])
        routed = []
        for part in range(MOE_PARTS):
            xs, h2, ids, wts, counts = _merge(
                ya, yr, yw, u, xs, g1, norm2_g[layer] * (1.0 + sc2), sh2,
                w_branch[layer].astype(BF16), w_out[layer].astype(BF16), w_router, b_router, part)
            routed.append((h2, ids, wts, counts))
        pairs = [_moe(h2, ids, wts, counts, moe_w_gate, moe_w_up, moe_w_down, layer)
                 for h2, ids, wts, counts in routed]
        for part in range(MOE_PARTS):
            xs = _combine(xs, pairs[part], routed[part][2], g2, part)
    return xs[:, n_ctx:]
```

```python
import functools

import jax
import jax.numpy as jnp
from jax import lax
from jax.experimental import pallas as pl
from jax.experimental.pallas import tpu as pltpu

F32 = jnp.float32
BF16 = jnp.bfloat16

D_MODEL = 1024
GRID_W = 64
NORM_EPS = 1e-6
ROPE_THETA = 10000.0

ATT_HEADS = 8
ATT_KV_HEADS = 2
ATT_HEAD_DIM = 64
ATT_GROUP = ATT_HEADS // ATT_KV_HEADS
ATT_W = ATT_HEADS * ATT_HEAD_DIM
ATT_KV_W = ATT_KV_HEADS * ATT_HEAD_DIM

RET_HEADS = 4
RET_HEAD_DIM = 128
RET_CHUNK = 128
RET_UNROLL = 17
RET_W = RET_HEADS * RET_HEAD_DIM

RWKV_HEADS = 8
RWKV_HEAD_DIM = 64
RWKV_W = RWKV_HEADS * RWKV_HEAD_DIM
RWKV_DECAY_LORA = 64
RWKV_AAA_LORA = 64
RWKV_GATE_LORA = 128
RWKV_GN_EPS = 64e-5
RWKV_DECAY_SCALE = 0.6065306597126334
RWKV_COLS = 3 * RWKV_W + RWKV_DECAY_LORA + RWKV_AAA_LORA + RWKV_GATE_LORA

N_BRANCH = 3
BRANCH_W = 512
IN_COLS = ATT_W + 2 * ATT_KV_W + 4 * RET_W + RWKV_COLS + N_BRANCH * D_MODEL

MOE_GROUPS = 4
MOE_EXPERTS_PER_GROUP = 8
MOE_EXPERTS = MOE_GROUPS * MOE_EXPERTS_PER_GROUP
MOE_HIDDEN = 512
MOE_BLOCK = 256
MOE_BLOCK_ROUND = 128
MOE_PARTS = 1

LANES = 128
ROW_TILE = 256
ATT_Q_TILE = 128
ATT_KEY_TILE = 256
ATT_Q_SCALE = ATT_HEAD_DIM ** -0.5 * 1.4426950408889634
ATT_BOUND_MARGIN = 1.01
ATT_EXP2_RANGE = 100.0
SCAN_STEPS = 64
VMEM_LIMIT = 56 * 1024 * 1024

ROW_R, ROW_A, ROW_W, ROW_K, ROW_B, N_SCAN_ROWS = 0, 1, 2, 4, 6, 8

ATT_COLS = ATT_W + 2 * ATT_KV_W
RWKV_REST_COLS = RWKV_COLS - 2 * RWKV_W
U_RWKV_RK = 0
U_RET = U_RWKV_RK + 2 * RWKV_W
U_ATT = U_RET + 4 * RET_W
U_RWKV_REST = U_ATT + ATT_COLS
U_GATE = U_RWKV_REST + RWKV_REST_COLS
GATE_BLOCK = 512
IN_PROJ_MOVES = (((0, U_ATT, ATT_COLS), (ATT_COLS, U_RET, 4 * RET_W),
                  (ATT_COLS + 4 * RET_W, U_RWKV_RK, 2 * RWKV_W)),
                 ((0, 0, IN_COLS // 2),))
assert U_ATT + ATT_COLS == IN_COLS // 2 and ATT_COLS + 4 * RET_W + 2 * RWKV_W == IN_COLS // 2


def _params(*sem):
    return pltpu.CompilerParams(dimension_semantics=sem, vmem_limit_bytes=VMEM_LIMIT)


def _dot(a, b):
    return jnp.dot(a, b, preferred_element_type=F32)


def _dot_nt(a, b):
    return lax.dot_general(a, b, (((1,), (1,)), ((), ())), preferred_element_type=F32)


def _split(a):
    hi = a.astype(BF16)
    lo = (a - hi.astype(F32)).astype(BF16)
    return hi, lo


def _dot3(a, b):
    ah, al = _split(a)
    bh, bl = _split(b)
    return _dot(ah, bh) + (_dot(al, bh) + _dot(ah, bl))


def _dot2_exact_rhs(a, b_bf16):
    ah, al = _split(a)
    return _dot(ah, b_bf16) + _dot(al, b_bf16)


def _group_ones(width, group):
    r = lax.broadcasted_iota(jnp.int32, (width, width), 0) // group
    c = lax.broadcasted_iota(jnp.int32, (width, width), 1) // group
    return jnp.where(r == c, 1.0, 0.0).astype(BF16)


def _silu(x):
    return x * jax.nn.sigmoid(x)


def _swap_halves(x, quarter):
    n = x.shape[-1]
    lane = lax.broadcasted_iota(jnp.int32, x.shape, x.ndim - 1)
    up = pltpu.roll(x, n - quarter, x.ndim - 1)
    down = pltpu.roll(x, quarter, x.ndim - 1)
    return jnp.where(lane % (2 * quarter) < quarter, up, down)


def _mod_kernel(c_ref, w_ref, b_ref, o_ref):
    o_ref[...] = _dot3(_silu(c_ref[...]), w_ref[...]) + b_ref[...]


def _modulation(cvec, ada_w, ada_b):
    depth, d, cols = ada_w.shape
    rows = cvec.shape[0]
    tn = 1536
    return pl.pallas_call(
        _mod_kernel,
        grid=(depth, cols // tn),
        in_specs=[pl.BlockSpec((rows, d), lambda l, j: (0, 0)),
                  pl.BlockSpec((None, d, tn), lambda l, j: (l, 0, j)),
                  pl.BlockSpec((None, 1, tn), lambda l, j: (l, 0, j))],
        out_specs=pl.BlockSpec((None, rows, tn), lambda l, j: (l, 0, j)),
        out_shape=jax.ShapeDtypeStruct((depth, rows, cols), F32),
        compiler_params=_params("parallel", "parallel"),
        name="modulation",
    )(cvec, ada_w, ada_b.reshape(depth, 1, cols))


def _in_proj_kernel(x_ref, a_ref, b_ref, w_ref, o_ref):
    x = x_ref[...]
    ms = jnp.mean(x * x, axis=-1, keepdims=True)
    h = (x * lax.rsqrt(ms + NORM_EPS) * a_ref[...] + b_ref[...]).astype(BF16)
    for half, moves in enumerate(IN_PROJ_MOVES):
        @pl.when(pl.program_id(0) == half)
        def _():
            for src, dst, width in moves:
                o_ref[:, dst:dst + width] = _dot(h, w_ref[:, src:src + width])


def _in_proj(x, mod_a, mod_b, w):
    bsz, t, d = x.shape
    cols = w.shape[1]
    tm, tn = ROW_TILE, cols // 2
    sel = lambda j, b, i: (2 * b + jnp.minimum(i, 1), 0, 0)
    return pl.pallas_call(
        _in_proj_kernel,
        grid=(cols // tn, bsz, t // tm),
        in_specs=[pl.BlockSpec((None, tm, d), lambda j, b, i: (b, i, 0)),
                  pl.BlockSpec((None, 1, d), sel),
                  pl.BlockSpec((None, 1, d), sel),
                  pl.BlockSpec((d, tn), lambda j, b, i: (0, j))],
        out_specs=pl.BlockSpec((None, tm, tn), lambda j, b, i: (b, i, j)),
        out_shape=jax.ShapeDtypeStruct((bsz, t, cols), F32),
        compiler_params=_params("parallel", "parallel", "parallel"),
        name="in_proj",
    )(x, mod_a, mod_b, w)


def _att_prep_kernel(u_ref, cos_ref, sin_ref, qg_ref, kg_ref, shift_ref, q_out, kv_out):
    hd = ATT_HEAD_DIM
    ones = _group_ones(LANES, hd)
    cos = cos_ref[...]
    sin = sin_ref[...]
    lane = lax.broadcasted_iota(jnp.int32, cos.shape, 1)
    low = lane < hd

    def two_heads(y, extra):
        fill = jnp.where(lane == hd, extra, 0.0)
        return jnp.where(low, y, fill), jnp.where(low, pltpu.roll(y, hd, 1), fill)

    n_q = ATT_W // LANES
    for j in range(n_q + 1):
        x = u_ref[:, j * LANES:(j + 1) * LANES]
        is_q = j < n_q
        gain = qg_ref[...] if is_q else kg_ref[...]
        ms = _dot2_exact_rhs(x * x, ones) * (1.0 / hd)
        y = x * lax.rsqrt(ms + NORM_EPS) * gain
        y = y * cos + _swap_halves(y, hd // 4) * sin
        if is_q:
            y = y * ATT_Q_SCALE
        out, base = (q_out, 2 * j) if is_q else (kv_out, 0)
        for h, yh in enumerate(two_heads(y, shift_ref[...] if is_q else 1.0)):
            out[:, (base + h) * LANES:(base + h + 1) * LANES] = yh.astype(BF16)
    v = u_ref[:, ATT_W + ATT_KV_W:]
    for h, vh in enumerate(two_heads(v, 1.0)):
        kv_out[:, (2 + h) * LANES:(3 + h) * LANES] = vh.astype(BF16)


def _att_prep(u, cos, sin, qn_g, kn_g, shift):
    bsz, t, _ = u.shape
    tm = ROW_TILE
    rep = LANES // ATT_HEAD_DIM
    shift = jnp.broadcast_to(shift.astype(F32), (1, LANES))
    return pl.pallas_call(
        _att_prep_kernel,
        grid=(bsz, t // tm),
        in_specs=[pl.BlockSpec((None, tm, ATT_COLS), lambda b, i: (b, i, U_ATT // ATT_COLS)),
                  pl.BlockSpec((tm, LANES), lambda b, i: (i, 0)),
                  pl.BlockSpec((tm, LANES), lambda b, i: (i, 0)),
                  pl.BlockSpec((1, LANES), lambda b, i: (0, 0)),
                  pl.BlockSpec((1, LANES), lambda b, i: (0, 0)),
                  pl.BlockSpec((1, LANES), lambda b, i: (0, 0))],
        out_specs=[pl.BlockSpec((None, tm, ATT_HEADS * LANES), lambda b, i: (b, i, 0)),
                   pl.BlockSpec((None, tm, 2 * ATT_KV_HEADS * LANES), lambda b, i: (b, i, 0))],
        out_shape=[jax.ShapeDtypeStruct((bsz, t, ATT_HEADS * LANES), BF16),
                   jax.ShapeDtypeStruct((bsz, t, 2 * ATT_KV_HEADS * LANES), BF16)],
        compiler_params=_params("parallel", "parallel"),
        name="att_prep",
    )(u, cos, sin, jnp.tile(qn_g, rep).reshape(1, LANES), jnp.tile(kn_g, rep).reshape(1, LANES), shift)


def _att_kernel(bounded_ref, q_ref, kv_ref, o_ref, *, n_ctx, tq):
    i = pl.program_id(1)
    hd = ATT_HEAD_DIM
    k_ref = v_ref = kv_ref

    def run(n_keys, bounded):
        tk = ATT_KEY_TILE
        for g in range(ATT_KV_HEADS):
            q = jnp.concatenate(
                [q_ref[:, (ATT_GROUP * g + h) * LANES:(ATT_GROUP * g + h + 1) * LANES] for h in range(ATT_GROUP)],
                axis=0)
            scores = lambda c: _dot_nt(q, k_ref[c * tk:(c + 1) * tk, g * LANES:(g + 1) * LANES])
            if not bounded:
                m = jnp.full((ATT_GROUP * tq, LANES), -jnp.inf, F32)
                for c in range(n_keys // tk):
                    s = scores(c)
                    for part in range(tk // LANES):
                        m = jnp.maximum(m, s[:, part * LANES:(part + 1) * LANES])
                m = jnp.max(m, axis=-1, keepdims=True)
            acc = jnp.zeros((ATT_GROUP * tq, LANES), F32)
            for c in range(n_keys // tk):
                p = jnp.exp2(scores(c) if bounded else scores(c) - m).astype(BF16)
                acc = acc + _dot(p, v_ref[c * tk:(c + 1) * tk, (ATT_KV_HEADS + g) * LANES:(ATT_KV_HEADS + g + 1) * LANES])
            o = acc[:, :hd] / acc[:, hd:hd + 1]
            for h in range(ATT_GROUP):
                c0 = (ATT_GROUP * g + h) * hd
                o_ref[:, c0:c0 + hd] = o[h * tq:(h + 1) * tq].astype(o_ref.dtype)

    is_ctx = i < n_ctx // tq
    bounded = bounded_ref[0] == 1

    @pl.when(is_ctx)
    def _():
        run(n_ctx, False)

    @pl.when(jnp.logical_and(jnp.logical_not(is_ctx), bounded))
    def _():
        run(k_ref.shape[0], True)

    @pl.when(jnp.logical_and(jnp.logical_not(is_ctx), jnp.logical_not(bounded)))
    def _():
        run(k_ref.shape[0], False)


def _attention(q, kv, bounded, n_ctx):
    bsz, t, _ = q.shape
    tq = ATT_Q_TILE
    kv_w = kv.shape[-1]
    assert n_ctx % ATT_KEY_TILE == 0 and t % ATT_KEY_TILE == 0
    grid_spec = pltpu.PrefetchScalarGridSpec(
        num_scalar_prefetch=1,
        grid=(bsz, t // tq),
        in_specs=[pl.BlockSpec((None, tq, ATT_HEADS * LANES), lambda b, i, f: (b, i, 0)),
                  pl.BlockSpec((None, t, kv_w), lambda b, i, f: (b, 0, 0))],
        out_specs=pl.BlockSpec((None, tq, ATT_W), lambda b, i, f: (b, i, 0)))
    return pl.pallas_call(
        functools.partial(_att_kernel, n_ctx=n_ctx, tq=tq),
        grid_spec=grid_spec,
        out_shape=jax.ShapeDtypeStruct((bsz, t, ATT_W), BF16),
        compiler_params=_params("parallel", "parallel"),
        name="attention",
    )(bounded, q, kv)


def _ret_kernel(q_ref, k_ref, v_ref, g_ref, cos_ref, sin_ref, lg_ref, gn_ref, o_ref,
                qs_ref, ks_ref, kvf_ref, kvb_ref, sf_ref, sb_ref, *, n_ctx):
    c = RET_CHUNK
    t = q_ref.shape[0]
    n_chunks = t // c
    n_cc = n_ctx // c
    quarter = RET_HEAD_DIM // 4
    scale = RET_HEAD_DIM ** -0.5
    lg_f = lg_ref[0]
    lg_b = lg_ref[1]
    row = lax.broadcasted_iota(jnp.int32, (c, c), 0)
    col = lax.broadcasted_iota(jnp.int32, (c, c), 1)
    rowf = row.astype(F32)
    lag = (row - col).astype(F32)

    def chunk(ci):
        r0 = pl.multiple_of(ci * c, c)
        cos = cos_ref[pl.ds(r0, c), :]
        sin = sin_ref[pl.ds(r0, c), :]
        q = q_ref[pl.ds(r0, c), :]
        k = k_ref[pl.ds(r0, c), :]
        q = q * cos + _swap_halves(q, quarter) * sin
        k = (k * cos + _swap_halves(k, quarter) * sin) * scale
        return r0, q, k, v_ref[pl.ds(r0, c), :]

    colf = col.astype(F32)
    d_key_f = jnp.exp(lg_f * (c - 1.0 - colf))
    d_key_b = jnp.exp(lg_b * colf)
    d_query_f = jnp.exp(lg_f * (rowf + 1.0))
    d_query_b = jnp.exp(lg_b * (float(c) - rowf))
    d_chunk_f = jnp.exp(lg_f * float(c))
    d_chunk_b = jnp.exp(lg_b * float(c))
    d_intra = (jnp.where(lag >= 0, jnp.exp(lg_f * jnp.maximum(lag, 0.0)), 0.0)
               + jnp.where(lag <= 0, jnp.exp(lg_b * jnp.maximum(-lag, 0.0)), 0.0))

    def summaries(n, carry):
        r0, q, k, v = chunk(n)
        qs_ref[pl.ds(r0, c), :] = q.astype(BF16)
        ks_ref[pl.ds(r0, c), :] = k.astype(BF16)
        vb = v.astype(BF16)
        kt = k.T
        kvf_ref[n] = _dot((kt * d_key_f).astype(BF16), vb)
        kvb_ref[n] = _dot((kt * d_key_b).astype(BF16), vb)
        return carry

    lax.fori_loop(0, n_chunks, summaries, 0, unroll=RET_UNROLL)

    def state_f(n, s):
        sf_ref[n] = s.astype(BF16)
        return d_chunk_f * s + kvf_ref[n]

    def state_b(n, s):
        ci = jnp.where(n < n_cc, n_cc - 1 - n, n_chunks - 1 - (n - n_cc))
        sb_ref[ci] = s.astype(BF16)
        return d_chunk_b * s + kvb_ref[ci]

    zero = jnp.zeros((RET_HEAD_DIM, RET_HEAD_DIM), F32)
    lax.fori_loop(0, n_chunks, state_f, zero)
    lax.fori_loop(0, n_chunks, state_b, zero)
    gn = gn_ref[...]

    def outputs(n, carry):
        r0 = pl.multiple_of(n * c, c)
        qb = qs_ref[pl.ds(r0, c), :]
        vb = v_ref[pl.ds(r0, c), :].astype(BF16)
        scores = _dot_nt(qb, ks_ref[pl.ds(r0, c), :]) * d_intra
        y = (_dot(scores.astype(BF16), vb) + _dot(qb, sf_ref[n]) * d_query_f) + _dot(qb, sb_ref[n]) * d_query_b
        yn = y * lax.rsqrt(jnp.mean(y * y, axis=-1, keepdims=True) + NORM_EPS) * gn
        o_ref[pl.ds(r0, c), :] = (_silu(g_ref[pl.ds(r0, c), :]) * yn).astype(o_ref.dtype)
        return carry

    lax.fori_loop(0, n_chunks, outputs, 0, unroll=RET_UNROLL)


def _retention(u, cos, sin, log_gamma, gn_g, n_ctx):
    bsz, t, _ = u.shape
    hd = RET_HEAD_DIM
    base = U_RET // hd
    spec = lambda off: pl.BlockSpec((None, t, hd), lambda b, h: (b, 0, base + off * RET_HEADS + h))
    lg = jnp.broadcast_to(log_gamma[:, :, None, None], (2, RET_HEADS, 1, LANES)).astype(F32)
    return pl.pallas_call(
        functools.partial(_ret_kernel, n_ctx=n_ctx),
        grid=(bsz, RET_HEADS),
        in_specs=[spec(0), spec(1), spec(2), spec(3),
                  pl.BlockSpec((t, hd), lambda b, h: (0, 0)),
                  pl.BlockSpec((t, hd), lambda b, h: (0, 0)),
                  pl.BlockSpec((2, None, 1, LANES), lambda b, h: (0, h, 0, 0)),
                  pl.BlockSpec((1, hd), lambda b, h: (0, h))],
        out_specs=pl.BlockSpec((None, t, hd), lambda b, h: (b, 0, h)),
        out_shape=jax.ShapeDtypeStruct((bsz, t, RET_W), BF16),
        scratch_shapes=[pltpu.VMEM((t, hd), BF16), pltpu.VMEM((t, hd), BF16),
                        pltpu.VMEM((t // RET_CHUNK, hd, hd), F32), pltpu.VMEM((t // RET_CHUNK, hd, hd), F32),
                        pltpu.VMEM((t // RET_CHUNK, hd, hd), BF16), pltpu.VMEM((t // RET_CHUNK, hd, hd), BF16)],
        compiler_params=_params("parallel", "parallel"),
        name="retention",
    )(u, u, u, u, cos, sin, lg, gn_g.reshape(1, RET_W))


def _rwkv_prep_kernel(rk_ref, rk_prev_ref, rk_next_ref, rest_ref, rest_prev_ref, rest_next_ref,
                      mu_rk_ref, mu_rest_ref, w0_ref, w2_ref, a0_ref, a2_ref, g2_ref, kk_ref, ka_ref, rk_gain_ref,
                      rows_out, v_out, gate_out, bonus_out, *, n_tiles):
    i = pl.program_id(1)
    tm = rk_ref.shape[0]
    has_prev = jnp.logical_and(i != 0, i != 1)
    has_next = jnp.logical_and(i != 0, i != n_tiles - 1)

    def shifted(x_ref, prev_ref, next_ref, mu_ref):
        x = x_ref[...]
        rows = lax.broadcasted_iota(jnp.int32, x.shape, 0)
        halo_prev = jnp.where(has_prev, prev_ref[7:8, :], 0.0)
        halo_next = jnp.where(has_next, next_ref[0:1, :], 0.0)
        prev = jnp.where(rows == 0, halo_prev, pltpu.roll(x, 1, 0))
        nxt = jnp.where(rows == tm - 1, halo_next, pltpu.roll(x, tm - 1, 0))
        return x + (prev - x) * mu_ref[0:1, :] + (nxt - x) * mu_ref[1:2, :]

    rk = shifted(rk_ref, rk_prev_ref, rk_next_ref, mu_rk_ref)
    rest = shifted(rest_ref, rest_prev_ref, rest_next_ref, mu_rest_ref)
    w = RWKV_W
    r = rk[:, 0:w]
    k = rk[:, w:2 * w]
    v = rest[:, 0:w]
    xw = rest[:, w:w + RWKV_DECAY_LORA]
    xa = rest[:, w + RWKV_DECAY_LORA:w + RWKV_DECAY_LORA + RWKV_AAA_LORA]
    xg = rest[:, w + RWKV_DECAY_LORA + RWKV_AAA_LORA:]

    ones = _group_ones(w, RWKV_HEAD_DIM)
    kk = k * kk_ref[...]
    kk = kk * lax.rsqrt(jnp.maximum(_dot2_exact_rhs(kk * kk, ones), 1e-12))
    rows_out[ROW_R] = r
    rows_out[ROW_A] = -kk
    v_out[...] = v
    tw = jnp.tanh(xw)
    k_sum = jnp.zeros_like(k)
    for d in range(2):
        decay_rate = jax.nn.sigmoid(w0_ref[d:d + 1, :] + _dot3(tw, w2_ref[d])) * RWKV_DECAY_SCALE
        a = jax.nn.sigmoid(a0_ref[d:d + 1, :] + _dot3(xa, a2_ref[d]))
        k_d = k * (1.0 + (a - 1.0) * ka_ref[...])
        rows_out[ROW_W + d] = jnp.exp(-decay_rate)
        rows_out[ROW_K + d] = k_d
        rows_out[ROW_B + d] = kk * a
        k_sum = k_sum + k_d
    gate_out[...] = _dot3(jax.nn.sigmoid(xg), g2_ref[...])
    bonus_out[...] = _dot2_exact_rhs(r * k_sum * rk_gain_ref[...], ones) * v


def _rwkv_prep(u, mu, w0, w2, a0, a2, g2, k_k, k_a, r_k):
    bsz, t, _ = u.shape
    tm = ROW_TILE
    n_tiles = t // tm
    w = RWKV_W
    rk_blk = U_RWKV_RK // (2 * w)
    rest_blk = U_RWKV_REST // RWKV_REST_COLS
    sub = tm // 8
    n_sub = t // 8
    prev_idx = lambda b, i: jnp.maximum(i * sub - 1, 0)
    next_idx = lambda b, i: jnp.minimum((i + 1) * sub, n_sub - 1)
    row = lambda a: a.reshape(1, -1)
    const = lambda shape: pl.BlockSpec(shape, lambda b, i: (0,) * len(shape))
    tok = lambda width: pl.BlockSpec((None, tm, width), lambda b, i: (b, i, 0))
    rows_spec = pl.BlockSpec((N_SCAN_ROWS, None, tm, w), lambda b, i: (0, b, i, 0))
    sd = lambda *lead: jax.ShapeDtypeStruct((*lead, bsz, t, w), F32)
    return pl.pallas_call(
        functools.partial(_rwkv_prep_kernel, n_tiles=n_tiles),
        grid=(bsz, n_tiles),
        in_specs=[pl.BlockSpec((None, tm, 2 * w), lambda b, i: (b, i, rk_blk)),
                  pl.BlockSpec((None, 8, 2 * w), lambda b, i: (b, prev_idx(b, i), rk_blk)),
                  pl.BlockSpec((None, 8, 2 * w), lambda b, i: (b, next_idx(b, i), rk_blk)),
                  pl.BlockSpec((None, tm, RWKV_REST_COLS), lambda b, i: (b, i, rest_blk)),
                  pl.BlockSpec((None, 8, RWKV_REST_COLS), lambda b, i: (b, prev_idx(b, i), rest_blk)),
                  pl.BlockSpec((None, 8, RWKV_REST_COLS), lambda b, i: (b, next_idx(b, i), rest_blk)),
                  const((2, 2 * w)), const((2, RWKV_REST_COLS)),
                  const((2, w)), const((2, RWKV_DECAY_LORA, w)), const((2, w)), const((2, RWKV_AAA_LORA, w)),
                  const((RWKV_GATE_LORA, w)), const((1, w)), const((1, w)), const((1, w))],
        out_specs=[rows_spec, tok(w), tok(w), tok(w)],
        out_shape=[sd(N_SCAN_ROWS), sd(), sd(), sd()],
        compiler_params=_params("parallel", "parallel"),
        name="rwkv_prep",
    )(u, u, u, u, u, u, mu[:, :2 * w], mu[:, 2 * w:], w0, w2, a0, a2, g2, row(k_k), row(k_a), row(r_k))


def _transpose_tokens(z_ref, scr, bsz):
    w = RWKV_W
    for b in range(bsz):
        scr[b * w:(b + 1) * w, :] = z_ref[b].T
    if bsz * w < scr.shape[0]:
        scr[bsz * w:, :] = jnp.zeros((scr.shape[0] - bsz * w, scr.shape[1]), F32)


def _layout_rows_kernel(z_ref, o_ref, scr, *, bsz):
    n = RWKV_HEAD_DIM
    _transpose_tokens(z_ref, scr, bsz)
    for j in range(n):
        x = scr[pl.ds(j, LANES // 2, stride=n), :]
        o_ref[j] = jnp.concatenate([x, x], axis=0).T


def _layout_v_kernel(z_ref, o_ref, scr, *, bsz):
    n = RWKV_HEAD_DIM
    ts = z_ref.shape[1]
    _transpose_tokens(z_ref, scr, bsz)
    for i in range(n // 2):
        x0 = scr[pl.ds(i, LANES // 2, stride=n), :]
        x1 = scr[pl.ds(n // 2 + i, LANES // 2, stride=n), :]
        o_ref[pl.ds(i, ts, stride=n // 2), :] = jnp.concatenate([x0, x1], axis=0).T


def _scan_layout(rows, v):
    g, bsz, t, w = rows.shape
    n = RWKV_HEAD_DIM
    ts = LANES
    scr = pltpu.VMEM((LANES // 2 * n, ts), F32)
    rows_l = pl.pallas_call(
        functools.partial(_layout_rows_kernel, bsz=bsz),
        grid=(g, t // ts),
        in_specs=[pl.BlockSpec((None, bsz, ts, w), lambda k, i: (k, 0, i, 0))],
        out_specs=pl.BlockSpec((None, n, ts, LANES), lambda k, i: (k, 0, i, 0)),
        out_shape=jax.ShapeDtypeStruct((g, n, t, LANES), F32),
        scratch_shapes=[scr],
        compiler_params=_params("parallel", "parallel"),
        name="rwkv_layout_rows",
    )(rows)
    v_l = pl.pallas_call(
        functools.partial(_layout_v_kernel, bsz=bsz),
        grid=(t // ts,),
        in_specs=[pl.BlockSpec((bsz, ts, w), lambda i: (0, i, 0))],
        out_specs=pl.BlockSpec((ts * n // 2, LANES), lambda i: (i, 0)),
        out_shape=jax.ShapeDtypeStruct((t * n // 2, LANES), F32),
        scratch_shapes=[scr],
        compiler_params=_params("parallel"),
        name="rwkv_layout_v",
    )(v)
    return rows_l, v_l


def _rwkv_scan_kernel(r_ref, a_ref, w_ref, k_ref, b_ref, v_ref, y_ref, s_ref, sa_ref):
    n = RWKV_HEAD_DIM
    half = n // 2
    ts = r_ref.shape[1]
    fwd = pl.program_id(0) == 0

    @pl.when(pl.program_id(1) == 0)
    def _():
        s_ref[...] = jnp.zeros_like(s_ref)

    t_first = jnp.where(fwd, 0, ts - 1)
    acc = jnp.zeros((half, LANES), F32)
    for j in range(n):
        acc = acc + s_ref[j] * a_ref[j, pl.ds(t_first, 1), :]
    sa_ref[...] = acc

    def step(m, carry):
        t = jnp.where(fwd, m, ts - 1 - m)
        tn = jnp.clip(jnp.where(fwd, t + 1, t - 1), 0, ts - 1)
        sa = sa_ref[...]
        v = v_ref[t]
        y = jnp.zeros((half, LANES), F32)
        sa_next = jnp.zeros((half, LANES), F32)
        for j in range(n):
            s = (s_ref[j] * w_ref[j, pl.ds(t, 1), :] + sa * b_ref[j, pl.ds(t, 1), :]) + v * k_ref[j, pl.ds(t, 1), :]
            s_ref[j] = s
            y = y + s * r_ref[j, pl.ds(t, 1), :]
            sa_next = sa_next + s * a_ref[j, pl.ds(tn, 1), :]
        y_ref[t] = y
        sa_ref[...] = sa_next
        return carry

    lax.fori_loop(0, ts, step, 0)


def _rwkv_scan(rows, v, n_ctx):
    _, n, t, lanes = rows.shape
    ts = SCAN_STEPS
    nb = t // ts
    ncb = n_ctx // ts

    def blk(d, s):
        back = jnp.where(s < ncb, ncb - 1 - s, nb - 1 - (s - ncb))
        return jnp.where(d == 0, s, back)

    shared = lambda kind: pl.BlockSpec((None, n, ts, lanes), lambda d, s: (kind, 0, blk(d, s), 0))
    per_dir = lambda kind: pl.BlockSpec((None, n, ts, lanes), lambda d, s: (kind + d, 0, blk(d, s), 0))
    return pl.pallas_call(
        _rwkv_scan_kernel,
        grid=(2, nb),
        in_specs=[shared(ROW_R), shared(ROW_A), per_dir(ROW_W), per_dir(ROW_K), per_dir(ROW_B),
                  pl.BlockSpec((ts, n // 2, lanes), lambda d, s: (blk(d, s), 0, 0))],
        out_specs=pl.BlockSpec((None, ts, n // 2, lanes), lambda d, s: (d, blk(d, s), 0, 0)),
        out_shape=jax.ShapeDtypeStruct((2, t, n // 2, lanes), F32),
        scratch_shapes=[pltpu.VMEM((n, n // 2, lanes), F32), pltpu.VMEM((n // 2, lanes), F32)],
        compiler_params=_params("arbitrary", "arbitrary"),
        name="rwkv_scan",
    )(rows, rows, rows, rows, rows, v)


def _rwkv_readout_kernel(yf_ref, yb_ref, bonus_ref, gate_ref, g_ref, b_ref, o_ref, scr, *, bsz):
    n = RWKV_HEAD_DIM
    w = RWKV_W
    ts = o_ref.shape[1]
    for i in range(n // 2):
        rows = pl.ds(i, ts, stride=n // 2)
        yt = (yf_ref[rows, :] + yb_ref[rows, :]).T
        scr[pl.ds(i, LANES // 2, stride=n), :] = yt[:LANES // 2]
        scr[pl.ds(n // 2 + i, LANES // 2, stride=n), :] = yt[LANES // 2:]
    ones = _group_ones(w, n)
    inv = 1.0 / n
    for b in range(bsz):
        y = scr[b * w:(b + 1) * w, :].T
        mean = _dot2_exact_rhs(y, ones) * inv
        yc = y - mean
        var = _dot2_exact_rhs(yc * yc, ones) * inv
        yn = yc * lax.rsqrt(var + RWKV_GN_EPS) * g_ref[...] + b_ref[...]
        o_ref[b] = ((yn + bonus_ref[b]) * gate_ref[b]).astype(o_ref.dtype)


def _rwkv_readout(y, bonus, gate, ln_g, ln_b):
    bsz, t, w = bonus.shape
    n = RWKV_HEAD_DIM
    ts = LANES
    tok = pl.BlockSpec((bsz, ts, w), lambda i: (0, i, 0))
    vec = pl.BlockSpec((1, w), lambda i: (0, 0))
    return pl.pallas_call(
        functools.partial(_rwkv_readout_kernel, bsz=bsz),
        grid=(t // ts,),
        in_specs=[pl.BlockSpec((None, ts * n // 2, LANES), lambda i: (0, i, 0)),
                  pl.BlockSpec((None, ts * n // 2, LANES), lambda i: (1, i, 0)), tok, tok, vec, vec],
        out_specs=tok,
        out_shape=jax.ShapeDtypeStruct((bsz, t, w), BF16),
        scratch_shapes=[pltpu.VMEM((LANES // 2 * n, ts), F32)],
        compiler_params=_params("parallel"),
        name="rwkv_readout",
    )(y, y, bonus, gate, ln_g.reshape(1, w), ln_b.reshape(1, w))


def _merge_kernel(ya_ref, yr_ref, yw_ref, g0a_ref, g0b_ref, g1a_ref, g1b_ref, g2a_ref, g2b_ref,
                  x_ref, gate1_ref, a2_ref, b2_ref, wb_ref, wo_ref, wr_ref, br_ref,
                  x_out, h_out, ids_out, wts_out, cnt_out):
    first = jnp.logical_and(pl.program_id(0) == 0, pl.program_id(1) == 0)

    @pl.when(first)
    def _():
        cnt_out[...] = jnp.zeros_like(cnt_out)

    gate = lambda lo, hi: jax.nn.sigmoid(jnp.concatenate([lo[...], hi[...]], axis=1))
    merged = (gate(g0a_ref, g0b_ref) * _dot(ya_ref[...], wb_ref[0])
              + gate(g1a_ref, g1b_ref) * _dot(yr_ref[...], wb_ref[1])
              + gate(g2a_ref, g2b_ref) * _dot(yw_ref[...], wb_ref[2]))
    x = x_ref[...] + gate1_ref[...] * _dot(merged.astype(BF16), wo_ref[...])
    x_out[...] = x
    h = x * lax.rsqrt(jnp.mean(x * x, axis=-1, keepdims=True) + NORM_EPS) * a2_ref[...] + b2_ref[...]
    h_out[...] = h.astype(BF16)

    tm = x.shape[0]
    logits = _dot3(h, wr_ref[...]) + br_ref[...]
    lane = lax.broadcasted_iota(jnp.int32, (tm, LANES), 1)
    lane_f = lane.astype(F32)
    neg = -jnp.inf
    big = float(LANES)
    first = lambda hit: jnp.min(jnp.where(hit, lane_f, big), axis=-1, keepdims=True).astype(jnp.int32)
    is_grp = jnp.logical_and(lane >= MOE_EXPERTS, lane < MOE_EXPERTS + MOE_GROUPS)
    gl = jnp.where(is_grp, logits, neg)
    gmax = jnp.max(gl, axis=-1, keepdims=True)
    gidx = first(gl == gmax) - MOE_EXPERTS
    p_grp = 1.0 / jnp.sum(jnp.where(is_grp, jnp.exp(gl - gmax), 0.0), axis=-1, keepdims=True)
    in_grp = jnp.logical_and(lane < MOE_EXPERTS, lane // MOE_EXPERTS_PER_GROUP == gidx)
    el = jnp.where(in_grp, logits, neg)
    v1 = jnp.max(el, axis=-1, keepdims=True)
    i1 = first(el == v1)
    el2 = jnp.where(lane == i1, neg, el)
    v2 = jnp.max(el2, axis=-1, keepdims=True)
    i2 = first(el2 == v2)
    e2 = jnp.exp(v2 - v1)
    w1 = p_grp / (1.0 + e2)
    w2 = p_grp * e2 / (1.0 + e2)
    wts_out[...] = jnp.where(lane == 0, w1, jnp.where(lane == 1, w2, 0.0))

    onehot = jnp.where(jnp.logical_or(lane == i1, lane == i2), 1.0, 0.0)
    rr = lax.broadcasted_iota(jnp.int32, (tm, tm), 0)
    cc = lax.broadcasted_iota(jnp.int32, (tm, tm), 1)
    below = jnp.where(cc < rr, 1.0, 0.0).astype(BF16)
    before = _dot(below, onehot.astype(BF16)) + cnt_out[0:1, :]
    rank1 = jnp.sum(jnp.where(lane == i1, before, 0.0), axis=-1, keepdims=True).astype(jnp.int32)
    rank2 = jnp.sum(jnp.where(lane == i2, before, 0.0), axis=-1, keepdims=True).astype(jnp.int32)
    ids_out[...] = jnp.where(lane == 0, i1, jnp.where(lane == 1, i2, jnp.where(lane == 2, rank1,
                                                                                  jnp.where(lane == 3, rank2, 0))))
    cnt_out[...] = cnt_out[...] + jnp.sum(onehot, axis=0, keepdims=True)


def _merge(ya, yr, yw, u, x, gate1, a2, b2, w_branch, w_out, w_router, b_router, part):
    bsz, t, d = x.shape
    nb = bsz // MOE_PARTS
    b0 = part * nb
    tm = ROW_TILE
    sel = lambda b, i: (2 * (b + b0) + jnp.minimum(i, 1), 0, 0)
    tok = lambda width, blk=0: pl.BlockSpec((None, tm, width), lambda b, i: (b + b0, i, blk))
    own = lambda width: pl.BlockSpec((None, tm, width), lambda b, i: (b, i, 0))
    const = lambda shape: pl.BlockSpec(shape, lambda b, i: (0,) * len(shape))
    mod = pl.BlockSpec((None, 1, d), sel)
    n_gate_blocks = N_BRANCH * d // GATE_BLOCK
    return pl.pallas_call(
        _merge_kernel,
        grid=(nb, t // tm),
        in_specs=[tok(BRANCH_W), tok(BRANCH_W), tok(BRANCH_W)]
                 + [tok(GATE_BLOCK, U_GATE // GATE_BLOCK + k) for k in range(n_gate_blocks)]
                 + [tok(d),
                  mod, mod, mod,
                  const((N_BRANCH, BRANCH_W, d)), const((d, d)), const((d, LANES)), const((1, LANES))],
        out_specs=[tok(d), own(d), own(LANES), own(LANES), const((8, LANES))],
        out_shape=[jax.ShapeDtypeStruct((bsz, t, d), F32), jax.ShapeDtypeStruct((nb, t, d), BF16),
                   jax.ShapeDtypeStruct((nb, t, LANES), jnp.int32), jax.ShapeDtypeStruct((nb, t, LANES), F32),
                   jax.ShapeDtypeStruct((8, LANES), F32)],
        input_output_aliases={3 + n_gate_blocks: 0},
        compiler_params=_params("arbitrary", "arbitrary"),
        name="merge_router",
    )(ya, yr, yw, *([u] * n_gate_blocks), x, gate1, a2, b2, w_branch, w_out, w_router, b_router)


def _moe_kernel(be_ref, na_ref, x_ref, wg_ref, wu_ref, wd_ref, o_ref, wg_s, wu_s, wd_s):
    i = pl.program_id(0)
    active = i < na_ref[0]
    new_expert = jnp.logical_or(i == 0, be_ref[i] != be_ref[jnp.maximum(i - 1, 0)])

    @pl.when(jnp.logical_and(active, new_expert))
    def _():
        wg_s[...] = wg_ref[...].astype(BF16)
        wu_s[...] = wu_ref[...].astype(BF16)
        wd_s[...] = wd_ref[...].astype(BF16)

    @pl.when(active)
    def _():
        x = x_ref[...]
        act = _silu(_dot(x, wg_s[...])) * _dot(x, wu_s[...])
        o_ref[...] = _dot(act.astype(BF16), wd_s[...]).astype(o_ref.dtype)

    @pl.when(i >= na_ref[0])
    def _():
        o_ref[...] = jnp.zeros_like(o_ref)


def _moe_experts(buf, block_expert, n_active, w_gate, w_up, w_down, layer):
    rows, d = buf.shape
    hid = w_gate.shape[-1]
    grid_spec = pltpu.PrefetchScalarGridSpec(
        num_scalar_prefetch=2,
        grid=(rows // MOE_BLOCK,),
        in_specs=[pl.BlockSpec((MOE_BLOCK, d), lambda i, be, na: (i, 0)),
                  pl.BlockSpec((None, None, d, hid), lambda i, be, na: (layer, be[i], 0, 0)),
                  pl.BlockSpec((None, None, d, hid), lambda i, be, na: (layer, be[i], 0, 0)),
                  pl.BlockSpec((None, None, hid, d), lambda i, be, na: (layer, be[i], 0, 0))],
        out_specs=pl.BlockSpec((MOE_BLOCK, d), lambda i, be, na: (i, 0)),
        scratch_shapes=[pltpu.VMEM((d, hid), BF16), pltpu.VMEM((d, hid), BF16), pltpu.VMEM((hid, d), BF16)])
    return pl.pallas_call(
        _moe_kernel,
        grid_spec=grid_spec,
        out_shape=jax.ShapeDtypeStruct((rows, d), F32),
        compiler_params=_params("arbitrary"),
        name="moe_experts",
    )(block_expert, n_active, buf, w_gate, w_up, w_down)


def _combine_kernel(x_ref, *refs):
    y_refs, (w_ref, g_ref, o_ref) = refs[:-3], refs[-3:]
    tm = x_ref.shape[0]
    w = w_ref[...]
    for c, y_ref in enumerate(y_refs):
        cols = slice(c * LANES, (c + 1) * LANES)
        y = y_ref[pl.ds(0, tm, stride=2), :] * w[:, 0:1] + y_ref[pl.ds(1, tm, stride=2), :] * w[:, 1:2]
        o_ref[:, cols] = x_ref[:, cols] + g_ref[:, cols] * y


def _combine(x, y_pairs, wts, gate2, part):
    bsz, t, d = x.shape
    nb = y_pairs.shape[0]
    b0 = part * nb
    tm = ROW_TILE
    sel = lambda b, i: (2 * (b + b0) + jnp.minimum(i, 1), 0, 0)
    x_spec = pl.BlockSpec((None, tm, d), lambda b, i: (b + b0, i, 0))
    return pl.pallas_call(
        _combine_kernel,
        grid=(nb, t // tm),
        in_specs=[x_spec]
                 + [pl.BlockSpec((None, 2 * tm, LANES), functools.partial(lambda c, b, i: (b, i, c), c))
                    for c in range(d // LANES)]
                 + [pl.BlockSpec((None, tm, LANES), lambda b, i: (b, i, 0)), pl.BlockSpec((None, 1, d), sel)],
        out_specs=x_spec,
        out_shape=jax.ShapeDtypeStruct((bsz, t, d), F32),
        input_output_aliases={0: 0},
        compiler_params=_params("parallel", "parallel"),
        name="moe_combine",
    )(x, *([y_pairs] * (d // LANES)), wts, gate2)


def _moe(h, ids, wts, counts, w_gate, w_up, w_down, layer):
    bsz, t, d = h.shape
    n_tok = bsz * t
    n_pair = 2 * n_tok
    n_blocks = -(-n_pair // MOE_BLOCK) + MOE_EXPERTS
    n_blocks = -(-n_blocks // MOE_BLOCK_ROUND) * MOE_BLOCK_ROUND
    counts = counts[0, :MOE_EXPERTS].astype(jnp.int32)
    padded = (counts + MOE_BLOCK - 1) // MOE_BLOCK * MOE_BLOCK
    pad_end = jnp.cumsum(padded)
    pad_start = pad_end - padded
    expert = ids[..., 0:2].reshape(n_pair)
    rank = ids[..., 2:4].reshape(n_pair)
    dest = pad_start.at[expert].get(mode="promise_in_bounds") + rank
    token = jnp.arange(n_pair, dtype=jnp.int32) // 2
    src = jnp.zeros((n_blocks * MOE_BLOCK,), jnp.int32).at[dest].set(
        token, unique_indices=True, mode="promise_in_bounds")
    block_start = jnp.arange(n_blocks, dtype=jnp.int32) * MOE_BLOCK
    block_expert = jnp.minimum(jnp.sum((pad_end[None, :] <= block_start[:, None]).astype(jnp.int32), axis=1),
                               MOE_EXPERTS - 1).astype(jnp.int32)
    n_active = (pad_end[-1:] // MOE_BLOCK).astype(jnp.int32)
    buf = h.reshape(n_tok, d).at[src].get(mode="promise_in_bounds")
    yb = _moe_experts(buf, block_expert, n_active, w_gate, w_up, w_down, layer)
    pairs = yb.at[dest].get(mode="promise_in_bounds", unique_indices=True)
    return pairs.reshape(bsz, 2 * t, d)


def _rope_tables(n_ctx, n_lat, head_dim):
    rows = n_lat // GRID_W
    row = jnp.broadcast_to(jnp.arange(rows, dtype=F32)[:, None], (rows, GRID_W)).reshape(-1)
    col = jnp.broadcast_to(jnp.arange(GRID_W, dtype=F32)[None, :], (rows, GRID_W)).reshape(-1)
    quarter = head_dim // 4
    inv_freq = ROPE_THETA ** (-jnp.arange(quarter, dtype=F32) / quarter)
    ang = jnp.stack([row[:, None] * inv_freq, col[:, None] * inv_freq], axis=1)
    cos, sin = jnp.cos(ang), jnp.sin(ang)
    cos_t = jnp.stack([cos, cos], axis=2).reshape(n_lat, head_dim)
    sin_t = jnp.stack([-sin, sin], axis=2).reshape(n_lat, head_dim)
    cos_t = jnp.concatenate([jnp.ones((n_ctx, head_dim), F32), cos_t], axis=0)
    sin_t = jnp.concatenate([jnp.zeros((n_ctx, head_dim), F32), sin_t], axis=0)
    rep = LANES // head_dim
    return jnp.tile(cos_t, (1, rep)), jnp.tile(sin_t, (1, rep))


def kernel(x, c, ctx, c_ctx, ada_w, ada_b, norm1_g, norm2_g, w_in, att_qn_g, att_kn_g, ret_decay_logit, ret_gn_g, rwkv_mu, rwkv_w0, rwkv_w2, rwkv_a0, rwkv_a2, rwkv_g2, rwkv_k_k, rwkv_k_a, rwkv_r_k, rwkv_ln_g, rwkv_ln_b, w_branch, w_out, router_grp_w, router_grp_b, router_exp_w, router_exp_b, moe_w_gate, moe_w_up, moe_w_down):
    bsz, n_lat, d = x.shape
    n_ctx = ctx.shape[1]
    depth = ada_w.shape[0]
    assert d == D_MODEL and n_ctx == ROW_TILE and n_lat % ROW_TILE == 0 and n_lat % GRID_W == 0
    assert 2 * bsz * RWKV_HEADS <= LANES
    t_all = n_ctx + n_lat
    assert t_all % LANES == 0 and n_ctx % SCAN_STEPS == 0

    att_cos, att_sin = _rope_tables(n_ctx, n_lat, ATT_HEAD_DIM)
    ret_cos, ret_sin = _rope_tables(n_ctx, n_lat, RET_HEAD_DIM)

    rows = -(-(bsz + 1) // 8) * 8
    cvec = jnp.zeros((rows, d), F32).at[:bsz].set(c).at[bsz].set(c_ctx)
    mods = _modulation(cvec, ada_w, ada_b)

    xs = jnp.concatenate([ctx, x], axis=1)
    for layer in range(depth):
        m = mods[layer].reshape(rows, 6, d)
        pick = lambda j: jnp.stack([jnp.broadcast_to(m[bsz, j], (bsz, d)), m[:bsz, j]], axis=1).reshape(2 * bsz, 1, d)
        sh1, sc1, g1, sh2, sc2, g2 = (pick(j) for j in range(6))
        w_l = w_in[layer].astype(BF16)
        u = _in_proj(xs, norm1_g[layer] * (1.0 + sc1), sh1, w_l)

        score_bound = (ATT_HEAD_DIM * ATT_Q_SCALE * ATT_BOUND_MARGIN) * (
            jnp.max(jnp.abs(att_qn_g[layer])) * jnp.max(jnp.abs(att_kn_g[layer])))
        bounded = 2.0 * score_bound <= ATT_EXP2_RANGE
        q_att, kv_att = _att_prep(u, att_cos, att_sin, att_qn_g[layer], att_kn_g[layer],
                                  jnp.where(bounded, -score_bound, 0.0))
        ya = _attention(q_att, kv_att, bounded.astype(jnp.int32).reshape(1), n_ctx)

        log_gamma = jax.nn.log_sigmoid(ret_decay_logit[layer].astype(F32))
        yr = _retention(u, ret_cos, ret_sin, log_gamma, ret_gn_g[layer], n_ctx)

        rows_t, v_t, gate, bonus = _rwkv_prep(
            u, rwkv_mu[layer], rwkv_w0[layer], rwkv_w2[layer], rwkv_a0[layer], rwkv_a2[layer], rwkv_g2[layer],
            rwkv_k_k[layer], rwkv_k_a[layer], rwkv_r_k[layer].reshape(-1))
        rows_s, v_s = _scan_layout(rows_t, v_t)
        y_scan = _rwkv_scan(rows_s, v_s.reshape(t_all, RWKV_HEAD_DIM // 2, LANES), n_ctx)
        yw = _rwkv_readout(y_scan.reshape(2, t_all * RWKV_HEAD_DIM // 2, LANES), bonus, gate,
                           rwkv_ln_g[layer], rwkv_ln_b[layer])

        w_router = jnp.zeros((d, LANES), F32).at[:, :MOE_EXPERTS].set(router_exp_w[layer]).at[
            :, MOE_EXPERTS:MOE_EXPERTS + MOE_GROUPS].set(router_grp_w[layer])
        b_router = jnp.zeros((1, LANES), F32).at[0, :MOE_EXPERTS].set(router_exp_b[layer]).at[
            0, MOE_EXPERTS:MOE_EXPERTS + MOE_GROUPS].set(router_grp_b[layer])
        routed = []
        for part in range(MOE_PARTS):
            xs, h2, ids, wts, counts = _merge(
                ya, yr, yw, u, xs, g1, norm2_g[layer] * (1.0 + sc2), sh2,
                w_branch[layer].astype(BF16), w_out[layer].astype(BF16), w_router, b_router, part)
            routed.append((h2, ids, wts, counts))
        pairs = [_moe(h2, ids, wts, counts, moe_w_gate, moe_w_up, moe_w_down, layer)
                 for h2, ids, wts, counts in routed]
        for part in range(MOE_PARTS):
            xs = _combine(xs, pairs[part], routed[part][2], g2, part)
    return xs[:, n_ctx:]
```

```python
import functools

import jax
import jax.numpy as jnp
from jax import lax
from jax.experimental import pallas as pl
from jax.experimental.pallas import tpu as pltpu

F32 = jnp.float32
BF16 = jnp.bfloat16

D_MODEL = 1024
GRID_W = 64
NORM_EPS = 1e-6
ROPE_THETA = 10000.0

ATT_HEADS = 8
ATT_KV_HEADS = 2
ATT_HEAD_DIM = 64
ATT_GROUP = ATT_HEADS // ATT_KV_HEADS
ATT_W = ATT_HEADS * ATT_HEAD_DIM
ATT_KV_W = ATT_KV_HEADS * ATT_HEAD_DIM

RET_HEADS = 4
RET_HEAD_DIM = 128
RET_CHUNK = 128
RET_UNROLL = 17
RET_W = RET_HEADS * RET_HEAD_DIM

RWKV_HEADS = 8
RWKV_HEAD_DIM = 64
RWKV_W = RWKV_HEADS * RWKV_HEAD_DIM
RWKV_DECAY_LORA = 64
RWKV_AAA_LORA = 64
RWKV_GATE_LORA = 128
RWKV_GN_EPS = 64e-5
RWKV_DECAY_SCALE = 0.6065306597126334
RWKV_COLS = 3 * RWKV_W + RWKV_DECAY_LORA + RWKV_AAA_LORA + RWKV_GATE_LORA

N_BRANCH = 3
BRANCH_W = 512
IN_COLS = ATT_W + 2 * ATT_KV_W + 4 * RET_W + RWKV_COLS + N_BRANCH * D_MODEL

MOE_GROUPS = 4
MOE_EXPERTS_PER_GROUP = 8
MOE_EXPERTS = MOE_GROUPS * MOE_EXPERTS_PER_GROUP
MOE_HIDDEN = 512
MOE_BLOCK = 256
MOE_PARTS = 1

LANES = 128
ROW_TILE = 256
ATT_Q_TILE = 128
ATT_KEY_TILE = 256
ATT_Q_SCALE = ATT_HEAD_DIM ** -0.5 * 1.4426950408889634
ATT_BOUND_MARGIN = 1.01
ATT_EXP2_RANGE = 100.0
SCAN_STEPS = 64
VMEM_LIMIT = 56 * 1024 * 1024

ROW_R, ROW_A, ROW_W, ROW_K, ROW_B, N_SCAN_ROWS = 0, 1, 2, 4, 6, 8

ATT_COLS = ATT_W + 2 * ATT_KV_W
RWKV_REST_COLS = RWKV_COLS - 2 * RWKV_W
U_RWKV_RK = 0
U_RET = U_RWKV_RK + 2 * RWKV_W
U_ATT = U_RET + 4 * RET_W
U_RWKV_REST = U_ATT + ATT_COLS
U_GATE = U_RWKV_REST + RWKV_REST_COLS
GATE_BLOCK = 512
IN_PROJ_MOVES = (((0, U_ATT, ATT_COLS), (ATT_COLS, U_RET, 4 * RET_W),
                  (ATT_COLS + 4 * RET_W, U_RWKV_RK, 2 * RWKV_W)),
                 ((0, 0, IN_COLS // 2),))
assert U_ATT + ATT_COLS == IN_COLS // 2 and ATT_COLS + 4 * RET_W + 2 * RWKV_W == IN_COLS // 2


def _params(*sem):
    return pltpu.CompilerParams(dimension_semantics=sem, vmem_limit_bytes=VMEM_LIMIT)


def _dot(a, b):
    return jnp.dot(a, b, preferred_element_type=F32)


def _dot_nt(a, b):
    return lax.dot_general(a, b, (((1,), (1,)), ((), ())), preferred_element_type=F32)


def _split(a):
    hi = a.astype(BF16)
    lo = (a - hi.astype(F32)).astype(BF16)
    return hi, lo


def _dot3(a, b):
    ah, al = _split(a)
    bh, bl = _split(b)
    return _dot(ah, bh) + (_dot(al, bh) + _dot(ah, bl))


def _dot2_exact_rhs(a, b_bf16):
    ah, al = _split(a)
    return _dot(ah, b_bf16) + _dot(al, b_bf16)


def _group_ones(width, group):
    r = lax.broadcasted_iota(jnp.int32, (width, width), 0) // group
    c = lax.broadcasted_iota(jnp.int32, (width, width), 1) // group
    return jnp.where(r == c, 1.0, 0.0).astype(BF16)


def _silu(x):
    return x * jax.nn.sigmoid(x)


def _swap_halves(x, quarter):
    n = x.shape[-1]
    lane = lax.broadcasted_iota(jnp.int32, x.shape, x.ndim - 1)
    up = pltpu.roll(x, n - quarter, x.ndim - 1)
    down = pltpu.roll(x, quarter, x.ndim - 1)
    return jnp.where(lane % (2 * quarter) < quarter, up, down)


def _mod_kernel(c_ref, w_ref, b_ref, o_ref):
    o_ref[...] = _dot3(_silu(c_ref[...]), w_ref[...]) + b_ref[...]


def _modulation(cvec, ada_w, ada_b):
    depth, d, cols = ada_w.shape
    rows = cvec.shape[0]
    tn = 1536
    return pl.pallas_call(
        _mod_kernel,
        grid=(depth, cols // tn),
        in_specs=[pl.BlockSpec((rows, d), lambda l, j: (0, 0)),
                  pl.BlockSpec((None, d, tn), lambda l, j: (l, 0, j)),
                  pl.BlockSpec((None, 1, tn), lambda l, j: (l, 0, j))],
        out_specs=pl.BlockSpec((None, rows, tn), lambda l, j: (l, 0, j)),
        out_shape=jax.ShapeDtypeStruct((depth, rows, cols), F32),
        compiler_params=_params("parallel", "parallel"),
        name="modulation",
    )(cvec, ada_w, ada_b.reshape(depth, 1, cols))


def _in_proj_kernel(x_ref, a_ref, b_ref, w_ref, o_ref):
    x = x_ref[...]
    ms = jnp.mean(x * x, axis=-1, keepdims=True)
    h = (x * lax.rsqrt(ms + NORM_EPS) * a_ref[...] + b_ref[...]).astype(BF16)
    for half, moves in enumerate(IN_PROJ_MOVES):
        @pl.when(pl.program_id(0) == half)
        def _():
            for src, dst, width in moves:
                o_ref[:, dst:dst + width] = _dot(h, w_ref[:, src:src + width])


def _in_proj(x, mod_a, mod_b, w):
    bsz, t, d = x.shape
    cols = w.shape[1]
    tm, tn = ROW_TILE, cols // 2
    sel = lambda j, b, i: (2 * b + jnp.minimum(i, 1), 0, 0)
    return pl.pallas_call(
        _in_proj_kernel,
        grid=(cols // tn, bsz, t // tm),
        in_specs=[pl.BlockSpec((None, tm, d), lambda j, b, i: (b, i, 0)),
                  pl.BlockSpec((None, 1, d), sel),
                  pl.BlockSpec((None, 1, d), sel),
                  pl.BlockSpec((d, tn), lambda j, b, i: (0, j))],
        out_specs=pl.BlockSpec((None, tm, tn), lambda j, b, i: (b, i, j)),
        out_shape=jax.ShapeDtypeStruct((bsz, t, cols), F32),
        compiler_params=_params("parallel", "parallel", "parallel"),
        name="in_proj",
    )(x, mod_a, mod_b, w)


def _att_prep_kernel(u_ref, cos_ref, sin_ref, qg_ref, kg_ref, shift_ref, q_out, kv_out):
    hd = ATT_HEAD_DIM
    ones = _group_ones(LANES, hd)
    cos = cos_ref[...]
    sin = sin_ref[...]
    lane = lax.broadcasted_iota(jnp.int32, cos.shape, 1)
    low = lane < hd

    def two_heads(y, extra):
        fill = jnp.where(lane == hd, extra, 0.0)
        return jnp.where(low, y, fill), jnp.where(low, pltpu.roll(y, hd, 1), fill)

    n_q = ATT_W // LANES
    for j in range(n_q + 1):
        x = u_ref[:, j * LANES:(j + 1) * LANES]
        is_q = j < n_q
        gain = qg_ref[...] if is_q else kg_ref[...]
        ms = _dot2_exact_rhs(x * x, ones) * (1.0 / hd)
        y = x * lax.rsqrt(ms + NORM_EPS) * gain
        y = y * cos + _swap_halves(y, hd // 4) * sin
        if is_q:
            y = y * ATT_Q_SCALE
        out, base = (q_out, 2 * j) if is_q else (kv_out, 0)
        for h, yh in enumerate(two_heads(y, shift_ref[...] if is_q else 1.0)):
            out[:, (base + h) * LANES:(base + h + 1) * LANES] = yh.astype(BF16)
    v = u_ref[:, ATT_W + ATT_KV_W:]
    for h, vh in enumerate(two_heads(v, 1.0)):
        kv_out[:, (2 + h) * LANES:(3 + h) * LANES] = vh.astype(BF16)


def _att_prep(u, cos, sin, qn_g, kn_g, shift):
    bsz, t, _ = u.shape
    tm = ROW_TILE
    rep = LANES // ATT_HEAD_DIM
    shift = jnp.broadcast_to(shift.astype(F32), (1, LANES))
    return pl.pallas_call(
        _att_prep_kernel,
        grid=(bsz, t // tm),
        in_specs=[pl.BlockSpec((None, tm, ATT_COLS), lambda b, i: (b, i, U_ATT // ATT_COLS)),
                  pl.BlockSpec((tm, LANES), lambda b, i: (i, 0)),
                  pl.BlockSpec((tm, LANES), lambda b, i: (i, 0)),
                  pl.BlockSpec((1, LANES), lambda b, i: (0, 0)),
                  pl.BlockSpec((1, LANES), lambda b, i: (0, 0)),
                  pl.BlockSpec((1, LANES), lambda b, i: (0, 0))],
        out_specs=[pl.BlockSpec((None, tm, ATT_HEADS * LANES), lambda b, i: (b, i, 0)),
                   pl.BlockSpec((None, tm, 2 * ATT_KV_HEADS * LANES), lambda b, i: (b, i, 0))],
        out_shape=[jax.ShapeDtypeStruct((bsz, t, ATT_HEADS * LANES), BF16),
                   jax.ShapeDtypeStruct((bsz, t, 2 * ATT_KV_HEADS * LANES), BF16)],
        compiler_params=_params("parallel", "parallel"),
        name="att_prep",
    )(u, cos, sin, jnp.tile(qn_g, rep).reshape(1, LANES), jnp.tile(kn_g, rep).reshape(1, LANES), shift)


def _att_kernel(bounded_ref, q_ref, kv_ref, o_ref, *, n_ctx, tq):
    i = pl.program_id(1)
    hd = ATT_HEAD_DIM
    k_ref = v_ref = kv_ref

    def run(n_keys, bounded):
        tk = ATT_KEY_TILE
        for g in range(ATT_KV_HEADS):
            q = jnp.concatenate(
                [q_ref[:, (ATT_GROUP * g + h) * LANES:(ATT_GROUP * g + h + 1) * LANES] for h in range(ATT_GROUP)],
                axis=0)
            scores = lambda c: _dot_nt(q, k_ref[c * tk:(c + 1) * tk, g * LANES:(g + 1) * LANES])
            if not bounded:
                m = jnp.full((ATT_GROUP * tq, LANES), -jnp.inf, F32)
                for c in range(n_keys // tk):
                    s = scores(c)
                    for part in range(tk // LANES):
                        m = jnp.maximum(m, s[:, part * LANES:(part + 1) * LANES])
                m = jnp.max(m, axis=-1, keepdims=True)
            acc = jnp.zeros((ATT_GROUP * tq, LANES), F32)
            for c in range(n_keys // tk):
                p = jnp.exp2(scores(c) if bounded else scores(c) - m).astype(BF16)
                acc = acc + _dot(p, v_ref[c * tk:(c + 1) * tk, (ATT_KV_HEADS + g) * LANES:(ATT_KV_HEADS + g + 1) * LANES])
            o = acc[:, :hd] / acc[:, hd:hd + 1]
            for h in range(ATT_GROUP):
                c0 = (ATT_GROUP * g + h) * hd
                o_ref[:, c0:c0 + hd] = o[h * tq:(h + 1) * tq].astype(o_ref.dtype)

    is_ctx = i < n_ctx // tq
    bounded = bounded_ref[0] == 1

    @pl.when(is_ctx)
    def _():
        run(n_ctx, False)

    @pl.when(jnp.logical_and(jnp.logical_not(is_ctx), bounded))
    def _():
        run(k_ref.shape[0], True)

    @pl.when(jnp.logical_and(jnp.logical_not(is_ctx), jnp.logical_not(bounded)))
    def _():
        run(k_ref.shape[0], False)


def _attention(q, kv, bounded, n_ctx):
    bsz, t, _ = q.shape
    tq = ATT_Q_TILE
    kv_w = kv.shape[-1]
    assert n_ctx % ATT_KEY_TILE == 0 and t % ATT_KEY_TILE == 0
    grid_spec = pltpu.PrefetchScalarGridSpec(
        num_scalar_prefetch=1,
        grid=(bsz, t // tq),
        in_specs=[pl.BlockSpec((None, tq, ATT_HEADS * LANES), lambda b, i, f: (b, i, 0)),
                  pl.BlockSpec((None, t, kv_w), lambda b, i, f: (b, 0, 0))],
        out_specs=pl.BlockSpec((None, tq, ATT_W), lambda b, i, f: (b, i, 0)))
    return pl.pallas_call(
        functools.partial(_att_kernel, n_ctx=n_ctx, tq=tq),
        grid_spec=grid_spec,
        out_shape=jax.ShapeDtypeStruct((bsz, t, ATT_W), BF16),
        compiler_params=_params("parallel", "parallel"),
        name="attention",
    )(bounded, q, kv)


def _ret_kernel(q_ref, k_ref, v_ref, g_ref, cos_ref, sin_ref, lg_ref, gn_ref, o_ref,
                qs_ref, ks_ref, kvf_ref, kvb_ref, sf_ref, sb_ref, *, n_ctx):
    c = RET_CHUNK
    t = q_ref.shape[0]
    n_chunks = t // c
    n_cc = n_ctx // c
    quarter = RET_HEAD_DIM // 4
    scale = RET_HEAD_DIM ** -0.5
    lg_f = lg_ref[0]
    lg_b = lg_ref[1]
    row = lax.broadcasted_iota(jnp.int32, (c, c), 0)
    col = lax.broadcasted_iota(jnp.int32, (c, c), 1)
    rowf = row.astype(F32)
    lag = (row - col).astype(F32)

    def chunk(ci):
        r0 = pl.multiple_of(ci * c, c)
        cos = cos_ref[pl.ds(r0, c), :]
        sin = sin_ref[pl.ds(r0, c), :]
        q = q_ref[pl.ds(r0, c), :]
        k = k_ref[pl.ds(r0, c), :]
        q = q * cos + _swap_halves(q, quarter) * sin
        k = (k * cos + _swap_halves(k, quarter) * sin) * scale
        return r0, q, k, v_ref[pl.ds(r0, c), :]

    colf = col.astype(F32)
    d_key_f = jnp.exp(lg_f * (c - 1.0 - colf))
    d_key_b = jnp.exp(lg_b * colf)
    d_query_f = jnp.exp(lg_f * (rowf + 1.0))
    d_query_b = jnp.exp(lg_b * (float(c) - rowf))
    d_chunk_f = jnp.exp(lg_f * float(c))
    d_chunk_b = jnp.exp(lg_b * float(c))
    d_intra = (jnp.where(lag >= 0, jnp.exp(lg_f * jnp.maximum(lag, 0.0)), 0.0)
               + jnp.where(lag <= 0, jnp.exp(lg_b * jnp.maximum(-lag, 0.0)), 0.0))

    def summaries(n, carry):
        r0, q, k, v = chunk(n)
        qs_ref[pl.ds(r0, c), :] = q.astype(BF16)
        ks_ref[pl.ds(r0, c), :] = k.astype(BF16)
        vb = v.astype(BF16)
        kt = k.T
        kvf_ref[n] = _dot((kt * d_key_f).astype(BF16), vb)
        kvb_ref[n] = _dot((kt * d_key_b).astype(BF16), vb)
        return carry

    lax.fori_loop(0, n_chunks, summaries, 0, unroll=RET_UNROLL)

    def state_f(n, s):
        sf_ref[n] = s.astype(BF16)
        return d_chunk_f * s + kvf_ref[n]

    def state_b(n, s):
        ci = jnp.where(n < n_cc, n_cc - 1 - n, n_chunks - 1 - (n - n_cc))
        sb_ref[ci] = s.astype(BF16)
        return d_chunk_b * s + kvb_ref[ci]

    zero = jnp.zeros((RET_HEAD_DIM, RET_HEAD_DIM), F32)
    lax.fori_loop(0, n_chunks, state_f, zero)
    lax.fori_loop(0, n_chunks, state_b, zero)
    gn = gn_ref[...]

    def outputs(n, carry):
        r0 = pl.multiple_of(n * c, c)
        qb = qs_ref[pl.ds(r0, c), :]
        vb = v_ref[pl.ds(r0, c), :].astype(BF16)
        scores = _dot_nt(qb, ks_ref[pl.ds(r0, c), :]) * d_intra
        y = (_dot(scores.astype(BF16), vb) + _dot(qb, sf_ref[n]) * d_query_f) + _dot(qb, sb_ref[n]) * d_query_b
        yn = y * lax.rsqrt(jnp.mean(y * y, axis=-1, keepdims=True) + NORM_EPS) * gn
        o_ref[pl.ds(r0, c), :] = (_silu(g_ref[pl.ds(r0, c), :]) * yn).astype(o_ref.dtype)
        return carry

    lax.fori_loop(0, n_chunks, outputs, 0, unroll=RET_UNROLL)


def _retention(u, cos, sin, log_gamma, gn_g, n_ctx):
    bsz, t, _ = u.shape
    hd = RET_HEAD_DIM
    base = U_RET // hd
    spec = lambda off: pl.BlockSpec((None, t, hd), lambda b, h: (b, 0, base + off * RET_HEADS + h))
    lg = jnp.broadcast_to(log_gamma[:, :, None, None], (2, RET_HEADS, 1, LANES)).astype(F32)
    return pl.pallas_call(
        functools.partial(_ret_kernel, n_ctx=n_ctx),
        grid=(bsz, RET_HEADS),
        in_specs=[spec(0), spec(1), spec(2), spec(3),
                  pl.BlockSpec((t, hd), lambda b, h: (0, 0)),
                  pl.BlockSpec((t, hd), lambda b, h: (0, 0)),
                  pl.BlockSpec((2, None, 1, LANES), lambda b, h: (0, h, 0, 0)),
                  pl.BlockSpec((1, hd), lambda b, h: (0, h))],
        out_specs=pl.BlockSpec((None, t, hd), lambda b, h: (b, 0, h)),
        out_shape=jax.ShapeDtypeStruct((bsz, t, RET_W), BF16),
        scratch_shapes=[pltpu.VMEM((t, hd), BF16), pltpu.VMEM((t, hd), BF16),
                        pltpu.VMEM((t // RET_CHUNK, hd, hd), F32), pltpu.VMEM((t // RET_CHUNK, hd, hd), F32),
                        pltpu.VMEM((t // RET_CHUNK, hd, hd), BF16), pltpu.VMEM((t // RET_CHUNK, hd, hd), BF16)],
        compiler_params=_params("parallel", "parallel"),
        name="retention",
    )(u, u, u, u, cos, sin, lg, gn_g.reshape(1, RET_W))


def _rwkv_prep_kernel(rk_ref, rk_prev_ref, rk_next_ref, rest_ref, rest_prev_ref, rest_next_ref,
                      mu_rk_ref, mu_rest_ref, w0_ref, w2_ref, a0_ref, a2_ref, g2_ref, kk_ref, ka_ref, rk_gain_ref,
                      rows_out, v_out, gate_out, bonus_out, *, n_tiles):
    i = pl.program_id(1)
    tm = rk_ref.shape[0]
    has_prev = jnp.logical_and(i != 0, i != 1)
    has_next = jnp.logical_and(i != 0, i != n_tiles - 1)

    def shifted(x_ref, prev_ref, next_ref, mu_ref):
        x = x_ref[...]
        rows = lax.broadcasted_iota(jnp.int32, x.shape, 0)
        halo_prev = jnp.where(has_prev, prev_ref[7:8, :], 0.0)
        halo_next = jnp.where(has_next, next_ref[0:1, :], 0.0)
        prev = jnp.where(rows == 0, halo_prev, pltpu.roll(x, 1, 0))
        nxt = jnp.where(rows == tm - 1, halo_next, pltpu.roll(x, tm - 1, 0))
        return x + (prev - x) * mu_ref[0:1, :] + (nxt - x) * mu_ref[1:2, :]

    rk = shifted(rk_ref, rk_prev_ref, rk_next_ref, mu_rk_ref)
    rest = shifted(rest_ref, rest_prev_ref, rest_next_ref, mu_rest_ref)
    w = RWKV_W
    r = rk[:, 0:w]
    k = rk[:, w:2 * w]
    v = rest[:, 0:w]
    xw = rest[:, w:w + RWKV_DECAY_LORA]
    xa = rest[:, w + RWKV_DECAY_LORA:w + RWKV_DECAY_LORA + RWKV_AAA_LORA]
    xg = rest[:, w + RWKV_DECAY_LORA + RWKV_AAA_LORA:]

    ones = _group_ones(w, RWKV_HEAD_DIM)
    kk = k * kk_ref[...]
    kk = kk * lax.rsqrt(jnp.maximum(_dot2_exact_rhs(kk * kk, ones), 1e-12))
    rows_out[ROW_R] = r
    rows_out[ROW_A] = -kk
    v_out[...] = v
    tw = jnp.tanh(xw)
    k_sum = jnp.zeros_like(k)
    for d in range(2):
        decay_rate = jax.nn.sigmoid(w0_ref[d:d + 1, :] + _dot3(tw, w2_ref[d])) * RWKV_DECAY_SCALE
        a = jax.nn.sigmoid(a0_ref[d:d + 1, :] + _dot3(xa, a2_ref[d]))
        k_d = k * (1.0 + (a - 1.0) * ka_ref[...])
        rows_out[ROW_W + d] = jnp.exp(-decay_rate)
        rows_out[ROW_K + d] = k_d
        rows_out[ROW_B + d] = kk * a
        k_sum = k_sum + k_d
    gate_out[...] = _dot3(jax.nn.sigmoid(xg), g2_ref[...])
    bonus_out[...] = _dot2_exact_rhs(r * k_sum * rk_gain_ref[...], ones) * v


def _rwkv_prep(u, mu, w0, w2, a0, a2, g2, k_k, k_a, r_k):
    bsz, t, _ = u.shape
    tm = ROW_TILE
    n_tiles = t // tm
    w = RWKV_W
    rk_blk = U_RWKV_RK // (2 * w)
    rest_blk = U_RWKV_REST // RWKV_REST_COLS
    sub = tm // 8
    n_sub = t // 8
    prev_idx = lambda b, i: jnp.maximum(i * sub - 1, 0)
    next_idx = lambda b, i: jnp.minimum((i + 1) * sub, n_sub - 1)
    row = lambda a: a.reshape(1, -1)
    const = lambda shape: pl.BlockSpec(shape, lambda b, i: (0,) * len(shape))
    tok = lambda width: pl.BlockSpec((None, tm, width), lambda b, i: (b, i, 0))
    rows_spec = pl.BlockSpec((N_SCAN_ROWS, None, tm, w), lambda b, i: (0, b, i, 0))
    sd = lambda *lead: jax.ShapeDtypeStruct((*lead, bsz, t, w), F32)
    return pl.pallas_call(
        functools.partial(_rwkv_prep_kernel, n_tiles=n_tiles),
        grid=(bsz, n_tiles),
        in_specs=[pl.BlockSpec((None, tm, 2 * w), lambda b, i: (b, i, rk_blk)),
                  pl.BlockSpec((None, 8, 2 * w), lambda b, i: (b, prev_idx(b, i), rk_blk)),
                  pl.BlockSpec((None, 8, 2 * w), lambda b, i: (b, next_idx(b, i), rk_blk)),
                  pl.BlockSpec((None, tm, RWKV_REST_COLS), lambda b, i: (b, i, rest_blk)),
                  pl.BlockSpec((None, 8, RWKV_REST_COLS), lambda b, i: (b, prev_idx(b, i), rest_blk)),
                  pl.BlockSpec((None, 8, RWKV_REST_COLS), lambda b, i: (b, next_idx(b, i), rest_blk)),
                  const((2, 2 * w)), const((2, RWKV_REST_COLS)),
                  const((2, w)), const((2, RWKV_DECAY_LORA, w)), const((2, w)), const((2, RWKV_AAA_LORA, w)),
                  const((RWKV_GATE_LORA, w)), const((1, w)), const((1, w)), const((1, w))],
        out_specs=[rows_spec, tok(w), tok(w), tok(w)],
        out_shape=[sd(N_SCAN_ROWS), sd(), sd(), sd()],
        compiler_params=_params("parallel", "parallel"),
        name="rwkv_prep",
    )(u, u, u, u, u, u, mu[:, :2 * w], mu[:, 2 * w:], w0, w2, a0, a2, g2, row(k_k), row(k_a), row(r_k))


def _transpose_tokens(z_ref, scr, bsz):
    w = RWKV_W
    for b in range(bsz):
        scr[b * w:(b + 1) * w, :] = z_ref[b].T
    if bsz * w < scr.shape[0]:
        scr[bsz * w:, :] = jnp.zeros((scr.shape[0] - bsz * w, scr.shape[1]), F32)


def _layout_rows_kernel(z_ref, o_ref, scr, *, bsz):
    n = RWKV_HEAD_DIM
    _transpose_tokens(z_ref, scr, bsz)
    for j in range(n):
        x = scr[pl.ds(j, LANES // 2, stride=n), :]
        o_ref[j] = jnp.concatenate([x, x], axis=0).T


def _layout_v_kernel(z_ref, o_ref, scr, *, bsz):
    n = RWKV_HEAD_DIM
    ts = z_ref.shape[1]
    _transpose_tokens(z_ref, scr, bsz)
    for i in range(n // 2):
        x0 = scr[pl.ds(i, LANES // 2, stride=n), :]
        x1 = scr[pl.ds(n // 2 + i, LANES // 2, stride=n), :]
        o_ref[pl.ds(i, ts, stride=n // 2), :] = jnp.concatenate([x0, x1], axis=0).T


def _scan_layout(rows, v):
    g, bsz, t, w = rows.shape
    n = RWKV_HEAD_DIM
    ts = LANES
    scr = pltpu.VMEM((LANES // 2 * n, ts), F32)
    rows_l = pl.pallas_call(
        functools.partial(_layout_rows_kernel, bsz=bsz),
        grid=(g, t // ts),
        in_specs=[pl.BlockSpec((None, bsz, ts, w), lambda k, i: (k, 0, i, 0))],
        out_specs=pl.BlockSpec((None, n, ts, LANES), lambda k, i: (k, 0, i, 0)),
        out_shape=jax.ShapeDtypeStruct((g, n, t, LANES), F32),
        scratch_shapes=[scr],
        compiler_params=_params("parallel", "parallel"),
        name="rwkv_layout_rows",
    )(rows)
    v_l = pl.pallas_call(
        functools.partial(_layout_v_kernel, bsz=bsz),
        grid=(t // ts,),
        in_specs=[pl.BlockSpec((bsz, ts, w), lambda i: (0, i, 0))],
        out_specs=pl.BlockSpec((ts * n // 2, LANES), lambda i: (i, 0)),
        out_shape=jax.ShapeDtypeStruct((t * n // 2, LANES), F32),
        scratch_shapes=[scr],
        compiler_params=_params("parallel"),
        name="rwkv_layout_v",
    )(v)
    return rows_l, v_l


def _rwkv_scan_kernel(r_ref, a_ref, w_ref, k_ref, b_ref, v_ref, y_ref, s_ref, sa_ref):
    n = RWKV_HEAD_DIM
    half = n // 2
    ts = r_ref.shape[1]
    fwd = pl.program_id(0) == 0

    @pl.when(pl.program_id(1) == 0)
    def _():
        s_ref[...] = jnp.zeros_like(s_ref)

    t_first = jnp.where(fwd, 0, ts - 1)
    acc = jnp.zeros((half, LANES), F32)
    for j in range(n):
        acc = acc + s_ref[j] * a_ref[j, pl.ds(t_first, 1), :]
    sa_ref[...] = acc

    def step(m, carry):
        t = jnp.where(fwd, m, ts - 1 - m)
        tn = jnp.clip(jnp.where(fwd, t + 1, t - 1), 0, ts - 1)
        sa = sa_ref[...]
        v = v_ref[t]
        y = jnp.zeros((half, LANES), F32)
        sa_next = jnp.zeros((half, LANES), F32)
        for j in range(n):
            s = (s_ref[j] * w_ref[j, pl.ds(t, 1), :] + sa * b_ref[j, pl.ds(t, 1), :]) + v * k_ref[j, pl.ds(t, 1), :]
            s_ref[j] = s
            y = y + s * r_ref[j, pl.ds(t, 1), :]
            sa_next = sa_next + s * a_ref[j, pl.ds(tn, 1), :]
        y_ref[t] = y
        sa_ref[...] = sa_next
        return carry

    lax.fori_loop(0, ts, step, 0)


def _rwkv_scan(rows, v, n_ctx):
    _, n, t, lanes = rows.shape
    ts = SCAN_STEPS
    nb = t // ts
    ncb = n_ctx // ts

    def blk(d, s):
        back = jnp.where(s < ncb, ncb - 1 - s, nb - 1 - (s - ncb))
        return jnp.where(d == 0, s, back)

    shared = lambda kind: pl.BlockSpec((None, n, ts, lanes), lambda d, s: (kind, 0, blk(d, s), 0))
    per_dir = lambda kind: pl.BlockSpec((None, n, ts, lanes), lambda d, s: (kind + d, 0, blk(d, s), 0))
    return pl.pallas_call(
        _rwkv_scan_kernel,
        grid=(2, nb),
        in_specs=[shared(ROW_R), shared(ROW_A), per_dir(ROW_W), per_dir(ROW_K), per_dir(ROW_B),
                  pl.BlockSpec((ts, n // 2, lanes), lambda d, s: (blk(d, s), 0, 0))],
        out_specs=pl.BlockSpec((None, ts, n // 2, lanes), lambda d, s: (d, blk(d, s), 0, 0)),
        out_shape=jax.ShapeDtypeStruct((2, t, n // 2, lanes), F32),
        scratch_shapes=[pltpu.VMEM((n, n // 2, lanes), F32), pltpu.VMEM((n // 2, lanes), F32)],
        compiler_params=_params("arbitrary", "arbitrary"),
        name="rwkv_scan",
    )(rows, rows, rows, rows, rows, v)


def _rwkv_readout_kernel(yf_ref, yb_ref, bonus_ref, gate_ref, g_ref, b_ref, o_ref, scr, *, bsz):
    n = RWKV_HEAD_DIM
    w = RWKV_W
    ts = o_ref.shape[1]
    for i in range(n // 2):
        rows = pl.ds(i, ts, stride=n // 2)
        yt = (yf_ref[rows, :] + yb_ref[rows, :]).T
        scr[pl.ds(i, LANES // 2, stride=n), :] = yt[:LANES // 2]
        scr[pl.ds(n // 2 + i, LANES // 2, stride=n), :] = yt[LANES // 2:]
    ones = _group_ones(w, n)
    inv = 1.0 / n
    for b in range(bsz):
        y = scr[b * w:(b + 1) * w, :].T
        mean = _dot2_exact_rhs(y, ones) * inv
        yc = y - mean
        var = _dot2_exact_rhs(yc * yc, ones) * inv
        yn = yc * lax.rsqrt(var + RWKV_GN_EPS) * g_ref[...] + b_ref[...]
        o_ref[b] = ((yn + bonus_ref[b]) * gate_ref[b]).astype(o_ref.dtype)


def _rwkv_readout(y, bonus, gate, ln_g, ln_b):
    bsz, t, w = bonus.shape
    n = RWKV_HEAD_DIM
    ts = LANES
    tok = pl.BlockSpec((bsz, ts, w), lambda i: (0, i, 0))
    vec = pl.BlockSpec((1, w), lambda i: (0, 0))
    return pl.pallas_call(
        functools.partial(_rwkv_readout_kernel, bsz=bsz),
        grid=(t // ts,),
        in_specs=[pl.BlockSpec((None, ts * n // 2, LANES), lambda i: (0, i, 0)),
                  pl.BlockSpec((None, ts * n // 2, LANES), lambda i: (1, i, 0)), tok, tok, vec, vec],
        out_specs=tok,
        out_shape=jax.ShapeDtypeStruct((bsz, t, w), BF16),
        scratch_shapes=[pltpu.VMEM((LANES // 2 * n, ts), F32)],
        compiler_params=_params("parallel"),
        name="rwkv_readout",
    )(y, y, bonus, gate, ln_g.reshape(1, w), ln_b.reshape(1, w))


def _merge_kernel(ya_ref, yr_ref, yw_ref, g0a_ref, g0b_ref, g1a_ref, g1b_ref, g2a_ref, g2b_ref,
                  x_ref, gate1_ref, a2_ref, b2_ref, wb_ref, wo_ref, wr_ref, br_ref,
                  x_out, h_out, ids_out, wts_out, cnt_out):
    first = jnp.logical_and(pl.program_id(0) == 0, pl.program_id(1) == 0)

    @pl.when(first)
    def _():
        cnt_out[...] = jnp.zeros_like(cnt_out)

    gate = lambda lo, hi: jax.nn.sigmoid(jnp.concatenate([lo[...], hi[...]], axis=1))
    merged = (gate(g0a_ref, g0b_ref) * _dot(ya_ref[...], wb_ref[0])
              + gate(g1a_ref, g1b_ref) * _dot(yr_ref[...], wb_ref[1])
              + gate(g2a_ref, g2b_ref) * _dot(yw_ref[...], wb_ref[2]))
    x = x_ref[...] + gate1_ref[...] * _dot(merged.astype(BF16), wo_ref[...])
    x_out[...] = x
    h = x * lax.rsqrt(jnp.mean(x * x, axis=-1, keepdims=True) + NORM_EPS) * a2_ref[...] + b2_ref[...]
    h_out[...] = h.astype(BF16)

    tm = x.shape[0]
    logits = _dot3(h, wr_ref[...]) + br_ref[...]
    lane = lax.broadcasted_iota(jnp.int32, (tm, LANES), 1)
    lane_f = lane.astype(F32)
    neg = -jnp.inf
    big = float(LANES)
    first = lambda hit: jnp.min(jnp.where(hit, lane_f, big), axis=-1, keepdims=True).astype(jnp.int32)
    is_grp = jnp.logical_and(lane >= MOE_EXPERTS, lane < MOE_EXPERTS + MOE_GROUPS)
    gl = jnp.where(is_grp, logits, neg)
    gmax = jnp.max(gl, axis=-1, keepdims=True)
    gidx = first(gl == gmax) - MOE_EXPERTS
    p_grp = 1.0 / jnp.sum(jnp.where(is_grp, jnp.exp(gl - gmax), 0.0), axis=-1, keepdims=True)
    in_grp = jnp.logical_and(lane < MOE_EXPERTS, lane // MOE_EXPERTS_PER_GROUP == gidx)
    el = jnp.where(in_grp, logits, neg)
    v1 = jnp.max(el, axis=-1, keepdims=True)
    i1 = first(el == v1)
    el2 = jnp.where(lane == i1, neg, el)
    v2 = jnp.max(el2, axis=-1, keepdims=True)
    i2 = first(el2 == v2)
    e2 = jnp.exp(v2 - v1)
    w1 = p_grp / (1.0 + e2)
    w2 = p_grp * e2 / (1.0 + e2)
    wts_out[...] = jnp.where(lane == 0, w1, jnp.where(lane == 1, w2, 0.0))

    onehot = jnp.where(jnp.logical_or(lane == i1, lane == i2), 1.0, 0.0)
    rr = lax.broadcasted_iota(jnp.int32, (tm, tm), 0)
    cc = lax.broadcasted_iota(jnp.int32, (tm, tm), 1)
    below = jnp.where(cc < rr, 1.0, 0.0).astype(BF16)
    before = _dot(below, onehot.astype(BF16)) + cnt_out[0:1, :]
    rank1 = jnp.sum(jnp.where(lane == i1, before, 0.0), axis=-1, keepdims=True).astype(jnp.int32)
    rank2 = jnp.sum(jnp.where(lane == i2, before, 0.0), axis=-1, keepdims=True).astype(jnp.int32)
    ids_out[...] = jnp.where(lane == 0, i1, jnp.where(lane == 1, i2, jnp.where(lane == 2, rank1,
                                                                                  jnp.where(lane == 3, rank2, 0))))
    cnt_out[...] = cnt_out[...] + jnp.sum(onehot, axis=0, keepdims=True)


def _merge(ya, yr, yw, u, x, gate1, a2, b2, w_branch, w_out, w_router, b_router, part):
    bsz, t, d = x.shape
    nb = bsz // MOE_PARTS
    b0 = part * nb
    tm = ROW_TILE
    sel = lambda b, i: (2 * (b + b0) + jnp.minimum(i, 1), 0, 0)
    tok = lambda width, blk=0: pl.BlockSpec((None, tm, width), lambda b, i: (b + b0, i, blk))
    own = lambda width: pl.BlockSpec((None, tm, width), lambda b, i: (b, i, 0))
    const = lambda shape: pl.BlockSpec(shape, lambda b, i: (0,) * len(shape))
    mod = pl.BlockSpec((None, 1, d), sel)
    n_gate_blocks = N_BRANCH * d // GATE_BLOCK
    return pl.pallas_call(
        _merge_kernel,
        grid=(nb, t // tm),
        in_specs=[tok(BRANCH_W), tok(BRANCH_W), tok(BRANCH_W)]
                 + [tok(GATE_BLOCK, U_GATE // GATE_BLOCK + k) for k in range(n_gate_blocks)]
                 + [tok(d),
                  mod, mod, mod,
                  const((N_BRANCH, BRANCH_W, d)), const((d, d)), const((d, LANES)), const((1, LANES))],
        out_specs=[tok(d), own(d), own(LANES), own(LANES), const((8, LANES))],
        out_shape=[jax.ShapeDtypeStruct((bsz, t, d), F32), jax.ShapeDtypeStruct((nb, t, d), BF16),
                   jax.ShapeDtypeStruct((nb, t, LANES), jnp.int32), jax.ShapeDtypeStruct((nb, t, LANES), F32),
                   jax.ShapeDtypeStruct((8, LANES), F32)],
        input_output_aliases={3 + n_gate_blocks: 0},
        compiler_params=_params("arbitrary", "arbitrary"),
        name="merge_router",
    )(ya, yr, yw, *([u] * n_gate_blocks), x, gate1, a2, b2, w_branch, w_out, w_router, b_router)


def _moe_kernel(be_ref, na_ref, x_ref, wg_ref, wu_ref, wd_ref, o_ref, wg_s, wu_s, wd_s):
    i = pl.program_id(0)
    active = i < na_ref[0]
    new_expert = jnp.logical_or(i == 0, be_ref[i] != be_ref[jnp.maximum(i - 1, 0)])

    @pl.when(jnp.logical_and(active, new_expert))
    def _():
        wg_s[...] = wg_ref[...].astype(BF16)
        wu_s[...] = wu_ref[...].astype(BF16)
        wd_s[...] = wd_ref[...].astype(BF16)

    @pl.when(active)
    def _():
        x = x_ref[...]
        act = _silu(_dot(x, wg_s[...])) * _dot(x, wu_s[...])
        o_ref[...] = _dot(act.astype(BF16), wd_s[...]).astype(o_ref.dtype)

    @pl.when(i >= na_ref[0])
    def _():
        o_ref[...] = jnp.zeros_like(o_ref)


def _moe_experts(buf, block_expert, n_active, w_gate, w_up, w_down, layer):
    rows, d = buf.shape
    hid = w_gate.shape[-1]
    grid_spec = pltpu.PrefetchScalarGridSpec(
        num_scalar_prefetch=2,
        grid=(rows // MOE_BLOCK,),
        in_specs=[pl.BlockSpec((MOE_BLOCK, d), lambda i, be, na: (i, 0)),
                  pl.BlockSpec((None, None, d, hid), lambda i, be, na: (layer, be[i], 0, 0)),
                  pl.BlockSpec((None, None, d, hid), lambda i, be, na: (layer, be[i], 0, 0)),
                  pl.BlockSpec((None, None, hid, d), lambda i, be, na: (layer, be[i], 0, 0))],
        out_specs=pl.BlockSpec((MOE_BLOCK, d), lambda i, be, na: (i, 0)),
        scratch_shapes=[pltpu.VMEM((d, hid), BF16), pltpu.VMEM((d, hid), BF16), pltpu.VMEM((hid, d), BF16)])
    return pl.pallas_call(
        _moe_kernel,
        grid_spec=grid_spec,
        out_shape=jax.ShapeDtypeStruct((rows, d), F32),
        compiler_params=_params("arbitrary"),
        name="moe_experts",
    )(block_expert, n_active, buf, w_gate, w_up, w_down)


def _combine_kernel(x_ref, *refs):
    y_refs, (w_ref, g_ref, o_ref) = refs[:-3], refs[-3:]
    tm = x_ref.shape[0]
    w = w_ref[...]
    for c, y_ref in enumerate(y_refs):
        cols = slice(c * LANES, (c + 1) * LANES)
        y = y_ref[pl.ds(0, tm, stride=2), :] * w[:, 0:1] + y_ref[pl.ds(1, tm, stride=2), :] * w[:, 1:2]
        o_ref[:, cols] = x_ref[:, cols] + g_ref[:, cols] * y


def _combine(x, y_pairs, wts, gate2, part):
    bsz, t, d = x.shape
    nb = y_pairs.shape[0]
    b0 = part * nb
    tm = ROW_TILE
    sel = lambda b, i: (2 * (b + b0) + jnp.minimum(i, 1), 0, 0)
    x_spec = pl.BlockSpec((None, tm, d), lambda b, i: (b + b0, i, 0))
    return pl.pallas_call(
        _combine_kernel,
        grid=(nb, t // tm),
        in_specs=[x_spec]
                 + [pl.BlockSpec((None, 2 * tm, LANES), functools.partial(lambda c, b, i: (b, i, c), c))
                    for c in range(d // LANES)]
                 + [pl.BlockSpec((None, tm, LANES), lambda b, i: (b, i, 0)), pl.BlockSpec((None, 1, d), sel)],
        out_specs=x_spec,
        out_shape=jax.ShapeDtypeStruct((bsz, t, d), F32),
        input_output_aliases={0: 0},
        compiler_params=_params("parallel", "parallel"),
        name="moe_combine",
    )(x, *([y_pairs] * (d // LANES)), wts, gate2)


def _moe(h, ids, wts, counts, w_gate, w_up, w_down, layer):
    bsz, t, d = h.shape
    n_tok = bsz * t
    n_pair = 2 * n_tok
    n_blocks = -(-n_pair // MOE_BLOCK) + MOE_EXPERTS
    counts = counts[0, :MOE_EXPERTS].astype(jnp.int32)
    padded = (counts + MOE_BLOCK - 1) // MOE_BLOCK * MOE_BLOCK
    pad_end = jnp.cumsum(padded)
    pad_start = pad_end - padded
    expert = ids[..., 0:2].reshape(n_pair)
    rank = ids[..., 2:4].reshape(n_pair)
    dest = pad_start.at[expert].get(mode="promise_in_bounds") + rank
    token = jnp.arange(n_pair, dtype=jnp.int32) // 2
    src = (jnp.arange(n_blocks * MOE_BLOCK, dtype=jnp.int32) % n_tok).at[dest].set(
        token, unique_indices=True, mode="promise_in_bounds")
    block_start = jnp.arange(n_blocks, dtype=jnp.int32) * MOE_BLOCK
    block_expert = jnp.minimum(jnp.sum((pad_end[None, :] <= block_start[:, None]).astype(jnp.int32), axis=1),
                               MOE_EXPERTS - 1).astype(jnp.int32)
    n_active = (pad_end[-1:] // MOE_BLOCK).astype(jnp.int32)
    buf = h.reshape(n_tok, d).at[src].get(mode="promise_in_bounds")
    yb = _moe_experts(buf, block_expert, n_active, w_gate, w_up, w_down, layer)
    pairs = yb.at[dest].get(mode="promise_in_bounds", unique_indices=True)
    return pairs.reshape(bsz, 2 * t, d)


def _rope_tables(n_ctx, n_lat, head_dim):
    rows = n_lat // GRID_W
    row = jnp.broadcast_to(jnp.arange(rows, dtype=F32)[:, None], (rows, GRID_W)).reshape(-1)
    col = jnp.broadcast_to(jnp.arange(GRID_W, dtype=F32)[None, :], (rows, GRID_W)).reshape(-1)
    quarter = head_dim // 4
    inv_freq = ROPE_THETA ** (-jnp.arange(quarter, dtype=F32) / quarter)
    ang = jnp.stack([row[:, None] * inv_freq, col[:, None] * inv_freq], axis=1)
    cos, sin = jnp.cos(ang), jnp.sin(ang)
    cos_t = jnp.stack([cos, cos], axis=2).reshape(n_lat, head_dim)
    sin_t = jnp.stack([-sin, sin], axis=2).reshape(n_lat, head_dim)
    cos_t = jnp.concatenate([jnp.ones((n_ctx, head_dim), F32), cos_t], axis=0)
    sin_t = jnp.concatenate([jnp.zeros((n_ctx, head_dim), F32), sin_t], axis=0)
    rep = LANES // head_dim
    return jnp.tile(cos_t, (1, rep)), jnp.tile(sin_t, (1, rep))


def kernel(x, c, ctx, c_ctx, ada_w, ada_b, norm1_g, norm2_g, w_in, att_qn_g, att_kn_g, ret_decay_logit, ret_gn_g, rwkv_mu, rwkv_w0, rwkv_w2, rwkv_a0, rwkv_a2, rwkv_g2, rwkv_k_k, rwkv_k_a, rwkv_r_k, rwkv_ln_g, rwkv_ln_b, w_branch, w_out, router_grp_w, router_grp_b, router_exp_w, router_exp_b, moe_w_gate, moe_w_up, moe_w_down):
    bsz, n_lat, d = x.shape
    n_ctx = ctx.shape[1]
    depth = ada_w.shape[0]
    assert d == D_MODEL and n_ctx == ROW_TILE and n_lat % ROW_TILE == 0 and n_lat % GRID_W == 0
    assert 2 * bsz * RWKV_HEADS <= LANES
    t_all = n_ctx + n_lat
    assert t_all % LANES == 0 and n_ctx % SCAN_STEPS == 0

    att_cos, att_sin = _rope_tables(n_ctx, n_lat, ATT_HEAD_DIM)
    ret_cos, ret_sin = _rope_tables(n_ctx, n_lat, RET_HEAD_DIM)

    rows = -(-(bsz + 1) // 8) * 8
    cvec = jnp.zeros((rows, d), F32).at[:bsz].set(c).at[bsz].set(c_ctx)
    mods = _modulation(cvec, ada_w, ada_b)

    xs = jnp.concatenate([ctx, x], axis=1)
    for layer in range(depth):
        m = mods[layer].reshape(rows, 6, d)
        pick = lambda j: jnp.stack([jnp.broadcast_to(m[bsz, j], (bsz, d)), m[:bsz, j]], axis=1).reshape(2 * bsz, 1, d)
        sh1, sc1, g1, sh2, sc2, g2 = (pick(j) for j in range(6))
        w_l = w_in[layer].astype(BF16)
        u = _in_proj(xs, norm1_g[layer] * (1.0 + sc1), sh1, w_l)

        score_bound = (ATT_HEAD_DIM * ATT_Q_SCALE * ATT_BOUND_MARGIN) * (
            jnp.max(jnp.abs(att_qn_g[layer])) * jnp.max(jnp.abs(att_kn_g[layer])))
        bounded = 2.0 * score_bound <= ATT_EXP2_RANGE
        q_att, kv_att = _att_prep(u, att_cos, att_sin, att_qn_g[layer], att_kn_g[layer],
                                  jnp.where(bounded, -score_bound, 0.0))
        ya = _attention(q_att, kv_att, bounded.astype(jnp.int32).reshape(1), n_ctx)

        log_gamma = jax.nn.log_sigmoid(ret_decay_logit[layer].astype(F32))
        yr = _retention(u, ret_cos, ret_sin, log_gamma, ret_gn_g[layer], n_ctx)

        rows_t, v_t, gate, bonus = _rwkv_prep(
            u, rwkv_mu[layer], rwkv_w0[layer], rwkv_w2[layer], rwkv_a0[layer], rwkv_a2[layer], rwkv_g2[layer],
            rwkv_k_k[layer], rwkv_k_a[layer], rwkv_r_k[layer].reshape(-1))
        rows_s, v_s = _scan_layout(rows_t, v_t)
        y_scan = _rwkv_scan(rows_s, v_s.reshape(t_all, RWKV_HEAD_DIM // 2, LANES), n_ctx)
        yw = _rwkv_readout(y_scan.reshape(2, t_all * RWKV_HEAD_DIM // 2, LANES), bonus, gate,
                           rwkv_ln_g[layer], rwkv_ln_b[layer])

        w_router = jnp.zeros((d, LANES), F32).at[:, :MOE_EXPERTS].set(router_exp_w[layer]).at[
            :, MOE_EXPERTS:MOE_EXPERTS + MOE_GROUPS].set(router_grp_w[layer])
        b_router = jnp.zeros((1, LANES), F32).at[0, :MOE_EXPERTS].set(router_exp_b[layer]).at[
            0, MOE_EXPERTS:MOE_EXPERTS + MOE_GROUPS].set(router_grp_b[layer])
        routed = []
        for part in range(MOE_PARTS):
            xs, h2, ids, wts, counts = _merge(
                ya, yr, yw, u, xs, g1, norm2_g[layer] * (1.0 + sc2), sh2,
                w_branch[layer].astype(BF16), w_out[layer].astype(BF16), w_router, b_router, part)
            routed.append((h2, ids, wts, counts))
        pairs = [_moe(h2, ids, wts, counts, moe_w_gate, moe_w_up, moe_w_down, layer)
                 for h2, ids, wts, counts in routed]
        for part in range(MOE_PARTS):
            xs = _combine(xs, pairs[part], routed[part][2], g2, part)
    return xs[:, n_ctx:]
```
